```python
import jax
import jax.numpy as jnp
from jax import lax
import numpy as np

D_MODEL = 1024
BATCH = 1
SEQ = 16384
DEPTH = 2
DEC_BATCH = 32
DEC_SEQ = 4
PAST_LEN = 16384
PAGE_SIZE = 128

H_A = 4
DH_A = 256
W_A = H_A * DH_A
CONV_W = 4
MLSTM_CHUNK = 64
G_B = 4
CHUNK_B = 128
W_B = 1024
DG_B = W_B // G_B
DIL_PATTERNS = ((128, 1), (512, 4), (2048, 16))
N_GROUPS_C = len(DIL_PATTERNS)
HG_C = 4
DH_C = 64
W_C = N_GROUPS_C * HG_C * DH_C
SPAN_C = 128
ROT_DIM = DH_C // 4
ROPE_THETA = 500000.0
EPS = 1e-6
W_TOT = W_A + W_B + W_C
_SPLITS = (W_A, W_A, W_A, W_A, H_A, H_A, W_A, W_B, W_B, W_B, W_C, W_C, W_C, W_C, D_MODEL, D_MODEL, D_MODEL)
N_IN = sum(_SPLITS)

kernel_name = 'hybrid_mlstm_chunkmlp_dilated_step'


def rmsnorm(x, g):
    xf = x.astype(jnp.float32)
    r = lax.rsqrt(jnp.mean(xf * xf, axis=-1, keepdims=True) + EPS)
    return (xf * r * g.astype(jnp.float32)).astype(x.dtype)


def layernorm(x, g, b):
    xf = x.astype(jnp.float32)
    mu = jnp.mean(xf, axis=-1, keepdims=True)
    var = jnp.mean(jnp.square(xf - mu), axis=-1, keepdims=True)
    return ((xf - mu) * lax.rsqrt(var + EPS) * g.astype(jnp.float32) + b.astype(jnp.float32)).astype(x.dtype)


def rope_partial(x, pos):
    half = ROT_DIM // 2
    inv = ROPE_THETA ** (-jnp.arange(half, dtype=jnp.float32) / half)
    ang = pos.astype(jnp.float32)[:, None] * inv[None, :]
    cos = jnp.cos(ang)[:, None, :]
    sin = jnp.sin(ang)[:, None, :]
    xr = x[..., :ROT_DIM].astype(jnp.float32)
    x1, x2 = xr[..., :half], xr[..., half:]
    rot = jnp.concatenate([x1 * cos - x2 * sin, x2 * cos + x1 * sin], axis=-1)
    return jnp.concatenate([rot.astype(x.dtype), x[..., ROT_DIM:]], axis=-1)


def causal_conv_silu(x_ext, w, b):
    T = x_ext.shape[1] - (CONV_W - 1)
    y = b
    for tap in range(CONV_W):
        y = y + x_ext[:, tap:tap + T] * w[tap]
    return jax.nn.silu(y)


def mlstm_chunk(carry, inp):
    C, n, m = carry
    q, k, v, ig, lf = inp
    L = q.shape[1]
    bT = jnp.swapaxes(jnp.cumsum(lf, axis=1), 1, 2)
    igT = jnp.swapaxes(ig, 1, 2)
    logw = bT[:, :, :, None] - bT[:, :, None, :] + igT[:, :, None, :]
    causal = jnp.tril(jnp.ones((L, L), dtype=bool))
    logw = jnp.where(causal, logw, -jnp.inf)
    inter = bT + m[:, :, None]
    m_t = jnp.maximum(inter, jnp.max(logw, axis=-1))
    w_intra = jnp.exp(logw - m_t[..., None])
    w_inter = jnp.exp(inter - m_t)
    s = w_intra * jnp.einsum('bthd,bshd->bhts', q, k)
    num = jnp.einsum('bhts,bshd->bhtd', s, v) + w_inter[..., None] * jnp.einsum('bhvk,bthk->bhtv', C, q)
    den = jnp.sum(s, axis=-1) + w_inter * jnp.einsum('bhk,bthk->bht', n, q)
    h = num / jnp.maximum(jnp.abs(den), jnp.exp(-m_t))[..., None]
    m_new = m_t[:, :, -1]
    decay = jnp.exp(inter[:, :, -1] - m_new)
    w_s = jnp.exp(bT[:, :, -1:] - bT + igT - m_new[..., None])
    C_new = decay[..., None, None] * C + jnp.einsum('bhs,bshv,bshk->bhvk', w_s, v, k)
    n_new = decay[..., None] * n + jnp.einsum('bhs,bshk->bhk', w_s, k)
    return (C_new, n_new, m_new), jnp.swapaxes(h, 1, 2)


def dilated_attn_prompt(q, k, v, dilation):
    B, S, H, Dh = q.shape
    L = S // dilation
    nb = -(-L // SPAN_C)
    Lp = nb * SPAN_C

    def to_sub(t):
        t = t.reshape(B, L, dilation, H, Dh).transpose(0, 2, 1, 3, 4)
        t = jnp.pad(t, ((0, 0), (0, 0), (0, Lp - L), (0, 0), (0, 0)))
        return t.reshape(B, dilation, nb, SPAN_C, H, Dh)

    def with_prev(t):
        prev = jnp.pad(t, ((0, 0), (0, 0), (1, 0), (0, 0), (0, 0), (0, 0)))[:, :, :-1]
        return jnp.concatenate([prev, t], axis=3)

    qs = to_sub(q)
    kb = with_prev(to_sub(k))
    vb = with_prev(to_sub(v))
    scores = jnp.einsum('brcihd,brcjhd->brchij', qs.astype(jnp.float32), kb.astype(jnp.float32)) * (Dh ** -0.5)
    i = jnp.arange(SPAN_C)[:, None]
    j = jnp.arange(2 * SPAN_C)[None, :]
    band = (j >= i) & (j <= i + SPAN_C)
    mask = band[None] & ((jnp.arange(nb)[:, None, None] > 0) | (j[None] >= SPAN_C))
    scores = jnp.where(mask[None, None, :, None], scores, -jnp.inf)
    mx = jnp.max(scores, axis=-1, keepdims=True)
    p = jnp.exp(scores - mx)
    den = jnp.sum(p, axis=-1)
    lse = mx[..., 0] + jnp.log(den)
    o = jnp.einsum('brchij,brcjhd->brcihd', p, vb.astype(jnp.float32)) / jnp.swapaxes(den, 3, 4)[..., None]
    o = o.reshape(B, dilation, Lp, H, Dh)[:, :, :L].transpose(0, 2, 1, 3, 4).reshape(B, S, H, Dh)
    lse = jnp.swapaxes(lse, 3, 4).reshape(B, dilation, Lp, H)[:, :, :L].transpose(0, 2, 1, 3).reshape(B, S, H)
    return o, lse


def dilated_attn_sample(q, k_all, v_all, dilation, n_buf):
    T = q.shape[1]
    steps = jnp.arange(SPAN_C + 1)
    idx = n_buf + jnp.arange(T)[:, None] - dilation * steps[None, :]
    valid = idx >= 0
    idx_c = jnp.maximum(idx, 0)
    kg = k_all[:, idx_c].astype(jnp.float32)
    vg = v_all[:, idx_c].astype(jnp.float32)
    scores = jnp.einsum('bthd,btmhd->bthm', q.astype(jnp.float32), kg) * (q.shape[-1] ** -0.5)
    scores = jnp.where(valid[None, :, None, :], scores, -jnp.inf)
    mx = jnp.max(scores, axis=-1, keepdims=True)
    p = jnp.exp(scores - mx)
    den = jnp.sum(p, axis=-1)
    o = jnp.einsum('bthm,btmhd->bthd', p, vg) / den[..., None]
    return o, mx[..., 0] + jnp.log(den)


def layer(x, pos, prompt, lw, st):
    (norm_g, w_in, b_igate, b_fgate, conv_w, conv_b, ln_v_g, ln_v_b,
     w_spatial, b_spatial, w_branch, w_out) = lw
    B, T, _ = x.shape
    f32 = jnp.float32
    xn = rmsnorm(x, norm_g)
    split_at = np.cumsum(_SPLITS)[:-1].tolist()
    (a_q, a_k, a_v, a_o, a_i, a_f, a_z, b_u, b_v, b_z,
     c_q, c_k, c_v, c_z, g_a, g_b, g_c) = jnp.split(xn @ w_in, split_at, axis=-1)
    new = {}

    qk_pre = jnp.concatenate([a_q, a_k], axis=-1)
    if prompt:
        hist = jnp.zeros((B, CONV_W - 1, 2 * W_A), qk_pre.dtype)
    else:
        hist = st[3].astype(qk_pre.dtype)
    qk_ext = jnp.concatenate([hist, qk_pre], axis=1)
    new['conv'] = qk_ext[:, -(CONV_W - 1):]
    qk = causal_conv_silu(qk_ext, conv_w, conv_b).astype(f32)
    q = qk[..., :W_A].reshape(B, T, H_A, DH_A)
    k = qk[..., W_A:].reshape(B, T, H_A, DH_A) * (DH_A ** -0.5)
    v = a_v.astype(f32).reshape(B, T, H_A, DH_A)
    ig = (a_i + b_igate).astype(f32)
    lf = jax.nn.log_sigmoid((a_f + b_fgate).astype(f32))
    if prompt:
        carry0 = (jnp.zeros((B, H_A, DH_A, DH_A), f32), jnp.zeros((B, H_A, DH_A), f32), jnp.zeros((B, H_A), f32))
        nc = T // MLSTM_CHUNK

        def to_chunks(t):
            return jnp.moveaxis(t.reshape((B, nc, MLSTM_CHUNK) + t.shape[2:]), 1, 0)

        (C1, n1, m1), hs = lax.scan(mlstm_chunk, carry0,
                                    (to_chunks(q), to_chunks(k), to_chunks(v), to_chunks(ig), to_chunks(lf)))
        h = jnp.moveaxis(hs, 0, 1).reshape(B, T, W_A)
    else:
        carry0 = (st[0].astype(f32), st[1].astype(f32), st[2].astype(f32))
        (C1, n1, m1), h = mlstm_chunk(carry0, (q, k, v, ig, lf))
        h = h.reshape(B, T, W_A)
    new['C'], new['n'], new['m'] = C1, n1, m1
    y_a = h.astype(x.dtype) * jax.nn.sigmoid(a_o) * jax.nn.silu(a_z)

    vn = layernorm(b_v, ln_v_g, ln_v_b)
    Lc = CHUNK_B if prompt else T
    vc = vn.reshape(B, T // Lc, Lc, G_B, DG_B)
    ws = w_spatial[:, :Lc, :Lc] * jnp.tril(jnp.ones((Lc, Lc), w_spatial.dtype))
    sp = jnp.einsum('gij,bcjgd->bcigd', ws, vc) + jnp.swapaxes(b_spatial[:, :Lc], 0, 1)[None, None, :, :, None]
    y_b = b_u * sp.reshape(B, T, W_B) * jax.nn.silu(b_z)
    if not prompt:
        new['chunk_v'] = vn

    cq = rope_partial(c_q.reshape(B, T, N_GROUPS_C * HG_C, DH_C), pos)
    ck = rope_partial(c_k.reshape(B, T, N_GROUPS_C * HG_C, DH_C), pos)
    cv = c_v.reshape(B, T, N_GROUPS_C * HG_C, DH_C)
    outs, lses = [], []
    for g, (win, dil) in enumerate(DIL_PATTERNS):
        sl = slice(g * HG_C, (g + 1) * HG_C)
        qg, kg, vg = cq[:, :, sl], ck[:, :, sl], cv[:, :, sl]
        if prompt:
            o, lse = dilated_attn_prompt(qg, kg, vg, dil)
            keep = min(win, T)
            new['kv%d' % g] = jnp.stack([kg[:, T - keep:], vg[:, T - keep:]], axis=2)
        else:
            buf = st[4 + g]
            n_buf = buf.shape[1]
            k_all = jnp.concatenate([buf[:, :, 0].astype(kg.dtype), kg], axis=1)
            v_all = jnp.concatenate([buf[:, :, 1].astype(vg.dtype), vg], axis=1)
            o, lse = dilated_attn_sample(qg, k_all, v_all, dil, n_buf)
            new['kv%d' % g] = jnp.stack([kg, vg], axis=2)
        outs.append(o)
        lses.append(lse)
    alpha = jax.nn.softmax(jnp.stack(lses, axis=0), axis=0)
    y_c = jnp.concatenate([outs[g] * alpha[g][..., None] for g in range(N_GROUPS_C)], axis=2)
    y_c = y_c.reshape(B, T, W_C).astype(x.dtype) * jax.nn.silu(c_z)

    pa = y_a @ w_branch[:W_A]
    pb = y_b @ w_branch[W_A:W_A + W_B]
    pc = y_c @ w_branch[W_A + W_B:]
    merged = jax.nn.sigmoid(g_a) * pa + jax.nn.sigmoid(g_b) * pb + jax.nn.sigmoid(g_c) * pc
    return x + merged @ w_out, new


def setup_inputs(seed: int = 0) -> dict:
    key = jax.random.key(seed)
    ks = jax.random.split(key, 32)
    f32 = jnp.float32

    def nrm(k, shape, s):
        return jax.random.normal(k, shape, f32) * s

    (w0, _), (w1, _), (w2, _) = DIL_PATTERNS
    w_branch = jnp.concatenate([nrm(ks[20], (DEPTH, W_A, D_MODEL), W_A ** -0.5),
                                nrm(ks[21], (DEPTH, W_B, D_MODEL), W_B ** -0.5),
                                nrm(ks[22], (DEPTH, W_C, D_MODEL), W_C ** -0.5)], axis=1)
    return {
        'x_prompt': nrm(ks[0], (BATCH, SEQ, D_MODEL), 1.0),
        'x_sample': nrm(ks[1], (DEC_BATCH, DEC_SEQ, D_MODEL), 1.0),
        'state_C': nrm(ks[2], (DEPTH, DEC_BATCH, H_A, DH_A, DH_A), 0.05),
        'state_n': nrm(ks[3], (DEPTH, DEC_BATCH, H_A, DH_A), 0.5),
        'state_m': nrm(ks[4], (DEPTH, DEC_BATCH, H_A), 1.0),
        'state_conv': nrm(ks[5], (DEPTH, DEC_BATCH, CONV_W - 1, 2 * W_A), 1.0),
        'cache_kv_w128': nrm(ks[6], (DEPTH, DEC_BATCH, min(w0, PAST_LEN), 2, HG_C, DH_C), 1.0),
        'cache_kv_w512': nrm(ks[7], (DEPTH, DEC_BATCH, min(w1, PAST_LEN), 2, HG_C, DH_C), 1.0),
        'cache_kv_w2048': nrm(ks[8], (DEPTH, DEC_BATCH, min(w2, PAST_LEN), 2, HG_C, DH_C), 1.0),
        'norm_g': 1.0 + nrm(ks[9], (DEPTH, D_MODEL), 0.01),
        'w_in': nrm(ks[10], (DEPTH, D_MODEL, N_IN), D_MODEL ** -0.5),
        'b_igate': nrm(ks[11], (DEPTH, H_A), 0.1),
        'b_fgate': jnp.linspace(3.0, 6.0, H_A, dtype=f32)[None, :] + nrm(ks[12], (DEPTH, H_A), 0.1),
        'conv_w': nrm(ks[13], (DEPTH, CONV_W, 2 * W_A), CONV_W ** -0.5),
        'conv_b': nrm(ks[14], (DEPTH, 2 * W_A), 0.01),
        'ln_v_g': 1.0 + nrm(ks[15], (DEPTH, W_B), 0.01),
        'ln_v_b': nrm(ks[16], (DEPTH, W_B), 0.01),
        'w_spatial': nrm(ks[17], (DEPTH, G_B, CHUNK_B, CHUNK_B), CHUNK_B ** -0.5),
        'b_spatial': 1.0 + nrm(ks[18], (DEPTH, G_B, CHUNK_B), 0.01),
        'w_branch': w_branch,
        'w_out': nrm(ks[23], (DEPTH, D_MODEL, D_MODEL), D_MODEL ** -0.5),
        'final_norm_g': 1.0 + nrm(ks[24], (D_MODEL,), 0.01),
    }


def reference(x_prompt, x_sample, state_C, state_n, state_m, state_conv, cache_kv_w128, cache_kv_w512,
              cache_kv_w2048, norm_g, w_in, b_igate, b_fgate, conv_w, conv_b, ln_v_g, ln_v_b, w_spatial,
              b_spatial, w_branch, w_out, final_norm_g):
    pos_p = jnp.arange(x_prompt.shape[1])
    pos_s = PAST_LEN + jnp.arange(x_sample.shape[1])
    hp, hs = x_prompt, x_sample
    p_st, s_st = [], []
    for l in range(DEPTH):
        lw = (norm_g[l], w_in[l], b_igate[l], b_fgate[l], conv_w[l], conv_b[l], ln_v_g[l], ln_v_b[l],
              w_spatial[l], b_spatial[l], w_branch[l], w_out[l])
        hp, sp = layer(hp, pos_p, True, lw, None)
        hs, ss = layer(hs, pos_s, False, lw, (state_C[l], state_n[l], state_m[l], state_conv[l],
                                              cache_kv_w128[l], cache_kv_w512[l], cache_kv_w2048[l]))
        p_st.append(sp)
        s_st.append(ss)

    def stk(lst, name):
        return jnp.stack([d[name] for d in lst], axis=0)

    y_prompt = rmsnorm(hp, final_norm_g)
    y_sample = rmsnorm(hs, final_norm_g)
    return (y_prompt, y_sample,
            stk(p_st, 'C'), stk(p_st, 'n'), stk(p_st, 'm'), stk(p_st, 'conv'),
            stk(p_st, 'kv0'), stk(p_st, 'kv1'), stk(p_st, 'kv2'),
            stk(s_st, 'C'), stk(s_st, 'n'), stk(s_st, 'm'), stk(s_st, 'conv'), stk(s_st, 'chunk_v'),
            stk(s_st, 'kv0'), stk(s_st, 'kv1'), stk(s_st, 'kv2'))
```

```python
import functools

import jax
import jax.numpy as jnp
from jax import lax
from jax.experimental import pallas as pl
from jax.experimental.pallas import tpu as pltpu

F32 = jnp.float32
BF16 = jnp.bfloat16

D_MODEL = 1024
H_A = 4
DH_A = 256
W_A = H_A * DH_A
CONV_W = 4
G_B = 4
CHUNK_B = 128
W_B = 1024
DG_B = W_B // G_B
DIL_PATTERNS = ((128, 1), (512, 4), (2048, 16))
HG_C = 4
DH_C = 64
WG_C = HG_C * DH_C
W_C = len(DIL_PATTERNS) * WG_C
SPAN_C = 128
ROT_DIM = DH_C // 4
ROPE_THETA = 500000.0
EPS = 1e-6
PAST_LEN = 16384

OFF_A = 0
OFF_BU, OFF_BV, OFF_BZ = 5120, 6144, 7168
OFF_CQ, OFF_CK, OFF_CV, OFF_CZ = 8192, 8960, 9728, 10496
OFF_GA, OFF_GB, OFF_GC = 11264, 12288, 13312
N_PACK = 14336
GATE_LANES = 128

MLSTM_L = 256
SAMPLE_ROWS = 8
ATT_TILE = 2048
VMEM_LIMIT = 56 * 1024 * 1024


def _cparams(n_axes):
    return pltpu.CompilerParams(dimension_semantics=("arbitrary",) * n_axes, vmem_limit_bytes=VMEM_LIMIT)


def _dot(a, b):
    return jnp.dot(a.astype(BF16), b.astype(BF16), preferred_element_type=F32)


def _dot_nt(a, b):
    return lax.dot_general(a.astype(BF16), b.astype(BF16), (((1,), (1,)), ((), ())), preferred_element_type=F32)


def _dot_tn(a, b):
    return lax.dot_general(a.astype(BF16), b.astype(BF16), (((0,), (0,)), ((), ())), preferred_element_type=F32)


def _silu(x):
    return x * jax.nn.sigmoid(x)


def _inproj_kernel(x_ref, g_ref, w_ref, wgc_ref, wgr_ref, p_ref, gc_ref, gr_ref, xn_ref):
    @pl.when(pl.program_id(1) == 0)
    def _():
        x = x_ref[...]
        r = lax.rsqrt(jnp.mean(x * x, axis=-1, keepdims=True) + EPS)
        xn = (x * r * g_ref[...]).astype(BF16)
        xn_ref[...] = xn
        gc_ref[...] = jnp.dot(xn, wgc_ref[...], preferred_element_type=F32)
        gr_ref[...] = lax.dot_general(wgr_ref[...], xn, (((1,), (1,)), ((), ())), preferred_element_type=F32)

    p_ref[...] = jnp.dot(xn_ref[...], w_ref[...], preferred_element_type=F32)


def _inproj(x, norm_g, w_main, wg_col, wg_row):
    rows = x.shape[0]
    tm = min(rows, 1024)
    tn = 2048
    return pl.pallas_call(
        _inproj_kernel,
        grid=(rows // tm, N_PACK // tn),
        in_specs=[
            pl.BlockSpec((tm, D_MODEL), lambda i, j: (i, 0)),
            pl.BlockSpec((1, D_MODEL), lambda i, j: (0, 0)),
            pl.BlockSpec((D_MODEL, tn), lambda i, j: (0, j)),
            pl.BlockSpec((D_MODEL, GATE_LANES), lambda i, j: (0, 0)),
            pl.BlockSpec((8, D_MODEL), lambda i, j: (0, 0)),
        ],
        out_specs=[
            pl.BlockSpec((tm, tn), lambda i, j: (i, j)),
            pl.BlockSpec((tm, GATE_LANES), lambda i, j: (i, 0)),
            pl.BlockSpec((8, tm), lambda i, j: (0, i)),
        ],
        out_shape=[
            jax.ShapeDtypeStruct((rows, N_PACK), F32),
            jax.ShapeDtypeStruct((rows, GATE_LANES), F32),
            jax.ShapeDtypeStruct((8, rows), F32),
        ],
        scratch_shapes=[pltpu.VMEM((tm, D_MODEL), BF16)],
        compiler_params=_cparams(2),
        name="inproj",
    )(x, norm_g, w_main, wg_col, wg_row)


def _conv_silu(x, tail, w, b):
    L = x.shape[0]
    ext = jnp.concatenate([tail, x], axis=0)
    y = b + x * w[CONV_W - 1:CONV_W]
    for back in range(1, CONV_W):
        y = y + pltpu.roll(ext, back, 0)[8:8 + L] * w[CONV_W - 1 - back:CONV_W - back]
    return _silu(y)


def _mlstm_kernel(p_ref, gc_ref, gr_ref, brow_ref, bcol_ref, cw_ref, cb_ref, c0_ref, n0_ref, m0_ref, tail0_ref,
                  y_ref, c_out, n_out, m_out, tail_out, c_s, n_s, m_s, tail_s, *, L, t_valid):
    ch = pl.program_id(1)

    @pl.when(ch == 0)
    def _():
        c_s[...] = c0_ref[0]
        n_s[...] = n0_ref[0]
        m_s[...] = m0_ref[0]
        tail_s[...] = tail0_ref[0]

    neg_inf = jnp.float32(-jnp.inf)
    g_col = gc_ref[0] + brow_ref[...]
    g_row = gr_ref[0] + bcol_ref[...]
    ig_col, lf_col = g_col, jax.nn.log_sigmoid(g_col)
    ig_row, lf_row = g_row, jax.nn.log_sigmoid(g_row)
    if t_valid < L:
        vc = lax.broadcasted_iota(jnp.int32, (L, GATE_LANES), 0) < t_valid
        vr = lax.broadcasted_iota(jnp.int32, (8, L), 1) < t_valid
        ig_col, lf_col = jnp.where(vc, ig_col, neg_inf), jnp.where(vc, lf_col, 0.0)
        ig_row, lf_row = jnp.where(vr, ig_row, neg_inf), jnp.where(vr, lf_row, 0.0)
    ti = lax.broadcasted_iota(jnp.int32, (L, L), 0)
    si = lax.broadcasted_iota(jnp.int32, (L, L), 1)
    causal = ti >= si
    b_col = jnp.dot(causal.astype(F32), lf_col, preferred_element_type=F32, precision=lax.Precision.HIGHEST)
    b_row = jnp.dot(lf_row, (ti <= si).astype(F32), preferred_element_type=F32, precision=lax.Precision.HIGHEST)
    last = t_valid - 1

    for h in range(H_A):
        cs = slice(h * DH_A, (h + 1) * DH_A)
        ks = slice(W_A + h * DH_A, W_A + (h + 1) * DH_A)
        q = _conv_silu(p_ref[0, :, cs], tail_s[:, cs], cw_ref[:, cs], cb_ref[:, cs])
        k = _conv_silu(p_ref[0, :, ks], tail_s[:, ks], cw_ref[:, ks], cb_ref[:, ks]) * (DH_A ** -0.5)
        v = p_ref[0, :, 2 * W_A + h * DH_A:2 * W_A + (h + 1) * DH_A]
        o = p_ref[0, :, 3 * W_A + h * DH_A:3 * W_A + (h + 1) * DH_A]
        z = p_ref[0, :, 4 * W_A + h * DH_A:4 * W_A + (h + 1) * DH_A]
        bc = b_col[:, H_A + h:H_A + h + 1]
        igc = ig_col[:, h:h + 1]
        br = b_row[H_A + h:H_A + h + 1, :]
        igr = ig_row[h:h + 1, :]
        m_prev = m_s[h][:, 0:1]
        c_prev = c_s[h]
        n_prev = n_s[h]

        logw = jnp.where(causal, bc - br + igr, neg_inf)
        inter = bc + m_prev
        m_t = jnp.maximum(inter, jnp.max(logw, axis=1, keepdims=True))
        w_intra = jnp.exp(logw - m_t)
        w_inter = jnp.exp(inter - m_t)
        s = w_intra * _dot_nt(q, k)
        num = _dot(s, v) + w_inter * _dot_nt(q, c_prev)
        den = jnp.sum(s, axis=1, keepdims=True) + w_inter * jnp.sum(q * n_prev, axis=1, keepdims=True)
        hh = num / jnp.maximum(jnp.abs(den), jnp.exp(-m_t))
        y_ref[0, :, cs] = hh * jax.nn.sigmoid(o) * _silu(z)

        m_new = m_t[last:last + 1, :]
        b_last = bc[last:last + 1, :]
        decay = jnp.exp(b_last + m_prev - m_new)
        w_s = jnp.exp(b_last - bc + igc - m_new)
        c_s[h] = decay * c_prev + _dot_tn(w_s * v, k)
        n_s[h] = decay * n_prev + jnp.sum(w_s * k, axis=0, keepdims=True)
        m_s[h] = jnp.broadcast_to(m_new, (1, GATE_LANES))

    if t_valid == L:
        tail_s[...] = p_ref[0, L - 8:L, 0:2 * W_A]
    else:
        ext = jnp.concatenate([tail_s[...], p_ref[0, :, 0:2 * W_A]], axis=0)
        tail_s[...] = pltpu.roll(ext, L + 8 - t_valid, 0)[0:8]

    @pl.when(ch == pl.num_programs(1) - 1)
    def _():
        c_out[0] = c_s[...]
        n_out[0] = n_s[...]
        m_out[0] = m_s[...]
        tail_out[0] = tail_s[...]


def _mlstm(p, gcol, grow, bias_row, bias_col, conv_w, conv_b, c0, n0, m0, tail0, *, L, t_valid):
    batch = c0.shape[0]
    rows = p.shape[0]
    nch = rows // (batch * L)
    p3 = p.reshape(batch * nch, L, N_PACK)
    gc3 = gcol.reshape(batch * nch, L, GATE_LANES)
    gr3 = grow.reshape(8, batch * nch, L).transpose(1, 0, 2)
    kern = functools.partial(_mlstm_kernel, L=L, t_valid=t_valid)
    chunk = lambda b, c: (b * nch + c, 0, 0)
    const2 = lambda b, c: (0, 0)
    per_b4 = lambda b, c: (b, 0, 0, 0)
    per_b3 = lambda b, c: (b, 0, 0)
    y, c1, n1, m1, tail1 = pl.pallas_call(
        kern,
        grid=(batch, nch),
        in_specs=[
            pl.BlockSpec((1, L, 5 * W_A), chunk),
            pl.BlockSpec((1, L, GATE_LANES), chunk),
            pl.BlockSpec((1, 8, L), chunk),
            pl.BlockSpec((1, GATE_LANES), const2),
            pl.BlockSpec((8, 1), const2),
            pl.BlockSpec((CONV_W, 2 * W_A), const2),
            pl.BlockSpec((1, 2 * W_A), const2),
            pl.BlockSpec((1, H_A, DH_A, DH_A), per_b4),
            pl.BlockSpec((1, H_A, 1, DH_A), per_b4),
            pl.BlockSpec((1, H_A, 1, GATE_LANES), per_b4),
            pl.BlockSpec((1, 8, 2 * W_A), per_b3),
        ],
        out_specs=[
            pl.BlockSpec((1, L, W_A), chunk),
            pl.BlockSpec((1, H_A, DH_A, DH_A), per_b4),
            pl.BlockSpec((1, H_A, 1, DH_A), per_b4),
            pl.BlockSpec((1, H_A, 1, GATE_LANES), per_b4),
            pl.BlockSpec((1, 8, 2 * W_A), per_b3),
        ],
        out_shape=[
            jax.ShapeDtypeStruct((batch * nch, L, W_A), F32),
            jax.ShapeDtypeStruct((batch, H_A, DH_A, DH_A), F32),
            jax.ShapeDtypeStruct((batch, H_A, 1, DH_A), F32),
            jax.ShapeDtypeStruct((batch, H_A, 1, GATE_LANES), F32),
            jax.ShapeDtypeStruct((batch, 8, 2 * W_A), F32),
        ],
        scratch_shapes=[
            pltpu.VMEM((H_A, DH_A, DH_A), F32),
            pltpu.VMEM((H_A, 1, DH_A), F32),
            pltpu.VMEM((H_A, 1, GATE_LANES), F32),
            pltpu.VMEM((8, 2 * W_A), F32),
        ],
        compiler_params=_cparams(2),
        name="mlstm",
    )(p3, gc3, gr3, bias_row, bias_col, conv_w, conv_b, c0, n0, m0, tail0)
    return y.reshape(rows, W_A), c1, n1[:, :, 0, :], m1[:, :, 0, 0], tail1[:, 8 - (CONV_W - 1):, :]


def _spatial_kernel(u_ref, v_ref, z_ref, lg_ref, lb_ref, w_ref, bc_ref, y_ref, vn_ref, *, nchunks):
    v = v_ref[...]
    mu = jnp.mean(v, axis=-1, keepdims=True)
    var = jnp.mean(jnp.square(v - mu), axis=-1, keepdims=True)
    vn_ref[...] = (v - mu) * lax.rsqrt(var + EPS) * lg_ref[...] + lb_ref[...]
    tri = (lax.broadcasted_iota(jnp.int32, (CHUNK_B, CHUNK_B), 0)
           >= lax.broadcasted_iota(jnp.int32, (CHUNK_B, CHUNK_B), 1))
    for g in range(G_B):
        wg = jnp.where(tri, w_ref[g], 0.0).astype(BF16)
        cs = slice(g * DG_B, (g + 1) * DG_B)
        for c in range(nchunks):
            rs = slice(c * CHUNK_B, (c + 1) * CHUNK_B)
            sp = jnp.dot(wg, vn_ref[rs, cs].astype(BF16), preferred_element_type=F32) + bc_ref[g]
            y_ref[rs, cs] = u_ref[rs, cs] * sp * _silu(z_ref[rs, cs])


def _spatial(p, ln_g, ln_b, w_sp, b_sp_col):
    rows = p.shape[0]
    tr = min(rows, 512)
    kern = functools.partial(_spatial_kernel, nchunks=tr // CHUNK_B)
    col = lambda off: pl.BlockSpec((tr, W_B), lambda i, off=off: (i, off // W_B))
    const2 = lambda i: (0, 0)
    const3 = lambda i: (0, 0, 0)
    return pl.pallas_call(
        kern,
        grid=(rows // tr,),
        in_specs=[
            col(OFF_BU), col(OFF_BV), col(OFF_BZ),
            pl.BlockSpec((1, W_B), const2),
            pl.BlockSpec((1, W_B), const2),
            pl.BlockSpec((G_B, CHUNK_B, CHUNK_B), const3),
            pl.BlockSpec((G_B, CHUNK_B, 1), const3),
        ],
        out_specs=[pl.BlockSpec((tr, W_B), lambda i: (i, 0)), pl.BlockSpec((tr, W_B), lambda i: (i, 0))],
        out_shape=[jax.ShapeDtypeStruct((rows, W_B), F32), jax.ShapeDtypeStruct((rows, W_B), F32)],
        compiler_params=_cparams(1),
        name="spatial",
    )(p, p, p, ln_g, ln_b, w_sp, b_sp_col)


def _rope(x, cos, sin):
    width = x.shape[1]
    lane = lax.broadcasted_iota(jnp.int32, (1, width), 1) % DH_C
    half = ROT_DIM // 2
    partner = jnp.where(lane < half, pltpu.roll(x, width - half, 1), pltpu.roll(x, half, 1))
    return x * cos + partner * sin


def _head_mask(shape_rows):
    lane = lax.broadcasted_iota(jnp.int32, (shape_rows, WG_C), 1) // DH_C
    return [(lane == h).astype(F32) for h in range(HG_C)]


def _dil_attn_kernel(q_ref, k_ref, v_ref, cos_ref, sin_ref, o_ref, lse_ref, kr_ref, qs, kbuf, vbuf, obuf, lbuf, *, d):
    i = pl.program_id(0)
    T = ATT_TILE
    HL = WG_C // 2

    def split(x):
        return x[:, 0:HL], x[:, HL:WG_C]

    def put(buf, lo, x):
        a, b = split(x)
        buf[0, lo:lo + T, :] = a
        buf[1, lo:lo + T, :] = b

    @pl.when(i == 0)
    def _():
        kbuf[:, 0:T, :] = jnp.zeros((2, T, HL), F32)
        vbuf[:, 0:T, :] = jnp.zeros((2, T, HL), F32)

    @pl.when(i > 0)
    def _():
        kbuf[:, 0:T, :] = kbuf[:, T:2 * T, :]
        vbuf[:, 0:T, :] = vbuf[:, T:2 * T, :]

    cos = jnp.concatenate([cos_ref[...], cos_ref[...]], axis=1)
    sin = jnp.concatenate([sin_ref[...], sin_ref[...]], axis=1)
    put(qs, 0, _rope(q_ref[...], cos, sin) * (DH_C ** -0.5))
    kr = _rope(k_ref[...], cos, sin)
    kr_ref[...] = kr
    put(kbuf, T, kr)
    put(vbuf, T, v_ref[...])

    hm = _head_mask(SPAN_C)
    neg_inf = jnp.float32(-jnp.inf)
    ii = lax.broadcasted_iota(jnp.int32, (HG_C * SPAN_C, 2 * SPAN_C), 0) % SPAN_C
    jj = lax.broadcasted_iota(jnp.int32, (HG_C * SPAN_C, 2 * SPAN_C), 1)
    band = (jj >= ii) & (jj <= ii + SPAN_C)
    nblk = T // SPAN_C

    def rows(start, size):
        return pl.ds(start, size) if d == 1 else pl.ds(start, size, stride=d)

    def take(buf, start, size):
        return jnp.concatenate([buf[0, rows(start, size), :], buf[1, rows(start, size), :]], axis=1)

    def body(blk, carry):
        r = blk % d
        c = blk // d
        qstart = r + SPAN_C * d * c
        kstart = T + r + SPAN_C * d * (c - 1)
        qb = take(qs, qstart, SPAN_C)
        kb = take(kbuf, kstart, 2 * SPAN_C)
        vb = take(vbuf, kstart, 2 * SPAN_C)
        q4 = jnp.concatenate([qb * hm[h] for h in range(HG_C)], axis=0)
        s = _dot_nt(q4, kb)
        has_prev = jnp.logical_or(i > 0, c > 0)
        mask = band & jnp.logical_or(jj >= SPAN_C, has_prev)
        s = jnp.where(mask, s, neg_inf)
        mx = jnp.max(s, axis=1, keepdims=True)
        p = jnp.exp(s - mx)
        den = jnp.sum(p, axis=1, keepdims=True)
        o4 = _dot(p, vb) / den
        lse4 = mx + jnp.log(den)
        o = o4[0:SPAN_C] * hm[0]
        lse = lse4[0:SPAN_C] * hm[0]
        for h in range(1, HG_C):
            o = o + o4[h * SPAN_C:(h + 1) * SPAN_C] * hm[h]
            lse = lse + lse4[h * SPAN_C:(h + 1) * SPAN_C] * hm[h]
        for half, (oh, lh) in enumerate(zip(split(o), split(lse))):
            obuf[half, rows(qstart, SPAN_C), :] = oh
            lbuf[half, rows(qstart, SPAN_C), :] = lh
        return carry

    lax.fori_loop(0, nblk, body, 0)
    o_ref[...] = jnp.concatenate([obuf[0], obuf[1]], axis=1)
    lse_ref[...] = jnp.concatenate([lbuf[0], lbuf[1]], axis=1)


def _dil_attn(p, cos_t, sin_t, g, d):
    rows = p.shape[0]
    T = ATT_TILE
    kern = functools.partial(_dil_attn_kernel, d=d)
    col = lambda off: pl.BlockSpec((T, WG_C), lambda i, off=off: (i, off // WG_C + g))
    tab = pl.BlockSpec((T, 2 * DH_C), lambda i: (i, 0))
    out = pl.BlockSpec((T, WG_C), lambda i: (i, 0))
    return pl.pallas_call(
        kern,
        grid=(rows // T,),
        in_specs=[col(OFF_CQ), col(OFF_CK), col(OFF_CV), tab, tab],
        out_specs=[out, out, out],
        out_shape=[jax.ShapeDtypeStruct((rows, WG_C), F32)] * 3,
        scratch_shapes=[pltpu.VMEM((2, T, WG_C // 2), F32), pltpu.VMEM((2, 2 * T, WG_C // 2), F32),
                        pltpu.VMEM((2, 2 * T, WG_C // 2), F32), pltpu.VMEM((2, T, WG_C // 2), F32),
                        pltpu.VMEM((2, T, WG_C // 2), F32)],
        compiler_params=_cparams(1),
        name="dil_attn_d%d" % d,
    )(p, p, p, cos_t, sin_t)


def _sample_attn_kernel(pc0, pc1, pc2, cos_ref, sin_ref, c0_ref, c1_ref, c2_ref,
                        o0, o1, o2, l0, l1, l2, kr_ref, *, n_tok):
    pc = jnp.concatenate([pc0[0], pc1[0], pc2[0]], axis=1)
    reps = W_C // (2 * DH_C)
    cos = jnp.concatenate([cos_ref[...]] * reps, axis=1)
    sin = jnp.concatenate([sin_ref[...]] * reps, axis=1)
    qr = _rope(pc[:, 0:W_C], cos, sin) * (DH_C ** -0.5)
    kr = _rope(pc[:, W_C:2 * W_C], cos, sin)
    v = pc[:, 2 * W_C:3 * W_C]
    kr_ref[0] = kr
    rowi = lax.broadcasted_iota(jnp.int32, (SAMPLE_ROWS, WG_C), 0)
    lanei = lax.broadcasted_iota(jnp.int32, (SAMPLE_ROWS, WG_C), 1) // DH_C
    hm8 = (rowi == lanei).astype(F32)
    jj = lax.broadcasted_iota(jnp.int32, (SAMPLE_ROWS, SPAN_C), 1)
    neg_inf = jnp.float32(-jnp.inf)
    crefs = (c0_ref, c1_ref, c2_ref)
    orefs = (o0, o1, o2)
    lrefs = (l0, l1, l2)
    for g, (_, d) in enumerate(DIL_PATTERNS):
        gs = slice(g * WG_C, (g + 1) * WG_C)
        kn = kr[:, gs]
        vn = v[:, gs]
        orefs[g][0] = jnp.zeros((SAMPLE_ROWS, WG_C), F32)
        lrefs[g][0] = jnp.zeros((SAMPLE_ROWS, WG_C), F32)
        for t in range(n_tok):
            off = 0 if d == 1 else t * 2 * WG_C
            kc = crefs[g][0, :, off:off + WG_C]
            vc = crefs[g][0, :, off + WG_C:off + 2 * WG_C]
            qm = jnp.broadcast_to(qr[t:t + 1, gs], (SAMPLE_ROWS, WG_C)) * hm8
            s = _dot_nt(qm, kc)
            if d == 1:
                s = jnp.where(jj >= t, s, neg_inf)
            new_rows = range(t + 1) if d == 1 else (t,)
            s_new = [jnp.sum(qm * kn[u:u + 1, :], axis=1, keepdims=True) for u in new_rows]
            mx = jnp.max(s, axis=1, keepdims=True)
            for sn in s_new:
                mx = jnp.maximum(mx, sn)
            p = jnp.exp(s - mx)
            den = jnp.sum(p, axis=1, keepdims=True)
            acc = _dot(p, vc)
            for u, sn in zip(new_rows, s_new):
                pn = jnp.exp(sn - mx)
                den = den + pn
                acc = acc + pn * vn[u:u + 1, :]
            o8 = acc / den
            lse8 = mx + jnp.log(den)
            orefs[g][0, t:t + 1, :] = jnp.sum(o8 * hm8, axis=0, keepdims=True)
            lrefs[g][0, t:t + 1, :] = jnp.sum(lse8 * hm8, axis=0, keepdims=True)


def _sample_attn(p, cos_s, sin_s, caches, n_tok):
    batch = p.shape[0] // SAMPLE_ROWS
    p3 = p.reshape(batch, SAMPLE_ROWS, N_PACK)
    views = []
    for (win, d), c in zip(DIL_PATTERNS, caches):
        assert c.shape[1] == SPAN_C * d
        views.append(c.reshape(batch, SPAN_C, d * 2 * WG_C))
    pcol = lambda k: pl.BlockSpec((1, SAMPLE_ROWS, 1024), lambda b, k=k: (b, 0, OFF_CQ // 1024 + k))
    tab = pl.BlockSpec((SAMPLE_ROWS, 2 * DH_C), lambda b: (0, 0))
    cache_spec = lambda d: pl.BlockSpec((1, SPAN_C, min(d, n_tok) * 2 * WG_C), lambda b: (b, 0, 0))
    out = pl.BlockSpec((1, SAMPLE_ROWS, WG_C), lambda b: (b, 0, 0))
    res = pl.pallas_call(
        functools.partial(_sample_attn_kernel, n_tok=n_tok),
        grid=(batch,),
        in_specs=[pcol(0), pcol(1), pcol(2), tab, tab] + [cache_spec(d) for _, d in DIL_PATTERNS],
        out_specs=[out] * 6 + [pl.BlockSpec((1, SAMPLE_ROWS, W_C), lambda b: (b, 0, 0))],
        out_shape=[jax.ShapeDtypeStruct((batch, SAMPLE_ROWS, WG_C), F32)] * 6
        + [jax.ShapeDtypeStruct((batch, SAMPLE_ROWS, W_C), F32)],
        compiler_params=_cparams(1),
        name="sample_attn",
    )(p3, p3, p3, cos_s, sin_s, *views)
    rows = batch * SAMPLE_ROWS
    outs = [r.reshape(rows, WG_C) for r in res[:6]]
    return outs[:3], outs[3:], res[6].reshape(rows, W_C)


def _merge_kernel(x_ref, ya_ref, yb_ref, o0, o1, o2, l0, l1, l2, z0, z1, z2, ga_ref, gb_ref, gc_ref,
                  wb_ref, wo_ref, fg_ref, out_ref, *, final):
    ls = [l0[...], l1[...], l2[...]]
    mx = jnp.maximum(jnp.maximum(ls[0], ls[1]), ls[2])
    es = [jnp.exp(l - mx) for l in ls]
    tot = es[0] + es[1] + es[2]
    pc = None
    for g, (o_ref, z_ref) in enumerate(((o0, z0), (o1, z1), (o2, z2))):
        yc = o_ref[...] * (es[g] / tot) * _silu(z_ref[...])
        t = jnp.dot(yc.astype(BF16), wb_ref[W_A + W_B + g * WG_C:W_A + W_B + (g + 1) * WG_C, :],
                    preferred_element_type=F32)
        pc = t if pc is None else pc + t
    pa = jnp.dot(ya_ref[...].astype(BF16), wb_ref[0:W_A, :], preferred_element_type=F32)
    pb = jnp.dot(yb_ref[...].astype(BF16), wb_ref[W_A:W_A + W_B, :], preferred_element_type=F32)
    merged = (jax.nn.sigmoid(ga_ref[...]) * pa + jax.nn.sigmoid(gb_ref[...]) * pb
              + jax.nn.sigmoid(gc_ref[...]) * pc)
    out = x_ref[...] + jnp.dot(merged.astype(BF16), wo_ref[...], preferred_element_type=F32)
    if final:
        r = lax.rsqrt(jnp.mean(out * out, axis=-1, keepdims=True) + EPS)
        out = out * r * fg_ref[...]
    out_ref[...] = out


def _merge(x, p, ya, yb, os_, ls_, w_branch, w_out, final_g, final):
    rows = x.shape[0]
    tm = min(rows, 512)
    full = lambda w: pl.BlockSpec((tm, w), lambda i: (i, 0))
    pcol = lambda off, w: pl.BlockSpec((tm, w), lambda i, off=off, w=w: (i, off // w))
    const2 = lambda i: (0, 0)
    return pl.pallas_call(
        functools.partial(_merge_kernel, final=final),
        grid=(rows // tm,),
        in_specs=[full(D_MODEL), full(W_A), full(W_B)] + [full(WG_C)] * 6
        + [pcol(OFF_CZ + g * WG_C, WG_C) for g in range(3)]
        + [pcol(OFF_GA, D_MODEL), pcol(OFF_GB, D_MODEL), pcol(OFF_GC, D_MODEL)]
        + [pl.BlockSpec((W_A + W_B + W_C, D_MODEL), const2), pl.BlockSpec((D_MODEL, D_MODEL), const2),
           pl.BlockSpec((1, D_MODEL), const2)],
        out_specs=full(D_MODEL),
        out_shape=jax.ShapeDtypeStruct((rows, D_MODEL), F32),
        compiler_params=_cparams(1),
        name="merge",
    )(x, ya, yb, *os_, *ls_, p, p, p, p, p, p, w_branch, w_out, final_g)


def _rope_tables(pos):
    half = ROT_DIM // 2
    inv = ROPE_THETA ** (-jnp.arange(half, dtype=F32) / half)
    ang = pos.astype(F32)[:, None] * inv[None, :]
    cos, sin = jnp.cos(ang), jnp.sin(ang)
    n = pos.shape[0]
    rest = DH_C - ROT_DIM
    cos_h = jnp.concatenate([cos, cos, jnp.ones((n, rest), F32)], axis=1)
    sin_h = jnp.concatenate([-sin, sin, jnp.zeros((n, rest), F32)], axis=1)
    return jnp.concatenate([cos_h, cos_h], axis=1), jnp.concatenate([sin_h, sin_h], axis=1)


def _stack_kv(k_rows, v_rows):
    b, t, _ = k_rows.shape
    return jnp.stack([k_rows.reshape(b, t, HG_C, DH_C), v_rows.reshape(b, t, HG_C, DH_C)], axis=2)


def kernel(x_prompt, x_sample, state_C, state_n, state_m, state_conv, cache_kv_w128, cache_kv_w512, cache_kv_w2048,
           norm_g, w_in, b_igate, b_fgate, conv_w, conv_b, ln_v_g, ln_v_b, w_spatial, b_spatial, w_branch, w_out,
           final_norm_g):
    depth = w_in.shape[0]
    bp, seq, _ = x_prompt.shape
    bs, n_tok, _ = x_sample.shape
    assert bp == 1 and seq % ATT_TILE == 0 and n_tok <= SAMPLE_ROWS // 2 and n_tok >= CONV_W - 1
    caches = (cache_kv_w128, cache_kv_w512, cache_kv_w2048)
    pad_tok = SAMPLE_ROWS - n_tok
    rows_s = bs * SAMPLE_ROWS

    w_main = jnp.concatenate([w_in[:, :, :4 * W_A], w_in[:, :, 4 * W_A + 2 * H_A:]], axis=2).astype(BF16)
    w_gate = w_in[:, :, 4 * W_A:4 * W_A + 2 * H_A]
    wg_col = jnp.pad(w_gate, ((0, 0), (0, 0), (0, GATE_LANES - 2 * H_A))).astype(BF16)
    wg_row = jnp.swapaxes(w_gate, 1, 2).astype(BF16)
    gate_bias = jnp.concatenate([b_igate, b_fgate], axis=1)
    bias_row = jnp.pad(gate_bias, ((0, 0), (0, GATE_LANES - 2 * H_A)))[:, None, :]
    bias_col = gate_bias[:, :, None]
    wb16 = w_branch.astype(BF16)
    wo16 = w_out.astype(BF16)
    b_sp_col = b_spatial[..., None]
    seqs_per_chunk = CHUNK_B // SAMPLE_ROWS
    w8 = jnp.pad(w_spatial[:, :, :n_tok, :n_tok], ((0, 0), (0, 0), (0, pad_tok), (0, pad_tok)))
    w_sp_s = jnp.einsum('ab,lgij->lgaibj', jnp.eye(seqs_per_chunk, dtype=F32), w8).reshape(
        depth, G_B, CHUNK_B, CHUNK_B)
    b_sp_s = jnp.tile(jnp.pad(b_spatial[:, :, :n_tok], ((0, 0), (0, 0), (0, pad_tok))),
                      (1, 1, seqs_per_chunk))[..., None]

    cos_p, sin_p = _rope_tables(jnp.arange(seq))
    cos_s, sin_s = _rope_tables(PAST_LEN + jnp.arange(SAMPLE_ROWS))

    hp = x_prompt.reshape(seq, D_MODEL)
    hs = jnp.pad(x_sample, ((0, 0), (0, pad_tok), (0, 0))).reshape(rows_s, D_MODEL)
    fg = final_norm_g[None, :]
    zeros_c = jnp.zeros((1, H_A, DH_A, DH_A), F32)
    zeros_n = jnp.zeros((1, H_A, 1, DH_A), F32)
    zeros_m = jnp.zeros((1, H_A, 1, GATE_LANES), F32)
    zeros_tail = jnp.zeros((1, 8, 2 * W_A), F32)

    p_out = {k: [] for k in ('C', 'n', 'm', 'conv', 'kv0', 'kv1', 'kv2')}
    s_out = {k: [] for k in ('C', 'n', 'm', 'conv', 'chunk_v', 'kv0', 'kv1', 'kv2')}
    for l in range(depth):
        final = l == depth - 1
        ng = norm_g[l][None, :]
        cw, cb = conv_w[l], conv_b[l][None, :]
        lg, lb = ln_v_g[l][None, :], ln_v_b[l][None, :]

        pp, gcol, grow = _inproj(hp, ng, w_main[l], wg_col[l], wg_row[l])
        ya, c1, n1, m1, conv1 = _mlstm(pp, gcol, grow, bias_row[l], bias_col[l], cw, cb,
                                       zeros_c, zeros_n, zeros_m, zeros_tail, L=MLSTM_L, t_valid=MLSTM_L)
        yb, _ = _spatial(pp, lg, lb, w_spatial[l], b_sp_col[l])
        os_, ls_, krs = [], [], []
        for g, (win, d) in enumerate(DIL_PATTERNS):
            o_g, l_g, kr_g = _dil_attn(pp, cos_p, sin_p, g, d)
            os_.append(o_g)
            ls_.append(l_g)
            keep = min(win, seq)
            v_g = pp[seq - keep:, OFF_CV + g * WG_C:OFF_CV + (g + 1) * WG_C]
            p_out['kv%d' % g].append(_stack_kv(kr_g[None, seq - keep:], v_g[None]))
        hp = _merge(hp, pp, ya, yb, os_, ls_, wb16[l], wo16[l], fg, final)
        p_out['C'].append(c1)
        p_out['n'].append(n1)
        p_out['m'].append(m1)
        p_out['conv'].append(conv1)

        ps, gcol, grow = _inproj(hs, ng, w_main[l], wg_col[l], wg_row[l])
        tail0 = jnp.pad(state_conv[l], ((0, 0), (8 - (CONV_W - 1), 0), (0, 0)))
        m0 = jnp.broadcast_to(state_m[l][:, :, None, None], (bs, H_A, 1, GATE_LANES))
        ya, c1, n1, m1, conv1 = _mlstm(ps, gcol, grow, bias_row[l], bias_col[l], cw, cb,
                                       state_C[l], state_n[l][:, :, None, :], m0, tail0,
                                       L=SAMPLE_ROWS, t_valid=n_tok)
        yb, vn = _spatial(ps, lg, lb, w_sp_s[l], b_sp_s[l])
        os_, ls_, kr = _sample_attn(ps, cos_s, sin_s, [c[l] for c in caches], n_tok)
        hs = _merge(hs, ps, ya, yb, os_, ls_, wb16[l], wo16[l], fg, final)
        s_out['C'].append(c1)
        s_out['n'].append(n1)
        s_out['m'].append(m1)
        s_out['conv'].append(conv1)
        s_out['chunk_v'].append(vn.reshape(bs, SAMPLE_ROWS, W_B)[:, :n_tok])
        kr3 = kr.reshape(bs, SAMPLE_ROWS, W_C)[:, :n_tok]
        v3 = ps[:, OFF_CV:OFF_CV + W_C].reshape(bs, SAMPLE_ROWS, W_C)[:, :n_tok]
        for g in range(len(DIL_PATTERNS)):
            gs = slice(g * WG_C, (g + 1) * WG_C)
            s_out['kv%d' % g].append(_stack_kv(kr3[:, :, gs], v3[:, :, gs]))

    stk = lambda d, k: jnp.stack(d[k], axis=0)
    y_prompt = hp.reshape(bp, seq, D_MODEL)
    y_sample = hs.reshape(bs, SAMPLE_ROWS, D_MODEL)[:, :n_tok]
    return (y_prompt, y_sample,
            stk(p_out, 'C'), stk(p_out, 'n'), stk(p_out, 'm'), stk(p_out, 'conv'),
            stk(p_out, 'kv0'), stk(p_out, 'kv1'), stk(p_out, 'kv2'),
            stk(s_out, 'C'), stk(s_out, 'n'), stk(s_out, 'm'), stk(s_out, 'conv'), stk(s_out, 'chunk_v'),
            stk(s_out, 'kv0'), stk(s_out, 'kv1'), stk(s_out, 'kv2'))
```

```python
import functools

import jax
import jax.numpy as jnp
from jax import lax
from jax.experimental import pallas as pl
from jax.experimental.pallas import tpu as pltpu

F32 = jnp.float32
BF16 = jnp.bfloat16

D_MODEL = 1024
H_A = 4
DH_A = 256
W_A = H_A * DH_A
CONV_W = 4
G_B = 4
CHUNK_B = 128
W_B = 1024
DG_B = W_B // G_B
DIL_PATTERNS = ((128, 1), (512, 4), (2048, 16))
HG_C = 4
DH_C = 64
WG_C = HG_C * DH_C
W_C = len(DIL_PATTERNS) * WG_C
SPAN_C = 128
ROT_DIM = DH_C // 4
ROPE_THETA = 500000.0
EPS = 1e-6
PAST_LEN = 16384

OFF_A = 0
OFF_BU, OFF_BV, OFF_BZ = 5120, 6144, 7168
OFF_CQ, OFF_CK, OFF_CV, OFF_CZ = 8192, 8960, 9728, 10496
OFF_GA, OFF_GB, OFF_GC = 11264, 12288, 13312
N_PACK = 14336
GATE_LANES = 128

INPROJ_TM = 2048
INPROJ_TN = 1024
MLSTM_L = 256
SAMPLE_ROWS = 8
ATT_TILE = 2048
VMEM_LIMIT = 56 * 1024 * 1024


def _cparams(n_axes):
    return pltpu.CompilerParams(dimension_semantics=("arbitrary",) * n_axes, vmem_limit_bytes=VMEM_LIMIT)


def _dot(a, b):
    return jnp.dot(a.astype(BF16), b.astype(BF16), preferred_element_type=F32)


def _dot_nt(a, b):
    return lax.dot_general(a.astype(BF16), b.astype(BF16), (((1,), (1,)), ((), ())), preferred_element_type=F32)


def _dot_tn(a, b):
    return lax.dot_general(a.astype(BF16), b.astype(BF16), (((0,), (0,)), ((), ())), preferred_element_type=F32)


def _silu(x):
    return x * jax.nn.sigmoid(x)


def _inproj_kernel(x_ref, g_ref, w_ref, wgc_ref, wgr_ref, p_ref, gc_ref, gr_ref, xn_ref):
    @pl.when(pl.program_id(1) == 0)
    def _():
        x = x_ref[...]
        r = lax.rsqrt(jnp.mean(x * x, axis=-1, keepdims=True) + EPS)
        xn = (x * r * g_ref[...]).astype(BF16)
        xn_ref[...] = xn
        gc_ref[...] = jnp.dot(xn, wgc_ref[...], preferred_element_type=F32)
        gr_ref[...] = lax.dot_general(wgr_ref[...], xn, (((1,), (1,)), ((), ())), preferred_element_type=F32)

    p_ref[...] = jnp.dot(xn_ref[...], w_ref[...], preferred_element_type=F32)


def _inproj(x, norm_g, w_main, wg_col, wg_row, l):
    rows = x.shape[0]
    tm = min(rows, INPROJ_TM)
    tn = INPROJ_TN
    return pl.pallas_call(
        _inproj_kernel,
        grid=(rows // tm, N_PACK // tn),
        in_specs=[
            pl.BlockSpec((tm, D_MODEL), lambda i, j: (i, 0)),
            pl.BlockSpec((None, 1, D_MODEL), lambda i, j: (l, 0, 0)),
            pl.BlockSpec((None, D_MODEL, tn), lambda i, j: (l, 0, j)),
            pl.BlockSpec((None, D_MODEL, GATE_LANES), lambda i, j: (l, 0, 0)),
            pl.BlockSpec((None, 8, D_MODEL), lambda i, j: (l, 0, 0)),
        ],
        out_specs=[
            pl.BlockSpec((tm, tn), lambda i, j: (i, j)),
            pl.BlockSpec((tm, GATE_LANES), lambda i, j: (i, 0)),
            pl.BlockSpec((None, 8, tm), lambda i, j: (i, 0, 0)),
        ],
        out_shape=[
            jax.ShapeDtypeStruct((rows, N_PACK), F32),
            jax.ShapeDtypeStruct((rows, GATE_LANES), F32),
            jax.ShapeDtypeStruct((rows // tm, 8, tm), F32),
        ],
        scratch_shapes=[pltpu.VMEM((tm, D_MODEL), BF16)],
        compiler_params=_cparams(2),
        name="inproj",
    )(x, norm_g, w_main, wg_col, wg_row)


def _pack_kernel(a_ref, b_ref, g_ref, w_ref, gc_ref, gr_ref, *, tn, first_shifted):
    j = pl.program_id(1)

    @pl.when(j < first_shifted)
    def _():
        w_ref[...] = a_ref[...].astype(BF16)

    @pl.when(j >= first_shifted)
    def _():
        full = jnp.concatenate([a_ref[...], b_ref[...]], axis=1)
        width = tn + GATE_LANES
        w_ref[...] = pltpu.roll(full, width - 2 * H_A, 1)[:, 0:tn].astype(BF16)

    @pl.when(j == 0)
    def _():
        lane = lax.broadcasted_iota(jnp.int32, (D_MODEL, GATE_LANES), 1)
        g = jnp.where(lane < 2 * H_A, g_ref[...], 0.0)
        gc_ref[...] = g.astype(BF16)
        gr_ref[...] = jnp.transpose(g)[0:8, :].astype(BF16)


def _pack_w_in(w_in):
    depth = w_in.shape[0]
    tn = 2048
    gate_off = 4 * W_A
    assert gate_off % tn == 0 and w_in.shape[2] == N_PACK + 2 * H_A
    kern = functools.partial(_pack_kernel, tn=tn, first_shifted=gate_off // tn)
    return pl.pallas_call(
        kern,
        grid=(depth, N_PACK // tn),
        in_specs=[
            pl.BlockSpec((None, D_MODEL, tn), lambda l, j: (l, 0, j)),
            pl.BlockSpec((None, D_MODEL, GATE_LANES), lambda l, j: (l, 0, (j + 1) * (tn // GATE_LANES))),
            pl.BlockSpec((None, D_MODEL, GATE_LANES), lambda l, j: (l, 0, gate_off // GATE_LANES)),
        ],
        out_specs=[
            pl.BlockSpec((None, D_MODEL, tn), lambda l, j: (l, 0, j)),
            pl.BlockSpec((None, D_MODEL, GATE_LANES), lambda l, j: (l, 0, 0)),
            pl.BlockSpec((None, 8, D_MODEL), lambda l, j: (l, 0, 0)),
        ],
        out_shape=[
            jax.ShapeDtypeStruct((depth, D_MODEL, N_PACK), BF16),
            jax.ShapeDtypeStruct((depth, D_MODEL, GATE_LANES), BF16),
            jax.ShapeDtypeStruct((depth, 8, D_MODEL), BF16),
        ],
        compiler_params=_cparams(2),
        name="pack_w_in",
    )(w_in, w_in, w_in)


def _conv_silu(x, tail, w, b):
    L = x.shape[0]
    ext = jnp.concatenate([tail, x], axis=0)
    y = b + x * w[CONV_W - 1:CONV_W]
    for back in range(1, CONV_W):
        y = y + pltpu.roll(ext, back, 0)[8:8 + L] * w[CONV_W - 1 - back:CONV_W - back]
    return _silu(y)


def _mlstm_kernel(p_ref, gc_ref, gr_ref, brow_ref, bcol_ref, cw_ref, cb_ref, c0_ref, n0_ref, m0_ref, tail0_ref,
                  y_ref, c_out, n_out, m_out, tail_out, c_s, n_s, m_s, tail_s, *, L, t_valid):
    ch = pl.program_id(1)

    @pl.when(ch == 0)
    def _():
        c_s[...] = c0_ref[0]
        n_s[...] = n0_ref[0]
        m_s[...] = m0_ref[0]
        tail_s[...] = tail0_ref[0]

    neg_inf = jnp.float32(-jnp.inf)
    g_col = gc_ref[0] + brow_ref[...]
    g_row = gr_ref[0] + bcol_ref[...]
    ig_col, lf_col = g_col, jax.nn.log_sigmoid(g_col)
    ig_row, lf_row = g_row, jax.nn.log_sigmoid(g_row)
    if t_valid < L:
        vc = lax.broadcasted_iota(jnp.int32, (L, GATE_LANES), 0) < t_valid
        vr = lax.broadcasted_iota(jnp.int32, (8, L), 1) < t_valid
        ig_col, lf_col = jnp.where(vc, ig_col, neg_inf), jnp.where(vc, lf_col, 0.0)
        ig_row, lf_row = jnp.where(vr, ig_row, neg_inf), jnp.where(vr, lf_row, 0.0)
    ti = lax.broadcasted_iota(jnp.int32, (L, L), 0)
    si = lax.broadcasted_iota(jnp.int32, (L, L), 1)
    causal = ti >= si
    b_col = jnp.dot(causal.astype(F32), lf_col, preferred_element_type=F32, precision=lax.Precision.HIGHEST)
    b_row = jnp.dot(lf_row, (ti <= si).astype(F32), preferred_element_type=F32, precision=lax.Precision.HIGHEST)
    last = t_valid - 1

    for h in range(H_A):
        cs = slice(h * DH_A, (h + 1) * DH_A)
        ks = slice(W_A + h * DH_A, W_A + (h + 1) * DH_A)
        q = _conv_silu(p_ref[0, :, cs], tail_s[:, cs], cw_ref[:, cs], cb_ref[:, cs])
        k = _conv_silu(p_ref[0, :, ks], tail_s[:, ks], cw_ref[:, ks], cb_ref[:, ks]) * (DH_A ** -0.5)
        v = p_ref[0, :, 2 * W_A + h * DH_A:2 * W_A + (h + 1) * DH_A]
        o = p_ref[0, :, 3 * W_A + h * DH_A:3 * W_A + (h + 1) * DH_A]
        z = p_ref[0, :, 4 * W_A + h * DH_A:4 * W_A + (h + 1) * DH_A]
        bc = b_col[:, H_A + h:H_A + h + 1]
        igc = ig_col[:, h:h + 1]
        br = b_row[H_A + h:H_A + h + 1, :]
        igr = ig_row[h:h + 1, :]
        m_prev = m_s[h][:, 0:1]
        c_prev = c_s[h]
        n_prev = n_s[h]

        logw = jnp.where(causal, bc - br + igr, neg_inf)
        inter = bc + m_prev
        m_t = jnp.maximum(inter, jnp.max(logw, axis=1, keepdims=True))
        w_intra = jnp.exp(logw - m_t)
        w_inter = jnp.exp(inter - m_t)
        s = w_intra * _dot_nt(q, k)
        num = _dot(s, v) + w_inter * _dot_nt(q, c_prev)
        den = jnp.sum(s, axis=1, keepdims=True) + w_inter * jnp.sum(q * n_prev, axis=1, keepdims=True)
        hh = num / jnp.maximum(jnp.abs(den), jnp.exp(-m_t))
        y_ref[0, :, cs] = hh * jax.nn.sigmoid(o) * _silu(z)

        m_new = m_t[last:last + 1, :]
        b_last = bc[last:last + 1, :]
        decay = jnp.exp(b_last + m_prev - m_new)
        w_s = jnp.exp(b_last - bc + igc - m_new)
        c_s[h] = decay * c_prev + _dot_tn(w_s * v, k)
        n_s[h] = decay * n_prev + jnp.sum(w_s * k, axis=0, keepdims=True)
        m_s[h] = jnp.broadcast_to(m_new, (1, GATE_LANES))

    if t_valid == L:
        tail_s[...] = p_ref[0, L - 8:L, 0:2 * W_A]
    else:
        ext = jnp.concatenate([tail_s[...], p_ref[0, :, 0:2 * W_A]], axis=0)
        tail_s[...] = pltpu.roll(ext, L + 8 - t_valid, 0)[0:8]

    @pl.when(ch == pl.num_programs(1) - 1)
    def _():
        c_out[0] = c_s[...]
        n_out[0] = n_s[...]
        m_out[0] = m_s[...]
        tail_out[0] = tail_s[...]


def _mlstm(p, gcol, grow, bias_row, bias_col, conv_w, conv_b, c0, n0, m0, tail0, l, ls, *, L, t_valid):
    batch = c0.shape[1]
    rows = p.shape[0]
    nch = rows // (batch * L)
    p3 = p.reshape(batch * nch, L, N_PACK)
    gc3 = gcol.reshape(batch * nch, L, GATE_LANES)
    tm = grow.shape[2]
    if L % GATE_LANES == 0:
        per = tm // L
        gr3, gr_index = grow, (lambda b, c: ((b * nch + c) // per, 0, (b * nch + c) % per))
    else:
        gr3 = grow.transpose(1, 0, 2).reshape(8, batch * nch, L).transpose(1, 0, 2)
        gr_index = lambda b, c: (b * nch + c, 0, 0)
    kern = functools.partial(_mlstm_kernel, L=L, t_valid=t_valid)
    chunk = lambda b, c: (b * nch + c, 0, 0)
    layer3 = lambda b, c: (l, 0, 0)
    per_b4 = lambda b, c: (b, 0, 0, 0)
    per_b3 = lambda b, c: (b, 0, 0)
    st5 = lambda b, c: (ls, b, 0, 0, 0)
    st4 = lambda b, c: (ls, b, 0, 0)
    y, c1, n1, m1, tail1 = pl.pallas_call(
        kern,
        grid=(batch, nch),
        in_specs=[
            pl.BlockSpec((1, L, 5 * W_A), chunk),
            pl.BlockSpec((1, L, GATE_LANES), chunk),
            pl.BlockSpec((1, 8, L), gr_index),
            pl.BlockSpec((None, 1, GATE_LANES), layer3),
            pl.BlockSpec((None, 8, 1), layer3),
            pl.BlockSpec((None, CONV_W, 2 * W_A), layer3),
            pl.BlockSpec((None, 1, 2 * W_A), layer3),
            pl.BlockSpec((None, 1, H_A, DH_A, DH_A), st5),
            pl.BlockSpec((None, 1, H_A, 1, DH_A), st5),
            pl.BlockSpec((None, 1, H_A, 1, GATE_LANES), st5),
            pl.BlockSpec((None, 1, 8, 2 * W_A), st4),
        ],
        out_specs=[
            pl.BlockSpec((1, L, W_A), chunk),
            pl.BlockSpec((1, H_A, DH_A, DH_A), per_b4),
            pl.BlockSpec((1, H_A, 1, DH_A), per_b4),
            pl.BlockSpec((1, H_A, 1, GATE_LANES), per_b4),
            pl.BlockSpec((1, 8, 2 * W_A), per_b3),
        ],
        out_shape=[
            jax.ShapeDtypeStruct((batch * nch, L, W_A), F32),
            jax.ShapeDtypeStruct((batch, H_A, DH_A, DH_A), F32),
            jax.ShapeDtypeStruct((batch, H_A, 1, DH_A), F32),
            jax.ShapeDtypeStruct((batch, H_A, 1, GATE_LANES), F32),
            jax.ShapeDtypeStruct((batch, 8, 2 * W_A), F32),
        ],
        scratch_shapes=[
            pltpu.VMEM((H_A, DH_A, DH_A), F32),
            pltpu.VMEM((H_A, 1, DH_A), F32),
            pltpu.VMEM((H_A, 1, GATE_LANES), F32),
            pltpu.VMEM((8, 2 * W_A), F32),
        ],
        compiler_params=_cparams(2),
        name="mlstm",
    )(p3, gc3, gr3, bias_row, bias_col, conv_w, conv_b, c0, n0, m0, tail0)
    return y.reshape(rows, W_A), c1, n1[:, :, 0, :], m1[:, :, 0, 0], tail1[:, 8 - (CONV_W - 1):, :]


def _spatial_kernel(u_ref, v_ref, z_ref, lg_ref, lb_ref, w_ref, bc_ref, y_ref, vn_ref, *, nchunks):
    v = v_ref[...]
    mu = jnp.mean(v, axis=-1, keepdims=True)
    var = jnp.mean(jnp.square(v - mu), axis=-1, keepdims=True)
    vn_ref[...] = (v - mu) * lax.rsqrt(var + EPS) * lg_ref[...] + lb_ref[...]
    tri = (lax.broadcasted_iota(jnp.int32, (CHUNK_B, CHUNK_B), 0)
           >= lax.broadcasted_iota(jnp.int32, (CHUNK_B, CHUNK_B), 1))
    for g in range(G_B):
        wg = jnp.where(tri, w_ref[g], 0.0).astype(BF16)
        cs = slice(g * DG_B, (g + 1) * DG_B)
        for c in range(nchunks):
            rs = slice(c * CHUNK_B, (c + 1) * CHUNK_B)
            sp = jnp.dot(wg, vn_ref[rs, cs].astype(BF16), preferred_element_type=F32) + bc_ref[g]
            y_ref[rs, cs] = u_ref[rs, cs] * sp * _silu(z_ref[rs, cs])


def _spatial(p, ln_g, ln_b, w_sp, b_sp_col, l):
    rows = p.shape[0]
    tr = min(rows, 512)
    kern = functools.partial(_spatial_kernel, nchunks=tr // CHUNK_B)
    col = lambda off: pl.BlockSpec((tr, W_B), lambda i, off=off: (i, off // W_B))
    return pl.pallas_call(
        kern,
        grid=(rows // tr,),
        in_specs=[
            col(OFF_BU), col(OFF_BV), col(OFF_BZ),
            pl.BlockSpec((None, 1, W_B), lambda i: (l, 0, 0)),
            pl.BlockSpec((None, 1, W_B), lambda i: (l, 0, 0)),
            pl.BlockSpec((None, G_B, CHUNK_B, CHUNK_B), lambda i: (l, 0, 0, 0)),
            pl.BlockSpec((None, G_B, CHUNK_B, 1), lambda i: (l, 0, 0, 0)),
        ],
        out_specs=[pl.BlockSpec((tr, W_B), lambda i: (i, 0)), pl.BlockSpec((tr, W_B), lambda i: (i, 0))],
        out_shape=[jax.ShapeDtypeStruct((rows, W_B), F32), jax.ShapeDtypeStruct((rows, W_B), F32)],
        compiler_params=_cparams(1),
        name="spatial",
    )(p, p, p, ln_g, ln_b, w_sp, b_sp_col)


def _rope(x, cos, sin):
    width = x.shape[1]
    lane = lax.broadcasted_iota(jnp.int32, (1, width), 1) % DH_C
    half = ROT_DIM // 2
    partner = jnp.where(lane < half, pltpu.roll(x, width - half, 1), pltpu.roll(x, half, 1))
    return x * cos + partner * sin


def _head_mask(shape_rows):
    lane = lax.broadcasted_iota(jnp.int32, (shape_rows, WG_C), 1) // DH_C
    return [(lane == h).astype(F32) for h in range(HG_C)]


def _dil_attn_kernel(q_ref, k_ref, v_ref, cos_ref, sin_ref, o_ref, lse_ref, kr_ref, qs, kbuf, vbuf, obuf, lbuf, *, d):
    i = pl.program_id(0)
    T = ATT_TILE
    HL = WG_C // 2

    def split(x):
        return x[:, 0:HL], x[:, HL:WG_C]

    def put(buf, lo, x):
        a, b = split(x)
        buf[0, lo:lo + T, :] = a
        buf[1, lo:lo + T, :] = b

    @pl.when(i == 0)
    def _():
        kbuf[:, 0:T, :] = jnp.zeros((2, T, HL), F32)
        vbuf[:, 0:T, :] = jnp.zeros((2, T, HL), F32)

    @pl.when(i > 0)
    def _():
        kbuf[:, 0:T, :] = kbuf[:, T:2 * T, :]
        vbuf[:, 0:T, :] = vbuf[:, T:2 * T, :]

    cos = jnp.concatenate([cos_ref[...], cos_ref[...]], axis=1)
    sin = jnp.concatenate([sin_ref[...], sin_ref[...]], axis=1)
    put(qs, 0, _rope(q_ref[...], cos, sin) * (DH_C ** -0.5))
    kr = _rope(k_ref[...], cos, sin)
    kr_ref[...] = kr
    put(kbuf, T, kr)
    put(vbuf, T, v_ref[...])

    hm = _head_mask(SPAN_C)
    neg_inf = jnp.float32(-jnp.inf)
    ii = lax.broadcasted_iota(jnp.int32, (HG_C * SPAN_C, 2 * SPAN_C), 0) % SPAN_C
    jj = lax.broadcasted_iota(jnp.int32, (HG_C * SPAN_C, 2 * SPAN_C), 1)
    band = (jj >= ii) & (jj <= ii + SPAN_C)
    nblk = T // SPAN_C

    def rows(start, size):
        return pl.ds(start, size) if d == 1 else pl.ds(start, size, stride=d)

    def take(buf, start, size):
        return jnp.concatenate([buf[0, rows(start, size), :], buf[1, rows(start, size), :]], axis=1)

    def body(blk, carry):
        r = blk % d
        c = blk // d
        qstart = r + SPAN_C * d * c
        kstart = T + r + SPAN_C * d * (c - 1)
        qb = take(qs, qstart, SPAN_C)
        kb = take(kbuf, kstart, 2 * SPAN_C)
        vb = take(vbuf, kstart, 2 * SPAN_C)
        q4 = jnp.concatenate([qb * hm[h] for h in range(HG_C)], axis=0)
        s = _dot_nt(q4, kb)
        has_prev = jnp.logical_or(i > 0, c > 0)
        mask = band & jnp.logical_or(jj >= SPAN_C, has_prev)
        s = jnp.where(mask, s, neg_inf)
        mx = jnp.max(s, axis=1, keepdims=True)
        p = jnp.exp(s - mx)
        den = jnp.sum(p, axis=1, keepdims=True)
        o4 = _dot(p, vb) / den
        lse4 = mx + jnp.log(den)
        o = o4[0:SPAN_C] * hm[0]
        lse = lse4[0:SPAN_C] * hm[0]
        for h in range(1, HG_C):
            o = o + o4[h * SPAN_C:(h + 1) * SPAN_C] * hm[h]
            lse = lse + lse4[h * SPAN_C:(h + 1) * SPAN_C] * hm[h]
        for half, (oh, lh) in enumerate(zip(split(o), split(lse))):
            obuf[half, rows(qstart, SPAN_C), :] = oh
            lbuf[half, rows(qstart, SPAN_C), :] = lh
        return carry

    lax.fori_loop(0, nblk, body, 0)
    o_ref[...] = jnp.concatenate([obuf[0], obuf[1]], axis=1)
    lse_ref[...] = jnp.concatenate([lbuf[0], lbuf[1]], axis=1)


def _dil_attn(p, cos_t, sin_t, g, d):
    rows = p.shape[0]
    T = ATT_TILE
    kern = functools.partial(_dil_attn_kernel, d=d)
    col = lambda off: pl.BlockSpec((T, WG_C), lambda i, off=off: (i, off // WG_C + g))
    tab = pl.BlockSpec((T, 2 * DH_C), lambda i: (i, 0))
    out = pl.BlockSpec((T, WG_C), lambda i: (i, 0))
    return pl.pallas_call(
        kern,
        grid=(rows // T,),
        in_specs=[col(OFF_CQ), col(OFF_CK), col(OFF_CV), tab, tab],
        out_specs=[out, out, out],
        out_shape=[jax.ShapeDtypeStruct((rows, WG_C), F32)] * 3,
        scratch_shapes=[pltpu.VMEM((2, T, WG_C // 2), F32), pltpu.VMEM((2, 2 * T, WG_C // 2), F32),
                        pltpu.VMEM((2, 2 * T, WG_C // 2), F32), pltpu.VMEM((2, T, WG_C // 2), F32),
                        pltpu.VMEM((2, T, WG_C // 2), F32)],
        compiler_params=_cparams(1),
        name="dil_attn_d%d" % d,
    )(p, p, p, cos_t, sin_t)


def _sample_attn_kernel(pc0, pc1, pc2, cos_ref, sin_ref, c0_ref, c1_ref, c2_ref,
                        o0, o1, o2, l0, l1, l2, kr_ref, *, n_tok):
    pc = jnp.concatenate([pc0[0], pc1[0], pc2[0]], axis=1)
    reps = W_C // (2 * DH_C)
    cos = jnp.concatenate([cos_ref[...]] * reps, axis=1)
    sin = jnp.concatenate([sin_ref[...]] * reps, axis=1)
    qr = _rope(pc[:, 0:W_C], cos, sin) * (DH_C ** -0.5)
    kr = _rope(pc[:, W_C:2 * W_C], cos, sin)
    v = pc[:, 2 * W_C:3 * W_C]
    kr_ref[0] = kr
    rowi = lax.broadcasted_iota(jnp.int32, (SAMPLE_ROWS, WG_C), 0)
    lanei = lax.broadcasted_iota(jnp.int32, (SAMPLE_ROWS, WG_C), 1) // DH_C
    hm8 = (rowi == lanei).astype(F32)
    jj = lax.broadcasted_iota(jnp.int32, (SAMPLE_ROWS, SPAN_C), 1)
    neg_inf = jnp.float32(-jnp.inf)
    crefs = (c0_ref, c1_ref, c2_ref)
    orefs = (o0, o1, o2)
    lrefs = (l0, l1, l2)
    for g, (_, d) in enumerate(DIL_PATTERNS):
        gs = slice(g * WG_C, (g + 1) * WG_C)
        kn = kr[:, gs]
        vn = v[:, gs]
        orefs[g][0] = jnp.zeros((SAMPLE_ROWS, WG_C), F32)
        lrefs[g][0] = jnp.zeros((SAMPLE_ROWS, WG_C), F32)
        for t in range(n_tok):
            off = 0 if d == 1 else t * 2 * WG_C
            kc = crefs[g][0, :, off:off + WG_C]
            vc = crefs[g][0, :, off + WG_C:off + 2 * WG_C]
            qm = jnp.broadcast_to(qr[t:t + 1, gs], (SAMPLE_ROWS, WG_C)) * hm8
            s = _dot_nt(qm, kc)
            if d == 1:
                s = jnp.where(jj >= t, s, neg_inf)
            new_rows = range(t + 1) if d == 1 else (t,)
            s_new = [jnp.sum(qm * kn[u:u + 1, :], axis=1, keepdims=True) for u in new_rows]
            mx = jnp.max(s, axis=1, keepdims=True)
            for sn in s_new:
                mx = jnp.maximum(mx, sn)
            p = jnp.exp(s - mx)
            den = jnp.sum(p, axis=1, keepdims=True)
            acc = _dot(p, vc)
            for u, sn in zip(new_rows, s_new):
                pn = jnp.exp(sn - mx)
                den = den + pn
                acc = acc + pn * vn[u:u + 1, :]
            o8 = acc / den
            lse8 = mx + jnp.log(den)
            orefs[g][0, t:t + 1, :] = jnp.sum(o8 * hm8, axis=0, keepdims=True)
            lrefs[g][0, t:t + 1, :] = jnp.sum(lse8 * hm8, axis=0, keepdims=True)


def _cache_views(caches, n_tok):
    views = []
    for (win, d), c in zip(DIL_PATTERNS, caches):
        depth, batch, n_buf = c.shape[:3]
        assert n_buf == SPAN_C * d
        keep = min(d, n_tok)
        v = c.reshape(depth, batch, SPAN_C, d, 2 * WG_C)[:, :, :, :keep]
        views.append(v.reshape(depth, batch, SPAN_C, keep * 2 * WG_C))
    return views


def _sample_attn(p, cos_s, sin_s, views, l, n_tok):
    batch = p.shape[0] // SAMPLE_ROWS
    p3 = p.reshape(batch, SAMPLE_ROWS, N_PACK)
    pcol = lambda k: pl.BlockSpec((1, SAMPLE_ROWS, 1024), lambda b, k=k: (b, 0, OFF_CQ // 1024 + k))
    tab = pl.BlockSpec((SAMPLE_ROWS, 2 * DH_C), lambda b: (0, 0))
    cache_spec = lambda d: pl.BlockSpec((None, 1, SPAN_C, min(d, n_tok) * 2 * WG_C), lambda b: (l, b, 0, 0))
    out = pl.BlockSpec((1, SAMPLE_ROWS, WG_C), lambda b: (b, 0, 0))
    res = pl.pallas_call(
        functools.partial(_sample_attn_kernel, n_tok=n_tok),
        grid=(batch,),
        in_specs=[pcol(0), pcol(1), pcol(2), tab, tab] + [cache_spec(d) for _, d in DIL_PATTERNS],
        out_specs=[out] * 6 + [pl.BlockSpec((1, SAMPLE_ROWS, W_C), lambda b: (b, 0, 0))],
        out_shape=[jax.ShapeDtypeStruct((batch, SAMPLE_ROWS, WG_C), F32)] * 6
        + [jax.ShapeDtypeStruct((batch, SAMPLE_ROWS, W_C), F32)],
        compiler_params=_cparams(1),
        name="sample_attn",
    )(p3, p3, p3, cos_s, sin_s, *views)
    rows = batch * SAMPLE_ROWS
    outs = [r.reshape(rows, WG_C) for r in res[:6]]
    return outs[:3], outs[3:], res[6].reshape(rows, W_C)


def _merge_kernel(x_ref, ya_ref, yb_ref, o0, o1, o2, l0, l1, l2, z0, z1, z2, ga_ref, gb_ref, gc_ref,
                  wb_ref, wo_ref, fg_ref, out_ref, *, final):
    ls = [l0[...], l1[...], l2[...]]
    mx = jnp.maximum(jnp.maximum(ls[0], ls[1]), ls[2])
    es = [jnp.exp(l - mx) for l in ls]
    tot = es[0] + es[1] + es[2]
    pc = None
    for g, (o_ref, z_ref) in enumerate(((o0, z0), (o1, z1), (o2, z2))):
        yc = o_ref[...] * (es[g] / tot) * _silu(z_ref[...])
        t = jnp.dot(yc.astype(BF16), wb_ref[W_A + W_B + g * WG_C:W_A + W_B + (g + 1) * WG_C, :],
                    preferred_element_type=F32)
        pc = t if pc is None else pc + t
    pa = jnp.dot(ya_ref[...].astype(BF16), wb_ref[0:W_A, :], preferred_element_type=F32)
    pb = jnp.dot(yb_ref[...].astype(BF16), wb_ref[W_A:W_A + W_B, :], preferred_element_type=F32)
    merged = (jax.nn.sigmoid(ga_ref[...]) * pa + jax.nn.sigmoid(gb_ref[...]) * pb
              + jax.nn.sigmoid(gc_ref[...]) * pc)
    out = x_ref[...] + jnp.dot(merged.astype(BF16), wo_ref[...], preferred_element_type=F32)
    if final:
        r = lax.rsqrt(jnp.mean(out * out, axis=-1, keepdims=True) + EPS)
        out = out * r * fg_ref[...]
    out_ref[...] = out


def _merge(x, p, ya, yb, os_, ls_, w_branch, w_out, final_g, l, final):
    rows = x.shape[0]
    tm = min(rows, 512)
    full = lambda w: pl.BlockSpec((tm, w), lambda i: (i, 0))
    pcol = lambda off, w: pl.BlockSpec((tm, w), lambda i, off=off, w=w: (i, off // w))
    const2 = lambda i: (0, 0)
    layer3 = lambda i: (l, 0, 0)
    return pl.pallas_call(
        functools.partial(_merge_kernel, final=final),
        grid=(rows // tm,),
        in_specs=[full(D_MODEL), full(W_A), full(W_B)] + [full(WG_C)] * 6
        + [pcol(OFF_CZ + g * WG_C, WG_C) for g in range(3)]
        + [pcol(OFF_GA, D_MODEL), pcol(OFF_GB, D_MODEL), pcol(OFF_GC, D_MODEL)]
        + [pl.BlockSpec((None, W_A + W_B + W_C, D_MODEL), layer3), pl.BlockSpec((None, D_MODEL, D_MODEL), layer3),
           pl.BlockSpec((1, D_MODEL), const2)],
        out_specs=full(D_MODEL),
        out_shape=jax.ShapeDtypeStruct((rows, D_MODEL), F32),
        compiler_params=_cparams(1),
        name="merge",
    )(x, ya, yb, *os_, *ls_, p, p, p, p, p, p, w_branch, w_out, final_g)


def _rope_tables(pos):
    half = ROT_DIM // 2
    inv = ROPE_THETA ** (-jnp.arange(half, dtype=F32) / half)
    ang = pos.astype(F32)[:, None] * inv[None, :]
    cos, sin = jnp.cos(ang), jnp.sin(ang)
    n = pos.shape[0]
    rest = DH_C - ROT_DIM
    cos_h = jnp.concatenate([cos, cos, jnp.ones((n, rest), F32)], axis=1)
    sin_h = jnp.concatenate([-sin, sin, jnp.zeros((n, rest), F32)], axis=1)
    return jnp.concatenate([cos_h, cos_h], axis=1), jnp.concatenate([sin_h, sin_h], axis=1)


def _stack_kv(k_rows, v_rows):
    b, t, _ = k_rows.shape
    return jnp.stack([k_rows.reshape(b, t, HG_C, DH_C), v_rows.reshape(b, t, HG_C, DH_C)], axis=2)


def kernel(x_prompt, x_sample, state_C, state_n, state_m, state_conv, cache_kv_w128, cache_kv_w512, cache_kv_w2048,
           norm_g, w_in, b_igate, b_fgate, conv_w, conv_b, ln_v_g, ln_v_b, w_spatial, b_spatial, w_branch, w_out,
           final_norm_g):
    depth = w_in.shape[0]
    bp, seq, _ = x_prompt.shape
    bs, n_tok, _ = x_sample.shape
    assert bp == 1 and seq % ATT_TILE == 0 and n_tok <= SAMPLE_ROWS // 2 and n_tok >= CONV_W - 1
    caches = (cache_kv_w128, cache_kv_w512, cache_kv_w2048)
    pad_tok = SAMPLE_ROWS - n_tok
    rows_s = bs * SAMPLE_ROWS

    w_main, wg_col, wg_row = _pack_w_in(w_in)
    gate_bias = jnp.concatenate([b_igate, b_fgate], axis=1)
    bias_row = jnp.pad(gate_bias, ((0, 0), (0, GATE_LANES - 2 * H_A)))[:, None, :]
    bias_col = gate_bias[:, :, None]
    wb16 = w_branch.astype(BF16)
    wo16 = w_out.astype(BF16)
    b_sp_col = b_spatial[..., None]
    norm_g3, conv_b3 = norm_g[:, None, :], conv_b[:, None, :]
    ln_g3, ln_b3 = ln_v_g[:, None, :], ln_v_b[:, None, :]
    seqs_per_chunk = CHUNK_B // SAMPLE_ROWS
    w8 = jnp.pad(w_spatial[:, :, :n_tok, :n_tok], ((0, 0), (0, 0), (0, pad_tok), (0, pad_tok)))
    w_sp_s = jnp.einsum('ab,lgij->lgaibj', jnp.eye(seqs_per_chunk, dtype=F32), w8).reshape(
        depth, G_B, CHUNK_B, CHUNK_B)
    b_sp_s = jnp.tile(jnp.pad(b_spatial[:, :, :n_tok], ((0, 0), (0, 0), (0, pad_tok))),
                      (1, 1, seqs_per_chunk))[..., None]

    cos_p, sin_p = _rope_tables(jnp.arange(seq))
    cos_s, sin_s = _rope_tables(PAST_LEN + jnp.arange(SAMPLE_ROWS))

    hp = x_prompt.reshape(seq, D_MODEL)
    hs = jnp.pad(x_sample, ((0, 0), (0, pad_tok), (0, 0))).reshape(rows_s, D_MODEL)
    fg = final_norm_g[None, :]
    zeros_c = jnp.zeros((1, 1, H_A, DH_A, DH_A), F32)
    zeros_n = jnp.zeros((1, 1, H_A, 1, DH_A), F32)
    zeros_m = jnp.zeros((1, 1, H_A, 1, GATE_LANES), F32)
    zeros_tail = jnp.zeros((1, 1, 8, 2 * W_A), F32)
    n0_s = state_n[:, :, :, None, :]
    m0_s = jnp.broadcast_to(state_m[:, :, :, None, None], (depth, bs, H_A, 1, GATE_LANES))
    tail0_s = jnp.pad(state_conv, ((0, 0), (0, 0), (8 - (CONV_W - 1), 0), (0, 0)))
    views = _cache_views(caches, n_tok)

    p_out = {k: [] for k in ('C', 'n', 'm', 'conv', 'kv0', 'kv1', 'kv2')}
    s_out = {k: [] for k in ('C', 'n', 'm', 'conv', 'chunk_v', 'kv0', 'kv1', 'kv2')}
    for l in range(depth):
        final = l == depth - 1

        pp, gcol, grow = _inproj(hp, norm_g3, w_main, wg_col, wg_row, l)
        ya, c1, n1, m1, conv1 = _mlstm(pp, gcol, grow, bias_row, bias_col, conv_w, conv_b3,
                                       zeros_c, zeros_n, zeros_m, zeros_tail, l, 0, L=MLSTM_L, t_valid=MLSTM_L)
        yb, _ = _spatial(pp, ln_g3, ln_b3, w_spatial, b_sp_col, l)
        os_, ls_, krs = [], [], []
        for g, (win, d) in enumerate(DIL_PATTERNS):
            o_g, l_g, kr_g = _dil_attn(pp, cos_p, sin_p, g, d)
            os_.append(o_g)
            ls_.append(l_g)
            keep = min(win, seq)
            v_g = pp[seq - keep:, OFF_CV + g * WG_C:OFF_CV + (g + 1) * WG_C]
            p_out['kv%d' % g].append(_stack_kv(kr_g[None, seq - keep:], v_g[None]))
        hp = _merge(hp, pp, ya, yb, os_, ls_, wb16, wo16, fg, l, final)
        p_out['C'].append(c1)
        p_out['n'].append(n1)
        p_out['m'].append(m1)
        p_out['conv'].append(conv1)

        ps, gcol, grow = _inproj(hs, norm_g3, w_main, wg_col, wg_row, l)
        ya, c1, n1, m1, conv1 = _mlstm(ps, gcol, grow, bias_row, bias_col, conv_w, conv_b3,
                                       state_C, n0_s, m0_s, tail0_s, l, l, L=SAMPLE_ROWS, t_valid=n_tok)
        yb, vn = _spatial(ps, ln_g3, ln_b3, w_sp_s, b_sp_s, l)
        os_, ls_, kr = _sample_attn(ps, cos_s, sin_s, views, l, n_tok)
        hs = _merge(hs, ps, ya, yb, os_, ls_, wb16, wo16, fg, l, final)
        s_out['C'].append(c1)
        s_out['n'].append(n1)
        s_out['m'].append(m1)
        s_out['conv'].append(conv1)
        s_out['chunk_v'].append(vn.reshape(bs, SAMPLE_ROWS, W_B)[:, :n_tok])
        kr3 = kr.reshape(bs, SAMPLE_ROWS, W_C)[:, :n_tok]
        v3 = ps[:, OFF_CV:OFF_CV + W_C].reshape(bs, SAMPLE_ROWS, W_C)[:, :n_tok]
        for g in range(len(DIL_PATTERNS)):
            gs = slice(g * WG_C, (g + 1) * WG_C)
            s_out['kv%d' % g].append(_stack_kv(kr3[:, :, gs], v3[:, :, gs]))

    stk = lambda d, k: jnp.stack(d[k], axis=0)
    y_prompt = hp.reshape(bp, seq, D_MODEL)
    y_sample = hs.reshape(bs, SAMPLE_ROWS, D_MODEL)[:, :n_tok]
    return (y_prompt, y_sample,
            stk(p_out, 'C'), stk(p_out, 'n'), stk(p_out, 'm'), stk(p_out, 'conv'),
            stk(p_out, 'kv0'), stk(p_out, 'kv1'), stk(p_out, 'kv2'),
            stk(s_out, 'C'), stk(s_out, 'n'), stk(s_out, 'm'), stk(s_out, 'conv'), stk(s_out, 'chunk_v'),
            stk(s_out, 'kv0'), stk(s_out, 'kv1'), stk(s_out, 'kv2'))
```

```python
import functools

import jax
import jax.numpy as jnp
from jax import lax
from jax.experimental import pallas as pl
from jax.experimental.pallas import tpu as pltpu

F32 = jnp.float32
BF16 = jnp.bfloat16

D_MODEL = 1024
H_A = 4
DH_A = 256
W_A = H_A * DH_A
CONV_W = 4
G_B = 4
CHUNK_B = 128
W_B = 1024
DG_B = W_B // G_B
DIL_PATTERNS = ((128, 1), (512, 4), (2048, 16))
HG_C = 4
DH_C = 64
WG_C = HG_C * DH_C
W_C = len(DIL_PATTERNS) * WG_C
SPAN_C = 128
ROT_DIM = DH_C // 4
ROPE_THETA = 500000.0
EPS = 1e-6
PAST_LEN = 16384

OFF_A = 0
OFF_BU, OFF_BV, OFF_BZ = 5120, 6144, 7168
OFF_CQ, OFF_CK, OFF_CV, OFF_CZ = 8192, 8960, 9728, 10496
OFF_GA, OFF_GB, OFF_GC = 11264, 12288, 13312
N_PACK = 14336
GATE_LANES = 128

INPROJ_TM = 2048
INPROJ_TN = 1024
MLSTM_L = 256
SAMPLE_ROWS = 8
ATT_TILE = 2048
VMEM_LIMIT = 56 * 1024 * 1024


def _cparams(n_axes):
    return pltpu.CompilerParams(dimension_semantics=("arbitrary",) * n_axes, vmem_limit_bytes=VMEM_LIMIT)


def _dot(a, b):
    return jnp.dot(a.astype(BF16), b.astype(BF16), preferred_element_type=F32)


def _dot_nt(a, b):
    return lax.dot_general(a.astype(BF16), b.astype(BF16), (((1,), (1,)), ((), ())), preferred_element_type=F32)


def _dot_tn(a, b):
    return lax.dot_general(a.astype(BF16), b.astype(BF16), (((0,), (0,)), ((), ())), preferred_element_type=F32)


def _silu(x):
    return x * jax.nn.sigmoid(x)


def _inproj_kernel(x_ref, g_ref, w_ref, wg_ref, p_ref, gc_ref, gr_ref, xn_ref):
    nt = (((1,), (1,)), ((), ()))

    @pl.when(pl.program_id(1) == 0)
    def _():
        x = x_ref[...]
        r = lax.rsqrt(jnp.mean(x * x, axis=-1, keepdims=True) + EPS)
        xn = (x * r * g_ref[...]).astype(BF16)
        xn_ref[...] = xn
        wg = wg_ref[...].astype(BF16)
        gc_ref[...] = lax.dot_general(xn, wg, nt, preferred_element_type=F32)
        gr_ref[...] = lax.dot_general(wg[0:8, :], xn, nt, preferred_element_type=F32)

    p_ref[...] = lax.dot_general(xn_ref[...], w_ref[...], nt, preferred_element_type=F32)


def _inproj(x, norm_g, w_main, w_in_t, l):
    rows = x.shape[0]
    gate_blk = 4 * W_A // GATE_LANES
    tm = min(rows, INPROJ_TM)
    tn = INPROJ_TN
    return pl.pallas_call(
        _inproj_kernel,
        grid=(rows // tm, N_PACK // tn),
        in_specs=[
            pl.BlockSpec((tm, D_MODEL), lambda i, j: (i, 0)),
            pl.BlockSpec((None, 1, D_MODEL), lambda i, j: (l, 0, 0)),
            pl.BlockSpec((None, tn, D_MODEL), lambda i, j: (l, j, 0)),
            pl.BlockSpec((None, GATE_LANES, D_MODEL), lambda i, j: (l, gate_blk, 0)),
        ],
        out_specs=[
            pl.BlockSpec((tm, tn), lambda i, j: (i, j)),
            pl.BlockSpec((tm, GATE_LANES), lambda i, j: (i, 0)),
            pl.BlockSpec((None, 8, tm), lambda i, j: (i, 0, 0)),
        ],
        out_shape=[
            jax.ShapeDtypeStruct((rows, N_PACK), F32),
            jax.ShapeDtypeStruct((rows, GATE_LANES), F32),
            jax.ShapeDtypeStruct((rows // tm, 8, tm), F32),
        ],
        scratch_shapes=[pltpu.VMEM((tm, D_MODEL), BF16)],
        compiler_params=_cparams(2),
        name="inproj",
    )(x, norm_g, w_main, w_in_t)


def _pack_kernel(a_ref, b_ref, w_ref, *, first_shifted):
    j = pl.program_id(1)

    @pl.when(j < first_shifted)
    def _():
        w_ref[...] = a_ref[...].astype(BF16)

    @pl.when(j >= first_shifted)
    def _():
        w_ref[...] = jnp.concatenate([a_ref[2 * H_A:, :], b_ref[...]], axis=0).astype(BF16)


def _pack_w_in(w_in_t):
    depth = w_in_t.shape[0]
    tn = 2048
    gate_off = 4 * W_A
    assert gate_off % tn == 0 and w_in_t.shape[1] == N_PACK + 2 * H_A and 2 * H_A == 8
    kern = functools.partial(_pack_kernel, first_shifted=gate_off // tn)
    return pl.pallas_call(
        kern,
        grid=(depth, N_PACK // tn),
        in_specs=[
            pl.BlockSpec((None, tn, D_MODEL), lambda l, j: (l, j, 0)),
            pl.BlockSpec((None, 8, D_MODEL), lambda l, j: (l, (j + 1) * (tn // 8), 0)),
        ],
        out_specs=pl.BlockSpec((None, tn, D_MODEL), lambda l, j: (l, j, 0)),
        out_shape=jax.ShapeDtypeStruct((depth, N_PACK, D_MODEL), BF16),
        compiler_params=_cparams(2),
        name="pack_w_in",
    )(w_in_t, w_in_t)


def _conv_silu(x, tail, w, b):
    L = x.shape[0]
    ext = jnp.concatenate([tail, x], axis=0)
    y = b + x * w[CONV_W - 1:CONV_W]
    for back in range(1, CONV_W):
        y = y + pltpu.roll(ext, back, 0)[8:8 + L] * w[CONV_W - 1 - back:CONV_W - back]
    return _silu(y)


def _mlstm_kernel(p_ref, gc_ref, gr_ref, brow_ref, bcol_ref, cw_ref, cb_ref, c0_ref, n0_ref, m0_ref, tail0_ref,
                  y_ref, c_out, n_out, m_out, tail_out, c_s, n_s, m_s, tail_s, *, L, t_valid):
    ch = pl.program_id(1)

    @pl.when(ch == 0)
    def _():
        c_s[...] = c0_ref[0]
        n_s[...] = n0_ref[0]
        m_s[...] = m0_ref[0]
        tail_s[...] = tail0_ref[0]

    neg_inf = jnp.float32(-jnp.inf)
    g_col = gc_ref[0] + brow_ref[...]
    g_row = gr_ref[0] + bcol_ref[...]
    ig_col, lf_col = g_col, jax.nn.log_sigmoid(g_col)
    ig_row, lf_row = g_row, jax.nn.log_sigmoid(g_row)
    if t_valid < L:
        vc = lax.broadcasted_iota(jnp.int32, (L, GATE_LANES), 0) < t_valid
        vr = lax.broadcasted_iota(jnp.int32, (8, L), 1) < t_valid
        ig_col, lf_col = jnp.where(vc, ig_col, neg_inf), jnp.where(vc, lf_col, 0.0)
        ig_row, lf_row = jnp.where(vr, ig_row, neg_inf), jnp.where(vr, lf_row, 0.0)
    ti = lax.broadcasted_iota(jnp.int32, (L, L), 0)
    si = lax.broadcasted_iota(jnp.int32, (L, L), 1)
    causal = ti >= si
    b_col = jnp.dot(causal.astype(F32), lf_col, preferred_element_type=F32, precision=lax.Precision.HIGHEST)
    b_row = jnp.dot(lf_row, (ti <= si).astype(F32), preferred_element_type=F32, precision=lax.Precision.HIGHEST)
    last = t_valid - 1

    for h in range(H_A):
        cs = slice(h * DH_A, (h + 1) * DH_A)
        ks = slice(W_A + h * DH_A, W_A + (h + 1) * DH_A)
        q = _conv_silu(p_ref[0, :, cs], tail_s[:, cs], cw_ref[:, cs], cb_ref[:, cs])
        k = _conv_silu(p_ref[0, :, ks], tail_s[:, ks], cw_ref[:, ks], cb_ref[:, ks]) * (DH_A ** -0.5)
        v = p_ref[0, :, 2 * W_A + h * DH_A:2 * W_A + (h + 1) * DH_A]
        o = p_ref[0, :, 3 * W_A + h * DH_A:3 * W_A + (h + 1) * DH_A]
        z = p_ref[0, :, 4 * W_A + h * DH_A:4 * W_A + (h + 1) * DH_A]
        bc = b_col[:, H_A + h:H_A + h + 1]
        igc = ig_col[:, h:h + 1]
        br = b_row[H_A + h:H_A + h + 1, :]
        igr = ig_row[h:h + 1, :]
        m_prev = m_s[h][:, 0:1]
        c_prev = c_s[h]
        n_prev = n_s[h]

        logw = jnp.where(causal, bc - br + igr, neg_inf)
        inter = bc + m_prev
        m_t = jnp.maximum(inter, jnp.max(logw, axis=1, keepdims=True))
        w_intra = jnp.exp(logw - m_t)
        w_inter = jnp.exp(inter - m_t)
        s = w_intra * _dot_nt(q, k)
        num = _dot(s, v) + w_inter * _dot_nt(q, c_prev)
        den = jnp.sum(s, axis=1, keepdims=True) + w_inter * jnp.sum(q * n_prev, axis=1, keepdims=True)
        hh = num / jnp.maximum(jnp.abs(den), jnp.exp(-m_t))
        y_ref[0, :, cs] = hh * jax.nn.sigmoid(o) * _silu(z)

        m_new = m_t[last:last + 1, :]
        b_last = bc[last:last + 1, :]
        decay = jnp.exp(b_last + m_prev - m_new)
        w_s = jnp.exp(b_last - bc + igc - m_new)
        c_s[h] = decay * c_prev + _dot_tn(w_s * v, k)
        n_s[h] = decay * n_prev + jnp.sum(w_s * k, axis=0, keepdims=True)
        m_s[h] = jnp.broadcast_to(m_new, (1, GATE_LANES))

    if t_valid == L:
        tail_s[...] = p_ref[0, L - 8:L, 0:2 * W_A]
    else:
        ext = jnp.concatenate([tail_s[...], p_ref[0, :, 0:2 * W_A]], axis=0)
        tail_s[...] = pltpu.roll(ext, L + 8 - t_valid, 0)[0:8]

    @pl.when(ch == pl.num_programs(1) - 1)
    def _():
        c_out[0] = c_s[...]
        n_out[0] = n_s[...]
        m_out[0] = m_s[...]
        tail_out[0] = tail_s[...]


def _mlstm(p, gcol, grow, bias_row, bias_col, conv_w, conv_b, c0, n0, m0, tail0, l, ls, *, L, t_valid):
    batch = c0.shape[1]
    rows = p.shape[0]
    nch = rows // (batch * L)
    p3 = p.reshape(batch * nch, L, N_PACK)
    gc3 = gcol.reshape(batch * nch, L, GATE_LANES)
    tm = grow.shape[2]
    if L % GATE_LANES == 0:
        per = tm // L
        gr3, gr_index = grow, (lambda b, c: ((b * nch + c) // per, 0, (b * nch + c) % per))
    else:
        gr3 = grow.transpose(1, 0, 2).reshape(8, batch * nch, L).transpose(1, 0, 2)
        gr_index = lambda b, c: (b * nch + c, 0, 0)
    kern = functools.partial(_mlstm_kernel, L=L, t_valid=t_valid)
    chunk = lambda b, c: (b * nch + c, 0, 0)
    layer3 = lambda b, c: (l, 0, 0)
    per_b4 = lambda b, c: (b, 0, 0, 0)
    per_b3 = lambda b, c: (b, 0, 0)
    st5 = lambda b, c: (ls, b, 0, 0, 0)
    st4 = lambda b, c: (ls, b, 0, 0)
    y, c1, n1, m1, tail1 = pl.pallas_call(
        kern,
        grid=(batch, nch),
        in_specs=[
            pl.BlockSpec((1, L, 5 * W_A), chunk),
            pl.BlockSpec((1, L, GATE_LANES), chunk),
            pl.BlockSpec((1, 8, L), gr_index),
            pl.BlockSpec((None, 1, GATE_LANES), layer3),
            pl.BlockSpec((None, 8, 1), layer3),
            pl.BlockSpec((None, CONV_W, 2 * W_A), layer3),
            pl.BlockSpec((None, 1, 2 * W_A), layer3),
            pl.BlockSpec((None, 1, H_A, DH_A, DH_A), st5),
            pl.BlockSpec((None, 1, H_A, 1, DH_A), st5),
            pl.BlockSpec((None, 1, H_A, 1, GATE_LANES), st5),
            pl.BlockSpec((None, 1, 8, 2 * W_A), st4),
        ],
        out_specs=[
            pl.BlockSpec((1, L, W_A), chunk),
            pl.BlockSpec((1, H_A, DH_A, DH_A), per_b4),
            pl.BlockSpec((1, H_A, 1, DH_A), per_b4),
            pl.BlockSpec((1, H_A, 1, GATE_LANES), per_b4),
            pl.BlockSpec((1, 8, 2 * W_A), per_b3),
        ],
        out_shape=[
            jax.ShapeDtypeStruct((batch * nch, L, W_A), F32),
            jax.ShapeDtypeStruct((batch, H_A, DH_A, DH_A), F32),
            jax.ShapeDtypeStruct((batch, H_A, 1, DH_A), F32),
            jax.ShapeDtypeStruct((batch, H_A, 1, GATE_LANES), F32),
            jax.ShapeDtypeStruct((batch, 8, 2 * W_A), F32),
        ],
        scratch_shapes=[
            pltpu.VMEM((H_A, DH_A, DH_A), F32),
            pltpu.VMEM((H_A, 1, DH_A), F32),
            pltpu.VMEM((H_A, 1, GATE_LANES), F32),
            pltpu.VMEM((8, 2 * W_A), F32),
        ],
        compiler_params=_cparams(2),
        name="mlstm",
    )(p3, gc3, gr3, bias_row, bias_col, conv_w, conv_b, c0, n0, m0, tail0)
    return y.reshape(rows, W_A), c1, n1[:, :, 0, :], m1[:, :, 0, 0], tail1[:, 8 - (CONV_W - 1):, :]


def _spatial_kernel(u_ref, v_ref, z_ref, lg_ref, lb_ref, w_ref, bc_ref, y_ref, vn_ref, *, nchunks):
    v = v_ref[...]
    mu = jnp.mean(v, axis=-1, keepdims=True)
    var = jnp.mean(jnp.square(v - mu), axis=-1, keepdims=True)
    vn_ref[...] = (v - mu) * lax.rsqrt(var + EPS) * lg_ref[...] + lb_ref[...]
    tri = (lax.broadcasted_iota(jnp.int32, (CHUNK_B, CHUNK_B), 0)
           >= lax.broadcasted_iota(jnp.int32, (CHUNK_B, CHUNK_B), 1))
    for g in range(G_B):
        wg = jnp.where(tri, w_ref[g], 0.0).astype(BF16)
        cs = slice(g * DG_B, (g + 1) * DG_B)
        for c in range(nchunks):
            rs = slice(c * CHUNK_B, (c + 1) * CHUNK_B)
            sp = jnp.dot(wg, vn_ref[rs, cs].astype(BF16), preferred_element_type=F32) + bc_ref[g]
            y_ref[rs, cs] = u_ref[rs, cs] * sp * _silu(z_ref[rs, cs])


def _spatial(p, ln_g, ln_b, w_sp, b_sp_col, l):
    rows = p.shape[0]
    tr = min(rows, 512)
    kern = functools.partial(_spatial_kernel, nchunks=tr // CHUNK_B)
    col = lambda off: pl.BlockSpec((tr, W_B), lambda i, off=off: (i, off // W_B))
    return pl.pallas_call(
        kern,
        grid=(rows // tr,),
        in_specs=[
            col(OFF_BU), col(OFF_BV), col(OFF_BZ),
            pl.BlockSpec((None, 1, W_B), lambda i: (l, 0, 0)),
            pl.BlockSpec((None, 1, W_B), lambda i: (l, 0, 0)),
            pl.BlockSpec((None, G_B, CHUNK_B, CHUNK_B), lambda i: (l, 0, 0, 0)),
            pl.BlockSpec((None, G_B, CHUNK_B, 1), lambda i: (l, 0, 0, 0)),
        ],
        out_specs=[pl.BlockSpec((tr, W_B), lambda i: (i, 0)), pl.BlockSpec((tr, W_B), lambda i: (i, 0))],
        out_shape=[jax.ShapeDtypeStruct((rows, W_B), F32), jax.ShapeDtypeStruct((rows, W_B), F32)],
        compiler_params=_cparams(1),
        name="spatial",
    )(p, p, p, ln_g, ln_b, w_sp, b_sp_col)


def _rope(x, cos, sin):
    width = x.shape[1]
    lane = lax.broadcasted_iota(jnp.int32, (1, width), 1) % DH_C
    half = ROT_DIM // 2
    partner = jnp.where(lane < half, pltpu.roll(x, width - half, 1), pltpu.roll(x, half, 1))
    return x * cos + partner * sin


def _head_mask(shape_rows):
    lane = lax.broadcasted_iota(jnp.int32, (shape_rows, WG_C), 1) // DH_C
    return [(lane == h).astype(F32) for h in range(HG_C)]


def _dil_attn_kernel(q_ref, k_ref, v_ref, cos_ref, sin_ref, o_ref, lse_ref, kr_ref, qs, kbuf, vbuf, obuf, lbuf, *, d):
    i = pl.program_id(0)
    T = ATT_TILE
    HL = WG_C // 2

    def split(x):
        return x[:, 0:HL], x[:, HL:WG_C]

    def put(buf, lo, x):
        a, b = split(x)
        buf[0, lo:lo + T, :] = a
        buf[1, lo:lo + T, :] = b

    @pl.when(i == 0)
    def _():
        kbuf[:, 0:T, :] = jnp.zeros((2, T, HL), F32)
        vbuf[:, 0:T, :] = jnp.zeros((2, T, HL), F32)

    @pl.when(i > 0)
    def _():
        kbuf[:, 0:T, :] = kbuf[:, T:2 * T, :]
        vbuf[:, 0:T, :] = vbuf[:, T:2 * T, :]

    cos = jnp.concatenate([cos_ref[...], cos_ref[...]], axis=1)
    sin = jnp.concatenate([sin_ref[...], sin_ref[...]], axis=1)
    put(qs, 0, _rope(q_ref[...], cos, sin) * (DH_C ** -0.5))
    kr = _rope(k_ref[...], cos, sin)
    kr_ref[...] = kr
    put(kbuf, T, kr)
    put(vbuf, T, v_ref[...])

    hm = _head_mask(SPAN_C)
    neg_inf = jnp.float32(-jnp.inf)
    ii = lax.broadcasted_iota(jnp.int32, (HG_C * SPAN_C, 2 * SPAN_C), 0) % SPAN_C
    jj = lax.broadcasted_iota(jnp.int32, (HG_C * SPAN_C, 2 * SPAN_C), 1)
    band = (jj >= ii) & (jj <= ii + SPAN_C)
    nblk = T // SPAN_C

    def rows(start, size):
        return pl.ds(start, size) if d == 1 else pl.ds(start, size, stride=d)

    def take(buf, start, size):
        return jnp.concatenate([buf[0, rows(start, size), :], buf[1, rows(start, size), :]], axis=1)

    def body(blk, carry):
        r = blk % d
        c = blk // d
        qstart = r + SPAN_C * d * c
        kstart = T + r + SPAN_C * d * (c - 1)
        qb = take(qs, qstart, SPAN_C)
        kb = take(kbuf, kstart, 2 * SPAN_C)
        vb = take(vbuf, kstart, 2 * SPAN_C)
        q4 = jnp.concatenate([qb * hm[h] for h in range(HG_C)], axis=0)
        s = _dot_nt(q4, kb)
        has_prev = jnp.logical_or(i > 0, c > 0)
        mask = band & jnp.logical_or(jj >= SPAN_C, has_prev)
        s = jnp.where(mask, s, neg_inf)
        mx = jnp.max(s, axis=1, keepdims=True)
        p = jnp.exp(s - mx)
        den = jnp.sum(p, axis=1, keepdims=True)
        o4 = _dot(p, vb) / den
        lse4 = mx + jnp.log(den)
        o = o4[0:SPAN_C] * hm[0]
        lse = lse4[0:SPAN_C] * hm[0]
        for h in range(1, HG_C):
            o = o + o4[h * SPAN_C:(h + 1) * SPAN_C] * hm[h]
            lse = lse + lse4[h * SPAN_C:(h + 1) * SPAN_C] * hm[h]
        for half, (oh, lh) in enumerate(zip(split(o), split(lse))):
            obuf[half, rows(qstart, SPAN_C), :] = oh
            lbuf[half, rows(qstart, SPAN_C), :] = lh
        return carry

    lax.fori_loop(0, nblk, body, 0)
    o_ref[...] = jnp.concatenate([obuf[0], obuf[1]], axis=1)
    lse_ref[...] = jnp.concatenate([lbuf[0], lbuf[1]], axis=1)


def _dil_attn(p, cos_t, sin_t, g, d):
    rows = p.shape[0]
    T = ATT_TILE
    kern = functools.partial(_dil_attn_kernel, d=d)
    col = lambda off: pl.BlockSpec((T, WG_C), lambda i, off=off: (i, off // WG_C + g))
    tab = pl.BlockSpec((T, 2 * DH_C), lambda i: (i, 0))
    out = pl.BlockSpec((T, WG_C), lambda i: (i, 0))
    return pl.pallas_call(
        kern,
        grid=(rows // T,),
        in_specs=[col(OFF_CQ), col(OFF_CK), col(OFF_CV), tab, tab],
        out_specs=[out, out, out],
        out_shape=[jax.ShapeDtypeStruct((rows, WG_C), F32)] * 3,
        scratch_shapes=[pltpu.VMEM((2, T, WG_C // 2), F32), pltpu.VMEM((2, 2 * T, WG_C // 2), F32),
                        pltpu.VMEM((2, 2 * T, WG_C // 2), F32), pltpu.VMEM((2, T, WG_C // 2), F32),
                        pltpu.VMEM((2, T, WG_C // 2), F32)],
        compiler_params=_cparams(1),
        name="dil_attn_d%d" % d,
    )(p, p, p, cos_t, sin_t)


def _sample_attn_kernel(pc0, pc1, pc2, cos_ref, sin_ref, c0_ref, c1_ref, c2_ref,
                        o0, o1, o2, l0, l1, l2, kr_ref, *, n_tok):
    pc = jnp.concatenate([pc0[0], pc1[0], pc2[0]], axis=1)
    reps = W_C // (2 * DH_C)
    cos = jnp.concatenate([cos_ref[...]] * reps, axis=1)
    sin = jnp.concatenate([sin_ref[...]] * reps, axis=1)
    qr = _rope(pc[:, 0:W_C], cos, sin) * (DH_C ** -0.5)
    kr = _rope(pc[:, W_C:2 * W_C], cos, sin)
    v = pc[:, 2 * W_C:3 * W_C]
    kr_ref[0] = kr
    nrow = HG_C * SAMPLE_ROWS
    rowh = lax.broadcasted_iota(jnp.int32, (nrow, WG_C), 0) // SAMPLE_ROWS
    laneh = lax.broadcasted_iota(jnp.int32, (nrow, WG_C), 1) // DH_C
    hm = (rowh == laneh).astype(F32)
    tok_col = lax.broadcasted_iota(jnp.int32, (nrow, 1), 0) % SAMPLE_ROWS
    out_row = lax.broadcasted_iota(jnp.int32, (SAMPLE_ROWS, WG_C), 0)
    neg_inf = jnp.float32(-jnp.inf)
    crefs = (c0_ref, c1_ref, c2_ref)
    orefs = (o0, o1, o2)
    lrefs = (l0, l1, l2)
    for g, (_, d) in enumerate(DIL_PATTERNS):
        gs = slice(g * WG_C, (g + 1) * WG_C)
        n_buf = SPAN_C * d
        kn = kr[:, gs]
        vn = v[:, gs]
        qm = jnp.concatenate([qr[:, gs]] * HG_C, axis=0) * hm
        s = _dot(qm, crefs[g][0, 0])
        tok = lax.broadcasted_iota(jnp.int32, (nrow, n_buf), 0) % SAMPLE_ROWS
        pos = lax.broadcasted_iota(jnp.int32, (nrow, n_buf), 1)
        attends = (pos >= tok) if d == 1 else ((pos & (d - 1)) == tok)
        s = jnp.where(attends | (tok >= n_tok), s, neg_inf)
        new_ok = [(u <= tok_col) if d == 1 else (u == tok_col) for u in range(n_tok)]
        s_new = [jnp.where(ok, jnp.sum(qm * kn[u:u + 1, :], axis=1, keepdims=True), neg_inf)
                 for u, ok in enumerate(new_ok)]
        mx = jnp.max(s, axis=1, keepdims=True)
        for sn in s_new:
            mx = jnp.maximum(mx, sn)
        p = jnp.exp(s - mx)
        den = jnp.sum(p, axis=1, keepdims=True)
        acc = _dot_nt(p, crefs[g][0, 1])
        for u, sn in enumerate(s_new):
            pn = jnp.exp(sn - mx)
            den = den + pn
            acc = acc + pn * vn[u:u + 1, :]
        o32 = acc / den * hm
        l32 = (mx + jnp.log(den)) * hm
        o8, l8 = o32[0:SAMPLE_ROWS], l32[0:SAMPLE_ROWS]
        for h in range(1, HG_C):
            o8 = o8 + o32[h * SAMPLE_ROWS:(h + 1) * SAMPLE_ROWS]
            l8 = l8 + l32[h * SAMPLE_ROWS:(h + 1) * SAMPLE_ROWS]
        orefs[g][0] = jnp.where(out_row < n_tok, o8, 0.0)
        lrefs[g][0] = jnp.where(out_row < n_tok, l8, 0.0)


def _cache_views(caches):
    views = []
    for (win, d), c in zip(DIL_PATTERNS, caches):
        depth, batch, n_buf = c.shape[:3]
        assert n_buf == SPAN_C * d
        views.append(jnp.transpose(c, (0, 1, 3, 4, 5, 2)).reshape(depth, batch, 2, WG_C, n_buf))
    return views


def _sample_attn(p, cos_s, sin_s, views, l, n_tok):
    batch = p.shape[0] // SAMPLE_ROWS
    p3 = p.reshape(batch, SAMPLE_ROWS, N_PACK)
    pcol = lambda k: pl.BlockSpec((1, SAMPLE_ROWS, 1024), lambda b, k=k: (b, 0, OFF_CQ // 1024 + k))
    tab = pl.BlockSpec((SAMPLE_ROWS, 2 * DH_C), lambda b: (0, 0))
    cache_spec = lambda d: pl.BlockSpec((None, 1, 2, WG_C, SPAN_C * d), lambda b: (l, b, 0, 0, 0))
    out = pl.BlockSpec((1, SAMPLE_ROWS, WG_C), lambda b: (b, 0, 0))
    res = pl.pallas_call(
        functools.partial(_sample_attn_kernel, n_tok=n_tok),
        grid=(batch,),
        in_specs=[pcol(0), pcol(1), pcol(2), tab, tab] + [cache_spec(d) for _, d in DIL_PATTERNS],
        out_specs=[out] * 6 + [pl.BlockSpec((1, SAMPLE_ROWS, W_C), lambda b: (b, 0, 0))],
        out_shape=[jax.ShapeDtypeStruct((batch, SAMPLE_ROWS, WG_C), F32)] * 6
        + [jax.ShapeDtypeStruct((batch, SAMPLE_ROWS, W_C), F32)],
        compiler_params=_cparams(1),
        name="sample_attn",
    )(p3, p3, p3, cos_s, sin_s, *views)
    rows = batch * SAMPLE_ROWS
    outs = [r.reshape(rows, WG_C) for r in res[:6]]
    return outs[:3], outs[3:], res[6].reshape(rows, W_C)


def _merge_kernel(x_ref, ya_ref, yb_ref, o0, o1, o2, l0, l1, l2, z0, z1, z2, ga_ref, gb_ref, gc_ref,
                  wb_ref, wo_ref, fg_ref, out_ref, *, final):
    ls = [l0[...], l1[...], l2[...]]
    mx = jnp.maximum(jnp.maximum(ls[0], ls[1]), ls[2])
    es = [jnp.exp(l - mx) for l in ls]
    tot = es[0] + es[1] + es[2]
    pc = None
    for g, (o_ref, z_ref) in enumerate(((o0, z0), (o1, z1), (o2, z2))):
        yc = o_ref[...] * (es[g] / tot) * _silu(z_ref[...])
        t = jnp.dot(yc.astype(BF16), wb_ref[W_A + W_B + g * WG_C:W_A + W_B + (g + 1) * WG_C, :],
                    preferred_element_type=F32)
        pc = t if pc is None else pc + t
    pa = jnp.dot(ya_ref[...].astype(BF16), wb_ref[0:W_A, :], preferred_element_type=F32)
    pb = jnp.dot(yb_ref[...].astype(BF16), wb_ref[W_A:W_A + W_B, :], preferred_element_type=F32)
    merged = (jax.nn.sigmoid(ga_ref[...]) * pa + jax.nn.sigmoid(gb_ref[...]) * pb
              + jax.nn.sigmoid(gc_ref[...]) * pc)
    out = x_ref[...] + jnp.dot(merged.astype(BF16), wo_ref[...], preferred_element_type=F32)
    if final:
        r = lax.rsqrt(jnp.mean(out * out, axis=-1, keepdims=True) + EPS)
        out = out * r * fg_ref[...]
    out_ref[...] = out


def _merge(x, p, ya, yb, os_, ls_, w_branch, w_out, final_g, l, final):
    rows = x.shape[0]
    tm = min(rows, 512)
    full = lambda w: pl.BlockSpec((tm, w), lambda i: (i, 0))
    pcol = lambda off, w: pl.BlockSpec((tm, w), lambda i, off=off, w=w: (i, off // w))
    const2 = lambda i: (0, 0)
    layer3 = lambda i: (l, 0, 0)
    return pl.pallas_call(
        functools.partial(_merge_kernel, final=final),
        grid=(rows // tm,),
        in_specs=[full(D_MODEL), full(W_A), full(W_B)] + [full(WG_C)] * 6
        + [pcol(OFF_CZ + g * WG_C, WG_C) for g in range(3)]
        + [pcol(OFF_GA, D_MODEL), pcol(OFF_GB, D_MODEL), pcol(OFF_GC, D_MODEL)]
        + [pl.BlockSpec((None, W_A + W_B + W_C, D_MODEL), layer3), pl.BlockSpec((None, D_MODEL, D_MODEL), layer3),
           pl.BlockSpec((1, D_MODEL), const2)],
        out_specs=full(D_MODEL),
        out_shape=jax.ShapeDtypeStruct((rows, D_MODEL), F32),
        compiler_params=_cparams(1),
        name="merge",
    )(x, ya, yb, *os_, *ls_, p, p, p, p, p, p, w_branch, w_out, final_g)


def _rope_tables(pos):
    half = ROT_DIM // 2
    inv = ROPE_THETA ** (-jnp.arange(half, dtype=F32) / half)
    ang = pos.astype(F32)[:, None] * inv[None, :]
    cos, sin = jnp.cos(ang), jnp.sin(ang)
    n = pos.shape[0]
    rest = DH_C - ROT_DIM
    cos_h = jnp.concatenate([cos, cos, jnp.ones((n, rest), F32)], axis=1)
    sin_h = jnp.concatenate([-sin, sin, jnp.zeros((n, rest), F32)], axis=1)
    return jnp.concatenate([cos_h, cos_h], axis=1), jnp.concatenate([sin_h, sin_h], axis=1)


def _stack_kv(k_rows, v_rows):
    b, t, _ = k_rows.shape
    return jnp.stack([k_rows.reshape(b, t, HG_C, DH_C), v_rows.reshape(b, t, HG_C, DH_C)], axis=2)


def kernel(x_prompt, x_sample, state_C, state_n, state_m, state_conv, cache_kv_w128, cache_kv_w512, cache_kv_w2048,
           norm_g, w_in, b_igate, b_fgate, conv_w, conv_b, ln_v_g, ln_v_b, w_spatial, b_spatial, w_branch, w_out,
           final_norm_g):
    depth = w_in.shape[0]
    bp, seq, _ = x_prompt.shape
    bs, n_tok, _ = x_sample.shape
    assert bp == 1 and seq % ATT_TILE == 0 and n_tok <= SAMPLE_ROWS // 2 and n_tok >= CONV_W - 1
    caches = (cache_kv_w128, cache_kv_w512, cache_kv_w2048)
    pad_tok = SAMPLE_ROWS - n_tok
    rows_s = bs * SAMPLE_ROWS

    w_in_t = jnp.swapaxes(w_in, 1, 2)
    w_main = _pack_w_in(w_in_t)
    gate_bias = jnp.concatenate([b_igate, b_fgate], axis=1)
    bias_row = jnp.pad(gate_bias, ((0, 0), (0, GATE_LANES - 2 * H_A)))[:, None, :]
    bias_col = gate_bias[:, :, None]
    wb16 = w_branch.astype(BF16)
    wo16 = w_out.astype(BF16)
    b_sp_col = b_spatial[..., None]
    norm_g3, conv_b3 = norm_g[:, None, :], conv_b[:, None, :]
    ln_g3, ln_b3 = ln_v_g[:, None, :], ln_v_b[:, None, :]
    seqs_per_chunk = CHUNK_B // SAMPLE_ROWS
    w8 = jnp.pad(w_spatial[:, :, :n_tok, :n_tok], ((0, 0), (0, 0), (0, pad_tok), (0, pad_tok)))
    w_sp_s = jnp.einsum('ab,lgij->lgaibj', jnp.eye(seqs_per_chunk, dtype=F32), w8).reshape(
        depth, G_B, CHUNK_B, CHUNK_B)
    b_sp_s = jnp.tile(jnp.pad(b_spatial[:, :, :n_tok], ((0, 0), (0, 0), (0, pad_tok))),
                      (1, 1, seqs_per_chunk))[..., None]

    cos_p, sin_p = _rope_tables(jnp.arange(seq))
    cos_s, sin_s = _rope_tables(PAST_LEN + jnp.arange(SAMPLE_ROWS))

    hp = x_prompt.reshape(seq, D_MODEL)
    hs = jnp.pad(x_sample, ((0, 0), (0, pad_tok), (0, 0))).reshape(rows_s, D_MODEL)
    fg = final_norm_g[None, :]
    zeros_c = jnp.zeros((1, 1, H_A, DH_A, DH_A), F32)
    zeros_n = jnp.zeros((1, 1, H_A, 1, DH_A), F32)
    zeros_m = jnp.zeros((1, 1, H_A, 1, GATE_LANES), F32)
    zeros_tail = jnp.zeros((1, 1, 8, 2 * W_A), F32)
    n0_s = state_n[:, :, :, None, :]
    m0_s = jnp.broadcast_to(state_m[:, :, :, None, None], (depth, bs, H_A, 1, GATE_LANES))
    tail0_s = jnp.pad(state_conv, ((0, 0), (0, 0), (8 - (CONV_W - 1), 0), (0, 0)))
    views = _cache_views(caches)

    p_out = {k: [] for k in ('C', 'n', 'm', 'conv', 'kv0', 'kv1', 'kv2')}
    s_out = {k: [] for k in ('C', 'n', 'm', 'conv', 'chunk_v', 'kv0', 'kv1', 'kv2')}
    for l in range(depth):
        final = l == depth - 1

        pp, gcol, grow = _inproj(hp, norm_g3, w_main, w_in_t, l)
        ya, c1, n1, m1, conv1 = _mlstm(pp, gcol, grow, bias_row, bias_col, conv_w, conv_b3,
                                       zeros_c, zeros_n, zeros_m, zeros_tail, l, 0, L=MLSTM_L, t_valid=MLSTM_L)
        yb, _ = _spatial(pp, ln_g3, ln_b3, w_spatial, b_sp_col, l)
        os_, ls_, krs = [], [], []
        for g, (win, d) in enumerate(DIL_PATTERNS):
            o_g, l_g, kr_g = _dil_attn(pp, cos_p, sin_p, g, d)
            os_.append(o_g)
            ls_.append(l_g)
            keep = min(win, seq)
            v_g = pp[seq - keep:, OFF_CV + g * WG_C:OFF_CV + (g + 1) * WG_C]
            p_out['kv%d' % g].append(_stack_kv(kr_g[None, seq - keep:], v_g[None]))
        hp = _merge(hp, pp, ya, yb, os_, ls_, wb16, wo16, fg, l, final)
        p_out['C'].append(c1)
        p_out['n'].append(n1)
        p_out['m'].append(m1)
        p_out['conv'].append(conv1)

        ps, gcol, grow = _inproj(hs, norm_g3, w_main, w_in_t, l)
        ya, c1, n1, m1, conv1 = _mlstm(ps, gcol, grow, bias_row, bias_col, conv_w, conv_b3,
                                       state_C, n0_s, m0_s, tail0_s, l, l, L=SAMPLE_ROWS, t_valid=n_tok)
        yb, vn = _spatial(ps, ln_g3, ln_b3, w_sp_s, b_sp_s, l)
        os_, ls_, kr = _sample_attn(ps, cos_s, sin_s, views, l, n_tok)
        hs = _merge(hs, ps, ya, yb, os_, ls_, wb16, wo16, fg, l, final)
        s_out['C'].append(c1)
        s_out['n'].append(n1)
        s_out['m'].append(m1)
        s_out['conv'].append(conv1)
        s_out['chunk_v'].append(vn.reshape(bs, SAMPLE_ROWS, W_B)[:, :n_tok])
        kr3 = kr.reshape(bs, SAMPLE_ROWS, W_C)[:, :n_tok]
        v3 = ps[:, OFF_CV:OFF_CV + W_C].reshape(bs, SAMPLE_ROWS, W_C)[:, :n_tok]
        for g in range(len(DIL_PATTERNS)):
            gs = slice(g * WG_C, (g + 1) * WG_C)
            s_out['kv%d' % g].append(_stack_kv(kr3[:, :, gs], v3[:, :, gs]))

    stk = lambda d, k: jnp.stack(d[k], axis=0)
    y_prompt = hp.reshape(bp, seq, D_MODEL)
    y_sample = hs.reshape(bs, SAMPLE_ROWS, D_MODEL)[:, :n_tok]
    return (y_prompt, y_sample,
            stk(p_out, 'C'), stk(p_out, 'n'), stk(p_out, 'm'), stk(p_out, 'conv'),
            stk(p_out, 'kv0'), stk(p_out, 'kv1'), stk(p_out, 'kv2'),
            stk(s_out, 'C'), stk(s_out, 'n'), stk(s_out, 'm'), stk(s_out, 'conv'), stk(s_out, 'chunk_v'),
            stk(s_out, 'kv0'), stk(s_out, 'kv1'), stk(s_out, 'kv2'))
```

```python
import functools

import jax
import jax.numpy as jnp
from jax import lax
from jax.experimental import pallas as pl
from jax.experimental.pallas import tpu as pltpu

F32 = jnp.float32
BF16 = jnp.bfloat16

D_MODEL = 1024
H_A = 4
DH_A = 256
W_A = H_A * DH_A
CONV_W = 4
G_B = 4
CHUNK_B = 128
W_B = 1024
DG_B = W_B // G_B
DIL_PATTERNS = ((128, 1), (512, 4), (2048, 16))
HG_C = 4
DH_C = 64
WG_C = HG_C * DH_C
W_C = len(DIL_PATTERNS) * WG_C
SPAN_C = 128
ROT_DIM = DH_C // 4
ROPE_THETA = 500000.0
EPS = 1e-6
PAST_LEN = 16384

OFF_A = 0
OFF_BU, OFF_BV, OFF_BZ = 5120, 6144, 7168
OFF_CQ, OFF_CK, OFF_CV, OFF_CZ = 8192, 8960, 9728, 10496
OFF_GA, OFF_GB, OFF_GC = 11264, 12288, 13312
N_PACK = 14336
GATE_LANES = 128

INPROJ_TM = 2048
INPROJ_TN = 1024
MLSTM_L = 256
SAMPLE_ROWS = 8
ATT_TILE = 2048
VMEM_LIMIT = 56 * 1024 * 1024


def _cparams(n_axes):
    return pltpu.CompilerParams(dimension_semantics=("arbitrary",) * n_axes, vmem_limit_bytes=VMEM_LIMIT)


def _dot(a, b):
    return jnp.dot(a.astype(BF16), b.astype(BF16), preferred_element_type=F32)


def _dot_nt(a, b):
    return lax.dot_general(a.astype(BF16), b.astype(BF16), (((1,), (1,)), ((), ())), preferred_element_type=F32)


def _dot_tn(a, b):
    return lax.dot_general(a.astype(BF16), b.astype(BF16), (((0,), (0,)), ((), ())), preferred_element_type=F32)


def _sigmoid(x):
    return 0.5 * jnp.tanh(0.5 * x) + 0.5


def _silu(x):
    return x * _sigmoid(x)


def _inproj_kernel(x_ref, g_ref, w_ref, wg_ref, p_ref, gc_ref, gr_ref, xn_ref):
    nt = (((1,), (1,)), ((), ()))

    @pl.when(pl.program_id(1) == 0)
    def _():
        x = x_ref[...]
        r = lax.rsqrt(jnp.mean(x * x, axis=-1, keepdims=True) + EPS)
        xn = (x * r * g_ref[...]).astype(BF16)
        xn_ref[...] = xn
        wg = wg_ref[...].astype(BF16)
        gc_ref[...] = lax.dot_general(xn, wg, nt, preferred_element_type=F32)
        gr_ref[...] = lax.dot_general(wg[0:8, :], xn, nt, preferred_element_type=F32)

    p_ref[...] = lax.dot_general(xn_ref[...], w_ref[...], nt, preferred_element_type=F32)


def _inproj(x, norm_g, w_main, w_in_t, l):
    rows = x.shape[0]
    gate_blk = 4 * W_A // GATE_LANES
    tm = min(rows, INPROJ_TM)
    tn = INPROJ_TN
    return pl.pallas_call(
        _inproj_kernel,
        grid=(rows // tm, N_PACK // tn),
        in_specs=[
            pl.BlockSpec((tm, D_MODEL), lambda i, j: (i, 0)),
            pl.BlockSpec((None, 1, D_MODEL), lambda i, j: (l, 0, 0)),
            pl.BlockSpec((None, tn, D_MODEL), lambda i, j: (l, j, 0)),
            pl.BlockSpec((None, GATE_LANES, D_MODEL), lambda i, j: (l, gate_blk, 0)),
        ],
        out_specs=[
            pl.BlockSpec((tm, tn), lambda i, j: (i, j)),
            pl.BlockSpec((tm, GATE_LANES), lambda i, j: (i, 0)),
            pl.BlockSpec((None, 8, tm), lambda i, j: (i, 0, 0)),
        ],
        out_shape=[
            jax.ShapeDtypeStruct((rows, N_PACK), F32),
            jax.ShapeDtypeStruct((rows, GATE_LANES), F32),
            jax.ShapeDtypeStruct((rows // tm, 8, tm), F32),
        ],
        scratch_shapes=[pltpu.VMEM((tm, D_MODEL), BF16)],
        compiler_params=_cparams(2),
        name="inproj",
    )(x, norm_g, w_main, w_in_t)


def _pack_kernel(a_ref, b_ref, w_ref, *, first_shifted):
    j = pl.program_id(1)

    @pl.when(j < first_shifted)
    def _():
        w_ref[...] = a_ref[...].astype(BF16)

    @pl.when(j >= first_shifted)
    def _():
        w_ref[...] = jnp.concatenate([a_ref[2 * H_A:, :], b_ref[...]], axis=0).astype(BF16)


def _pack_w_in(w_in_t):
    depth = w_in_t.shape[0]
    tn = 2048
    gate_off = 4 * W_A
    assert gate_off % tn == 0 and w_in_t.shape[1] == N_PACK + 2 * H_A and 2 * H_A == 8
    kern = functools.partial(_pack_kernel, first_shifted=gate_off // tn)
    return pl.pallas_call(
        kern,
        grid=(depth, N_PACK // tn),
        in_specs=[
            pl.BlockSpec((None, tn, D_MODEL), lambda l, j: (l, j, 0)),
            pl.BlockSpec((None, 8, D_MODEL), lambda l, j: (l, (j + 1) * (tn // 8), 0)),
        ],
        out_specs=pl.BlockSpec((None, tn, D_MODEL), lambda l, j: (l, j, 0)),
        out_shape=jax.ShapeDtypeStruct((depth, N_PACK, D_MODEL), BF16),
        compiler_params=_cparams(2),
        name="pack_w_in",
    )(w_in_t, w_in_t)


def _conv_silu(ext_ref, cols, w, b, L):
    y = b
    for back in range(CONV_W):
        y = y + ext_ref[8 - back:8 - back + L, cols] * w[CONV_W - 1 - back:CONV_W - back]
    return _silu(y)


def _mlstm_kernel(p_ref, gc_ref, gr_ref, brow_ref, bcol_ref, cw_ref, cb_ref, c0_ref, n0_ref, m0_ref, tail0_ref,
                  y_ref, c_out, n_out, m_out, tail_out, c_s, n_s, m_s, ext_s, *, L, t_valid):
    ch = pl.program_id(1)

    @pl.when(ch == 0)
    def _():
        c_s[...] = c0_ref[0]
        n_s[...] = n0_ref[0]
        m_s[...] = m0_ref[0]
        ext_s[0:8, :] = tail0_ref[0]

    ext_s[8:8 + L, :] = p_ref[0, :, 0:2 * W_A]

    neg_inf = jnp.float32(-jnp.inf)
    g_col = gc_ref[0] + brow_ref[...]
    g_row = gr_ref[0] + bcol_ref[...]
    ig_col, lf_col = g_col, jax.nn.log_sigmoid(g_col)
    ig_row, lf_row = g_row, jax.nn.log_sigmoid(g_row)
    if t_valid < L:
        vc = lax.broadcasted_iota(jnp.int32, (L, GATE_LANES), 0) < t_valid
        vr = lax.broadcasted_iota(jnp.int32, (8, L), 1) < t_valid
        ig_col, lf_col = jnp.where(vc, ig_col, neg_inf), jnp.where(vc, lf_col, 0.0)
        ig_row, lf_row = jnp.where(vr, ig_row, neg_inf), jnp.where(vr, lf_row, 0.0)
    ti = lax.broadcasted_iota(jnp.int32, (L, L), 0)
    si = lax.broadcasted_iota(jnp.int32, (L, L), 1)
    causal = ti >= si
    b_col = jnp.dot(causal.astype(F32), lf_col, preferred_element_type=F32, precision=lax.Precision.HIGHEST)
    b_row = jnp.dot(lf_row, (ti <= si).astype(F32), preferred_element_type=F32, precision=lax.Precision.HIGHEST)
    last = t_valid - 1

    for h in range(H_A):
        cs = slice(h * DH_A, (h + 1) * DH_A)
        ks = slice(W_A + h * DH_A, W_A + (h + 1) * DH_A)
        q = _conv_silu(ext_s, cs, cw_ref[:, cs], cb_ref[:, cs], L)
        k = _conv_silu(ext_s, ks, cw_ref[:, ks], cb_ref[:, ks], L) * (DH_A ** -0.5)
        v = p_ref[0, :, 2 * W_A + h * DH_A:2 * W_A + (h + 1) * DH_A]
        o = p_ref[0, :, 3 * W_A + h * DH_A:3 * W_A + (h + 1) * DH_A]
        z = p_ref[0, :, 4 * W_A + h * DH_A:4 * W_A + (h + 1) * DH_A]
        bc = b_col[:, H_A + h:H_A + h + 1]
        igc = ig_col[:, h:h + 1]
        br = b_row[H_A + h:H_A + h + 1, :]
        igr = ig_row[h:h + 1, :]
        m_prev = m_s[h][:, 0:1]
        c_prev = c_s[h]
        n_prev = n_s[h]

        logw = jnp.where(causal, bc - br + igr, neg_inf)
        inter = bc + m_prev
        m_t = jnp.maximum(inter, jnp.max(logw, axis=1, keepdims=True))
        w_intra = jnp.exp(logw - m_t)
        w_inter = jnp.exp(inter - m_t)
        s = w_intra * _dot_nt(q, k)
        num = _dot(s, v) + w_inter * _dot_nt(q, c_prev)
        den = jnp.sum(s, axis=1, keepdims=True) + w_inter * jnp.sum(q * n_prev, axis=1, keepdims=True)
        hh = num / jnp.maximum(jnp.abs(den), jnp.exp(-m_t))
        y_ref[0, :, cs] = hh * _sigmoid(o) * _silu(z)

        m_new = m_t[last:last + 1, :]
        b_last = bc[last:last + 1, :]
        decay = jnp.exp(b_last + m_prev - m_new)
        w_s = jnp.exp(b_last - bc + igc - m_new)
        c_s[h] = decay * c_prev + _dot_tn(w_s * v, k)
        n_s[h] = decay * n_prev + jnp.sum(w_s * k, axis=0, keepdims=True)
        m_s[h] = jnp.broadcast_to(m_new, (1, GATE_LANES))

    ext_s[0:8, :] = ext_s[t_valid:t_valid + 8, :]

    @pl.when(ch == pl.num_programs(1) - 1)
    def _():
        c_out[0] = c_s[...]
        n_out[0] = n_s[...]
        m_out[0] = m_s[...]
        tail_out[0] = ext_s[0:8, :]


def _mlstm(p, gcol, grow, bias_row, bias_col, conv_w, conv_b, c0, n0, m0, tail0, l, ls, *, L, t_valid):
    batch = c0.shape[1]
    rows = p.shape[0]
    nch = rows // (batch * L)
    p3 = p.reshape(batch * nch, L, N_PACK)
    gc3 = gcol.reshape(batch * nch, L, GATE_LANES)
    tm = grow.shape[2]
    if L % GATE_LANES == 0:
        per = tm // L
        gr3, gr_index = grow, (lambda b, c: ((b * nch + c) // per, 0, (b * nch + c) % per))
    else:
        gr3 = grow.transpose(1, 0, 2).reshape(8, batch * nch, L).transpose(1, 0, 2)
        gr_index = lambda b, c: (b * nch + c, 0, 0)
    kern = functools.partial(_mlstm_kernel, L=L, t_valid=t_valid)
    chunk = lambda b, c: (b * nch + c, 0, 0)
    layer3 = lambda b, c: (l, 0, 0)
    per_b4 = lambda b, c: (b, 0, 0, 0)
    per_b3 = lambda b, c: (b, 0, 0)
    st5 = lambda b, c: (ls, b, 0, 0, 0)
    st4 = lambda b, c: (ls, b, 0, 0)
    y, c1, n1, m1, tail1 = pl.pallas_call(
        kern,
        grid=(batch, nch),
        in_specs=[
            pl.BlockSpec((1, L, 5 * W_A), chunk),
            pl.BlockSpec((1, L, GATE_LANES), chunk),
            pl.BlockSpec((1, 8, L), gr_index),
            pl.BlockSpec((None, 1, GATE_LANES), layer3),
            pl.BlockSpec((None, 8, 1), layer3),
            pl.BlockSpec((None, CONV_W, 2 * W_A), layer3),
            pl.BlockSpec((None, 1, 2 * W_A), layer3),
            pl.BlockSpec((None, 1, H_A, DH_A, DH_A), st5),
            pl.BlockSpec((None, 1, H_A, 1, DH_A), st5),
            pl.BlockSpec((None, 1, H_A, 1, GATE_LANES), st5),
            pl.BlockSpec((None, 1, 8, 2 * W_A), st4),
        ],
        out_specs=[
            pl.BlockSpec((1, L, W_A), chunk),
            pl.BlockSpec((1, H_A, DH_A, DH_A), per_b4),
            pl.BlockSpec((1, H_A, 1, DH_A), per_b4),
            pl.BlockSpec((1, H_A, 1, GATE_LANES), per_b4),
            pl.BlockSpec((1, 8, 2 * W_A), per_b3),
        ],
        out_shape=[
            jax.ShapeDtypeStruct((batch * nch, L, W_A), F32),
            jax.ShapeDtypeStruct((batch, H_A, DH_A, DH_A), F32),
            jax.ShapeDtypeStruct((batch, H_A, 1, DH_A), F32),
            jax.ShapeDtypeStruct((batch, H_A, 1, GATE_LANES), F32),
            jax.ShapeDtypeStruct((batch, 8, 2 * W_A), F32),
        ],
        scratch_shapes=[
            pltpu.VMEM((H_A, DH_A, DH_A), F32),
            pltpu.VMEM((H_A, 1, DH_A), F32),
            pltpu.VMEM((H_A, 1, GATE_LANES), F32),
            pltpu.VMEM((L + 8, 2 * W_A), F32),
        ],
        compiler_params=_cparams(2),
        name="mlstm",
    )(p3, gc3, gr3, bias_row, bias_col, conv_w, conv_b, c0, n0, m0, tail0)
    return y.reshape(rows, W_A), c1, n1[:, :, 0, :], m1[:, :, 0, 0], tail1[:, 8 - (CONV_W - 1):, :]


def _spatial_kernel(u_ref, v_ref, z_ref, lg_ref, lb_ref, w_ref, bc_ref, y_ref, vn_ref, *, nchunks):
    v = v_ref[...]
    mu = jnp.mean(v, axis=-1, keepdims=True)
    var = jnp.mean(jnp.square(v - mu), axis=-1, keepdims=True)
    vn_ref[...] = (v - mu) * lax.rsqrt(var + EPS) * lg_ref[...] + lb_ref[...]
    tri = (lax.broadcasted_iota(jnp.int32, (CHUNK_B, CHUNK_B), 0)
           >= lax.broadcasted_iota(jnp.int32, (CHUNK_B, CHUNK_B), 1))
    for g in range(G_B):
        wg = jnp.where(tri, w_ref[g], 0.0).astype(BF16)
        cs = slice(g * DG_B, (g + 1) * DG_B)
        for c in range(nchunks):
            rs = slice(c * CHUNK_B, (c + 1) * CHUNK_B)
            sp = jnp.dot(wg, vn_ref[rs, cs].astype(BF16), preferred_element_type=F32) + bc_ref[g]
            y_ref[rs, cs] = u_ref[rs, cs] * sp * _silu(z_ref[rs, cs])


def _spatial(p, ln_g, ln_b, w_sp, b_sp_col, l):
    rows = p.shape[0]
    tr = min(rows, 512)
    kern = functools.partial(_spatial_kernel, nchunks=tr // CHUNK_B)
    col = lambda off: pl.BlockSpec((tr, W_B), lambda i, off=off: (i, off // W_B))
    return pl.pallas_call(
        kern,
        grid=(rows // tr,),
        in_specs=[
            col(OFF_BU), col(OFF_BV), col(OFF_BZ),
            pl.BlockSpec((None, 1, W_B), lambda i: (l, 0, 0)),
            pl.BlockSpec((None, 1, W_B), lambda i: (l, 0, 0)),
            pl.BlockSpec((None, G_B, CHUNK_B, CHUNK_B), lambda i: (l, 0, 0, 0)),
            pl.BlockSpec((None, G_B, CHUNK_B, 1), lambda i: (l, 0, 0, 0)),
        ],
        out_specs=[pl.BlockSpec((tr, W_B), lambda i: (i, 0)), pl.BlockSpec((tr, W_B), lambda i: (i, 0))],
        out_shape=[jax.ShapeDtypeStruct((rows, W_B), F32), jax.ShapeDtypeStruct((rows, W_B), F32)],
        compiler_params=_cparams(1),
        name="spatial",
    )(p, p, p, ln_g, ln_b, w_sp, b_sp_col)


def _rope(x, cos, sin):
    width = x.shape[1]
    lane = lax.broadcasted_iota(jnp.int32, (1, width), 1) % DH_C
    half = ROT_DIM // 2
    partner = jnp.where(lane < half, pltpu.roll(x, width - half, 1), pltpu.roll(x, half, 1))
    return x * cos + partner * sin


def _dil_attn_kernel(q_ref, k_ref, v_ref, cos_ref, sin_ref, o_ref, lse_ref, kr_ref, qs, kbuf, vbuf, bias, *, d):
    i = pl.program_id(0)
    T = ATT_TILE
    HL = WG_C // 2
    nblk = T // SPAN_C
    nres = T // (SPAN_C * d)
    cur = (i % 2) * T
    prev = T - cur
    neg_inf = jnp.float32(-jnp.inf)

    def split(x):
        return x[:, 0:HL], x[:, HL:WG_C]

    def put(buf, lo, x):
        a, b = split(x)
        buf[0, pl.ds(lo, T), :] = a
        buf[1, pl.ds(lo, T), :] = b

    @pl.when(i == 0)
    def _():
        ii = lax.broadcasted_iota(jnp.int32, (SPAN_C, 2 * SPAN_C), 0)
        jj = lax.broadcasted_iota(jnp.int32, (SPAN_C, 2 * SPAN_C), 1)
        band = (jj >= ii) & (jj <= ii + SPAN_C)
        bias[1] = jnp.where(band, 0.0, neg_inf)
        bias[0] = jnp.where(band & (jj >= SPAN_C), 0.0, neg_inf)
        kbuf[:, T:2 * T, :] = jnp.zeros((2, T, HL), F32)
        vbuf[:, T:2 * T, :] = jnp.zeros((2, T, HL), F32)

    cos = jnp.concatenate([cos_ref[...], cos_ref[...]], axis=1)
    sin = jnp.concatenate([sin_ref[...], sin_ref[...]], axis=1)
    put(qs, 0, _rope(q_ref[...], cos, sin) * (DH_C ** -0.5))
    kr = _rope(k_ref[...], cos, sin)
    kr_ref[...] = kr
    put(kbuf, cur, kr)
    put(vbuf, cur, v_ref[...])

    lane_head = lax.broadcasted_iota(jnp.int32, (SPAN_C, WG_C), 1) // DH_C
    hm = [(lane_head == h).astype(F32) for h in range(HG_C)]

    def rows(start, size):
        return pl.ds(start, size) if d == 1 else pl.ds(start, size, stride=d)

    def take(buf, start, size):
        return jnp.concatenate([buf[0, rows(start, size), :], buf[1, rows(start, size), :]], axis=1)

    def pick(per_head):
        out = per_head[0]
        for h in range(1, HG_C):
            out = jnp.where(lane_head >= h, per_head[h], out)
        return out

    def body(blk, carry):
        r = blk % d
        c = blk // d
        qstart = r + SPAN_C * d * c
        in_tile = cur + qstart
        before = jnp.where(c > 0, in_tile - SPAN_C * d, prev + r + SPAN_C * d * (nres - 1))
        qb = take(qs, qstart, SPAN_C)
        kb = jnp.concatenate([take(kbuf, before, SPAN_C), take(kbuf, in_tile, SPAN_C)], axis=0)
        vb = jnp.concatenate([take(vbuf, before, SPAN_C), take(vbuf, in_tile, SPAN_C)], axis=0)
        q4 = jnp.concatenate([qb * hm[h] for h in range(HG_C)], axis=0)
        has_prev = jnp.logical_or(i > 0, c > 0).astype(jnp.int32)
        s = _dot_nt(q4, kb).reshape(HG_C, SPAN_C, 2 * SPAN_C) + bias[has_prev]
        mx = jnp.max(s, axis=2, keepdims=True)
        p = jnp.exp(s - mx)
        den = jnp.sum(p, axis=2, keepdims=True)
        pv = _dot(p.reshape(HG_C * SPAN_C, 2 * SPAN_C), vb).reshape(HG_C, SPAN_C, WG_C)
        rden = 1.0 / den
        lse = mx + jnp.log(den)
        o = pick([pv[h] for h in range(HG_C)]) * pick([jnp.broadcast_to(rden[h], (SPAN_C, WG_C)) for h in range(HG_C)])
        lse_b = pick([jnp.broadcast_to(lse[h], (SPAN_C, WG_C)) for h in range(HG_C)])
        for half, (oh, lh) in enumerate(zip(split(o), split(lse_b))):
            o_ref[half, rows(qstart, SPAN_C), :] = oh
            lse_ref[half, rows(qstart, SPAN_C), :] = lh
        return carry

    lax.fori_loop(0, nblk, body, 0, unroll=2)


def _dil_attn(p, cos_t, sin_t, g, d):
    rows = p.shape[0]
    T = ATT_TILE
    kern = functools.partial(_dil_attn_kernel, d=d)
    col = lambda off: pl.BlockSpec((T, WG_C), lambda i, off=off: (i, off // WG_C + g))
    tab = pl.BlockSpec((T, 2 * DH_C), lambda i: (i, 0))
    slab = pl.BlockSpec((2, T, WG_C // 2), lambda i: (0, i, 0))
    slab_shape = jax.ShapeDtypeStruct((2, rows, WG_C // 2), F32)
    return pl.pallas_call(
        kern,
        grid=(rows // T,),
        in_specs=[col(OFF_CQ), col(OFF_CK), col(OFF_CV), tab, tab],
        out_specs=[slab, slab, pl.BlockSpec((T, WG_C), lambda i: (i, 0))],
        out_shape=[slab_shape, slab_shape, jax.ShapeDtypeStruct((rows, WG_C), F32)],
        scratch_shapes=[pltpu.VMEM((2, T, WG_C // 2), F32), pltpu.VMEM((2, 2 * T, WG_C // 2), F32),
                        pltpu.VMEM((2, 2 * T, WG_C // 2), F32), pltpu.VMEM((2, SPAN_C, 2 * SPAN_C), F32)],
        compiler_params=_cparams(1),
        name="dil_attn_d%d" % d,
    )(p, p, p, cos_t, sin_t)


def _sample_attn_kernel(pc0, pc1, pc2, cos_ref, sin_ref, c0_ref, c1_ref, c2_ref,
                        o0, o1, o2, l0, l1, l2, kr_ref, *, n_tok):
    pc = jnp.concatenate([pc0[0], pc1[0], pc2[0]], axis=1)
    reps = W_C // (2 * DH_C)
    cos = jnp.concatenate([cos_ref[...]] * reps, axis=1)
    sin = jnp.concatenate([sin_ref[...]] * reps, axis=1)
    qr = _rope(pc[:, 0:W_C], cos, sin) * (DH_C ** -0.5)
    kr = _rope(pc[:, W_C:2 * W_C], cos, sin)
    v = pc[:, 2 * W_C:3 * W_C]
    kr_ref[0] = kr
    nrow = HG_C * SAMPLE_ROWS
    rowh = lax.broadcasted_iota(jnp.int32, (nrow, WG_C), 0) // SAMPLE_ROWS
    laneh = lax.broadcasted_iota(jnp.int32, (nrow, WG_C), 1) // DH_C
    hm = (rowh == laneh).astype(F32)
    tok_col = lax.broadcasted_iota(jnp.int32, (nrow, 1), 0) % SAMPLE_ROWS
    out_row = lax.broadcasted_iota(jnp.int32, (SAMPLE_ROWS, WG_C), 0)
    neg_inf = jnp.float32(-jnp.inf)
    crefs = (c0_ref, c1_ref, c2_ref)
    orefs = (o0, o1, o2)
    lrefs = (l0, l1, l2)
    for g, (_, d) in enumerate(DIL_PATTERNS):
        gs = slice(g * WG_C, (g + 1) * WG_C)
        n_buf = SPAN_C * d
        kn = kr[:, gs]
        vn = v[:, gs]
        qm = jnp.concatenate([qr[:, gs]] * HG_C, axis=0) * hm
        s = _dot(qm, crefs[g][0, 0])
        tok = lax.broadcasted_iota(jnp.int32, (nrow, n_buf), 0) % SAMPLE_ROWS
        pos = lax.broadcasted_iota(jnp.int32, (nrow, n_buf), 1)
        attends = (pos >= tok) if d == 1 else ((pos & (d - 1)) == tok)
        s = jnp.where(attends | (tok >= n_tok), s, neg_inf)
        new_ok = [(u <= tok_col) if d == 1 else (u == tok_col) for u in range(n_tok)]
        s_new = [jnp.where(ok, jnp.sum(qm * kn[u:u + 1, :], axis=1, keepdims=True), neg_inf)
                 for u, ok in enumerate(new_ok)]
        mx = jnp.max(s, axis=1, keepdims=True)
        for sn in s_new:
            mx = jnp.maximum(mx, sn)
        p = jnp.exp(s - mx)
        den = jnp.sum(p, axis=1, keepdims=True)
        acc = _dot_nt(p, crefs[g][0, 1])
        for u, sn in enumerate(s_new):
            pn = jnp.exp(sn - mx)
            den = den + pn
            acc = acc + pn * vn[u:u + 1, :]
        o32 = acc / den * hm
        l32 = (mx + jnp.log(den)) * hm
        o8, l8 = o32[0:SAMPLE_ROWS], l32[0:SAMPLE_ROWS]
        for h in range(1, HG_C):
            o8 = o8 + o32[h * SAMPLE_ROWS:(h + 1) * SAMPLE_ROWS]
            l8 = l8 + l32[h * SAMPLE_ROWS:(h + 1) * SAMPLE_ROWS]
        o8 = jnp.where(out_row < n_tok, o8, 0.0)
        l8 = jnp.where(out_row < n_tok, l8, 0.0)
        for half in range(2):
            hs = slice(half * (WG_C // 2), (half + 1) * (WG_C // 2))
            orefs[g][half] = o8[:, hs]
            lrefs[g][half] = l8[:, hs]


def _cache_views(caches):
    views = []
    for (win, d), c in zip(DIL_PATTERNS, caches):
        depth, batch, n_buf = c.shape[:3]
        assert n_buf == SPAN_C * d
        views.append(jnp.transpose(c, (0, 1, 3, 4, 5, 2)).reshape(depth, batch, 2, WG_C, n_buf))
    return views


def _sample_attn(p, cos_s, sin_s, views, l, n_tok):
    batch = p.shape[0] // SAMPLE_ROWS
    p3 = p.reshape(batch, SAMPLE_ROWS, N_PACK)
    pcol = lambda k: pl.BlockSpec((1, SAMPLE_ROWS, 1024), lambda b, k=k: (b, 0, OFF_CQ // 1024 + k))
    tab = pl.BlockSpec((SAMPLE_ROWS, 2 * DH_C), lambda b: (0, 0))
    cache_spec = lambda d: pl.BlockSpec((None, 1, 2, WG_C, SPAN_C * d), lambda b: (l, b, 0, 0, 0))
    out = pl.BlockSpec((2, SAMPLE_ROWS, WG_C // 2), lambda b: (0, b, 0))
    res = pl.pallas_call(
        functools.partial(_sample_attn_kernel, n_tok=n_tok),
        grid=(batch,),
        in_specs=[pcol(0), pcol(1), pcol(2), tab, tab] + [cache_spec(d) for _, d in DIL_PATTERNS],
        out_specs=[out] * 6 + [pl.BlockSpec((1, SAMPLE_ROWS, W_C), lambda b: (b, 0, 0))],
        out_shape=[jax.ShapeDtypeStruct((2, batch * SAMPLE_ROWS, WG_C // 2), F32)] * 6
        + [jax.ShapeDtypeStruct((batch, SAMPLE_ROWS, W_C), F32)],
        compiler_params=_cparams(1),
        name="sample_attn",
    )(p3, p3, p3, cos_s, sin_s, *views)
    return list(res[:3]), list(res[3:6]), res[6].reshape(batch * SAMPLE_ROWS, W_C)


def _merge_kernel(x_ref, ya_ref, yb_ref, o0, o1, o2, l0, l1, l2, z0, z1, z2, ga_ref, gb_ref, gc_ref,
                  wb_ref, wo_ref, fg_ref, out_ref, *, final):
    unslab = lambda ref: jnp.concatenate([ref[0], ref[1]], axis=1)
    ls = [unslab(l0), unslab(l1), unslab(l2)]
    mx = jnp.maximum(jnp.maximum(ls[0], ls[1]), ls[2])
    es = [jnp.exp(l - mx) for l in ls]
    tot = es[0] + es[1] + es[2]
    pc = None
    for g, (o_ref, z_ref) in enumerate(((o0, z0), (o1, z1), (o2, z2))):
        yc = unslab(o_ref) * (es[g] / tot) * _silu(z_ref[...])
        t = jnp.dot(yc.astype(BF16), wb_ref[W_A + W_B + g * WG_C:W_A + W_B + (g + 1) * WG_C, :],
                    preferred_element_type=F32)
        pc = t if pc is None else pc + t
    pa = jnp.dot(ya_ref[...].astype(BF16), wb_ref[0:W_A, :], preferred_element_type=F32)
    pb = jnp.dot(yb_ref[...].astype(BF16), wb_ref[W_A:W_A + W_B, :], preferred_element_type=F32)
    merged = _sigmoid(ga_ref[...]) * pa + _sigmoid(gb_ref[...]) * pb + _sigmoid(gc_ref[...]) * pc
    out = x_ref[...] + jnp.dot(merged.astype(BF16), wo_ref[...], preferred_element_type=F32)
    if final:
        r = lax.rsqrt(jnp.mean(out * out, axis=-1, keepdims=True) + EPS)
        out = out * r * fg_ref[...]
    out_ref[...] = out


def _merge(x, p, ya, yb, os_, ls_, w_branch, w_out, final_g, l, final):
    rows = x.shape[0]
    tm = min(rows, 512)
    full = lambda w: pl.BlockSpec((tm, w), lambda i: (i, 0))
    pcol = lambda off, w: pl.BlockSpec((tm, w), lambda i, off=off, w=w: (i, off // w))
    const2 = lambda i: (0, 0)
    layer3 = lambda i: (l, 0, 0)
    return pl.pallas_call(
        functools.partial(_merge_kernel, final=final),
        grid=(rows // tm,),
        in_specs=[full(D_MODEL), full(W_A), full(W_B)]
        + [pl.BlockSpec((2, tm, WG_C // 2), lambda i: (0, i, 0))] * 6
        + [pcol(OFF_CZ + g * WG_C, WG_C) for g in range(3)]
        + [pcol(OFF_GA, D_MODEL), pcol(OFF_GB, D_MODEL), pcol(OFF_GC, D_MODEL)]
        + [pl.BlockSpec((None, W_A + W_B + W_C, D_MODEL), layer3), pl.BlockSpec((None, D_MODEL, D_MODEL), layer3),
           pl.BlockSpec((1, D_MODEL), const2)],
        out_specs=full(D_MODEL),
        out_shape=jax.ShapeDtypeStruct((rows, D_MODEL), F32),
        compiler_params=_cparams(1),
        name="merge",
    )(x, ya, yb, *os_, *ls_, p, p, p, p, p, p, w_branch, w_out, final_g)


def _rope_tables(pos):
    half = ROT_DIM // 2
    inv = ROPE_THETA ** (-jnp.arange(half, dtype=F32) / half)
    ang = pos.astype(F32)[:, None] * inv[None, :]
    cos, sin = jnp.cos(ang), jnp.sin(ang)
    n = pos.shape[0]
    rest = DH_C - ROT_DIM
    cos_h = jnp.concatenate([cos, cos, jnp.ones((n, rest), F32)], axis=1)
    sin_h = jnp.concatenate([-sin, sin, jnp.zeros((n, rest), F32)], axis=1)
    return jnp.concatenate([cos_h, cos_h], axis=1), jnp.concatenate([sin_h, sin_h], axis=1)


def _stack_kv(k_rows, v_rows):
    b, t, _ = k_rows.shape
    return jnp.stack([k_rows.reshape(b, t, HG_C, DH_C), v_rows.reshape(b, t, HG_C, DH_C)], axis=2)


def kernel(x_prompt, x_sample, state_C, state_n, state_m, state_conv, cache_kv_w128, cache_kv_w512, cache_kv_w2048,
           norm_g, w_in, b_igate, b_fgate, conv_w, conv_b, ln_v_g, ln_v_b, w_spatial, b_spatial, w_branch, w_out,
           final_norm_g):
    depth = w_in.shape[0]
    bp, seq, _ = x_prompt.shape
    bs, n_tok, _ = x_sample.shape
    assert bp == 1 and seq % ATT_TILE == 0 and n_tok <= SAMPLE_ROWS // 2 and n_tok >= CONV_W - 1
    caches = (cache_kv_w128, cache_kv_w512, cache_kv_w2048)
    pad_tok = SAMPLE_ROWS - n_tok
    rows_s = bs * SAMPLE_ROWS

    w_in_t = jnp.swapaxes(w_in, 1, 2)
    w_main = _pack_w_in(w_in_t)
    gate_bias = jnp.concatenate([b_igate, b_fgate], axis=1)
    bias_row = jnp.pad(gate_bias, ((0, 0), (0, GATE_LANES - 2 * H_A)))[:, None, :]
    bias_col = gate_bias[:, :, None]
    wb16 = w_branch.astype(BF16)
    wo16 = w_out.astype(BF16)
    b_sp_col = b_spatial[..., None]
    norm_g3, conv_b3 = norm_g[:, None, :], conv_b[:, None, :]
    ln_g3, ln_b3 = ln_v_g[:, None, :], ln_v_b[:, None, :]
    seqs_per_chunk = CHUNK_B // SAMPLE_ROWS
    w8 = jnp.pad(w_spatial[:, :, :n_tok, :n_tok], ((0, 0), (0, 0), (0, pad_tok), (0, pad_tok)))
    w_sp_s = jnp.einsum('ab,lgij->lgaibj', jnp.eye(seqs_per_chunk, dtype=F32), w8).reshape(
        depth, G_B, CHUNK_B, CHUNK_B)
    b_sp_s = jnp.tile(jnp.pad(b_spatial[:, :, :n_tok], ((0, 0), (0, 0), (0, pad_tok))),
                      (1, 1, seqs_per_chunk))[..., None]

    cos_p, sin_p = _rope_tables(jnp.arange(seq))
    cos_s, sin_s = _rope_tables(PAST_LEN + jnp.arange(SAMPLE_ROWS))

    hp = x_prompt.reshape(seq, D_MODEL)
    hs = jnp.pad(x_sample, ((0, 0), (0, pad_tok), (0, 0))).reshape(rows_s, D_MODEL)
    fg = final_norm_g[None, :]
    zeros_c = jnp.zeros((1, 1, H_A, DH_A, DH_A), F32)
    zeros_n = jnp.zeros((1, 1, H_A, 1, DH_A), F32)
    zeros_m = jnp.zeros((1, 1, H_A, 1, GATE_LANES), F32)
    zeros_tail = jnp.zeros((1, 1, 8, 2 * W_A), F32)
    n0_s = state_n[:, :, :, None, :]
    m0_s = jnp.broadcast_to(state_m[:, :, :, None, None], (depth, bs, H_A, 1, GATE_LANES))
    tail0_s = jnp.pad(state_conv, ((0, 0), (0, 0), (8 - (CONV_W - 1), 0), (0, 0)))
    views = _cache_views(caches)

    p_out = {k: [] for k in ('C', 'n', 'm', 'conv', 'kv0', 'kv1', 'kv2')}
    s_out = {k: [] for k in ('C', 'n', 'm', 'conv', 'chunk_v', 'kv0', 'kv1', 'kv2')}
    for l in range(depth):
        final = l == depth - 1

        pp, gcol, grow = _inproj(hp, norm_g3, w_main, w_in_t, l)
        ya, c1, n1, m1, conv1 = _mlstm(pp, gcol, grow, bias_row, bias_col, conv_w, conv_b3,
                                       zeros_c, zeros_n, zeros_m, zeros_tail, l, 0, L=MLSTM_L, t_valid=MLSTM_L)
        yb, _ = _spatial(pp, ln_g3, ln_b3, w_spatial, b_sp_col, l)
        os_, ls_, krs = [], [], []
        for g, (win, d) in enumerate(DIL_PATTERNS):
            o_g, l_g, kr_g = _dil_attn(pp, cos_p, sin_p, g, d)
            os_.append(o_g)
            ls_.append(l_g)
            keep = min(win, seq)
            v_g = pp[seq - keep:, OFF_CV + g * WG_C:OFF_CV + (g + 1) * WG_C]
            p_out['kv%d' % g].append(_stack_kv(kr_g[None, seq - keep:], v_g[None]))
        hp = _merge(hp, pp, ya, yb, os_, ls_, wb16, wo16, fg, l, final)
        p_out['C'].append(c1)
        p_out['n'].append(n1)
        p_out['m'].append(m1)
        p_out['conv'].append(conv1)

        ps, gcol, grow = _inproj(hs, norm_g3, w_main, w_in_t, l)
        ya, c1, n1, m1, conv1 = _mlstm(ps, gcol, grow, bias_row, bias_col, conv_w, conv_b3,
                                       state_C, n0_s, m0_s, tail0_s, l, l, L=SAMPLE_ROWS, t_valid=n_tok)
        yb, vn = _spatial(ps, ln_g3, ln_b3, w_sp_s, b_sp_s, l)
        os_, ls_, kr = _sample_attn(ps, cos_s, sin_s, views, l, n_tok)
        hs = _merge(hs, ps, ya, yb, os_, ls_, wb16, wo16, fg, l, final)
        s_out['C'].append(c1)
        s_out['n'].append(n1)
        s_out['m'].append(m1)
        s_out['conv'].append(conv1)
        s_out['chunk_v'].append(vn.reshape(bs, SAMPLE_ROWS, W_B)[:, :n_tok])
        kr3 = kr.reshape(bs, SAMPLE_ROWS, W_C)[:, :n_tok]
        v3 = ps[:, OFF_CV:OFF_CV + W_C].reshape(bs, SAMPLE_ROWS, W_C)[:, :n_tok]
        for g in range(len(DIL_PATTERNS)):
            gs = slice(g * WG_C, (g + 1) * WG_C)
            s_out['kv%d' % g].append(_stack_kv(kr3[:, :, gs], v3[:, :, gs]))

    stk = lambda d, k: jnp.stack(d[k], axis=0)
    y_prompt = hp.reshape(bp, seq, D_MODEL)
    y_sample = hs.reshape(bs, SAMPLE_ROWS, D_MODEL)[:, :n_tok]
    return (y_prompt, y_sample,
            stk(p_out, 'C'), stk(p_out, 'n'), stk(p_out, 'm'), stk(p_out, 'conv'),
            stk(p_out, 'kv0'), stk(p_out, 'kv1'), stk(p_out, 'kv2'),
            stk(s_out, 'C'), stk(s_out, 'n'), stk(s_out, 'm'), stk(s_out, 'conv'), stk(s_out, 'chunk_v'),
            stk(s_out, 'kv0'), stk(s_out, 'kv1'), stk(s_out, 'kv2'))
```

```python
import functools

import jax
import jax.numpy as jnp
from jax import lax
from jax.experimental import pallas as pl
from jax.experimental.pallas import tpu as pltpu

F32 = jnp.float32
BF16 = jnp.bfloat16

D_MODEL = 1024
H_A = 4
DH_A = 256
W_A = H_A * DH_A
CONV_W = 4
G_B = 4
CHUNK_B = 128
W_B = 1024
DG_B = W_B // G_B
DIL_PATTERNS = ((128, 1), (512, 4), (2048, 16))
HG_C = 4
DH_C = 64
WG_C = HG_C * DH_C
W_C = len(DIL_PATTERNS) * WG_C
SPAN_C = 128
ROT_DIM = DH_C // 4
ROPE_THETA = 500000.0
EPS = 1e-6
PAST_LEN = 16384

N_PACK = 14336
WBLK = 1024
BLK_AO, BLK_AZ, BLK_BU, BLK_BV, BLK_BZ, BLK_C0, BLK_GA, BLK_GB, BLK_GC = 3, 4, 5, 6, 7, 8, 11, 12, 13
N_P1 = 6144
OFF_CQ, OFF_CK, OFF_CV, OFF_CZ = 3072, 3840, 4608, 5376
GATE_LANES = 128

INPROJ_TM = 2048
POST_TM = 256
MLSTM_L = 256
SAMPLE_ROWS = 8
ATT_TILE = 2048
VMEM_LIMIT = 56 * 1024 * 1024


def _cparams(n_axes):
    return pltpu.CompilerParams(dimension_semantics=("arbitrary",) * n_axes, vmem_limit_bytes=VMEM_LIMIT)


def _dot(a, b):
    return jnp.dot(a.astype(BF16), b.astype(BF16), preferred_element_type=F32)


def _dot_nt(a, b):
    return lax.dot_general(a.astype(BF16), b.astype(BF16), (((1,), (1,)), ((), ())), preferred_element_type=F32)


def _dot_tn(a, b):
    return lax.dot_general(a.astype(BF16), b.astype(BF16), (((0,), (0,)), ((), ())), preferred_element_type=F32)


def _sigmoid(x):
    return 0.5 * jnp.tanh(0.5 * x) + 0.5


def _silu(x):
    return x * _sigmoid(x)


def _inproj_kernel(x_ref, g_ref, w_ref, wg_ref, p_ref, gc_ref, gr_ref, xn_ref):
    nt = (((1,), (1,)), ((), ()))

    @pl.when(pl.program_id(1) == 0)
    def _():
        x = x_ref[...]
        r = lax.rsqrt(jnp.mean(x * x, axis=-1, keepdims=True) + EPS)
        xn = (x * r * g_ref[...]).astype(BF16)
        xn_ref[...] = xn
        wg = wg_ref[...].astype(BF16)
        gc_ref[...] = lax.dot_general(xn, wg, nt, preferred_element_type=F32)
        gr_ref[...] = lax.dot_general(wg[0:8, :], xn, nt, preferred_element_type=F32)

    p_ref[...] = lax.dot_general(xn_ref[...], w_ref[...], nt, preferred_element_type=F32)


def _inproj(x, norm_g, w_main, w_in_t, l):
    rows = x.shape[0]
    gate_blk = 4 * W_A // GATE_LANES
    tm = min(rows, INPROJ_TM)
    tn = WBLK
    n_a = 3 * W_A // tn
    wblk = lambda i, j: (l, jnp.where(j < n_a, j, j + (BLK_C0 - n_a)), 0)
    return pl.pallas_call(
        _inproj_kernel,
        grid=(rows // tm, N_P1 // tn),
        in_specs=[
            pl.BlockSpec((tm, D_MODEL), lambda i, j: (i, 0)),
            pl.BlockSpec((None, 1, D_MODEL), lambda i, j: (l, 0, 0)),
            pl.BlockSpec((None, tn, D_MODEL), wblk),
            pl.BlockSpec((None, GATE_LANES, D_MODEL), lambda i, j: (l, gate_blk, 0)),
        ],
        out_specs=[
            pl.BlockSpec((tm, tn), lambda i, j: (i, j)),
            pl.BlockSpec((tm, GATE_LANES), lambda i, j: (i, 0)),
            pl.BlockSpec((None, 8, tm), lambda i, j: (i, 0, 0)),
        ],
        out_shape=[
            jax.ShapeDtypeStruct((rows, N_P1), F32),
            jax.ShapeDtypeStruct((rows, GATE_LANES), F32),
            jax.ShapeDtypeStruct((rows // tm, 8, tm), F32),
        ],
        scratch_shapes=[pltpu.VMEM((tm, D_MODEL), BF16)],
        compiler_params=_cparams(2),
        name="inproj",
    )(x, norm_g, w_main, w_in_t)


def _pack_kernel(a_ref, b_ref, w_ref, *, first_shifted):
    j = pl.program_id(1)

    @pl.when(j < first_shifted)
    def _():
        w_ref[...] = a_ref[...].astype(BF16)

    @pl.when(j >= first_shifted)
    def _():
        w_ref[...] = jnp.concatenate([a_ref[2 * H_A:, :], b_ref[...]], axis=0).astype(BF16)


def _pack_w_in(w_in_t):
    depth = w_in_t.shape[0]
    tn = 2048
    gate_off = 4 * W_A
    assert gate_off % tn == 0 and w_in_t.shape[1] == N_PACK + 2 * H_A and 2 * H_A == 8
    kern = functools.partial(_pack_kernel, first_shifted=gate_off // tn)
    return pl.pallas_call(
        kern,
        grid=(depth, N_PACK // tn),
        in_specs=[
            pl.BlockSpec((None, tn, D_MODEL), lambda l, j: (l, j, 0)),
            pl.BlockSpec((None, 8, D_MODEL), lambda l, j: (l, (j + 1) * (tn // 8), 0)),
        ],
        out_specs=pl.BlockSpec((None, tn, D_MODEL), lambda l, j: (l, j, 0)),
        out_shape=jax.ShapeDtypeStruct((depth, N_PACK, D_MODEL), BF16),
        compiler_params=_cparams(2),
        name="pack_w_in",
    )(w_in_t, w_in_t)


def _conv_silu(ext_ref, cols, w, b, L):
    y = b
    for back in range(CONV_W):
        y = y + ext_ref[8 - back:8 - back + L, cols] * w[CONV_W - 1 - back:CONV_W - back]
    return _silu(y)


def _mlstm_kernel(p_ref, gc_ref, gr_ref, brow_ref, bcol_ref, cw_ref, cb_ref, c0_ref, n0_ref, m0_ref, tail0_ref,
                  y_ref, c_out, n_out, m_out, tail_out, c_s, n_s, m_s, ext_s, *, L, t_valid):
    ch = pl.program_id(1)

    @pl.when(ch == 0)
    def _():
        c_s[...] = c0_ref[0]
        n_s[...] = n0_ref[0]
        m_s[...] = m0_ref[0]
        ext_s[0:8, :] = tail0_ref[0]

    ext_s[8:8 + L, :] = p_ref[0, :, 0:2 * W_A]

    neg_inf = jnp.float32(-jnp.inf)
    g_col = gc_ref[0] + brow_ref[...]
    g_row = gr_ref[0] + bcol_ref[...]
    ig_col, lf_col = g_col, jax.nn.log_sigmoid(g_col)
    ig_row, lf_row = g_row, jax.nn.log_sigmoid(g_row)
    if t_valid < L:
        vc = lax.broadcasted_iota(jnp.int32, (L, GATE_LANES), 0) < t_valid
        vr = lax.broadcasted_iota(jnp.int32, (8, L), 1) < t_valid
        ig_col, lf_col = jnp.where(vc, ig_col, neg_inf), jnp.where(vc, lf_col, 0.0)
        ig_row, lf_row = jnp.where(vr, ig_row, neg_inf), jnp.where(vr, lf_row, 0.0)
    ti = lax.broadcasted_iota(jnp.int32, (L, L), 0)
    si = lax.broadcasted_iota(jnp.int32, (L, L), 1)
    causal = ti >= si
    b_col = jnp.dot(causal.astype(F32), lf_col, preferred_element_type=F32, precision=lax.Precision.HIGHEST)
    b_row = jnp.dot(lf_row, (ti <= si).astype(F32), preferred_element_type=F32, precision=lax.Precision.HIGHEST)
    last = t_valid - 1

    for h in range(H_A):
        cs = slice(h * DH_A, (h + 1) * DH_A)
        ks = slice(W_A + h * DH_A, W_A + (h + 1) * DH_A)
        q = _conv_silu(ext_s, cs, cw_ref[:, cs], cb_ref[:, cs], L)
        k = _conv_silu(ext_s, ks, cw_ref[:, ks], cb_ref[:, ks], L) * (DH_A ** -0.5)
        v = p_ref[0, :, 2 * W_A + h * DH_A:2 * W_A + (h + 1) * DH_A]
        bc = b_col[:, H_A + h:H_A + h + 1]
        igc = ig_col[:, h:h + 1]
        br = b_row[H_A + h:H_A + h + 1, :]
        igr = ig_row[h:h + 1, :]
        m_prev = m_s[h][:, 0:1]
        c_prev = c_s[h]
        n_prev = n_s[h]

        logw = jnp.where(causal, bc - br + igr, neg_inf)
        inter = bc + m_prev
        m_t = jnp.maximum(inter, jnp.max(logw, axis=1, keepdims=True))
        w_intra = jnp.exp(logw - m_t)
        w_inter = jnp.exp(inter - m_t)
        s = w_intra * _dot_nt(q, k)
        num = _dot(s, v) + w_inter * _dot_nt(q, c_prev)
        den = jnp.sum(s, axis=1, keepdims=True) + w_inter * jnp.sum(q * n_prev, axis=1, keepdims=True)
        hh = num / jnp.maximum(jnp.abs(den), jnp.exp(-m_t))
        y_ref[0, :, cs] = hh

        m_new = m_t[last:last + 1, :]
        b_last = bc[last:last + 1, :]
        decay = jnp.exp(b_last + m_prev - m_new)
        w_s = jnp.exp(b_last - bc + igc - m_new)
        c_s[h] = decay * c_prev + _dot_tn(w_s * v, k)
        n_s[h] = decay * n_prev + jnp.sum(w_s * k, axis=0, keepdims=True)
        m_s[h] = jnp.broadcast_to(m_new, (1, GATE_LANES))

    ext_s[0:8, :] = ext_s[t_valid:t_valid + 8, :]

    @pl.when(ch == pl.num_programs(1) - 1)
    def _():
        c_out[0] = c_s[...]
        n_out[0] = n_s[...]
        m_out[0] = m_s[...]
        tail_out[0] = ext_s[0:8, :]


def _mlstm(p, gcol, grow, bias_row, bias_col, conv_w, conv_b, c0, n0, m0, tail0, l, ls, *, L, t_valid):
    batch = c0.shape[1]
    rows = p.shape[0]
    nch = rows // (batch * L)
    p3 = p.reshape(batch * nch, L, N_P1)
    gc3 = gcol.reshape(batch * nch, L, GATE_LANES)
    tm = grow.shape[2]
    if L % GATE_LANES == 0:
        per = tm // L
        gr3, gr_index = grow, (lambda b, c: ((b * nch + c) // per, 0, (b * nch + c) % per))
    else:
        gr3 = grow.transpose(1, 0, 2).reshape(8, batch * nch, L).transpose(1, 0, 2)
        gr_index = lambda b, c: (b * nch + c, 0, 0)
    kern = functools.partial(_mlstm_kernel, L=L, t_valid=t_valid)
    chunk = lambda b, c: (b * nch + c, 0, 0)
    layer3 = lambda b, c: (l, 0, 0)
    per_b4 = lambda b, c: (b, 0, 0, 0)
    per_b3 = lambda b, c: (b, 0, 0)
    st5 = lambda b, c: (ls, b, 0, 0, 0)
    st4 = lambda b, c: (ls, b, 0, 0)
    y, c1, n1, m1, tail1 = pl.pallas_call(
        kern,
        grid=(batch, nch),
        in_specs=[
            pl.BlockSpec((1, L, 3 * W_A), chunk),
            pl.BlockSpec((1, L, GATE_LANES), chunk),
            pl.BlockSpec((1, 8, L), gr_index),
            pl.BlockSpec((None, 1, GATE_LANES), layer3),
            pl.BlockSpec((None, 8, 1), layer3),
            pl.BlockSpec((None, CONV_W, 2 * W_A), layer3),
            pl.BlockSpec((None, 1, 2 * W_A), layer3),
            pl.BlockSpec((None, 1, H_A, DH_A, DH_A), st5),
            pl.BlockSpec((None, 1, H_A, 1, DH_A), st5),
            pl.BlockSpec((None, 1, H_A, 1, GATE_LANES), st5),
            pl.BlockSpec((None, 1, 8, 2 * W_A), st4),
        ],
        out_specs=[
            pl.BlockSpec((1, L, W_A), chunk),
            pl.BlockSpec((1, H_A, DH_A, DH_A), per_b4),
            pl.BlockSpec((1, H_A, 1, DH_A), per_b4),
            pl.BlockSpec((1, H_A, 1, GATE_LANES), per_b4),
            pl.BlockSpec((1, 8, 2 * W_A), per_b3),
        ],
        out_shape=[
            jax.ShapeDtypeStruct((batch * nch, L, W_A), F32),
            jax.ShapeDtypeStruct((batch, H_A, DH_A, DH_A), F32),
            jax.ShapeDtypeStruct((batch, H_A, 1, DH_A), F32),
            jax.ShapeDtypeStruct((batch, H_A, 1, GATE_LANES), F32),
            jax.ShapeDtypeStruct((batch, 8, 2 * W_A), F32),
        ],
        scratch_shapes=[
            pltpu.VMEM((H_A, DH_A, DH_A), F32),
            pltpu.VMEM((H_A, 1, DH_A), F32),
            pltpu.VMEM((H_A, 1, GATE_LANES), F32),
            pltpu.VMEM((L + 8, 2 * W_A), F32),
        ],
        compiler_params=_cparams(2),
        name="mlstm",
    )(p3, gc3, gr3, bias_row, bias_col, conv_w, conv_b, c0, n0, m0, tail0)
    return y.reshape(rows, W_A), c1, n1[:, :, 0, :], m1[:, :, 0, 0], tail1[:, 8 - (CONV_W - 1):, :]


def _rope(x, cos, sin):
    width = x.shape[1]
    lane = lax.broadcasted_iota(jnp.int32, (1, width), 1) % DH_C
    half = ROT_DIM // 2
    partner = jnp.where(lane < half, pltpu.roll(x, width - half, 1), pltpu.roll(x, half, 1))
    return x * cos + partner * sin


def _dil_attn_kernel(q_ref, k_ref, v_ref, cos_ref, sin_ref, o_ref, lse_ref, kr_ref, qs, kbuf, vbuf, bias, *, d):
    i = pl.program_id(0)
    T = ATT_TILE
    HL = WG_C // 2
    nblk = T // SPAN_C
    nres = T // (SPAN_C * d)
    cur = (i % 2) * T
    prev = T - cur
    neg_inf = jnp.float32(-jnp.inf)

    def split(x):
        return x[:, 0:HL], x[:, HL:WG_C]

    def put(buf, lo, x):
        a, b = split(x)
        buf[0, pl.ds(lo, T), :] = a
        buf[1, pl.ds(lo, T), :] = b

    @pl.when(i == 0)
    def _():
        ii = lax.broadcasted_iota(jnp.int32, (SPAN_C, 2 * SPAN_C), 0)
        jj = lax.broadcasted_iota(jnp.int32, (SPAN_C, 2 * SPAN_C), 1)
        band = (jj >= ii) & (jj <= ii + SPAN_C)
        bias[1] = jnp.where(band, 0.0, neg_inf)
        bias[0] = jnp.where(band & (jj >= SPAN_C), 0.0, neg_inf)
        kbuf[:, T:2 * T, :] = jnp.zeros((2, T, HL), F32)
        vbuf[:, T:2 * T, :] = jnp.zeros((2, T, HL), F32)

    cos = jnp.concatenate([cos_ref[...], cos_ref[...]], axis=1)
    sin = jnp.concatenate([sin_ref[...], sin_ref[...]], axis=1)
    put(qs, 0, _rope(q_ref[...], cos, sin) * (DH_C ** -0.5))
    kr = _rope(k_ref[...], cos, sin)
    kr_ref[...] = kr
    put(kbuf, cur, kr)
    put(vbuf, cur, v_ref[...])

    lane_head = lax.broadcasted_iota(jnp.int32, (SPAN_C, WG_C), 1) // DH_C
    hm = [(lane_head == h).astype(F32) for h in range(HG_C)]

    def rows(start, size):
        return pl.ds(start, size) if d == 1 else pl.ds(start, size, stride=d)

    def take(buf, start, size):
        return jnp.concatenate([buf[0, rows(start, size), :], buf[1, rows(start, size), :]], axis=1)

    def pick(per_head):
        out = per_head[0]
        for h in range(1, HG_C):
            out = jnp.where(lane_head >= h, per_head[h], out)
        return out

    def body(blk, carry):
        r = blk % d
        c = blk // d
        qstart = r + SPAN_C * d * c
        in_tile = cur + qstart
        before = jnp.where(c > 0, in_tile - SPAN_C * d, prev + r + SPAN_C * d * (nres - 1))
        qb = take(qs, qstart, SPAN_C)
        kb = jnp.concatenate([take(kbuf, before, SPAN_C), take(kbuf, in_tile, SPAN_C)], axis=0)
        vb = jnp.concatenate([take(vbuf, before, SPAN_C), take(vbuf, in_tile, SPAN_C)], axis=0)
        q4 = jnp.concatenate([qb * hm[h] for h in range(HG_C)], axis=0)
        has_prev = jnp.logical_or(i > 0, c > 0).astype(jnp.int32)
        s = _dot_nt(q4, kb).reshape(HG_C, SPAN_C, 2 * SPAN_C) + bias[has_prev]
        mx = jnp.max(s, axis=2, keepdims=True)
        p = jnp.exp(s - mx)
        den = jnp.sum(p, axis=2, keepdims=True)
        pv = _dot(p.reshape(HG_C * SPAN_C, 2 * SPAN_C), vb).reshape(HG_C, SPAN_C, WG_C)
        rden = 1.0 / den
        lse = mx + jnp.log(den)
        o = pick([pv[h] for h in range(HG_C)]) * pick([jnp.broadcast_to(rden[h], (SPAN_C, WG_C)) for h in range(HG_C)])
        lse_b = pick([jnp.broadcast_to(lse[h], (SPAN_C, WG_C)) for h in range(HG_C)])
        for half, (oh, lh) in enumerate(zip(split(o), split(lse_b))):
            o_ref[half, rows(qstart, SPAN_C), :] = oh
            lse_ref[half, rows(qstart, SPAN_C), :] = lh
        return carry

    lax.fori_loop(0, nblk, body, 0, unroll=2)


def _dil_attn(p, cos_t, sin_t, g, d):
    rows = p.shape[0]
    T = ATT_TILE
    kern = functools.partial(_dil_attn_kernel, d=d)
    col = lambda off: pl.BlockSpec((T, WG_C), lambda i, off=off: (i, off // WG_C + g))
    tab = pl.BlockSpec((T, 2 * DH_C), lambda i: (i, 0))
    slab = pl.BlockSpec((2, T, WG_C // 2), lambda i: (0, i, 0))
    slab_shape = jax.ShapeDtypeStruct((2, rows, WG_C // 2), F32)
    return pl.pallas_call(
        kern,
        grid=(rows // T,),
        in_specs=[col(OFF_CQ), col(OFF_CK), col(OFF_CV), tab, tab],
        out_specs=[slab, slab, pl.BlockSpec((T, WG_C), lambda i: (i, 0))],
        out_shape=[slab_shape, slab_shape, jax.ShapeDtypeStruct((rows, WG_C), F32)],
        scratch_shapes=[pltpu.VMEM((2, T, WG_C // 2), F32), pltpu.VMEM((2, 2 * T, WG_C // 2), F32),
                        pltpu.VMEM((2, 2 * T, WG_C // 2), F32), pltpu.VMEM((2, SPAN_C, 2 * SPAN_C), F32)],
        compiler_params=_cparams(1),
        name="dil_attn_d%d" % d,
    )(p, p, p, cos_t, sin_t)


def _sample_attn_kernel(pc0, pc1, pc2, cos_ref, sin_ref, c0_ref, c1_ref, c2_ref,
                        o0, o1, o2, l0, l1, l2, kr_ref, *, n_tok):
    pc = jnp.concatenate([pc0[0], pc1[0], pc2[0]], axis=1)
    reps = W_C // (2 * DH_C)
    cos = jnp.concatenate([cos_ref[...]] * reps, axis=1)
    sin = jnp.concatenate([sin_ref[...]] * reps, axis=1)
    qr = _rope(pc[:, 0:W_C], cos, sin) * (DH_C ** -0.5)
    kr = _rope(pc[:, W_C:2 * W_C], cos, sin)
    v = pc[:, 2 * W_C:3 * W_C]
    kr_ref[0] = kr
    nrow = HG_C * SAMPLE_ROWS
    rowh = lax.broadcasted_iota(jnp.int32, (nrow, WG_C), 0) // SAMPLE_ROWS
    laneh = lax.broadcasted_iota(jnp.int32, (nrow, WG_C), 1) // DH_C
    hm = (rowh == laneh).astype(F32)
    tok_col = lax.broadcasted_iota(jnp.int32, (nrow, 1), 0) % SAMPLE_ROWS
    out_row = lax.broadcasted_iota(jnp.int32, (SAMPLE_ROWS, WG_C), 0)
    neg_inf = jnp.float32(-jnp.inf)
    crefs = (c0_ref, c1_ref, c2_ref)
    orefs = (o0, o1, o2)
    lrefs = (l0, l1, l2)
    for g, (_, d) in enumerate(DIL_PATTERNS):
        gs = slice(g * WG_C, (g + 1) * WG_C)
        n_buf = SPAN_C * d
        kn = kr[:, gs]
        vn = v[:, gs]
        qm = jnp.concatenate([qr[:, gs]] * HG_C, axis=0) * hm
        s = _dot(qm, crefs[g][0, 0])
        tok = lax.broadcasted_iota(jnp.int32, (nrow, n_buf), 0) % SAMPLE_ROWS
        pos = lax.broadcasted_iota(jnp.int32, (nrow, n_buf), 1)
        attends = (pos >= tok) if d == 1 else ((pos & (d - 1)) == tok)
        s = jnp.where(attends | (tok >= n_tok), s, neg_inf)
        new_ok = [(u <= tok_col) if d == 1 else (u == tok_col) for u in range(n_tok)]
        s_new = [jnp.where(ok, jnp.sum(qm * kn[u:u + 1, :], axis=1, keepdims=True), neg_inf)
                 for u, ok in enumerate(new_ok)]
        mx = jnp.max(s, axis=1, keepdims=True)
        for sn in s_new:
            mx = jnp.maximum(mx, sn)
        p = jnp.exp(s - mx)
        den = jnp.sum(p, axis=1, keepdims=True)
        acc = _dot_nt(p, crefs[g][0, 1])
        for u, sn in enumerate(s_new):
            pn = jnp.exp(sn - mx)
            den = den + pn
            acc = acc + pn * vn[u:u + 1, :]
        o32 = acc / den * hm
        l32 = (mx + jnp.log(den)) * hm
        o8, l8 = o32[0:SAMPLE_ROWS], l32[0:SAMPLE_ROWS]
        for h in range(1, HG_C):
            o8 = o8 + o32[h * SAMPLE_ROWS:(h + 1) * SAMPLE_ROWS]
            l8 = l8 + l32[h * SAMPLE_ROWS:(h + 1) * SAMPLE_ROWS]
        o8 = jnp.where(out_row < n_tok, o8, 0.0)
        l8 = jnp.where(out_row < n_tok, l8, 0.0)
        for half in range(2):
            hs = slice(half * (WG_C // 2), (half + 1) * (WG_C // 2))
            orefs[g][half] = o8[:, hs]
            lrefs[g][half] = l8[:, hs]


def _cache_views(caches):
    views = []
    for (win, d), c in zip(DIL_PATTERNS, caches):
        depth, batch, n_buf = c.shape[:3]
        assert n_buf == SPAN_C * d
        views.append(jnp.transpose(c, (0, 1, 3, 4, 5, 2)).reshape(depth, batch, 2, WG_C, n_buf))
    return views


def _sample_attn(p, cos_s, sin_s, views, l, n_tok):
    batch = p.shape[0] // SAMPLE_ROWS
    p3 = p.reshape(batch, SAMPLE_ROWS, N_P1)
    pcol = lambda k: pl.BlockSpec((1, SAMPLE_ROWS, 1024), lambda b, k=k: (b, 0, OFF_CQ // 1024 + k))
    tab = pl.BlockSpec((SAMPLE_ROWS, 2 * DH_C), lambda b: (0, 0))
    cache_spec = lambda d: pl.BlockSpec((None, 1, 2, WG_C, SPAN_C * d), lambda b: (l, b, 0, 0, 0))
    out = pl.BlockSpec((2, SAMPLE_ROWS, WG_C // 2), lambda b: (0, b, 0))
    res = pl.pallas_call(
        functools.partial(_sample_attn_kernel, n_tok=n_tok),
        grid=(batch,),
        in_specs=[pcol(0), pcol(1), pcol(2), tab, tab] + [cache_spec(d) for _, d in DIL_PATTERNS],
        out_specs=[out] * 6 + [pl.BlockSpec((1, SAMPLE_ROWS, W_C), lambda b: (b, 0, 0))],
        out_shape=[jax.ShapeDtypeStruct((2, batch * SAMPLE_ROWS, WG_C // 2), F32)] * 6
        + [jax.ShapeDtypeStruct((batch, SAMPLE_ROWS, W_C), F32)],
        compiler_params=_cparams(1),
        name="sample_attn",
    )(p3, p3, p3, cos_s, sin_s, *views)
    return list(res[:3]), list(res[3:6]), res[6].reshape(batch * SAMPLE_ROWS, W_C)


def _post_kernel(x_ref, h_ref, o0, o1, o2, l0, l1, l2, cz_ref, ng_ref, w_ao, w_az, w_bu, w_bv, w_bz, w_ga, w_gb,
                 w_gc, lg_ref, lb_ref, wsp_ref, bsp_ref, wb_ref, wo_ref, fg_ref, out_ref, *rest, nchunks, final,
                 emit_vn):
    if emit_vn:
        vn_out, vn_s, yb_s = rest
    else:
        vn_s, yb_s = rest
    nt = (((1,), (1,)), ((), ()))
    x = x_ref[...]
    r = lax.rsqrt(jnp.mean(x * x, axis=-1, keepdims=True) + EPS)
    xn = (x * r * ng_ref[...]).astype(BF16)
    proj = lambda w_ref: lax.dot_general(xn, w_ref[...], nt, preferred_element_type=F32)
    branch = lambda y, lo, hi: jnp.dot(y.astype(BF16), wb_ref[lo:hi, :], preferred_element_type=F32)

    ya = h_ref[...] * _sigmoid(proj(w_ao)) * _silu(proj(w_az))
    acc = _sigmoid(proj(w_ga)) * branch(ya, 0, W_A)

    v = proj(w_bv)
    mu = jnp.mean(v, axis=-1, keepdims=True)
    var = jnp.mean(jnp.square(v - mu), axis=-1, keepdims=True)
    vn = (v - mu) * lax.rsqrt(var + EPS) * lg_ref[...] + lb_ref[...]
    if emit_vn:
        vn_out[...] = vn
    vn_s[...] = vn.astype(BF16)
    gate_b = proj(w_bu) * _silu(proj(w_bz))
    tri = (lax.broadcasted_iota(jnp.int32, (CHUNK_B, CHUNK_B), 0)
           >= lax.broadcasted_iota(jnp.int32, (CHUNK_B, CHUNK_B), 1))
    for g in range(G_B):
        wg = jnp.where(tri, wsp_ref[g], 0.0).astype(BF16)
        cs = slice(g * DG_B, (g + 1) * DG_B)
        for c in range(nchunks):
            rs = slice(c * CHUNK_B, (c + 1) * CHUNK_B)
            yb_s[rs, cs] = jnp.dot(wg, vn_s[rs, cs], preferred_element_type=F32) + bsp_ref[g]
    acc = acc + _sigmoid(proj(w_gb)) * branch(gate_b * yb_s[...], W_A, W_A + W_B)

    unslab = lambda ref: jnp.concatenate([ref[0], ref[1]], axis=1)
    ls = [unslab(l0), unslab(l1), unslab(l2)]
    mx = jnp.maximum(jnp.maximum(ls[0], ls[1]), ls[2])
    es = [jnp.exp(l - mx) for l in ls]
    inv_tot = 1.0 / (es[0] + es[1] + es[2])
    pc = None
    for g, o_ref in enumerate((o0, o1, o2)):
        gs = slice(g * WG_C, (g + 1) * WG_C)
        yc = unslab(o_ref) * (es[g] * inv_tot) * _silu(cz_ref[:, gs])
        t = branch(yc, W_A + W_B + g * WG_C, W_A + W_B + (g + 1) * WG_C)
        pc = t if pc is None else pc + t
    acc = acc + _sigmoid(proj(w_gc)) * pc

    out = x + jnp.dot(acc.astype(BF16), wo_ref[...], preferred_element_type=F32)
    if final:
        rr = lax.rsqrt(jnp.mean(out * out, axis=-1, keepdims=True) + EPS)
        out = out * rr * fg_ref[...]
    out_ref[...] = out


def _post(x, p1, h, os_, ls_, norm_g, w_main, ln_g, ln_b, w_sp, b_sp_col, w_branch, w_out, final_g, l, final,
          emit_vn):
    rows = x.shape[0]
    tm = min(rows, POST_TM)
    once = pl.Buffered(1)
    row = lambda w: pl.BlockSpec((tm, w), lambda i: (i, 0))
    slab = pl.BlockSpec((2, tm, WG_C // 2), lambda i: (0, i, 0))
    wblk = lambda b: pl.BlockSpec((None, WBLK, D_MODEL), lambda i, b=b: (l, b, 0), pipeline_mode=once)
    vec = lambda w: pl.BlockSpec((None, 1, w), lambda i: (l, 0, 0), pipeline_mode=once)
    in_specs = (
        [row(D_MODEL), row(W_A)] + [slab] * 6
        + [pl.BlockSpec((tm, W_C), lambda i: (i, OFF_CZ // W_C)), vec(D_MODEL)]
        + [wblk(b) for b in (BLK_AO, BLK_AZ, BLK_BU, BLK_BV, BLK_BZ, BLK_GA, BLK_GB, BLK_GC)]
        + [vec(W_B), vec(W_B),
           pl.BlockSpec((None, G_B, CHUNK_B, CHUNK_B), lambda i: (l, 0, 0, 0), pipeline_mode=once),
           pl.BlockSpec((None, G_B, CHUNK_B, 1), lambda i: (l, 0, 0, 0), pipeline_mode=once),
           pl.BlockSpec((None, W_A + W_B + W_C, D_MODEL), lambda i: (l, 0, 0), pipeline_mode=once),
           pl.BlockSpec((None, D_MODEL, D_MODEL), lambda i: (l, 0, 0), pipeline_mode=once),
           pl.BlockSpec((1, D_MODEL), lambda i: (0, 0), pipeline_mode=once)])
    out_specs = [row(D_MODEL)]
    out_shape = [jax.ShapeDtypeStruct((rows, D_MODEL), F32)]
    if emit_vn:
        out_specs.append(row(W_B))
        out_shape.append(jax.ShapeDtypeStruct((rows, W_B), F32))
    res = pl.pallas_call(
        functools.partial(_post_kernel, nchunks=tm // CHUNK_B, final=final, emit_vn=emit_vn),
        grid=(rows // tm,),
        in_specs=in_specs,
        out_specs=out_specs,
        out_shape=out_shape,
        scratch_shapes=[pltpu.VMEM((tm, W_B), BF16), pltpu.VMEM((tm, W_B), F32)],
        compiler_params=_cparams(1),
        name="post",
    )(x, h, *os_, *ls_, p1, norm_g, *([w_main] * 8), ln_g, ln_b, w_sp, b_sp_col, w_branch, w_out, final_g)
    return (res[0], res[1]) if emit_vn else (res[0], None)


def _rope_tables(pos):
    half = ROT_DIM // 2
    inv = ROPE_THETA ** (-jnp.arange(half, dtype=F32) / half)
    ang = pos.astype(F32)[:, None] * inv[None, :]
    cos, sin = jnp.cos(ang), jnp.sin(ang)
    n = pos.shape[0]
    rest = DH_C - ROT_DIM
    cos_h = jnp.concatenate([cos, cos, jnp.ones((n, rest), F32)], axis=1)
    sin_h = jnp.concatenate([-sin, sin, jnp.zeros((n, rest), F32)], axis=1)
    return jnp.concatenate([cos_h, cos_h], axis=1), jnp.concatenate([sin_h, sin_h], axis=1)


def _stack_kv(k_rows, v_rows):
    b, t, _ = k_rows.shape
    return jnp.stack([k_rows.reshape(b, t, HG_C, DH_C), v_rows.reshape(b, t, HG_C, DH_C)], axis=2)


def kernel(x_prompt, x_sample, state_C, state_n, state_m, state_conv, cache_kv_w128, cache_kv_w512, cache_kv_w2048,
           norm_g, w_in, b_igate, b_fgate, conv_w, conv_b, ln_v_g, ln_v_b, w_spatial, b_spatial, w_branch, w_out,
           final_norm_g):
    depth = w_in.shape[0]
    bp, seq, _ = x_prompt.shape
    bs, n_tok, _ = x_sample.shape
    assert bp == 1 and seq % ATT_TILE == 0 and n_tok <= SAMPLE_ROWS // 2 and n_tok >= CONV_W - 1
    caches = (cache_kv_w128, cache_kv_w512, cache_kv_w2048)
    pad_tok = SAMPLE_ROWS - n_tok
    rows_s = bs * SAMPLE_ROWS

    w_in_t = jnp.swapaxes(w_in, 1, 2)
    w_main = _pack_w_in(w_in_t)
    gate_bias = jnp.concatenate([b_igate, b_fgate], axis=1)
    bias_row = jnp.pad(gate_bias, ((0, 0), (0, GATE_LANES - 2 * H_A)))[:, None, :]
    bias_col = gate_bias[:, :, None]
    wb16 = w_branch.astype(BF16)
    wo16 = w_out.astype(BF16)
    b_sp_col = b_spatial[..., None]
    norm_g3, conv_b3 = norm_g[:, None, :], conv_b[:, None, :]
    ln_g3, ln_b3 = ln_v_g[:, None, :], ln_v_b[:, None, :]
    seqs_per_chunk = CHUNK_B // SAMPLE_ROWS
    w8 = jnp.pad(w_spatial[:, :, :n_tok, :n_tok], ((0, 0), (0, 0), (0, pad_tok), (0, pad_tok)))
    w_sp_s = jnp.einsum('ab,lgij->lgaibj', jnp.eye(seqs_per_chunk, dtype=F32), w8).reshape(
        depth, G_B, CHUNK_B, CHUNK_B)
    b_sp_s = jnp.tile(jnp.pad(b_spatial[:, :, :n_tok], ((0, 0), (0, 0), (0, pad_tok))),
                      (1, 1, seqs_per_chunk))[..., None]

    cos_p, sin_p = _rope_tables(jnp.arange(seq))
    cos_s, sin_s = _rope_tables(PAST_LEN + jnp.arange(SAMPLE_ROWS))

    hp = x_prompt.reshape(seq, D_MODEL)
    hs = jnp.pad(x_sample, ((0, 0), (0, pad_tok), (0, 0))).reshape(rows_s, D_MODEL)
    fg = final_norm_g[None, :]
    zeros_c = jnp.zeros((1, 1, H_A, DH_A, DH_A), F32)
    zeros_n = jnp.zeros((1, 1, H_A, 1, DH_A), F32)
    zeros_m = jnp.zeros((1, 1, H_A, 1, GATE_LANES), F32)
    zeros_tail = jnp.zeros((1, 1, 8, 2 * W_A), F32)
    n0_s = state_n[:, :, :, None, :]
    m0_s = jnp.broadcast_to(state_m[:, :, :, None, None], (depth, bs, H_A, 1, GATE_LANES))
    tail0_s = jnp.pad(state_conv, ((0, 0), (0, 0), (8 - (CONV_W - 1), 0), (0, 0)))
    views = _cache_views(caches)

    p_out = {k: [] for k in ('C', 'n', 'm', 'conv', 'kv0', 'kv1', 'kv2')}
    s_out = {k: [] for k in ('C', 'n', 'm', 'conv', 'chunk_v', 'kv0', 'kv1', 'kv2')}
    for l in range(depth):
        final = l == depth - 1

        pp, gcol, grow = _inproj(hp, norm_g3, w_main, w_in_t, l)
        ya, c1, n1, m1, conv1 = _mlstm(pp, gcol, grow, bias_row, bias_col, conv_w, conv_b3,
                                       zeros_c, zeros_n, zeros_m, zeros_tail, l, 0, L=MLSTM_L, t_valid=MLSTM_L)
        os_, ls_, krs = [], [], []
        for g, (win, d) in enumerate(DIL_PATTERNS):
            o_g, l_g, kr_g = _dil_attn(pp, cos_p, sin_p, g, d)
            os_.append(o_g)
            ls_.append(l_g)
            keep = min(win, seq)
            v_g = pp[seq - keep:, OFF_CV + g * WG_C:OFF_CV + (g + 1) * WG_C]
            p_out['kv%d' % g].append(_stack_kv(kr_g[None, seq - keep:], v_g[None]))
        hp, _ = _post(hp, pp, ya, os_, ls_, norm_g3, w_main, ln_g3, ln_b3, w_spatial, b_sp_col, wb16, wo16, fg, l,
                      final, False)
        p_out['C'].append(c1)
        p_out['n'].append(n1)
        p_out['m'].append(m1)
        p_out['conv'].append(conv1)

        ps, gcol, grow = _inproj(hs, norm_g3, w_main, w_in_t, l)
        ya, c1, n1, m1, conv1 = _mlstm(ps, gcol, grow, bias_row, bias_col, conv_w, conv_b3,
                                       state_C, n0_s, m0_s, tail0_s, l, l, L=SAMPLE_ROWS, t_valid=n_tok)
        os_, ls_, kr = _sample_attn(ps, cos_s, sin_s, views, l, n_tok)
        hs, vn = _post(hs, ps, ya, os_, ls_, norm_g3, w_main, ln_g3, ln_b3, w_sp_s, b_sp_s, wb16, wo16, fg, l,
                       final, True)
        s_out['C'].append(c1)
        s_out['n'].append(n1)
        s_out['m'].append(m1)
        s_out['conv'].append(conv1)
        s_out['chunk_v'].append(vn.reshape(bs, SAMPLE_ROWS, W_B)[:, :n_tok])
        kr3 = kr.reshape(bs, SAMPLE_ROWS, W_C)[:, :n_tok]
        v3 = ps[:, OFF_CV:OFF_CV + W_C].reshape(bs, SAMPLE_ROWS, W_C)[:, :n_tok]
        for g in range(len(DIL_PATTERNS)):
            gs = slice(g * WG_C, (g + 1) * WG_C)
            s_out['kv%d' % g].append(_stack_kv(kr3[:, :, gs], v3[:, :, gs]))

    stk = lambda d, k: jnp.stack(d[k], axis=0)
    y_prompt = hp.reshape(bp, seq, D_MODEL)
    y_sample = hs.reshape(bs, SAMPLE_ROWS, D_MODEL)[:, :n_tok]
    return (y_prompt, y_sample,
            stk(p_out, 'C'), stk(p_out, 'n'), stk(p_out, 'm'), stk(p_out, 'conv'),
            stk(p_out, 'kv0'), stk(p_out, 'kv1'), stk(p_out, 'kv2'),
            stk(s_out, 'C'), stk(s_out, 'n'), stk(s_out, 'm'), stk(s_out, 'conv'), stk(s_out, 'chunk_v'),
            stk(s_out, 'kv0'), stk(s_out, 'kv1'), stk(s_out, 'kv2'))
```

```python
import functools

import jax
import jax.numpy as jnp
from jax import lax
from jax.experimental import pallas as pl
from jax.experimental.pallas import tpu as pltpu

F32 = jnp.float32
BF16 = jnp.bfloat16

D_MODEL = 1024
H_A = 4
DH_A = 256
W_A = H_A * DH_A
CONV_W = 4
G_B = 4
CHUNK_B = 128
W_B = 1024
DG_B = W_B // G_B
DIL_PATTERNS = ((128, 1), (512, 4), (2048, 16))
HG_C = 4
DH_C = 64
WG_C = HG_C * DH_C
W_C = len(DIL_PATTERNS) * WG_C
SPAN_C = 128
ROT_DIM = DH_C // 4
ROPE_THETA = 500000.0
EPS = 1e-6
PAST_LEN = 16384

N_PACK = 14336
WBLK = 1024
BLK_AO, BLK_AZ, BLK_BU, BLK_BV, BLK_BZ, BLK_C0, BLK_GA, BLK_GB, BLK_GC = 3, 4, 5, 6, 7, 8, 11, 12, 13
N_P1 = 6144
OFF_CQ, OFF_CK, OFF_CV, OFF_CZ = 3072, 3840, 4608, 5376
GATE_LANES = 128

INPROJ_TM = 2048
POST_TM = 512
MLSTM_L = 256
SAMPLE_ROWS = 8
ATT_TILE = 2048
VMEM_LIMIT = 56 * 1024 * 1024


def _cparams(n_axes):
    return pltpu.CompilerParams(dimension_semantics=("arbitrary",) * n_axes, vmem_limit_bytes=VMEM_LIMIT)


def _dot(a, b):
    return jnp.dot(a.astype(BF16), b.astype(BF16), preferred_element_type=F32)


def _dot_nt(a, b):
    return lax.dot_general(a.astype(BF16), b.astype(BF16), (((1,), (1,)), ((), ())), preferred_element_type=F32)


def _dot_tn(a, b):
    return lax.dot_general(a.astype(BF16), b.astype(BF16), (((0,), (0,)), ((), ())), preferred_element_type=F32)


def _sigmoid(x):
    return 0.5 * jnp.tanh(0.5 * x) + 0.5


def _silu(x):
    return x * _sigmoid(x)


def _inproj_kernel(x_ref, g_ref, w_ref, wg_ref, p_ref, gc_ref, gr_ref, xn_ref):
    nt = (((1,), (1,)), ((), ()))

    @pl.when(pl.program_id(1) == 0)
    def _():
        x = x_ref[...]
        r = lax.rsqrt(jnp.mean(x * x, axis=-1, keepdims=True) + EPS)
        xn = (x * r * g_ref[...]).astype(BF16)
        xn_ref[...] = xn
        wg = wg_ref[...].astype(BF16)
        gc_ref[...] = lax.dot_general(xn, wg, nt, preferred_element_type=F32)
        gr_ref[...] = lax.dot_general(wg[0:8, :], xn, nt, preferred_element_type=F32)

    p_ref[...] = lax.dot_general(xn_ref[...], w_ref[...], nt, preferred_element_type=F32)


def _inproj(x, norm_g, w_main, w_in_t, l):
    rows = x.shape[0]
    gate_blk = 4 * W_A // GATE_LANES
    tm = min(rows, INPROJ_TM)
    tn = WBLK
    n_a = 3 * W_A // tn
    wblk = lambda i, j: (l, jnp.where(j < n_a, j, j + (BLK_C0 - n_a)), 0)
    return pl.pallas_call(
        _inproj_kernel,
        grid=(rows // tm, N_P1 // tn),
        in_specs=[
            pl.BlockSpec((tm, D_MODEL), lambda i, j: (i, 0)),
            pl.BlockSpec((None, 1, D_MODEL), lambda i, j: (l, 0, 0)),
            pl.BlockSpec((None, tn, D_MODEL), wblk),
            pl.BlockSpec((None, GATE_LANES, D_MODEL), lambda i, j: (l, gate_blk, 0)),
        ],
        out_specs=[
            pl.BlockSpec((tm, tn), lambda i, j: (i, j)),
            pl.BlockSpec((tm, GATE_LANES), lambda i, j: (i, 0)),
            pl.BlockSpec((None, 8, tm), lambda i, j: (i, 0, 0)),
        ],
        out_shape=[
            jax.ShapeDtypeStruct((rows, N_P1), F32),
            jax.ShapeDtypeStruct((rows, GATE_LANES), F32),
            jax.ShapeDtypeStruct((rows // tm, 8, tm), F32),
        ],
        scratch_shapes=[pltpu.VMEM((tm, D_MODEL), BF16)],
        compiler_params=_cparams(2),
        name="inproj",
    )(x, norm_g, w_main, w_in_t)


def _pack_kernel(a_ref, b_ref, w_ref, *, first_shifted):
    j = pl.program_id(1)

    @pl.when(j < first_shifted)
    def _():
        w_ref[...] = a_ref[...].astype(BF16)

    @pl.when(j >= first_shifted)
    def _():
        w_ref[...] = jnp.concatenate([a_ref[2 * H_A:, :], b_ref[...]], axis=0).astype(BF16)


def _pack_w_in(w_in_t):
    depth = w_in_t.shape[0]
    tn = 2048
    gate_off = 4 * W_A
    assert gate_off % tn == 0 and w_in_t.shape[1] == N_PACK + 2 * H_A and 2 * H_A == 8
    kern = functools.partial(_pack_kernel, first_shifted=gate_off // tn)
    return pl.pallas_call(
        kern,
        grid=(depth, N_PACK // tn),
        in_specs=[
            pl.BlockSpec((None, tn, D_MODEL), lambda l, j: (l, j, 0)),
            pl.BlockSpec((None, 8, D_MODEL), lambda l, j: (l, (j + 1) * (tn // 8), 0)),
        ],
        out_specs=pl.BlockSpec((None, tn, D_MODEL), lambda l, j: (l, j, 0)),
        out_shape=jax.ShapeDtypeStruct((depth, N_PACK, D_MODEL), BF16),
        compiler_params=_cparams(2),
        name="pack_w_in",
    )(w_in_t, w_in_t)


def _conv_silu(ext_ref, cols, w, b, L):
    y = b
    for back in range(CONV_W):
        y = y + ext_ref[8 - back:8 - back + L, cols] * w[CONV_W - 1 - back:CONV_W - back]
    return _silu(y)


def _mlstm_kernel(p_ref, gc_ref, gr_ref, brow_ref, bcol_ref, cw_ref, cb_ref, c0_ref, n0_ref, m0_ref, tail0_ref,
                  y_ref, c_out, n_out, m_out, tail_out, c_s, n_s, m_s, ext_s, *, L, t_valid):
    ch = pl.program_id(1)

    @pl.when(ch == 0)
    def _():
        c_s[...] = c0_ref[0]
        n_s[...] = n0_ref[0]
        m_s[...] = m0_ref[0]
        ext_s[0:8, :] = tail0_ref[0]

    ext_s[8:8 + L, :] = p_ref[0, :, 0:2 * W_A]

    neg_inf = jnp.float32(-jnp.inf)
    g_col = gc_ref[0] + brow_ref[...]
    g_row = gr_ref[0] + bcol_ref[...]
    ig_col, lf_col = g_col, jax.nn.log_sigmoid(g_col)
    ig_row, lf_row = g_row, jax.nn.log_sigmoid(g_row)
    if t_valid < L:
        vc = lax.broadcasted_iota(jnp.int32, (L, GATE_LANES), 0) < t_valid
        vr = lax.broadcasted_iota(jnp.int32, (8, L), 1) < t_valid
        ig_col, lf_col = jnp.where(vc, ig_col, neg_inf), jnp.where(vc, lf_col, 0.0)
        ig_row, lf_row = jnp.where(vr, ig_row, neg_inf), jnp.where(vr, lf_row, 0.0)
    ti = lax.broadcasted_iota(jnp.int32, (L, L), 0)
    si = lax.broadcasted_iota(jnp.int32, (L, L), 1)
    causal = ti >= si
    b_col = jnp.dot(causal.astype(F32), lf_col, preferred_element_type=F32, precision=lax.Precision.HIGHEST)
    b_row = jnp.dot(lf_row, (ti <= si).astype(F32), preferred_element_type=F32, precision=lax.Precision.HIGHEST)
    last = t_valid - 1

    for h in range(H_A):
        cs = slice(h * DH_A, (h + 1) * DH_A)
        ks = slice(W_A + h * DH_A, W_A + (h + 1) * DH_A)
        q = _conv_silu(ext_s, cs, cw_ref[:, cs], cb_ref[:, cs], L)
        k = _conv_silu(ext_s, ks, cw_ref[:, ks], cb_ref[:, ks], L) * (DH_A ** -0.5)
        v = p_ref[0, :, 2 * W_A + h * DH_A:2 * W_A + (h + 1) * DH_A]
        bc = b_col[:, H_A + h:H_A + h + 1]
        igc = ig_col[:, h:h + 1]
        br = b_row[H_A + h:H_A + h + 1, :]
        igr = ig_row[h:h + 1, :]
        m_prev = m_s[h][:, 0:1]
        c_prev = c_s[h]
        n_prev = n_s[h]

        logw = jnp.where(causal, bc - br + igr, neg_inf)
        inter = bc + m_prev
        m_t = jnp.maximum(inter, jnp.max(logw, axis=1, keepdims=True))
        w_intra = jnp.exp(logw - m_t)
        w_inter = jnp.exp(inter - m_t)
        s = w_intra * _dot_nt(q, k)
        num = _dot(s, v) + w_inter * _dot_nt(q, c_prev)
        den = jnp.sum(s, axis=1, keepdims=True) + w_inter * jnp.sum(q * n_prev, axis=1, keepdims=True)
        hh = num / jnp.maximum(jnp.abs(den), jnp.exp(-m_t))
        y_ref[0, :, cs] = hh

        m_new = m_t[last:last + 1, :]
        b_last = bc[last:last + 1, :]
        decay = jnp.exp(b_last + m_prev - m_new)
        w_s = jnp.exp(b_last - bc + igc - m_new)
        c_s[h] = decay * c_prev + _dot_tn(w_s * v, k)
        n_s[h] = decay * n_prev + jnp.sum(w_s * k, axis=0, keepdims=True)
        m_s[h] = jnp.broadcast_to(m_new, (1, GATE_LANES))

    ext_s[0:8, :] = ext_s[t_valid:t_valid + 8, :]

    @pl.when(ch == pl.num_programs(1) - 1)
    def _():
        c_out[0] = c_s[...]
        n_out[0] = n_s[...]
        m_out[0] = m_s[...]
        tail_out[0] = ext_s[0:8, :]


def _mlstm(p, gcol, grow, bias_row, bias_col, conv_w, conv_b, c0, n0, m0, tail0, l, ls, *, L, t_valid):
    batch = c0.shape[1]
    rows = p.shape[0]
    nch = rows // (batch * L)
    p3 = p.reshape(batch * nch, L, N_P1)
    gc3 = gcol.reshape(batch * nch, L, GATE_LANES)
    tm = grow.shape[2]
    if L % GATE_LANES == 0:
        per = tm // L
        gr3, gr_index = grow, (lambda b, c: ((b * nch + c) // per, 0, (b * nch + c) % per))
    else:
        gr3 = grow.transpose(1, 0, 2).reshape(8, batch * nch, L).transpose(1, 0, 2)
        gr_index = lambda b, c: (b * nch + c, 0, 0)
    kern = functools.partial(_mlstm_kernel, L=L, t_valid=t_valid)
    chunk = lambda b, c: (b * nch + c, 0, 0)
    layer3 = lambda b, c: (l, 0, 0)
    per_b4 = lambda b, c: (b, 0, 0, 0)
    per_b3 = lambda b, c: (b, 0, 0)
    st5 = lambda b, c: (ls, b, 0, 0, 0)
    st4 = lambda b, c: (ls, b, 0, 0)
    y, c1, n1, m1, tail1 = pl.pallas_call(
        kern,
        grid=(batch, nch),
        in_specs=[
            pl.BlockSpec((1, L, 3 * W_A), chunk),
            pl.BlockSpec((1, L, GATE_LANES), chunk),
            pl.BlockSpec((1, 8, L), gr_index),
            pl.BlockSpec((None, 1, GATE_LANES), layer3),
            pl.BlockSpec((None, 8, 1), layer3),
            pl.BlockSpec((None, CONV_W, 2 * W_A), layer3),
            pl.BlockSpec((None, 1, 2 * W_A), layer3),
            pl.BlockSpec((None, 1, H_A, DH_A, DH_A), st5),
            pl.BlockSpec((None, 1, H_A, 1, DH_A), st5),
            pl.BlockSpec((None, 1, H_A, 1, GATE_LANES), st5),
            pl.BlockSpec((None, 1, 8, 2 * W_A), st4),
        ],
        out_specs=[
            pl.BlockSpec((1, L, W_A), chunk),
            pl.BlockSpec((1, H_A, DH_A, DH_A), per_b4),
            pl.BlockSpec((1, H_A, 1, DH_A), per_b4),
            pl.BlockSpec((1, H_A, 1, GATE_LANES), per_b4),
            pl.BlockSpec((1, 8, 2 * W_A), per_b3),
        ],
        out_shape=[
            jax.ShapeDtypeStruct((batch * nch, L, W_A), F32),
            jax.ShapeDtypeStruct((batch, H_A, DH_A, DH_A), F32),
            jax.ShapeDtypeStruct((batch, H_A, 1, DH_A), F32),
            jax.ShapeDtypeStruct((batch, H_A, 1, GATE_LANES), F32),
            jax.ShapeDtypeStruct((batch, 8, 2 * W_A), F32),
        ],
        scratch_shapes=[
            pltpu.VMEM((H_A, DH_A, DH_A), F32),
            pltpu.VMEM((H_A, 1, DH_A), F32),
            pltpu.VMEM((H_A, 1, GATE_LANES), F32),
            pltpu.VMEM((L + 8, 2 * W_A), F32),
        ],
        compiler_params=_cparams(2),
        name="mlstm",
    )(p3, gc3, gr3, bias_row, bias_col, conv_w, conv_b, c0, n0, m0, tail0)
    return y.reshape(rows, W_A), c1, n1[:, :, 0, :], m1[:, :, 0, 0], tail1[:, 8 - (CONV_W - 1):, :]


def _rope(x, cos, sin):
    width = x.shape[1]
    lane = lax.broadcasted_iota(jnp.int32, (1, width), 1) % DH_C
    half = ROT_DIM // 2
    partner = jnp.where(lane < half, pltpu.roll(x, width - half, 1), pltpu.roll(x, half, 1))
    return x * cos + partner * sin


def _dil_attn_kernel(q_ref, k_ref, v_ref, cos_ref, sin_ref, o_ref, lse_ref, kr_ref, qs, kbuf, vbuf, bias, *, d):
    i = pl.program_id(0)
    T = ATT_TILE
    HL = WG_C // 2
    nblk = T // SPAN_C
    nres = T // (SPAN_C * d)
    cur = (i % 2) * T
    prev = T - cur
    neg_inf = jnp.float32(-jnp.inf)

    def split(x):
        return x[:, 0:HL], x[:, HL:WG_C]

    def put(buf, lo, x):
        a, b = split(x)
        buf[0, pl.ds(lo, T), :] = a
        buf[1, pl.ds(lo, T), :] = b

    @pl.when(i == 0)
    def _():
        ii = lax.broadcasted_iota(jnp.int32, (SPAN_C, 2 * SPAN_C), 0)
        jj = lax.broadcasted_iota(jnp.int32, (SPAN_C, 2 * SPAN_C), 1)
        band = (jj >= ii) & (jj <= ii + SPAN_C)
        bias[1] = jnp.where(band, 0.0, neg_inf)
        bias[0] = jnp.where(band & (jj >= SPAN_C), 0.0, neg_inf)
        kbuf[:, T:2 * T, :] = jnp.zeros((2, T, HL), F32)
        vbuf[:, T:2 * T, :] = jnp.zeros((2, T, HL), F32)

    cos = jnp.concatenate([cos_ref[...], cos_ref[...]], axis=1)
    sin = jnp.concatenate([sin_ref[...], sin_ref[...]], axis=1)
    put(qs, 0, _rope(q_ref[...], cos, sin) * (DH_C ** -0.5))
    kr = _rope(k_ref[...], cos, sin)
    kr_ref[...] = kr
    put(kbuf, cur, kr)
    put(vbuf, cur, v_ref[...])

    lane_head = lax.broadcasted_iota(jnp.int32, (SPAN_C, WG_C), 1) // DH_C
    hm = [(lane_head == h).astype(F32) for h in range(HG_C)]

    def rows(start, size):
        return pl.ds(start, size) if d == 1 else pl.ds(start, size, stride=d)

    def take(buf, start, size):
        return jnp.concatenate([buf[0, rows(start, size), :], buf[1, rows(start, size), :]], axis=1)

    def pick(per_head):
        out = per_head[0]
        for h in range(1, HG_C):
            out = jnp.where(lane_head >= h, per_head[h], out)
        return out

    def body(blk, carry):
        r = blk % d
        c = blk // d
        qstart = r + SPAN_C * d * c
        in_tile = cur + qstart
        before = jnp.where(c > 0, in_tile - SPAN_C * d, prev + r + SPAN_C * d * (nres - 1))
        qb = take(qs, qstart, SPAN_C)
        kb = jnp.concatenate([take(kbuf, before, SPAN_C), take(kbuf, in_tile, SPAN_C)], axis=0)
        vb = jnp.concatenate([take(vbuf, before, SPAN_C), take(vbuf, in_tile, SPAN_C)], axis=0)
        q4 = jnp.concatenate([qb * hm[h] for h in range(HG_C)], axis=0)
        has_prev = jnp.logical_or(i > 0, c > 0).astype(jnp.int32)
        s = _dot_nt(q4, kb).reshape(HG_C, SPAN_C, 2 * SPAN_C) + bias[has_prev]
        mx = jnp.max(s, axis=2, keepdims=True)
        p = jnp.exp(s - mx)
        den = jnp.sum(p, axis=2, keepdims=True)
        pv = _dot(p.reshape(HG_C * SPAN_C, 2 * SPAN_C), vb).reshape(HG_C, SPAN_C, WG_C)
        rden = 1.0 / den
        lse = mx + jnp.log(den)
        o = pick([pv[h] for h in range(HG_C)]) * pick([jnp.broadcast_to(rden[h], (SPAN_C, WG_C)) for h in range(HG_C)])
        lse_b = pick([jnp.broadcast_to(lse[h], (SPAN_C, WG_C)) for h in range(HG_C)])
        for half, (oh, lh) in enumerate(zip(split(o), split(lse_b))):
            o_ref[half, rows(qstart, SPAN_C), :] = oh
            lse_ref[half, rows(qstart, SPAN_C), :] = lh
        return carry

    lax.fori_loop(0, nblk, body, 0, unroll=2)


def _dil_attn(p, cos_t, sin_t, g, d):
    rows = p.shape[0]
    T = ATT_TILE
    kern = functools.partial(_dil_attn_kernel, d=d)
    col = lambda off: pl.BlockSpec((T, WG_C), lambda i, off=off: (i, off // WG_C + g))
    tab = pl.BlockSpec((T, 2 * DH_C), lambda i: (i, 0))
    slab = pl.BlockSpec((2, T, WG_C // 2), lambda i: (0, i, 0))
    slab_shape = jax.ShapeDtypeStruct((2, rows, WG_C // 2), F32)
    return pl.pallas_call(
        kern,
        grid=(rows // T,),
        in_specs=[col(OFF_CQ), col(OFF_CK), col(OFF_CV), tab, tab],
        out_specs=[slab, slab, pl.BlockSpec((T, WG_C), lambda i: (i, 0))],
        out_shape=[slab_shape, slab_shape, jax.ShapeDtypeStruct((rows, WG_C), F32)],
        scratch_shapes=[pltpu.VMEM((2, T, WG_C // 2), F32), pltpu.VMEM((2, 2 * T, WG_C // 2), F32),
                        pltpu.VMEM((2, 2 * T, WG_C // 2), F32), pltpu.VMEM((2, SPAN_C, 2 * SPAN_C), F32)],
        compiler_params=_cparams(1),
        name="dil_attn_d%d" % d,
    )(p, p, p, cos_t, sin_t)


def _sample_attn_kernel(pc0, pc1, pc2, cos_ref, sin_ref, c0_ref, c1_ref, c2_ref,
                        o0, o1, o2, l0, l1, l2, kr_ref, *, n_tok):
    pc = jnp.concatenate([pc0[0], pc1[0], pc2[0]], axis=1)
    reps = W_C // (2 * DH_C)
    cos = jnp.concatenate([cos_ref[...]] * reps, axis=1)
    sin = jnp.concatenate([sin_ref[...]] * reps, axis=1)
    qr = _rope(pc[:, 0:W_C], cos, sin) * (DH_C ** -0.5)
    kr = _rope(pc[:, W_C:2 * W_C], cos, sin)
    v = pc[:, 2 * W_C:3 * W_C]
    kr_ref[0] = kr
    nrow = HG_C * SAMPLE_ROWS
    rowh = lax.broadcasted_iota(jnp.int32, (nrow, WG_C), 0) // SAMPLE_ROWS
    laneh = lax.broadcasted_iota(jnp.int32, (nrow, WG_C), 1) // DH_C
    hm = (rowh == laneh).astype(F32)
    tok_col = lax.broadcasted_iota(jnp.int32, (nrow, 1), 0) % SAMPLE_ROWS
    out_row = lax.broadcasted_iota(jnp.int32, (SAMPLE_ROWS, WG_C), 0)
    neg_inf = jnp.float32(-jnp.inf)
    crefs = (c0_ref, c1_ref, c2_ref)
    orefs = (o0, o1, o2)
    lrefs = (l0, l1, l2)
    for g, (_, d) in enumerate(DIL_PATTERNS):
        gs = slice(g * WG_C, (g + 1) * WG_C)
        n_buf = SPAN_C * d
        kn = kr[:, gs]
        vn = v[:, gs]
        qm = jnp.concatenate([qr[:, gs]] * HG_C, axis=0) * hm
        s = _dot(qm, crefs[g][0, 0])
        tok = lax.broadcasted_iota(jnp.int32, (nrow, n_buf), 0) % SAMPLE_ROWS
        pos = lax.broadcasted_iota(jnp.int32, (nrow, n_buf), 1)
        attends = (pos >= tok) if d == 1 else ((pos & (d - 1)) == tok)
        s = jnp.where(attends | (tok >= n_tok), s, neg_inf)
        new_ok = [(u <= tok_col) if d == 1 else (u == tok_col) for u in range(n_tok)]
        s_new = [jnp.where(ok, jnp.sum(qm * kn[u:u + 1, :], axis=1, keepdims=True), neg_inf)
                 for u, ok in enumerate(new_ok)]
        mx = jnp.max(s, axis=1, keepdims=True)
        for sn in s_new:
            mx = jnp.maximum(mx, sn)
        p = jnp.exp(s - mx)
        den = jnp.sum(p, axis=1, keepdims=True)
        acc = _dot_nt(p, crefs[g][0, 1])
        for u, sn in enumerate(s_new):
            pn = jnp.exp(sn - mx)
            den = den + pn
            acc = acc + pn * vn[u:u + 1, :]
        o32 = acc / den * hm
        l32 = (mx + jnp.log(den)) * hm
        o8, l8 = o32[0:SAMPLE_ROWS], l32[0:SAMPLE_ROWS]
        for h in range(1, HG_C):
            o8 = o8 + o32[h * SAMPLE_ROWS:(h + 1) * SAMPLE_ROWS]
            l8 = l8 + l32[h * SAMPLE_ROWS:(h + 1) * SAMPLE_ROWS]
        o8 = jnp.where(out_row < n_tok, o8, 0.0)
        l8 = jnp.where(out_row < n_tok, l8, 0.0)
        for half in range(2):
            hs = slice(half * (WG_C // 2), (half + 1) * (WG_C // 2))
            orefs[g][half] = o8[:, hs]
            lrefs[g][half] = l8[:, hs]


def _cache_views(caches):
    views = []
    for (win, d), c in zip(DIL_PATTERNS, caches):
        depth, batch, n_buf = c.shape[:3]
        assert n_buf == SPAN_C * d
        views.append(jnp.transpose(c, (0, 1, 3, 4, 5, 2)).reshape(depth, batch, 2, WG_C, n_buf))
    return views


def _sample_attn(p, cos_s, sin_s, views, l, n_tok):
    batch = p.shape[0] // SAMPLE_ROWS
    p3 = p.reshape(batch, SAMPLE_ROWS, N_P1)
    pcol = lambda k: pl.BlockSpec((1, SAMPLE_ROWS, 1024), lambda b, k=k: (b, 0, OFF_CQ // 1024 + k))
    tab = pl.BlockSpec((SAMPLE_ROWS, 2 * DH_C), lambda b: (0, 0))
    cache_spec = lambda d: pl.BlockSpec((None, 1, 2, WG_C, SPAN_C * d), lambda b: (l, b, 0, 0, 0))
    out = pl.BlockSpec((2, SAMPLE_ROWS, WG_C // 2), lambda b: (0, b, 0))
    res = pl.pallas_call(
        functools.partial(_sample_attn_kernel, n_tok=n_tok),
        grid=(batch,),
        in_specs=[pcol(0), pcol(1), pcol(2), tab, tab] + [cache_spec(d) for _, d in DIL_PATTERNS],
        out_specs=[out] * 6 + [pl.BlockSpec((1, SAMPLE_ROWS, W_C), lambda b: (b, 0, 0))],
        out_shape=[jax.ShapeDtypeStruct((2, batch * SAMPLE_ROWS, WG_C // 2), F32)] * 6
        + [jax.ShapeDtypeStruct((batch, SAMPLE_ROWS, W_C), F32)],
        compiler_params=_cparams(1),
        name="sample_attn",
    )(p3, p3, p3, cos_s, sin_s, *views)
    return list(res[:3]), list(res[3:6]), res[6].reshape(batch * SAMPLE_ROWS, W_C)


def _post_kernel(x_ref, h_ref, o0, o1, o2, l0, l1, l2, cz_ref, ng_ref, w_ao, w_az, w_bu, w_bv, w_bz, w_ga, w_gb,
                 w_gc, lg_ref, lb_ref, wsp_ref, bsp_ref, wb_ref, wo_ref, fg_ref, out_ref, *rest, nchunks, final,
                 emit_vn):
    if emit_vn:
        vn_out, vn_s, yb_s = rest
    else:
        vn_s, yb_s = rest
    nt = (((1,), (1,)), ((), ()))
    x = x_ref[...]
    r = lax.rsqrt(jnp.mean(x * x, axis=-1, keepdims=True) + EPS)
    xn = (x * r * ng_ref[...]).astype(BF16)
    proj = lambda w_ref: lax.dot_general(xn, w_ref[...], nt, preferred_element_type=F32)
    branch = lambda y, lo, hi: jnp.dot(y.astype(BF16), wb_ref[lo:hi, :], preferred_element_type=F32)

    ya = h_ref[...] * _sigmoid(proj(w_ao)) * _silu(proj(w_az))
    acc = _sigmoid(proj(w_ga)) * branch(ya, 0, W_A)

    v = proj(w_bv)
    mu = jnp.mean(v, axis=-1, keepdims=True)
    var = jnp.mean(jnp.square(v - mu), axis=-1, keepdims=True)
    vn = (v - mu) * lax.rsqrt(var + EPS) * lg_ref[...] + lb_ref[...]
    if emit_vn:
        vn_out[...] = vn
    vn_s[...] = vn.astype(BF16)
    gate_b = proj(w_bu) * _silu(proj(w_bz))
    tri = (lax.broadcasted_iota(jnp.int32, (CHUNK_B, CHUNK_B), 0)
           >= lax.broadcasted_iota(jnp.int32, (CHUNK_B, CHUNK_B), 1))
    for g in range(G_B):
        wg = jnp.where(tri, wsp_ref[g], 0.0).astype(BF16)
        cs = slice(g * DG_B, (g + 1) * DG_B)
        for c in range(nchunks):
            rs = slice(c * CHUNK_B, (c + 1) * CHUNK_B)
            yb_s[rs, cs] = jnp.dot(wg, vn_s[rs, cs], preferred_element_type=F32) + bsp_ref[g]
    acc = acc + _sigmoid(proj(w_gb)) * branch(gate_b * yb_s[...], W_A, W_A + W_B)

    unslab = lambda ref: jnp.concatenate([ref[0], ref[1]], axis=1)
    ls = [unslab(l0), unslab(l1), unslab(l2)]
    mx = jnp.maximum(jnp.maximum(ls[0], ls[1]), ls[2])
    es = [jnp.exp(l - mx) for l in ls]
    inv_tot = 1.0 / (es[0] + es[1] + es[2])
    pc = None
    for g, o_ref in enumerate((o0, o1, o2)):
        gs = slice(g * WG_C, (g + 1) * WG_C)
        yc = unslab(o_ref) * (es[g] * inv_tot) * _silu(cz_ref[:, gs])
        t = branch(yc, W_A + W_B + g * WG_C, W_A + W_B + (g + 1) * WG_C)
        pc = t if pc is None else pc + t
    acc = acc + _sigmoid(proj(w_gc)) * pc

    out = x + jnp.dot(acc.astype(BF16), wo_ref[...], preferred_element_type=F32)
    if final:
        rr = lax.rsqrt(jnp.mean(out * out, axis=-1, keepdims=True) + EPS)
        out = out * rr * fg_ref[...]
    out_ref[...] = out


def _post(x, p1, h, os_, ls_, norm_g, w_main, ln_g, ln_b, w_sp, b_sp_col, w_branch, w_out, final_g, l, final,
          emit_vn):
    rows = x.shape[0]
    tm = min(rows, POST_TM)
    once = pl.Buffered(1)
    row = lambda w: pl.BlockSpec((tm, w), lambda i: (i, 0))
    slab = pl.BlockSpec((2, tm, WG_C // 2), lambda i: (0, i, 0))
    wblk = lambda b: pl.BlockSpec((None, WBLK, D_MODEL), lambda i, b=b: (l, b, 0), pipeline_mode=once)
    vec = lambda w: pl.BlockSpec((None, 1, w), lambda i: (l, 0, 0), pipeline_mode=once)
    in_specs = (
        [row(D_MODEL), row(W_A)] + [slab] * 6
        + [pl.BlockSpec((tm, W_C), lambda i: (i, OFF_CZ // W_C)), vec(D_MODEL)]
        + [wblk(b) for b in (BLK_AO, BLK_AZ, BLK_BU, BLK_BV, BLK_BZ, BLK_GA, BLK_GB, BLK_GC)]
        + [vec(W_B), vec(W_B),
           pl.BlockSpec((None, G_B, CHUNK_B, CHUNK_B), lambda i: (l, 0, 0, 0), pipeline_mode=once),
           pl.BlockSpec((None, G_B, CHUNK_B, 1), lambda i: (l, 0, 0, 0), pipeline_mode=once),
           pl.BlockSpec((None, W_A + W_B + W_C, D_MODEL), lambda i: (l, 0, 0), pipeline_mode=once),
           pl.BlockSpec((None, D_MODEL, D_MODEL), lambda i: (l, 0, 0), pipeline_mode=once),
           pl.BlockSpec((1, D_MODEL), lambda i: (0, 0), pipeline_mode=once)])
    out_specs = [row(D_MODEL)]
    out_shape = [jax.ShapeDtypeStruct((rows, D_MODEL), F32)]
    if emit_vn:
        out_specs.append(row(W_B))
        out_shape.append(jax.ShapeDtypeStruct((rows, W_B), F32))
    res = pl.pallas_call(
        functools.partial(_post_kernel, nchunks=tm // CHUNK_B, final=final, emit_vn=emit_vn),
        grid=(rows // tm,),
        in_specs=in_specs,
        out_specs=out_specs,
        out_shape=out_shape,
        scratch_shapes=[pltpu.VMEM((tm, W_B), BF16), pltpu.VMEM((tm, W_B), F32)],
        compiler_params=_cparams(1),
        name="post",
    )(x, h, *os_, *ls_, p1, norm_g, *([w_main] * 8), ln_g, ln_b, w_sp, b_sp_col, w_branch, w_out, final_g)
    return (res[0], res[1]) if emit_vn else (res[0], None)


def _rope_tables(pos):
    half = ROT_DIM // 2
    inv = ROPE_THETA ** (-jnp.arange(half, dtype=F32) / half)
    rest = DH_C - ROT_DIM
    inv_h = jnp.concatenate([inv, inv, jnp.zeros((rest,), F32)])
    sgn_h = jnp.concatenate([-jnp.ones((half,), F32), jnp.ones((half,), F32), jnp.zeros((rest,), F32)])
    inv_l = jnp.concatenate([inv_h, inv_h])[None, :]
    sgn_l = jnp.concatenate([sgn_h, sgn_h])[None, :]
    ang = pos.astype(F32)[:, None] * inv_l
    return jnp.cos(ang), jnp.sin(ang) * sgn_l


def _stack_kv(k_rows, v_rows):
    b, t, _ = k_rows.shape
    return jnp.stack([k_rows.reshape(b, t, HG_C, DH_C), v_rows.reshape(b, t, HG_C, DH_C)], axis=2)


def kernel(x_prompt, x_sample, state_C, state_n, state_m, state_conv, cache_kv_w128, cache_kv_w512, cache_kv_w2048,
           norm_g, w_in, b_igate, b_fgate, conv_w, conv_b, ln_v_g, ln_v_b, w_spatial, b_spatial, w_branch, w_out,
           final_norm_g):
    depth = w_in.shape[0]
    bp, seq, _ = x_prompt.shape
    bs, n_tok, _ = x_sample.shape
    assert bp == 1 and seq % ATT_TILE == 0 and n_tok <= SAMPLE_ROWS // 2 and n_tok >= CONV_W - 1
    caches = (cache_kv_w128, cache_kv_w512, cache_kv_w2048)
    pad_tok = SAMPLE_ROWS - n_tok
    rows_s = bs * SAMPLE_ROWS

    w_in_t = jnp.swapaxes(w_in, 1, 2)
    w_main = _pack_w_in(w_in_t)
    gate_bias = jnp.concatenate([b_igate, b_fgate], axis=1)
    bias_row = jnp.pad(gate_bias, ((0, 0), (0, GATE_LANES - 2 * H_A)))[:, None, :]
    bias_col = gate_bias[:, :, None]
    wb16 = w_branch.astype(BF16)
    wo16 = w_out.astype(BF16)
    b_sp_col = b_spatial[..., None]
    norm_g3, conv_b3 = norm_g[:, None, :], conv_b[:, None, :]
    ln_g3, ln_b3 = ln_v_g[:, None, :], ln_v_b[:, None, :]
    seqs_per_chunk = CHUNK_B // SAMPLE_ROWS
    w8 = jnp.pad(w_spatial[:, :, :n_tok, :n_tok], ((0, 0), (0, 0), (0, pad_tok), (0, pad_tok)))
    w_sp_s = jnp.einsum('ab,lgij->lgaibj', jnp.eye(seqs_per_chunk, dtype=F32), w8).reshape(
        depth, G_B, CHUNK_B, CHUNK_B)
    b_sp_s = jnp.tile(jnp.pad(b_spatial[:, :, :n_tok], ((0, 0), (0, 0), (0, pad_tok))),
                      (1, 1, seqs_per_chunk))[..., None]

    cos_p, sin_p = _rope_tables(jnp.arange(seq))
    cos_s, sin_s = _rope_tables(PAST_LEN + jnp.arange(SAMPLE_ROWS))

    hp = x_prompt.reshape(seq, D_MODEL)
    hs = jnp.pad(x_sample, ((0, 0), (0, pad_tok), (0, 0))).reshape(rows_s, D_MODEL)
    fg = final_norm_g[None, :]
    zeros_c = jnp.zeros((1, 1, H_A, DH_A, DH_A), F32)
    zeros_n = jnp.zeros((1, 1, H_A, 1, DH_A), F32)
    zeros_m = jnp.zeros((1, 1, H_A, 1, GATE_LANES), F32)
    zeros_tail = jnp.zeros((1, 1, 8, 2 * W_A), F32)
    n0_s = state_n[:, :, :, None, :]
    m0_s = jnp.broadcast_to(state_m[:, :, :, None, None], (depth, bs, H_A, 1, GATE_LANES))
    tail0_s = jnp.pad(state_conv, ((0, 0), (0, 0), (8 - (CONV_W - 1), 0), (0, 0)))
    views = _cache_views(caches)

    p_out = {k: [] for k in ('C', 'n', 'm', 'conv', 'kv0', 'kv1', 'kv2')}
    s_out = {k: [] for k in ('C', 'n', 'm', 'conv', 'chunk_v', 'kv0', 'kv1', 'kv2')}
    for l in range(depth):
        final = l == depth - 1

        pp, gcol, grow = _inproj(hp, norm_g3, w_main, w_in_t, l)
        ya, c1, n1, m1, conv1 = _mlstm(pp, gcol, grow, bias_row, bias_col, conv_w, conv_b3,
                                       zeros_c, zeros_n, zeros_m, zeros_tail, l, 0, L=MLSTM_L, t_valid=MLSTM_L)
        os_, ls_, krs = [], [], []
        for g, (win, d) in enumerate(DIL_PATTERNS):
            o_g, l_g, kr_g = _dil_attn(pp, cos_p, sin_p, g, d)
            os_.append(o_g)
            ls_.append(l_g)
            keep = min(win, seq)
            v_g = pp[seq - keep:, OFF_CV + g * WG_C:OFF_CV + (g + 1) * WG_C]
            p_out['kv%d' % g].append(_stack_kv(kr_g[None, seq - keep:], v_g[None]))
        hp, _ = _post(hp, pp, ya, os_, ls_, norm_g3, w_main, ln_g3, ln_b3, w_spatial, b_sp_col, wb16, wo16, fg, l,
                      final, False)
        p_out['C'].append(c1)
        p_out['n'].append(n1)
        p_out['m'].append(m1)
        p_out['conv'].append(conv1)

        ps, gcol, grow = _inproj(hs, norm_g3, w_main, w_in_t, l)
        ya, c1, n1, m1, conv1 = _mlstm(ps, gcol, grow, bias_row, bias_col, conv_w, conv_b3,
                                       state_C, n0_s, m0_s, tail0_s, l, l, L=SAMPLE_ROWS, t_valid=n_tok)
        os_, ls_, kr = _sample_attn(ps, cos_s, sin_s, views, l, n_tok)
        hs, vn = _post(hs, ps, ya, os_, ls_, norm_g3, w_main, ln_g3, ln_b3, w_sp_s, b_sp_s, wb16, wo16, fg, l,
                       final, True)
        s_out['C'].append(c1)
        s_out['n'].append(n1)
        s_out['m'].append(m1)
        s_out['conv'].append(conv1)
        s_out['chunk_v'].append(vn.reshape(bs, SAMPLE_ROWS, W_B)[:, :n_tok])
        kr3 = kr.reshape(bs, SAMPLE_ROWS, W_C)[:, :n_tok]
        v3 = ps[:, OFF_CV:OFF_CV + W_C].reshape(bs, SAMPLE_ROWS, W_C)[:, :n_tok]
        for g in range(len(DIL_PATTERNS)):
            gs = slice(g * WG_C, (g + 1) * WG_C)
            s_out['kv%d' % g].append(_stack_kv(kr3[:, :, gs], v3[:, :, gs]))

    stk = lambda d, k: jnp.stack(d[k], axis=0)
    y_prompt = hp.reshape(bp, seq, D_MODEL)
    y_sample = hs.reshape(bs, SAMPLE_ROWS, D_MODEL)[:, :n_tok]
    return (y_prompt, y_sample,
            stk(p_out, 'C'), stk(p_out, 'n'), stk(p_out, 'm'), stk(p_out, 'conv'),
            stk(p_out, 'kv0'), stk(p_out, 'kv1'), stk(p_out, 'kv2'),
            stk(s_out, 'C'), stk(s_out, 'n'), stk(s_out, 'm'), stk(s_out, 'conv'), stk(s_out, 'chunk_v'),
            stk(s_out, 'kv0'), stk(s_out, 'kv1'), stk(s_out, 'kv2'))
```

```python
import functools

import jax
import jax.numpy as jnp
from jax import lax
from jax.experimental import pallas as pl
from jax.experimental.pallas import tpu as pltpu

F32 = jnp.float32
BF16 = jnp.bfloat16

D_MODEL = 1024
H_A = 4
DH_A = 256
W_A = H_A * DH_A
CONV_W = 4
G_B = 4
CHUNK_B = 128
W_B = 1024
DG_B = W_B // G_B
DIL_PATTERNS = ((128, 1), (512, 4), (2048, 16))
HG_C = 4
DH_C = 64
WG_C = HG_C * DH_C
W_C = len(DIL_PATTERNS) * WG_C
SPAN_C = 128
ROT_DIM = DH_C // 4
ROPE_THETA = 500000.0
EPS = 1e-6
PAST_LEN = 16384

N_PACK = 14336
WBLK = 1024
BLK_AO, BLK_AZ, BLK_BU, BLK_BV, BLK_BZ, BLK_C0, BLK_GA, BLK_GB, BLK_GC = 3, 4, 5, 6, 7, 8, 11, 12, 13
N_P1 = 6144
OFF_CQ, OFF_CK, OFF_CV, OFF_CZ = 3072, 3840, 4608, 5376
GATE_LANES = 128

INPROJ_TM = 2048
POST_TM = 512
MLSTM_BAND = 128
MLSTM_L = 256
SAMPLE_ROWS = 8
ATT_TILE = 2048
VMEM_LIMIT = 56 * 1024 * 1024


def _cparams(n_axes):
    return pltpu.CompilerParams(dimension_semantics=("arbitrary",) * n_axes, vmem_limit_bytes=VMEM_LIMIT)


def _dot(a, b):
    return jnp.dot(a.astype(BF16), b.astype(BF16), preferred_element_type=F32)


def _dot_nt(a, b):
    return lax.dot_general(a.astype(BF16), b.astype(BF16), (((1,), (1,)), ((), ())), preferred_element_type=F32)


def _dot_tn(a, b):
    return lax.dot_general(a.astype(BF16), b.astype(BF16), (((0,), (0,)), ((), ())), preferred_element_type=F32)


def _sigmoid(x):
    return 0.5 * jnp.tanh(0.5 * x) + 0.5


def _silu(x):
    return x * _sigmoid(x)


def _inproj_kernel(x_ref, g_ref, w_ref, wg_ref, p_ref, gc_ref, gr_ref, xn_ref):
    nt = (((1,), (1,)), ((), ()))

    @pl.when(pl.program_id(1) == 0)
    def _():
        x = x_ref[...]
        r = lax.rsqrt(jnp.mean(x * x, axis=-1, keepdims=True) + EPS)
        xn = (x * r * g_ref[...]).astype(BF16)
        xn_ref[...] = xn
        wg = wg_ref[...].astype(BF16)
        gc_ref[...] = lax.dot_general(xn, wg, nt, preferred_element_type=F32)
        gr_ref[...] = lax.dot_general(wg[0:8, :], xn, nt, preferred_element_type=F32)

    p_ref[...] = lax.dot_general(xn_ref[...], w_ref[...], nt, preferred_element_type=F32)


def _inproj(x, norm_g, w_main, w_in_t, l):
    rows = x.shape[0]
    gate_blk = 4 * W_A // GATE_LANES
    tm = min(rows, INPROJ_TM)
    tn = WBLK
    n_a = 3 * W_A // tn
    wblk = lambda i, j: (l, jnp.where(j < n_a, j, j + (BLK_C0 - n_a)), 0)
    return pl.pallas_call(
        _inproj_kernel,
        grid=(rows // tm, N_P1 // tn),
        in_specs=[
            pl.BlockSpec((tm, D_MODEL), lambda i, j: (i, 0)),
            pl.BlockSpec((None, 1, D_MODEL), lambda i, j: (l, 0, 0)),
            pl.BlockSpec((None, tn, D_MODEL), wblk),
            pl.BlockSpec((None, GATE_LANES, D_MODEL), lambda i, j: (l, gate_blk, 0)),
        ],
        out_specs=[
            pl.BlockSpec((tm, tn), lambda i, j: (i, j)),
            pl.BlockSpec((tm, GATE_LANES), lambda i, j: (i, 0)),
            pl.BlockSpec((None, 8, tm), lambda i, j: (i, 0, 0)),
        ],
        out_shape=[
            jax.ShapeDtypeStruct((rows, N_P1), F32),
            jax.ShapeDtypeStruct((rows, GATE_LANES), F32),
            jax.ShapeDtypeStruct((rows // tm, 8, tm), F32),
        ],
        scratch_shapes=[pltpu.VMEM((tm, D_MODEL), BF16)],
        compiler_params=_cparams(2),
        name="inproj",
    )(x, norm_g, w_main, w_in_t)


def _pack_kernel(a_ref, b_ref, w_ref, *, first_shifted):
    j = pl.program_id(1)

    @pl.when(j < first_shifted)
    def _():
        w_ref[...] = a_ref[...].astype(BF16)

    @pl.when(j >= first_shifted)
    def _():
        w_ref[...] = jnp.concatenate([a_ref[2 * H_A:, :], b_ref[...]], axis=0).astype(BF16)


def _pack_w_in(w_in_t):
    depth = w_in_t.shape[0]
    tn = 2048
    gate_off = 4 * W_A
    assert gate_off % tn == 0 and w_in_t.shape[1] == N_PACK + 2 * H_A and 2 * H_A == 8
    kern = functools.partial(_pack_kernel, first_shifted=gate_off // tn)
    return pl.pallas_call(
        kern,
        grid=(depth, N_PACK // tn),
        in_specs=[
            pl.BlockSpec((None, tn, D_MODEL), lambda l, j: (l, j, 0)),
            pl.BlockSpec((None, 8, D_MODEL), lambda l, j: (l, (j + 1) * (tn // 8), 0)),
        ],
        out_specs=pl.BlockSpec((None, tn, D_MODEL), lambda l, j: (l, j, 0)),
        out_shape=jax.ShapeDtypeStruct((depth, N_PACK, D_MODEL), BF16),
        compiler_params=_cparams(2),
        name="pack_w_in",
    )(w_in_t, w_in_t)


def _conv_silu(ext_ref, cols, w, b, L):
    y = b
    for back in range(CONV_W):
        y = y + ext_ref[8 - back:8 - back + L, cols] * w[CONV_W - 1 - back:CONV_W - back]
    return _silu(y)


def _mlstm_core(g_col, g_row, get_v, ext_s, cw_ref, cb_ref, c_s, n_s, m_s, y_ref, L, t_valid, side_work=None):
    tick = side_work if side_work is not None else (lambda: None)
    neg_inf = jnp.float32(-jnp.inf)
    ig_col, lf_col = g_col, jax.nn.log_sigmoid(g_col)
    ig_row, lf_row = g_row, jax.nn.log_sigmoid(g_row)
    if t_valid < L:
        vc = lax.broadcasted_iota(jnp.int32, (L, GATE_LANES), 0) < t_valid
        vr = lax.broadcasted_iota(jnp.int32, (8, L), 1) < t_valid
        ig_col, lf_col = jnp.where(vc, ig_col, neg_inf), jnp.where(vc, lf_col, 0.0)
        ig_row, lf_row = jnp.where(vr, ig_row, neg_inf), jnp.where(vr, lf_row, 0.0)
    ti = lax.broadcasted_iota(jnp.int32, (L, L), 0)
    si = lax.broadcasted_iota(jnp.int32, (L, L), 1)
    causal = ti >= si
    b_col = jnp.dot(causal.astype(F32), lf_col, preferred_element_type=F32, precision=lax.Precision.HIGHEST)
    b_row = jnp.dot(lf_row, (ti <= si).astype(F32), preferred_element_type=F32, precision=lax.Precision.HIGHEST)
    last = t_valid - 1
    band = min(L, MLSTM_BAND)
    nt = (((1,), (1,)), ((), ()))

    for h in range(H_A):
        cs = slice(h * DH_A, (h + 1) * DH_A)
        ks = slice(W_A + h * DH_A, W_A + (h + 1) * DH_A)
        q = _conv_silu(ext_s, cs, cw_ref[:, cs], cb_ref[:, cs], L)
        tick()
        k = _conv_silu(ext_s, ks, cw_ref[:, ks], cb_ref[:, ks], L) * (DH_A ** -0.5)
        tick()
        v = get_v(h)
        q16, k16, v16 = q.astype(BF16), k.astype(BF16), v.astype(BF16)
        bc = b_col[:, H_A + h:H_A + h + 1]
        igc = ig_col[:, h:h + 1]
        br = b_row[H_A + h:H_A + h + 1, :]
        igr = ig_row[h:h + 1, :]
        m_prev = m_s[h][:, 0:1]
        c_prev = c_s[h]
        n_prev = n_s[h]
        c16 = c_prev.astype(BF16)

        for r in range(L // band):
            rs = slice(r * band, (r + 1) * band)
            kw = (r + 1) * band
            ti = lax.broadcasted_iota(jnp.int32, (band, kw), 0) + r * band
            si = lax.broadcasted_iota(jnp.int32, (band, kw), 1)
            logw = jnp.where(ti >= si, bc[rs] - br[:, 0:kw] + igr[:, 0:kw], neg_inf)
            inter = bc[rs] + m_prev
            m_t = jnp.maximum(inter, jnp.max(logw, axis=1, keepdims=True))
            w_intra = jnp.exp(logw - m_t)
            w_inter = jnp.exp(inter - m_t)
            s = w_intra * lax.dot_general(q16[rs], k16[0:kw], nt, preferred_element_type=F32)
            num = (jnp.dot(s.astype(BF16), v16[0:kw], preferred_element_type=F32)
                   + w_inter * lax.dot_general(q16[rs], c16, nt, preferred_element_type=F32))
            den = (jnp.sum(s, axis=1, keepdims=True)
                   + w_inter * jnp.sum(q[rs] * n_prev, axis=1, keepdims=True))
            y_ref[0, rs, cs] = num / jnp.maximum(jnp.abs(den), jnp.exp(-m_t))
            if r == last // band:
                m_new = m_t[last - r * band:last - r * band + 1, :]
            tick()

        b_last = bc[last:last + 1, :]
        decay = jnp.exp(b_last + m_prev - m_new)
        w_s = jnp.exp(b_last - bc + igc - m_new)
        c_s[h] = decay * c_prev + lax.dot_general((w_s * v).astype(BF16), k16, (((0,), (0,)), ((), ())),
                                                  preferred_element_type=F32)
        n_s[h] = decay * n_prev + jnp.sum(w_s * k, axis=0, keepdims=True)
        m_s[h] = jnp.broadcast_to(m_new, (1, GATE_LANES))
        tick()

    ext_s[0:8, :] = ext_s[t_valid:t_valid + 8, :]


def _mlstm_state_io(c0_ref, n0_ref, m0_ref, tail0_ref, c_s, n_s, m_s, ext_s):
    @pl.when(pl.program_id(1) == 0)
    def _():
        c_s[...] = c0_ref[0]
        n_s[...] = n0_ref[0]
        m_s[...] = m0_ref[0]
        ext_s[0:8, :] = tail0_ref[0]


def _mlstm_state_out(c_out, n_out, m_out, tail_out, c_s, n_s, m_s, ext_s):
    @pl.when(pl.program_id(1) == pl.num_programs(1) - 1)
    def _():
        c_out[0] = c_s[...]
        n_out[0] = n_s[...]
        m_out[0] = m_s[...]
        tail_out[0] = ext_s[0:8, :]


def _mlstm_kernel(p_ref, gc_ref, gr_ref, brow_ref, bcol_ref, cw_ref, cb_ref, c0_ref, n0_ref, m0_ref, tail0_ref,
                  y_ref, c_out, n_out, m_out, tail_out, c_s, n_s, m_s, ext_s, *, L, t_valid):
    _mlstm_state_io(c0_ref, n0_ref, m0_ref, tail0_ref, c_s, n_s, m_s, ext_s)
    ext_s[8:8 + L, :] = p_ref[0, :, 0:2 * W_A]
    get_v = lambda h: p_ref[0, :, 2 * W_A + h * DH_A:2 * W_A + (h + 1) * DH_A]
    _mlstm_core(gc_ref[0] + brow_ref[...], gr_ref[0] + bcol_ref[...], get_v, ext_s, cw_ref, cb_ref,
                c_s, n_s, m_s, y_ref, L, t_valid)
    _mlstm_state_out(c_out, n_out, m_out, tail_out, c_s, n_s, m_s, ext_s)


def _seq_a_kernel(x_ref, ng_ref, w_aq, w_ak, w_av, w_c0, w_c1, w_c2, wg_ref, brow_ref, bcol_ref, cw_ref, cb_ref,
                  c0_ref, n0_ref, m0_ref, tail0_ref, y_ref, pc_ref, c_out, n_out, m_out, tail_out,
                  c_s, n_s, m_s, ext_s, v_s, *, L):
    _mlstm_state_io(c0_ref, n0_ref, m0_ref, tail0_ref, c_s, n_s, m_s, ext_s)
    nt = (((1,), (1,)), ((), ()))
    x = x_ref[0]
    r = lax.rsqrt(jnp.mean(x * x, axis=-1, keepdims=True) + EPS)
    xn = (x * r * ng_ref[...]).astype(BF16)
    proj = lambda w_ref: lax.dot_general(xn, w_ref[...], nt, preferred_element_type=F32)
    wg = wg_ref[...].astype(BF16)
    g_col = lax.dot_general(xn, wg, nt, preferred_element_type=F32) + brow_ref[...]
    g_row = lax.dot_general(wg[0:8, :], xn, nt, preferred_element_type=F32) + bcol_ref[...]
    ext_s[8:8 + L, 0:W_A] = proj(w_aq)
    ext_s[8:8 + L, W_A:2 * W_A] = proj(w_ak)
    v_s[...] = proj(w_av)
    piece = 256
    todo = [(w_ref, j, lo) for j, w_ref in enumerate((w_c0, w_c1, w_c2)) for lo in range(0, WBLK, piece)]

    def emit_piece():
        w_ref, j, lo = todo.pop(0)
        pc_ref[0, :, j * WBLK + lo:j * WBLK + lo + piece] = lax.dot_general(
            xn, w_ref[lo:lo + piece, :], nt, preferred_element_type=F32)

    side_work = lambda: emit_piece() if todo else None
    get_v = lambda h: v_s[:, h * DH_A:(h + 1) * DH_A]
    _mlstm_core(g_col, g_row, get_v, ext_s, cw_ref, cb_ref, c_s, n_s, m_s, y_ref, L, L, side_work)
    while todo:
        emit_piece()
    _mlstm_state_out(c_out, n_out, m_out, tail_out, c_s, n_s, m_s, ext_s)


def _seq_a(x, norm_g, w_main, w_in_t, bias_row, bias_col, conv_w, conv_b, c0, n0, m0, tail0, l, *, L):
    rows = x.shape[0]
    nch = rows // L
    x3 = x.reshape(nch, L, D_MODEL)
    gate_blk = 4 * W_A // GATE_LANES
    once = pl.Buffered(1)
    chunk = lambda b, c: (c, 0, 0)
    layer3 = lambda b, c: (l, 0, 0)
    wblk = lambda blk: pl.BlockSpec((None, WBLK, D_MODEL), lambda b, c, blk=blk: (l, blk, 0), pipeline_mode=once)
    st5 = lambda b, c: (0, 0, 0, 0, 0)
    st4 = lambda b, c: (0, 0, 0, 0)
    first4 = lambda b, c: (0, 0, 0, 0)
    first3 = lambda b, c: (0, 0, 0)
    y, pc, c1, n1, m1, tail1 = pl.pallas_call(
        functools.partial(_seq_a_kernel, L=L),
        grid=(1, nch),
        in_specs=[
            pl.BlockSpec((1, L, D_MODEL), chunk),
            pl.BlockSpec((None, 1, D_MODEL), layer3, pipeline_mode=once),
        ] + [wblk(b) for b in (0, 1, 2, BLK_C0, BLK_C0 + 1, BLK_C0 + 2)] + [
            pl.BlockSpec((None, GATE_LANES, D_MODEL), lambda b, c: (l, gate_blk, 0), pipeline_mode=once),
            pl.BlockSpec((None, 1, GATE_LANES), layer3, pipeline_mode=once),
            pl.BlockSpec((None, 8, 1), layer3, pipeline_mode=once),
            pl.BlockSpec((None, CONV_W, 2 * W_A), layer3, pipeline_mode=once),
            pl.BlockSpec((None, 1, 2 * W_A), layer3, pipeline_mode=once),
            pl.BlockSpec((None, 1, H_A, DH_A, DH_A), st5, pipeline_mode=once),
            pl.BlockSpec((None, 1, H_A, 1, DH_A), st5, pipeline_mode=once),
            pl.BlockSpec((None, 1, H_A, 1, GATE_LANES), st5, pipeline_mode=once),
            pl.BlockSpec((None, 1, 8, 2 * W_A), st4, pipeline_mode=once),
        ],
        out_specs=[
            pl.BlockSpec((1, L, W_A), chunk),
            pl.BlockSpec((1, L, 3 * WBLK), chunk),
            pl.BlockSpec((1, H_A, DH_A, DH_A), first4),
            pl.BlockSpec((1, H_A, 1, DH_A), first4),
            pl.BlockSpec((1, H_A, 1, GATE_LANES), first4),
            pl.BlockSpec((1, 8, 2 * W_A), first3),
        ],
        out_shape=[
            jax.ShapeDtypeStruct((nch, L, W_A), F32),
            jax.ShapeDtypeStruct((nch, L, 3 * WBLK), F32),
            jax.ShapeDtypeStruct((1, H_A, DH_A, DH_A), F32),
            jax.ShapeDtypeStruct((1, H_A, 1, DH_A), F32),
            jax.ShapeDtypeStruct((1, H_A, 1, GATE_LANES), F32),
            jax.ShapeDtypeStruct((1, 8, 2 * W_A), F32),
        ],
        scratch_shapes=[
            pltpu.VMEM((H_A, DH_A, DH_A), F32),
            pltpu.VMEM((H_A, 1, DH_A), F32),
            pltpu.VMEM((H_A, 1, GATE_LANES), F32),
            pltpu.VMEM((L + 8, 2 * W_A), F32),
            pltpu.VMEM((L, W_A), F32),
        ],
        compiler_params=_cparams(2),
        name="seq_a",
    )(x3, norm_g, *([w_main] * 6), w_in_t, bias_row, bias_col, conv_w, conv_b, c0, n0, m0, tail0)
    return (y.reshape(rows, W_A), pc.reshape(rows, 3 * WBLK), c1, n1[:, :, 0, :], m1[:, :, 0, 0],
            tail1[:, 8 - (CONV_W - 1):, :])


def _mlstm(p, gcol, grow, bias_row, bias_col, conv_w, conv_b, c0, n0, m0, tail0, l, ls, *, L, t_valid):
    batch = c0.shape[1]
    rows = p.shape[0]
    nch = rows // (batch * L)
    p3 = p.reshape(batch * nch, L, N_P1)
    gc3 = gcol.reshape(batch * nch, L, GATE_LANES)
    tm = grow.shape[2]
    if L % GATE_LANES == 0:
        per = tm // L
        gr3, gr_index = grow, (lambda b, c: ((b * nch + c) // per, 0, (b * nch + c) % per))
    else:
        gr3 = grow.transpose(1, 0, 2).reshape(8, batch * nch, L).transpose(1, 0, 2)
        gr_index = lambda b, c: (b * nch + c, 0, 0)
    kern = functools.partial(_mlstm_kernel, L=L, t_valid=t_valid)
    chunk = lambda b, c: (b * nch + c, 0, 0)
    layer3 = lambda b, c: (l, 0, 0)
    per_b4 = lambda b, c: (b, 0, 0, 0)
    per_b3 = lambda b, c: (b, 0, 0)
    st5 = lambda b, c: (ls, b, 0, 0, 0)
    st4 = lambda b, c: (ls, b, 0, 0)
    y, c1, n1, m1, tail1 = pl.pallas_call(
        kern,
        grid=(batch, nch),
        in_specs=[
            pl.BlockSpec((1, L, 3 * W_A), chunk),
            pl.BlockSpec((1, L, GATE_LANES), chunk),
            pl.BlockSpec((1, 8, L), gr_index),
            pl.BlockSpec((None, 1, GATE_LANES), layer3),
            pl.BlockSpec((None, 8, 1), layer3),
            pl.BlockSpec((None, CONV_W, 2 * W_A), layer3),
            pl.BlockSpec((None, 1, 2 * W_A), layer3),
            pl.BlockSpec((None, 1, H_A, DH_A, DH_A), st5),
            pl.BlockSpec((None, 1, H_A, 1, DH_A), st5),
            pl.BlockSpec((None, 1, H_A, 1, GATE_LANES), st5),
            pl.BlockSpec((None, 1, 8, 2 * W_A), st4),
        ],
        out_specs=[
            pl.BlockSpec((1, L, W_A), chunk),
            pl.BlockSpec((1, H_A, DH_A, DH_A), per_b4),
            pl.BlockSpec((1, H_A, 1, DH_A), per_b4),
            pl.BlockSpec((1, H_A, 1, GATE_LANES), per_b4),
            pl.BlockSpec((1, 8, 2 * W_A), per_b3),
        ],
        out_shape=[
            jax.ShapeDtypeStruct((batch * nch, L, W_A), F32),
            jax.ShapeDtypeStruct((batch, H_A, DH_A, DH_A), F32),
            jax.ShapeDtypeStruct((batch, H_A, 1, DH_A), F32),
            jax.ShapeDtypeStruct((batch, H_A, 1, GATE_LANES), F32),
            jax.ShapeDtypeStruct((batch, 8, 2 * W_A), F32),
        ],
        scratch_shapes=[
            pltpu.VMEM((H_A, DH_A, DH_A), F32),
            pltpu.VMEM((H_A, 1, DH_A), F32),
            pltpu.VMEM((H_A, 1, GATE_LANES), F32),
            pltpu.VMEM((L + 8, 2 * W_A), F32),
        ],
        compiler_params=_cparams(2),
        name="mlstm",
    )(p3, gc3, gr3, bias_row, bias_col, conv_w, conv_b, c0, n0, m0, tail0)
    return y.reshape(rows, W_A), c1, n1[:, :, 0, :], m1[:, :, 0, 0], tail1[:, 8 - (CONV_W - 1):, :]


def _rope(x, cos, sin):
    width = x.shape[1]
    lane = lax.broadcasted_iota(jnp.int32, (1, width), 1) % DH_C
    half = ROT_DIM // 2
    partner = jnp.where(lane < half, pltpu.roll(x, width - half, 1), pltpu.roll(x, half, 1))
    return x * cos + partner * sin


def _dil_attn_kernel(q_ref, k_ref, v_ref, cos_ref, sin_ref, o_ref, lse_ref, kr_ref, qs, kbuf, vbuf, bias, *, d):
    i = pl.program_id(0)
    T = ATT_TILE
    HL = WG_C // 2
    nblk = T // SPAN_C
    nres = T // (SPAN_C * d)
    cur = (i % 2) * T
    prev = T - cur
    neg_inf = jnp.float32(-jnp.inf)

    def split(x):
        return x[:, 0:HL], x[:, HL:WG_C]

    def put(buf, lo, x):
        a, b = split(x)
        buf[0, pl.ds(lo, T), :] = a
        buf[1, pl.ds(lo, T), :] = b

    @pl.when(i == 0)
    def _():
        ii = lax.broadcasted_iota(jnp.int32, (SPAN_C, 2 * SPAN_C), 0)
        jj = lax.broadcasted_iota(jnp.int32, (SPAN_C, 2 * SPAN_C), 1)
        band = (jj >= ii) & (jj <= ii + SPAN_C)
        bias[1] = jnp.where(band, 0.0, neg_inf)
        bias[0] = jnp.where(band & (jj >= SPAN_C), 0.0, neg_inf)
        kbuf[:, T:2 * T, :] = jnp.zeros((2, T, HL), F32)
        vbuf[:, T:2 * T, :] = jnp.zeros((2, T, HL), F32)

    cos = jnp.concatenate([cos_ref[...], cos_ref[...]], axis=1)
    sin = jnp.concatenate([sin_ref[...], sin_ref[...]], axis=1)
    put(qs, 0, _rope(q_ref[...], cos, sin) * (DH_C ** -0.5))
    kr = _rope(k_ref[...], cos, sin)
    kr_ref[...] = kr
    put(kbuf, cur, kr)
    put(vbuf, cur, v_ref[...])

    lane_head = lax.broadcasted_iota(jnp.int32, (SPAN_C, WG_C), 1) // DH_C
    hm = [(lane_head == h).astype(F32) for h in range(HG_C)]

    def rows(start, size):
        return pl.ds(start, size) if d == 1 else pl.ds(start, size, stride=d)

    def take(buf, start, size):
        return jnp.concatenate([buf[0, rows(start, size), :], buf[1, rows(start, size), :]], axis=1)

    def pick(per_head):
        out = per_head[0]
        for h in range(1, HG_C):
            out = jnp.where(lane_head >= h, per_head[h], out)
        return out

    def body(blk, carry):
        r = blk % d
        c = blk // d
        qstart = r + SPAN_C * d * c
        in_tile = cur + qstart
        before = jnp.where(c > 0, in_tile - SPAN_C * d, prev + r + SPAN_C * d * (nres - 1))
        qb = take(qs, qstart, SPAN_C)
        kb = jnp.concatenate([take(kbuf, before, SPAN_C), take(kbuf, in_tile, SPAN_C)], axis=0)
        vb = jnp.concatenate([take(vbuf, before, SPAN_C), take(vbuf, in_tile, SPAN_C)], axis=0)
        q4 = jnp.concatenate([qb * hm[h] for h in range(HG_C)], axis=0)
        has_prev = jnp.logical_or(i > 0, c > 0).astype(jnp.int32)
        s = _dot_nt(q4, kb).reshape(HG_C, SPAN_C, 2 * SPAN_C) + bias[has_prev]
        mx = jnp.max(s, axis=2, keepdims=True)
        p = jnp.exp(s - mx)
        den = jnp.sum(p, axis=2, keepdims=True)
        pv = _dot(p.reshape(HG_C * SPAN_C, 2 * SPAN_C), vb).reshape(HG_C, SPAN_C, WG_C)
        rden = 1.0 / den
        lse = mx + jnp.log(den)
        o = pick([pv[h] for h in range(HG_C)]) * pick([jnp.broadcast_to(rden[h], (SPAN_C, WG_C)) for h in range(HG_C)])
        lse_b = pick([jnp.broadcast_to(lse[h], (SPAN_C, WG_C)) for h in range(HG_C)])
        for half, (oh, lh) in enumerate(zip(split(o), split(lse_b))):
            o_ref[half, rows(qstart, SPAN_C), :] = oh
            lse_ref[half, rows(qstart, SPAN_C), :] = lh
        return carry

    lax.fori_loop(0, nblk, body, 0, unroll=2)


def _dil_attn(p, base, cos_t, sin_t, g, d):
    rows = p.shape[0]
    T = ATT_TILE
    kern = functools.partial(_dil_attn_kernel, d=d)
    col = lambda off: pl.BlockSpec((T, WG_C), lambda i, off=off: (i, (base + off - OFF_CQ) // WG_C + g))
    tab = pl.BlockSpec((T, 2 * DH_C), lambda i: (i, 0))
    slab = pl.BlockSpec((2, T, WG_C // 2), lambda i: (0, i, 0))
    slab_shape = jax.ShapeDtypeStruct((2, rows, WG_C // 2), F32)
    return pl.pallas_call(
        kern,
        grid=(rows // T,),
        in_specs=[col(OFF_CQ), col(OFF_CK), col(OFF_CV), tab, tab],
        out_specs=[slab, slab, pl.BlockSpec((T, WG_C), lambda i: (i, 0))],
        out_shape=[slab_shape, slab_shape, jax.ShapeDtypeStruct((rows, WG_C), F32)],
        scratch_shapes=[pltpu.VMEM((2, T, WG_C // 2), F32), pltpu.VMEM((2, 2 * T, WG_C // 2), F32),
                        pltpu.VMEM((2, 2 * T, WG_C // 2), F32), pltpu.VMEM((2, SPAN_C, 2 * SPAN_C), F32)],
        compiler_params=_cparams(1),
        name="dil_attn_d%d" % d,
    )(p, p, p, cos_t, sin_t)


def _sample_attn_kernel(pc0, pc1, pc2, cos_ref, sin_ref, c0_ref, c1_ref, c2_ref,
                        o0, o1, o2, l0, l1, l2, kr_ref, *, n_tok):
    pc = jnp.concatenate([pc0[0], pc1[0], pc2[0]], axis=1)
    reps = W_C // (2 * DH_C)
    cos = jnp.concatenate([cos_ref[...]] * reps, axis=1)
    sin = jnp.concatenate([sin_ref[...]] * reps, axis=1)
    qr = _rope(pc[:, 0:W_C], cos, sin) * (DH_C ** -0.5)
    kr = _rope(pc[:, W_C:2 * W_C], cos, sin)
    v = pc[:, 2 * W_C:3 * W_C]
    kr_ref[0] = kr
    nrow = HG_C * SAMPLE_ROWS
    rowh = lax.broadcasted_iota(jnp.int32, (nrow, WG_C), 0) // SAMPLE_ROWS
    laneh = lax.broadcasted_iota(jnp.int32, (nrow, WG_C), 1) // DH_C
    hm = (rowh == laneh).astype(F32)
    tok_col = lax.broadcasted_iota(jnp.int32, (nrow, 1), 0) % SAMPLE_ROWS
    out_row = lax.broadcasted_iota(jnp.int32, (SAMPLE_ROWS, WG_C), 0)
    neg_inf = jnp.float32(-jnp.inf)
    crefs = (c0_ref, c1_ref, c2_ref)
    orefs = (o0, o1, o2)
    lrefs = (l0, l1, l2)
    for g, (_, d) in enumerate(DIL_PATTERNS):
        gs = slice(g * WG_C, (g + 1) * WG_C)
        n_buf = SPAN_C * d
        kn = kr[:, gs]
        vn = v[:, gs]
        qm = jnp.concatenate([qr[:, gs]] * HG_C, axis=0) * hm
        s = _dot(qm, crefs[g][0, 0])
        tok = lax.broadcasted_iota(jnp.int32, (nrow, n_buf), 0) % SAMPLE_ROWS
        pos = lax.broadcasted_iota(jnp.int32, (nrow, n_buf), 1)
        attends = (pos >= tok) if d == 1 else ((pos & (d - 1)) == tok)
        s = jnp.where(attends | (tok >= n_tok), s, neg_inf)
        new_ok = [(u <= tok_col) if d == 1 else (u == tok_col) for u in range(n_tok)]
        s_new = [jnp.where(ok, jnp.sum(qm * kn[u:u + 1, :], axis=1, keepdims=True), neg_inf)
                 for u, ok in enumerate(new_ok)]
        mx = jnp.max(s, axis=1, keepdims=True)
        for sn in s_new:
            mx = jnp.maximum(mx, sn)
        p = jnp.exp(s - mx)
        den = jnp.sum(p, axis=1, keepdims=True)
        acc = _dot_nt(p, crefs[g][0, 1])
        for u, sn in enumerate(s_new):
            pn = jnp.exp(sn - mx)
            den = den + pn
            acc = acc + pn * vn[u:u + 1, :]
        o32 = acc / den * hm
        l32 = (mx + jnp.log(den)) * hm
        o8, l8 = o32[0:SAMPLE_ROWS], l32[0:SAMPLE_ROWS]
        for h in range(1, HG_C):
            o8 = o8 + o32[h * SAMPLE_ROWS:(h + 1) * SAMPLE_ROWS]
            l8 = l8 + l32[h * SAMPLE_ROWS:(h + 1) * SAMPLE_ROWS]
        o8 = jnp.where(out_row < n_tok, o8, 0.0)
        l8 = jnp.where(out_row < n_tok, l8, 0.0)
        for half in range(2):
            hs = slice(half * (WG_C // 2), (half + 1) * (WG_C // 2))
            orefs[g][half] = o8[:, hs]
            lrefs[g][half] = l8[:, hs]


def _cache_views(caches):
    views = []
    for (win, d), c in zip(DIL_PATTERNS, caches):
        depth, batch, n_buf = c.shape[:3]
        assert n_buf == SPAN_C * d
        views.append(jnp.transpose(c, (0, 1, 3, 4, 5, 2)).reshape(depth, batch, 2, WG_C, n_buf))
    return views


def _sample_attn(p, cos_s, sin_s, views, l, n_tok):
    batch = p.shape[0] // SAMPLE_ROWS
    p3 = p.reshape(batch, SAMPLE_ROWS, N_P1)
    pcol = lambda k: pl.BlockSpec((1, SAMPLE_ROWS, 1024), lambda b, k=k: (b, 0, OFF_CQ // 1024 + k))
    tab = pl.BlockSpec((SAMPLE_ROWS, 2 * DH_C), lambda b: (0, 0))
    cache_spec = lambda d: pl.BlockSpec((None, 1, 2, WG_C, SPAN_C * d), lambda b: (l, b, 0, 0, 0))
    out = pl.BlockSpec((2, SAMPLE_ROWS, WG_C // 2), lambda b: (0, b, 0))
    res = pl.pallas_call(
        functools.partial(_sample_attn_kernel, n_tok=n_tok),
        grid=(batch,),
        in_specs=[pcol(0), pcol(1), pcol(2), tab, tab] + [cache_spec(d) for _, d in DIL_PATTERNS],
        out_specs=[out] * 6 + [pl.BlockSpec((1, SAMPLE_ROWS, W_C), lambda b: (b, 0, 0))],
        out_shape=[jax.ShapeDtypeStruct((2, batch * SAMPLE_ROWS, WG_C // 2), F32)] * 6
        + [jax.ShapeDtypeStruct((batch, SAMPLE_ROWS, W_C), F32)],
        compiler_params=_cparams(1),
        name="sample_attn",
    )(p3, p3, p3, cos_s, sin_s, *views)
    return list(res[:3]), list(res[3:6]), res[6].reshape(batch * SAMPLE_ROWS, W_C)


def _post_kernel(x_ref, h_ref, o0, o1, o2, l0, l1, l2, cz_ref, ng_ref, w_ao, w_az, w_bu, w_bv, w_bz, w_ga, w_gb,
                 w_gc, lg_ref, lb_ref, wsp_ref, bsp_ref, wb_ref, wo_ref, fg_ref, out_ref, *rest, nchunks, final,
                 emit_vn):
    if emit_vn:
        vn_out, vn_s, yb_s = rest
    else:
        vn_s, yb_s = rest
    nt = (((1,), (1,)), ((), ()))
    x = x_ref[...]
    r = lax.rsqrt(jnp.mean(x * x, axis=-1, keepdims=True) + EPS)
    xn = (x * r * ng_ref[...]).astype(BF16)
    proj = lambda w_ref: lax.dot_general(xn, w_ref[...], nt, preferred_element_type=F32)
    branch = lambda y, lo, hi: jnp.dot(y.astype(BF16), wb_ref[lo:hi, :], preferred_element_type=F32)

    ya = h_ref[...] * _sigmoid(proj(w_ao)) * _silu(proj(w_az))
    acc = _sigmoid(proj(w_ga)) * branch(ya, 0, W_A)

    v = proj(w_bv)
    mu = jnp.mean(v, axis=-1, keepdims=True)
    var = jnp.mean(jnp.square(v - mu), axis=-1, keepdims=True)
    vn = (v - mu) * lax.rsqrt(var + EPS) * lg_ref[...] + lb_ref[...]
    if emit_vn:
        vn_out[...] = vn
    vn_s[...] = vn.astype(BF16)
    gate_b = proj(w_bu) * _silu(proj(w_bz))
    tri = (lax.broadcasted_iota(jnp.int32, (CHUNK_B, CHUNK_B), 0)
           >= lax.broadcasted_iota(jnp.int32, (CHUNK_B, CHUNK_B), 1))
    for g in range(G_B):
        wg = jnp.where(tri, wsp_ref[g], 0.0).astype(BF16)
        cs = slice(g * DG_B, (g + 1) * DG_B)
        for c in range(nchunks):
            rs = slice(c * CHUNK_B, (c + 1) * CHUNK_B)
            yb_s[rs, cs] = jnp.dot(wg, vn_s[rs, cs], preferred_element_type=F32) + bsp_ref[g]
    acc = acc + _sigmoid(proj(w_gb)) * branch(gate_b * yb_s[...], W_A, W_A + W_B)

    unslab = lambda ref: jnp.concatenate([ref[0], ref[1]], axis=1)
    ls = [unslab(l0), unslab(l1), unslab(l2)]
    mx = jnp.maximum(jnp.maximum(ls[0], ls[1]), ls[2])
    es = [jnp.exp(l - mx) for l in ls]
    inv_tot = 1.0 / (es[0] + es[1] + es[2])
    pc = None
    for g, o_ref in enumerate((o0, o1, o2)):
        gs = slice(g * WG_C, (g + 1) * WG_C)
        yc = unslab(o_ref) * (es[g] * inv_tot) * _silu(cz_ref[:, gs])
        t = branch(yc, W_A + W_B + g * WG_C, W_A + W_B + (g + 1) * WG_C)
        pc = t if pc is None else pc + t
    acc = acc + _sigmoid(proj(w_gc)) * pc

    out = x + jnp.dot(acc.astype(BF16), wo_ref[...], preferred_element_type=F32)
    if final:
        rr = lax.rsqrt(jnp.mean(out * out, axis=-1, keepdims=True) + EPS)
        out = out * rr * fg_ref[...]
    out_ref[...] = out


def _post(x, p1, base, h, os_, ls_, norm_g, w_main, ln_g, ln_b, w_sp, b_sp_col, w_branch, w_out, final_g, l, final,
          emit_vn):
    rows = x.shape[0]
    tm = min(rows, POST_TM)
    once = pl.Buffered(1)
    row = lambda w: pl.BlockSpec((tm, w), lambda i: (i, 0))
    slab = pl.BlockSpec((2, tm, WG_C // 2), lambda i: (0, i, 0))
    wblk = lambda b: pl.BlockSpec((None, WBLK, D_MODEL), lambda i, b=b: (l, b, 0), pipeline_mode=once)
    vec = lambda w: pl.BlockSpec((None, 1, w), lambda i: (l, 0, 0), pipeline_mode=once)
    in_specs = (
        [row(D_MODEL), row(W_A)] + [slab] * 6
        + [pl.BlockSpec((tm, W_C), lambda i: (i, (base + OFF_CZ - OFF_CQ) // W_C)), vec(D_MODEL)]
        + [wblk(b) for b in (BLK_AO, BLK_AZ, BLK_BU, BLK_BV, BLK_BZ, BLK_GA, BLK_GB, BLK_GC)]
        + [vec(W_B), vec(W_B),
           pl.BlockSpec((None, G_B, CHUNK_B, CHUNK_B), lambda i: (l, 0, 0, 0), pipeline_mode=once),
           pl.BlockSpec((None, G_B, CHUNK_B, 1), lambda i: (l, 0, 0, 0), pipeline_mode=once),
           pl.BlockSpec((None, W_A + W_B + W_C, D_MODEL), lambda i: (l, 0, 0), pipeline_mode=once),
           pl.BlockSpec((None, D_MODEL, D_MODEL), lambda i: (l, 0, 0), pipeline_mode=once),
           pl.BlockSpec((1, D_MODEL), lambda i: (0, 0), pipeline_mode=once)])
    out_specs = [row(D_MODEL)]
    out_shape = [jax.ShapeDtypeStruct((rows, D_MODEL), F32)]
    if emit_vn:
        out_specs.append(row(W_B))
        out_shape.append(jax.ShapeDtypeStruct((rows, W_B), F32))
    res = pl.pallas_call(
        functools.partial(_post_kernel, nchunks=tm // CHUNK_B, final=final, emit_vn=emit_vn),
        grid=(rows // tm,),
        in_specs=in_specs,
        out_specs=out_specs,
        out_shape=out_shape,
        scratch_shapes=[pltpu.VMEM((tm, W_B), BF16), pltpu.VMEM((tm, W_B), F32)],
        compiler_params=_cparams(1),
        name="post",
    )(x, h, *os_, *ls_, p1, norm_g, *([w_main] * 8), ln_g, ln_b, w_sp, b_sp_col, w_branch, w_out, final_g)
    return (res[0], res[1]) if emit_vn else (res[0], None)


def _rope_tables(pos):
    half = ROT_DIM // 2
    inv = ROPE_THETA ** (-jnp.arange(half, dtype=F32) / half)
    rest = DH_C - ROT_DIM
    inv_h = jnp.concatenate([inv, inv, jnp.zeros((rest,), F32)])
    sgn_h = jnp.concatenate([-jnp.ones((half,), F32), jnp.ones((half,), F32), jnp.zeros((rest,), F32)])
    inv_l = jnp.concatenate([inv_h, inv_h])[None, :]
    sgn_l = jnp.concatenate([sgn_h, sgn_h])[None, :]
    ang = pos.astype(F32)[:, None] * inv_l
    return jnp.cos(ang), jnp.sin(ang) * sgn_l


def _stack_kv(k_rows, v_rows):
    b, t, _ = k_rows.shape
    return jnp.stack([k_rows.reshape(b, t, HG_C, DH_C), v_rows.reshape(b, t, HG_C, DH_C)], axis=2)


def kernel(x_prompt, x_sample, state_C, state_n, state_m, state_conv, cache_kv_w128, cache_kv_w512, cache_kv_w2048,
           norm_g, w_in, b_igate, b_fgate, conv_w, conv_b, ln_v_g, ln_v_b, w_spatial, b_spatial, w_branch, w_out,
           final_norm_g):
    depth = w_in.shape[0]
    bp, seq, _ = x_prompt.shape
    bs, n_tok, _ = x_sample.shape
    assert bp == 1 and seq % ATT_TILE == 0 and n_tok <= SAMPLE_ROWS // 2 and n_tok >= CONV_W - 1
    caches = (cache_kv_w128, cache_kv_w512, cache_kv_w2048)
    pad_tok = SAMPLE_ROWS - n_tok
    rows_s = bs * SAMPLE_ROWS

    w_in_t = jnp.swapaxes(w_in, 1, 2)
    w_main = _pack_w_in(w_in_t)
    gate_bias = jnp.concatenate([b_igate, b_fgate], axis=1)
    bias_row = jnp.pad(gate_bias, ((0, 0), (0, GATE_LANES - 2 * H_A)))[:, None, :]
    bias_col = gate_bias[:, :, None]
    wb16 = w_branch.astype(BF16)
    wo16 = w_out.astype(BF16)
    b_sp_col = b_spatial[..., None]
    norm_g3, conv_b3 = norm_g[:, None, :], conv_b[:, None, :]
    ln_g3, ln_b3 = ln_v_g[:, None, :], ln_v_b[:, None, :]
    seqs_per_chunk = CHUNK_B // SAMPLE_ROWS
    w8 = jnp.pad(w_spatial[:, :, :n_tok, :n_tok], ((0, 0), (0, 0), (0, pad_tok), (0, pad_tok)))
    w_sp_s = jnp.einsum('ab,lgij->lgaibj', jnp.eye(seqs_per_chunk, dtype=F32), w8).reshape(
        depth, G_B, CHUNK_B, CHUNK_B)
    b_sp_s = jnp.tile(jnp.pad(b_spatial[:, :, :n_tok], ((0, 0), (0, 0), (0, pad_tok))),
                      (1, 1, seqs_per_chunk))[..., None]

    cos_p, sin_p = _rope_tables(jnp.arange(seq))
    cos_s, sin_s = _rope_tables(PAST_LEN + jnp.arange(SAMPLE_ROWS))

    hp = x_prompt.reshape(seq, D_MODEL)
    hs = jnp.pad(x_sample, ((0, 0), (0, pad_tok), (0, 0))).reshape(rows_s, D_MODEL)
    fg = final_norm_g[None, :]
    zeros_c = jnp.zeros((1, 1, H_A, DH_A, DH_A), F32)
    zeros_n = jnp.zeros((1, 1, H_A, 1, DH_A), F32)
    zeros_m = jnp.zeros((1, 1, H_A, 1, GATE_LANES), F32)
    zeros_tail = jnp.zeros((1, 1, 8, 2 * W_A), F32)
    n0_s = state_n[:, :, :, None, :]
    m0_s = jnp.broadcast_to(state_m[:, :, :, None, None], (depth, bs, H_A, 1, GATE_LANES))
    tail0_s = jnp.pad(state_conv, ((0, 0), (0, 0), (8 - (CONV_W - 1), 0), (0, 0)))
    views = _cache_views(caches)

    p_out = {k: [] for k in ('C', 'n', 'm', 'conv', 'kv0', 'kv1', 'kv2')}
    s_out = {k: [] for k in ('C', 'n', 'm', 'conv', 'chunk_v', 'kv0', 'kv1', 'kv2')}
    for l in range(depth):
        final = l == depth - 1

        ya, pp, c1, n1, m1, conv1 = _seq_a(hp, norm_g3, w_main, w_in_t, bias_row, bias_col, conv_w, conv_b3,
                                           zeros_c, zeros_n, zeros_m, zeros_tail, l, L=MLSTM_L)
        os_, ls_, krs = [], [], []
        for g, (win, d) in enumerate(DIL_PATTERNS):
            o_g, l_g, kr_g = _dil_attn(pp, 0, cos_p, sin_p, g, d)
            os_.append(o_g)
            ls_.append(l_g)
            keep = min(win, seq)
            v_g = pp[seq - keep:, OFF_CV - OFF_CQ + g * WG_C:OFF_CV - OFF_CQ + (g + 1) * WG_C]
            p_out['kv%d' % g].append(_stack_kv(kr_g[None, seq - keep:], v_g[None]))
        hp, _ = _post(hp, pp, 0, ya, os_, ls_, norm_g3, w_main, ln_g3, ln_b3, w_spatial, b_sp_col, wb16, wo16, fg, l,
                      final, False)
        p_out['C'].append(c1)
        p_out['n'].append(n1)
        p_out['m'].append(m1)
        p_out['conv'].append(conv1)

        ps, gcol, grow = _inproj(hs, norm_g3, w_main, w_in_t, l)
        ya, c1, n1, m1, conv1 = _mlstm(ps, gcol, grow, bias_row, bias_col, conv_w, conv_b3,
                                       state_C, n0_s, m0_s, tail0_s, l, l, L=SAMPLE_ROWS, t_valid=n_tok)
        os_, ls_, kr = _sample_attn(ps, cos_s, sin_s, views, l, n_tok)
        hs, vn = _post(hs, ps, OFF_CQ, ya, os_, ls_, norm_g3, w_main, ln_g3, ln_b3, w_sp_s, b_sp_s, wb16, wo16, fg, l,
                       final, True)
        s_out['C'].append(c1)
        s_out['n'].append(n1)
        s_out['m'].append(m1)
        s_out['conv'].append(conv1)
        s_out['chunk_v'].append(vn.reshape(bs, SAMPLE_ROWS, W_B)[:, :n_tok])
        kr3 = kr.reshape(bs, SAMPLE_ROWS, W_C)[:, :n_tok]
        v3 = ps[:, OFF_CV:OFF_CV + W_C].reshape(bs, SAMPLE_ROWS, W_C)[:, :n_tok]
        for g in range(len(DIL_PATTERNS)):
            gs = slice(g * WG_C, (g + 1) * WG_C)
            s_out['kv%d' % g].append(_stack_kv(kr3[:, :, gs], v3[:, :, gs]))

    stk = lambda d, k: jnp.stack(d[k], axis=0)
    y_prompt = hp.reshape(bp, seq, D_MODEL)
    y_sample = hs.reshape(bs, SAMPLE_ROWS, D_MODEL)[:, :n_tok]
    return (y_prompt, y_sample,
            stk(p_out, 'C'), stk(p_out, 'n'), stk(p_out, 'm'), stk(p_out, 'conv'),
            stk(p_out, 'kv0'), stk(p_out, 'kv1'), stk(p_out, 'kv2'),
            stk(s_out, 'C'), stk(s_out, 'n'), stk(s_out, 'm'), stk(s_out, 'conv'), stk(s_out, 'chunk_v'),
            stk(s_out, 'kv0'), stk(s_out, 'kv1'), stk(s_out, 'kv2'))
```

```python
import functools

import jax
import jax.numpy as jnp
from jax import lax
from jax.experimental import pallas as pl
from jax.experimental.pallas import tpu as pltpu

F32 = jnp.float32
BF16 = jnp.bfloat16

D_MODEL = 1024
H_A = 4
DH_A = 256
W_A = H_A * DH_A
CONV_W = 4
G_B = 4
CHUNK_B = 128
W_B = 1024
DG_B = W_B // G_B
DIL_PATTERNS = ((128, 1), (512, 4), (2048, 16))
HG_C = 4
DH_C = 64
WG_C = HG_C * DH_C
W_C = len(DIL_PATTERNS) * WG_C
SPAN_C = 128
ROT_DIM = DH_C // 4
ROPE_THETA = 500000.0
EPS = 1e-6
PAST_LEN = 16384

N_PACK = 14336
WBLK = 1024
BLK_AO, BLK_AZ, BLK_BU, BLK_BV, BLK_BZ, BLK_C0, BLK_GA, BLK_GB, BLK_GC = 3, 4, 5, 6, 7, 8, 11, 12, 13
N_P1 = 6144
OFF_CQ, OFF_CK, OFF_CV, OFF_CZ = 3072, 3840, 4608, 5376
GATE_LANES = 128

INPROJ_TM = 2048
POST_TM = 512
MLSTM_BAND = 128
MLSTM_L = 256
SAMPLE_ROWS = 8
ATT_TILE = 2048
VMEM_LIMIT = 56 * 1024 * 1024


def _cparams(n_axes):
    return pltpu.CompilerParams(dimension_semantics=("arbitrary",) * n_axes, vmem_limit_bytes=VMEM_LIMIT)


def _dot(a, b):
    return jnp.dot(a.astype(BF16), b.astype(BF16), preferred_element_type=F32)


def _dot_nt(a, b):
    return lax.dot_general(a.astype(BF16), b.astype(BF16), (((1,), (1,)), ((), ())), preferred_element_type=F32)


def _dot_tn(a, b):
    return lax.dot_general(a.astype(BF16), b.astype(BF16), (((0,), (0,)), ((), ())), preferred_element_type=F32)


def _sigmoid(x):
    return 0.5 * jnp.tanh(0.5 * x) + 0.5


def _silu(x):
    return x * _sigmoid(x)


def _inproj_kernel(x_ref, g_ref, w_ref, wg_ref, p_ref, gc_ref, gr_ref, xn_ref):
    nt = (((1,), (1,)), ((), ()))

    @pl.when(pl.program_id(1) == 0)
    def _():
        x = x_ref[...]
        r = lax.rsqrt(jnp.mean(x * x, axis=-1, keepdims=True) + EPS)
        xn = (x * r * g_ref[...]).astype(BF16)
        xn_ref[...] = xn
        wg = wg_ref[...].astype(BF16)
        gc_ref[...] = lax.dot_general(xn, wg, nt, preferred_element_type=F32)
        gr_ref[...] = lax.dot_general(wg[0:8, :], xn, nt, preferred_element_type=F32)

    p_ref[...] = lax.dot_general(xn_ref[...], w_ref[...], nt, preferred_element_type=F32)


def _inproj(x, norm_g, w_main, w_in_t, l):
    rows = x.shape[0]
    gate_blk = 4 * W_A // GATE_LANES
    tm = min(rows, INPROJ_TM)
    tn = WBLK
    n_a = 3 * W_A // tn
    wblk = lambda i, j: (l, jnp.where(j < n_a, j, j + (BLK_C0 - n_a)), 0)
    return pl.pallas_call(
        _inproj_kernel,
        grid=(rows // tm, N_P1 // tn),
        in_specs=[
            pl.BlockSpec((tm, D_MODEL), lambda i, j: (i, 0)),
            pl.BlockSpec((None, 1, D_MODEL), lambda i, j: (l, 0, 0)),
            pl.BlockSpec((None, tn, D_MODEL), wblk),
            pl.BlockSpec((None, GATE_LANES, D_MODEL), lambda i, j: (l, gate_blk, 0)),
        ],
        out_specs=[
            pl.BlockSpec((tm, tn), lambda i, j: (i, j)),
            pl.BlockSpec((tm, GATE_LANES), lambda i, j: (i, 0)),
            pl.BlockSpec((None, 8, tm), lambda i, j: (i, 0, 0)),
        ],
        out_shape=[
            jax.ShapeDtypeStruct((rows, N_P1), F32),
            jax.ShapeDtypeStruct((rows, GATE_LANES), F32),
            jax.ShapeDtypeStruct((rows // tm, 8, tm), F32),
        ],
        scratch_shapes=[pltpu.VMEM((tm, D_MODEL), BF16)],
        compiler_params=_cparams(2),
        name="inproj",
    )(x, norm_g, w_main, w_in_t)


def _pack_kernel(a_ref, b_ref, w_ref, *, first_shifted):
    j = pl.program_id(1)

    @pl.when(j < first_shifted)
    def _():
        w_ref[...] = a_ref[...].astype(BF16)

    @pl.when(j >= first_shifted)
    def _():
        w_ref[...] = jnp.concatenate([a_ref[2 * H_A:, :], b_ref[...]], axis=0).astype(BF16)


def _pack_w_in(w_in_t):
    depth = w_in_t.shape[0]
    tn = 2048
    gate_off = 4 * W_A
    assert gate_off % tn == 0 and w_in_t.shape[1] == N_PACK + 2 * H_A and 2 * H_A == 8
    kern = functools.partial(_pack_kernel, first_shifted=gate_off // tn)
    return pl.pallas_call(
        kern,
        grid=(depth, N_PACK // tn),
        in_specs=[
            pl.BlockSpec((None, tn, D_MODEL), lambda l, j: (l, j, 0)),
            pl.BlockSpec((None, 8, D_MODEL), lambda l, j: (l, (j + 1) * (tn // 8), 0)),
        ],
        out_specs=pl.BlockSpec((None, tn, D_MODEL), lambda l, j: (l, j, 0)),
        out_shape=jax.ShapeDtypeStruct((depth, N_PACK, D_MODEL), BF16),
        compiler_params=_cparams(2),
        name="pack_w_in",
    )(w_in_t, w_in_t)


def _conv_silu(ext_ref, cols, w, b, L):
    y = b
    for back in range(CONV_W):
        y = y + ext_ref[8 - back:8 - back + L, cols] * w[CONV_W - 1 - back:CONV_W - back]
    return _silu(y)


def _mlstm_core(g_col, g_row, get_v, ext_s, cw_ref, cb_ref, c_s, n_s, m_s, y_ref, L, t_valid, side_work=None):
    tick = side_work if side_work is not None else (lambda: None)
    neg_inf = jnp.float32(-jnp.inf)
    ig_col, lf_col = g_col, jax.nn.log_sigmoid(g_col)
    ig_row, lf_row = g_row, jax.nn.log_sigmoid(g_row)
    if t_valid < L:
        vc = lax.broadcasted_iota(jnp.int32, (L, GATE_LANES), 0) < t_valid
        vr = lax.broadcasted_iota(jnp.int32, (8, L), 1) < t_valid
        ig_col, lf_col = jnp.where(vc, ig_col, neg_inf), jnp.where(vc, lf_col, 0.0)
        ig_row, lf_row = jnp.where(vr, ig_row, neg_inf), jnp.where(vr, lf_row, 0.0)
    ti = lax.broadcasted_iota(jnp.int32, (L, L), 0)
    si = lax.broadcasted_iota(jnp.int32, (L, L), 1)
    causal = ti >= si
    b_col = jnp.dot(causal.astype(F32), lf_col, preferred_element_type=F32, precision=lax.Precision.HIGHEST)
    b_row = jnp.dot(lf_row, (ti <= si).astype(F32), preferred_element_type=F32, precision=lax.Precision.HIGHEST)
    last = t_valid - 1
    band = min(L, MLSTM_BAND)
    nt = (((1,), (1,)), ((), ()))

    for h in range(H_A):
        cs = slice(h * DH_A, (h + 1) * DH_A)
        ks = slice(W_A + h * DH_A, W_A + (h + 1) * DH_A)
        q = _conv_silu(ext_s, cs, cw_ref[:, cs], cb_ref[:, cs], L)
        tick()
        k = _conv_silu(ext_s, ks, cw_ref[:, ks], cb_ref[:, ks], L) * (DH_A ** -0.5)
        tick()
        v = get_v(h)
        q16, k16, v16 = q.astype(BF16), k.astype(BF16), v.astype(BF16)
        bc = b_col[:, H_A + h:H_A + h + 1]
        igc = ig_col[:, h:h + 1]
        br = b_row[H_A + h:H_A + h + 1, :]
        igr = ig_row[h:h + 1, :]
        m_prev = m_s[h][:, 0:1]
        c_prev = c_s[h]
        n_prev = n_s[h]
        c16 = c_prev.astype(BF16)

        for r in range(L // band):
            rs = slice(r * band, (r + 1) * band)
            kw = (r + 1) * band
            ti = lax.broadcasted_iota(jnp.int32, (band, kw), 0) + r * band
            si = lax.broadcasted_iota(jnp.int32, (band, kw), 1)
            logw = jnp.where(ti >= si, bc[rs] - br[:, 0:kw] + igr[:, 0:kw], neg_inf)
            inter = bc[rs] + m_prev
            m_t = jnp.maximum(inter, jnp.max(logw, axis=1, keepdims=True))
            w_intra = jnp.exp(logw - m_t)
            w_inter = jnp.exp(inter - m_t)
            s = w_intra * lax.dot_general(q16[rs], k16[0:kw], nt, preferred_element_type=F32)
            num = (jnp.dot(s.astype(BF16), v16[0:kw], preferred_element_type=F32)
                   + w_inter * lax.dot_general(q16[rs], c16, nt, preferred_element_type=F32))
            den = (jnp.sum(s, axis=1, keepdims=True)
                   + w_inter * jnp.sum(q[rs] * n_prev, axis=1, keepdims=True))
            y_ref[0, rs, cs] = num / jnp.maximum(jnp.abs(den), jnp.exp(-m_t))
            if r == last // band:
                m_new = m_t[last - r * band:last - r * band + 1, :]
            tick()

        b_last = bc[last:last + 1, :]
        decay = jnp.exp(b_last + m_prev - m_new)
        w_s = jnp.exp(b_last - bc + igc - m_new)
        c_s[h] = decay * c_prev + lax.dot_general((w_s * v).astype(BF16), k16, (((0,), (0,)), ((), ())),
                                                  preferred_element_type=F32)
        n_s[h] = decay * n_prev + jnp.sum(w_s * k, axis=0, keepdims=True)
        m_s[h] = jnp.broadcast_to(m_new, (1, GATE_LANES))
        tick()

    ext_s[0:8, :] = ext_s[t_valid:t_valid + 8, :]


def _mlstm_state_io(c0_ref, n0_ref, m0_ref, tail0_ref, c_s, n_s, m_s, ext_s):
    @pl.when(pl.program_id(1) == 0)
    def _():
        c_s[...] = c0_ref[0]
        n_s[...] = n0_ref[0]
        m_s[...] = m0_ref[0]
        ext_s[0:8, :] = tail0_ref[0]


def _mlstm_state_out(c_out, n_out, m_out, tail_out, c_s, n_s, m_s, ext_s):
    @pl.when(pl.program_id(1) == pl.num_programs(1) - 1)
    def _():
        c_out[0] = c_s[...]
        n_out[0] = n_s[...]
        m_out[0] = m_s[...]
        tail_out[0] = ext_s[0:8, :]


def _mlstm_kernel(p_ref, gc_ref, gr_ref, brow_ref, bcol_ref, cw_ref, cb_ref, c0_ref, n0_ref, m0_ref, tail0_ref,
                  y_ref, c_out, n_out, m_out, tail_out, c_s, n_s, m_s, ext_s, *, L, t_valid):
    _mlstm_state_io(c0_ref, n0_ref, m0_ref, tail0_ref, c_s, n_s, m_s, ext_s)
    ext_s[8:8 + L, :] = p_ref[0, :, 0:2 * W_A]
    get_v = lambda h: p_ref[0, :, 2 * W_A + h * DH_A:2 * W_A + (h + 1) * DH_A]
    _mlstm_core(gc_ref[0] + brow_ref[...], gr_ref[0] + bcol_ref[...], get_v, ext_s, cw_ref, cb_ref,
                c_s, n_s, m_s, y_ref, L, t_valid)
    _mlstm_state_out(c_out, n_out, m_out, tail_out, c_s, n_s, m_s, ext_s)


def _seq_a_kernel(x_ref, cos_ref, sin_ref, ng_ref, w_aq, w_ak, w_av, w_c0, w_c1, w_c2, wg_ref, brow_ref, bcol_ref,
                  cw_ref, cb_ref, c0_ref, n0_ref, m0_ref, tail0_ref, y_ref, pc_ref, c_out, n_out, m_out, tail_out,
                  c_s, n_s, m_s, ext_s, v_s, *, L):
    _mlstm_state_io(c0_ref, n0_ref, m0_ref, tail0_ref, c_s, n_s, m_s, ext_s)
    nt = (((1,), (1,)), ((), ()))
    x = x_ref[0]
    r = lax.rsqrt(jnp.mean(x * x, axis=-1, keepdims=True) + EPS)
    xn = (x * r * ng_ref[...]).astype(BF16)
    proj = lambda w_ref: lax.dot_general(xn, w_ref[...], nt, preferred_element_type=F32)
    wg = wg_ref[...].astype(BF16)
    g_col = lax.dot_general(xn, wg, nt, preferred_element_type=F32) + brow_ref[...]
    g_row = lax.dot_general(wg[0:8, :], xn, nt, preferred_element_type=F32) + bcol_ref[...]
    ext_s[8:8 + L, 0:W_A] = proj(w_aq)
    ext_s[8:8 + L, W_A:2 * W_A] = proj(w_ak)
    v_s[...] = proj(w_av)
    piece = 256
    todo = [(w_ref, j, lo) for j, w_ref in enumerate((w_c0, w_c1, w_c2)) for lo in range(0, WBLK, piece)]

    assert piece == WG_C

    def emit_piece():
        w_ref, j, lo = todo.pop(0)
        col = j * WBLK + lo
        y = lax.dot_general(xn, w_ref[lo:lo + piece, :], nt, preferred_element_type=F32)
        if col < 2 * W_C:
            cos = jnp.concatenate([cos_ref[0], cos_ref[0]], axis=1)
            sin = jnp.concatenate([sin_ref[0], sin_ref[0]], axis=1)
            y = _rope(y, cos, sin)
            if col < W_C:
                y = y * (DH_C ** -0.5)
        pc_ref[0, :, col:col + piece] = y

    side_work = lambda: emit_piece() if todo else None
    get_v = lambda h: v_s[:, h * DH_A:(h + 1) * DH_A]
    _mlstm_core(g_col, g_row, get_v, ext_s, cw_ref, cb_ref, c_s, n_s, m_s, y_ref, L, L, side_work)
    while todo:
        emit_piece()
    _mlstm_state_out(c_out, n_out, m_out, tail_out, c_s, n_s, m_s, ext_s)


def _seq_a(x, cos_t, sin_t, norm_g, w_main, w_in_t, bias_row, bias_col, conv_w, conv_b, c0, n0, m0, tail0, l, *, L):
    rows = x.shape[0]
    nch = rows // L
    x3 = x.reshape(nch, L, D_MODEL)
    gate_blk = 4 * W_A // GATE_LANES
    once = pl.Buffered(1)
    chunk = lambda b, c: (c, 0, 0)
    layer3 = lambda b, c: (l, 0, 0)
    wblk = lambda blk: pl.BlockSpec((None, WBLK, D_MODEL), lambda b, c, blk=blk: (l, blk, 0), pipeline_mode=once)
    st5 = lambda b, c: (0, 0, 0, 0, 0)
    st4 = lambda b, c: (0, 0, 0, 0)
    first4 = lambda b, c: (0, 0, 0, 0)
    first3 = lambda b, c: (0, 0, 0)
    y, pc, c1, n1, m1, tail1 = pl.pallas_call(
        functools.partial(_seq_a_kernel, L=L),
        grid=(1, nch),
        in_specs=[
            pl.BlockSpec((1, L, D_MODEL), chunk),
            pl.BlockSpec((1, L, 2 * DH_C), chunk),
            pl.BlockSpec((1, L, 2 * DH_C), chunk),
            pl.BlockSpec((None, 1, D_MODEL), layer3, pipeline_mode=once),
        ] + [wblk(b) for b in (0, 1, 2, BLK_C0, BLK_C0 + 1, BLK_C0 + 2)] + [
            pl.BlockSpec((None, GATE_LANES, D_MODEL), lambda b, c: (l, gate_blk, 0), pipeline_mode=once),
            pl.BlockSpec((None, 1, GATE_LANES), layer3, pipeline_mode=once),
            pl.BlockSpec((None, 8, 1), layer3, pipeline_mode=once),
            pl.BlockSpec((None, CONV_W, 2 * W_A), layer3, pipeline_mode=once),
            pl.BlockSpec((None, 1, 2 * W_A), layer3, pipeline_mode=once),
            pl.BlockSpec((None, 1, H_A, DH_A, DH_A), st5, pipeline_mode=once),
            pl.BlockSpec((None, 1, H_A, 1, DH_A), st5, pipeline_mode=once),
            pl.BlockSpec((None, 1, H_A, 1, GATE_LANES), st5, pipeline_mode=once),
            pl.BlockSpec((None, 1, 8, 2 * W_A), st4, pipeline_mode=once),
        ],
        out_specs=[
            pl.BlockSpec((1, L, W_A), chunk),
            pl.BlockSpec((1, L, 3 * WBLK), chunk),
            pl.BlockSpec((1, H_A, DH_A, DH_A), first4),
            pl.BlockSpec((1, H_A, 1, DH_A), first4),
            pl.BlockSpec((1, H_A, 1, GATE_LANES), first4),
            pl.BlockSpec((1, 8, 2 * W_A), first3),
        ],
        out_shape=[
            jax.ShapeDtypeStruct((nch, L, W_A), F32),
            jax.ShapeDtypeStruct((nch, L, 3 * WBLK), F32),
            jax.ShapeDtypeStruct((1, H_A, DH_A, DH_A), F32),
            jax.ShapeDtypeStruct((1, H_A, 1, DH_A), F32),
            jax.ShapeDtypeStruct((1, H_A, 1, GATE_LANES), F32),
            jax.ShapeDtypeStruct((1, 8, 2 * W_A), F32),
        ],
        scratch_shapes=[
            pltpu.VMEM((H_A, DH_A, DH_A), F32),
            pltpu.VMEM((H_A, 1, DH_A), F32),
            pltpu.VMEM((H_A, 1, GATE_LANES), F32),
            pltpu.VMEM((L + 8, 2 * W_A), F32),
            pltpu.VMEM((L, W_A), F32),
        ],
        compiler_params=_cparams(2),
        name="seq_a",
    )(x3, cos_t.reshape(nch, L, 2 * DH_C), sin_t.reshape(nch, L, 2 * DH_C), norm_g, *([w_main] * 6), w_in_t,
      bias_row, bias_col, conv_w, conv_b, c0, n0, m0, tail0)
    return (y.reshape(rows, W_A), pc.reshape(rows, 3 * WBLK), c1, n1[:, :, 0, :], m1[:, :, 0, 0],
            tail1[:, 8 - (CONV_W - 1):, :])


def _mlstm(p, gcol, grow, bias_row, bias_col, conv_w, conv_b, c0, n0, m0, tail0, l, ls, *, L, t_valid):
    batch = c0.shape[1]
    rows = p.shape[0]
    nch = rows // (batch * L)
    p3 = p.reshape(batch * nch, L, N_P1)
    gc3 = gcol.reshape(batch * nch, L, GATE_LANES)
    tm = grow.shape[2]
    if L % GATE_LANES == 0:
        per = tm // L
        gr3, gr_index = grow, (lambda b, c: ((b * nch + c) // per, 0, (b * nch + c) % per))
    else:
        gr3 = grow.transpose(1, 0, 2).reshape(8, batch * nch, L).transpose(1, 0, 2)
        gr_index = lambda b, c: (b * nch + c, 0, 0)
    kern = functools.partial(_mlstm_kernel, L=L, t_valid=t_valid)
    chunk = lambda b, c: (b * nch + c, 0, 0)
    layer3 = lambda b, c: (l, 0, 0)
    per_b4 = lambda b, c: (b, 0, 0, 0)
    per_b3 = lambda b, c: (b, 0, 0)
    st5 = lambda b, c: (ls, b, 0, 0, 0)
    st4 = lambda b, c: (ls, b, 0, 0)
    y, c1, n1, m1, tail1 = pl.pallas_call(
        kern,
        grid=(batch, nch),
        in_specs=[
            pl.BlockSpec((1, L, 3 * W_A), chunk),
            pl.BlockSpec((1, L, GATE_LANES), chunk),
            pl.BlockSpec((1, 8, L), gr_index),
            pl.BlockSpec((None, 1, GATE_LANES), layer3),
            pl.BlockSpec((None, 8, 1), layer3),
            pl.BlockSpec((None, CONV_W, 2 * W_A), layer3),
            pl.BlockSpec((None, 1, 2 * W_A), layer3),
            pl.BlockSpec((None, 1, H_A, DH_A, DH_A), st5),
            pl.BlockSpec((None, 1, H_A, 1, DH_A), st5),
            pl.BlockSpec((None, 1, H_A, 1, GATE_LANES), st5),
            pl.BlockSpec((None, 1, 8, 2 * W_A), st4),
        ],
        out_specs=[
            pl.BlockSpec((1, L, W_A), chunk),
            pl.BlockSpec((1, H_A, DH_A, DH_A), per_b4),
            pl.BlockSpec((1, H_A, 1, DH_A), per_b4),
            pl.BlockSpec((1, H_A, 1, GATE_LANES), per_b4),
            pl.BlockSpec((1, 8, 2 * W_A), per_b3),
        ],
        out_shape=[
            jax.ShapeDtypeStruct((batch * nch, L, W_A), F32),
            jax.ShapeDtypeStruct((batch, H_A, DH_A, DH_A), F32),
            jax.ShapeDtypeStruct((batch, H_A, 1, DH_A), F32),
            jax.ShapeDtypeStruct((batch, H_A, 1, GATE_LANES), F32),
            jax.ShapeDtypeStruct((batch, 8, 2 * W_A), F32),
        ],
        scratch_shapes=[
            pltpu.VMEM((H_A, DH_A, DH_A), F32),
            pltpu.VMEM((H_A, 1, DH_A), F32),
            pltpu.VMEM((H_A, 1, GATE_LANES), F32),
            pltpu.VMEM((L + 8, 2 * W_A), F32),
        ],
        compiler_params=_cparams(2),
        name="mlstm",
    )(p3, gc3, gr3, bias_row, bias_col, conv_w, conv_b, c0, n0, m0, tail0)
    return y.reshape(rows, W_A), c1, n1[:, :, 0, :], m1[:, :, 0, 0], tail1[:, 8 - (CONV_W - 1):, :]


def _rope(x, cos, sin):
    width = x.shape[1]
    lane = lax.broadcasted_iota(jnp.int32, (1, width), 1) % DH_C
    half = ROT_DIM // 2
    partner = jnp.where(lane < half, pltpu.roll(x, width - half, 1), pltpu.roll(x, half, 1))
    return x * cos + partner * sin


def _dil_attn_kernel(q_ref, k_ref, v_ref, o_ref, mx_ref, den_ref, qs, kbuf, vbuf, bias, *, d):
    i = pl.program_id(0)
    T = ATT_TILE
    HL = WG_C // 2
    nblk = T // SPAN_C
    nres = T // (SPAN_C * d)
    cur = (i % 2) * T
    prev = T - cur
    neg_inf = jnp.float32(-jnp.inf)

    def split(x):
        return x[:, 0:HL], x[:, HL:WG_C]

    def put(buf, lo, x):
        a, b = split(x)
        buf[0, pl.ds(lo, T), :] = a
        buf[1, pl.ds(lo, T), :] = b

    @pl.when(i == 0)
    def _():
        ii = lax.broadcasted_iota(jnp.int32, (SPAN_C, 2 * SPAN_C), 0)
        jj = lax.broadcasted_iota(jnp.int32, (SPAN_C, 2 * SPAN_C), 1)
        band = (jj >= ii) & (jj <= ii + SPAN_C)
        bias[1] = jnp.where(band, 0.0, neg_inf)
        bias[0] = jnp.where(band & (jj >= SPAN_C), 0.0, neg_inf)
        kbuf[:, T:2 * T, :] = jnp.zeros((2, T, HL), F32)
        vbuf[:, T:2 * T, :] = jnp.zeros((2, T, HL), F32)

    put(qs, 0, q_ref[...])
    put(kbuf, cur, k_ref[...])
    put(vbuf, cur, v_ref[...])

    lane_head = lax.broadcasted_iota(jnp.int32, (SPAN_C, WG_C), 1) // DH_C
    hm = [(lane_head == h).astype(F32) for h in range(HG_C)]

    def rows(start, size):
        return pl.ds(start, size) if d == 1 else pl.ds(start, size, stride=d)

    def take(buf, start, size):
        return jnp.concatenate([buf[0, rows(start, size), :], buf[1, rows(start, size), :]], axis=1)

    def pick(per_head):
        out = jnp.broadcast_to(per_head[0], (SPAN_C, WG_C))
        for h in range(1, HG_C):
            out = jnp.where(lane_head >= h, per_head[h], out)
        return out

    def body(blk, carry):
        r = blk % d
        c = blk // d
        qstart = r + SPAN_C * d * c
        in_tile = cur + qstart
        before = jnp.where(c > 0, in_tile - SPAN_C * d, prev + r + SPAN_C * d * (nres - 1))
        qb = take(qs, qstart, SPAN_C)
        kb = jnp.concatenate([take(kbuf, before, SPAN_C), take(kbuf, in_tile, SPAN_C)], axis=0)
        vb = jnp.concatenate([take(vbuf, before, SPAN_C), take(vbuf, in_tile, SPAN_C)], axis=0)
        has_prev = jnp.logical_or(i > 0, c > 0).astype(jnp.int32)
        kb16, vb16 = kb.astype(BF16), vb.astype(BF16)
        pvs, mxs, dens = [], [], []
        for h0 in range(0, HG_C, 2):
            q2 = jnp.concatenate([qb * hm[h0], qb * hm[h0 + 1]], axis=0)
            s = _dot_nt(q2, kb16).reshape(2, SPAN_C, 2 * SPAN_C) + bias[has_prev]
            mx = jnp.max(s, axis=2, keepdims=True)
            p = jnp.exp(s - mx)
            den = jnp.sum(p, axis=2, keepdims=True)
            pv = _dot(p.reshape(2 * SPAN_C, 2 * SPAN_C), vb16).reshape(2, SPAN_C, WG_C)
            pvs += [pv[0], pv[1]]
            mxs += [mx[0], mx[1]]
            dens += [den[0], den[1]]
        outs = (pick(pvs), pick(mxs), pick(dens))
        for ref, val in zip((o_ref, mx_ref, den_ref), outs):
            for half, part in enumerate(split(val)):
                ref[half, rows(qstart, SPAN_C), :] = part
        return carry

    lax.fori_loop(0, nblk, body, 0, unroll=8)


def _dil_attn(p, base, g, d):
    rows = p.shape[0]
    T = ATT_TILE
    kern = functools.partial(_dil_attn_kernel, d=d)
    col = lambda off: pl.BlockSpec((T, WG_C), lambda i, off=off: (i, (base + off - OFF_CQ) // WG_C + g))
    slab = pl.BlockSpec((2, T, WG_C // 2), lambda i: (0, i, 0))
    slab_shape = jax.ShapeDtypeStruct((2, rows, WG_C // 2), F32)
    return pl.pallas_call(
        kern,
        grid=(rows // T,),
        in_specs=[col(OFF_CQ), col(OFF_CK), col(OFF_CV)],
        out_specs=[slab, slab, slab],
        out_shape=[slab_shape, slab_shape, slab_shape],
        scratch_shapes=[pltpu.VMEM((2, T, WG_C // 2), F32), pltpu.VMEM((2, 2 * T, WG_C // 2), F32),
                        pltpu.VMEM((2, 2 * T, WG_C // 2), F32), pltpu.VMEM((2, SPAN_C, 2 * SPAN_C), F32)],
        compiler_params=_cparams(1),
        name="dil_attn_d%d" % d,
    )(p, p, p)


def _sample_attn_kernel(pc0, pc1, pc2, cos_ref, sin_ref, c0_ref, c1_ref, c2_ref,
                        o0, o1, o2, m0, m1, m2, d0, d1, d2, kr_ref, *, n_tok):
    pc = jnp.concatenate([pc0[0], pc1[0], pc2[0]], axis=1)
    reps = W_C // (2 * DH_C)
    cos = jnp.concatenate([cos_ref[...]] * reps, axis=1)
    sin = jnp.concatenate([sin_ref[...]] * reps, axis=1)
    qr = _rope(pc[:, 0:W_C], cos, sin) * (DH_C ** -0.5)
    kr = _rope(pc[:, W_C:2 * W_C], cos, sin)
    v = pc[:, 2 * W_C:3 * W_C]
    kr_ref[0] = kr
    nrow = HG_C * SAMPLE_ROWS
    rowh = lax.broadcasted_iota(jnp.int32, (nrow, WG_C), 0) // SAMPLE_ROWS
    laneh = lax.broadcasted_iota(jnp.int32, (nrow, WG_C), 1) // DH_C
    hm = (rowh == laneh).astype(F32)
    tok_col = lax.broadcasted_iota(jnp.int32, (nrow, 1), 0) % SAMPLE_ROWS
    out_row = lax.broadcasted_iota(jnp.int32, (SAMPLE_ROWS, WG_C), 0)
    neg_inf = jnp.float32(-jnp.inf)
    crefs = (c0_ref, c1_ref, c2_ref)
    orefs = (o0, o1, o2)
    mrefs = (m0, m1, m2)
    drefs = (d0, d1, d2)
    for g, (_, d) in enumerate(DIL_PATTERNS):
        gs = slice(g * WG_C, (g + 1) * WG_C)
        n_buf = SPAN_C * d
        kn = kr[:, gs]
        vn = v[:, gs]
        qm = jnp.concatenate([qr[:, gs]] * HG_C, axis=0) * hm
        s = _dot(qm, crefs[g][0, 0])
        tok = lax.broadcasted_iota(jnp.int32, (nrow, n_buf), 0) % SAMPLE_ROWS
        pos = lax.broadcasted_iota(jnp.int32, (nrow, n_buf), 1)
        attends = (pos >= tok) if d == 1 else ((pos & (d - 1)) == tok)
        s = jnp.where(attends | (tok >= n_tok), s, neg_inf)
        new_ok = [(u <= tok_col) if d == 1 else (u == tok_col) for u in range(n_tok)]
        s_new = [jnp.where(ok, jnp.sum(qm * kn[u:u + 1, :], axis=1, keepdims=True), neg_inf)
                 for u, ok in enumerate(new_ok)]
        mx = jnp.max(s, axis=1, keepdims=True)
        for sn in s_new:
            mx = jnp.maximum(mx, sn)
        p = jnp.exp(s - mx)
        den = jnp.sum(p, axis=1, keepdims=True)
        acc = _dot_nt(p, crefs[g][0, 1])
        for u, sn in enumerate(s_new):
            pn = jnp.exp(sn - mx)
            den = den + pn
            acc = acc + pn * vn[u:u + 1, :]
        stacked = (acc * hm, mx * hm, den * hm)
        for ref, val, pad in zip((orefs[g], mrefs[g], drefs[g]), stacked, (0.0, 0.0, 1.0)):
            v8 = val[0:SAMPLE_ROWS]
            for h in range(1, HG_C):
                v8 = v8 + val[h * SAMPLE_ROWS:(h + 1) * SAMPLE_ROWS]
            v8 = jnp.where(out_row < n_tok, v8, pad)
            for half in range(2):
                ref[half] = v8[:, half * (WG_C // 2):(half + 1) * (WG_C // 2)]


def _cache_views(caches):
    views = []
    for (win, d), c in zip(DIL_PATTERNS, caches):
        depth, batch, n_buf = c.shape[:3]
        assert n_buf == SPAN_C * d
        views.append(jnp.transpose(c, (0, 1, 3, 4, 5, 2)).reshape(depth, batch, 2, WG_C, n_buf))
    return views


def _sample_attn(p, cos_s, sin_s, views, l, n_tok):
    batch = p.shape[0] // SAMPLE_ROWS
    p3 = p.reshape(batch, SAMPLE_ROWS, N_P1)
    pcol = lambda k: pl.BlockSpec((1, SAMPLE_ROWS, 1024), lambda b, k=k: (b, 0, OFF_CQ // 1024 + k))
    tab = pl.BlockSpec((SAMPLE_ROWS, 2 * DH_C), lambda b: (0, 0))
    cache_spec = lambda d: pl.BlockSpec((None, 1, 2, WG_C, SPAN_C * d), lambda b: (l, b, 0, 0, 0))
    out = pl.BlockSpec((2, SAMPLE_ROWS, WG_C // 2), lambda b: (0, b, 0))
    res = pl.pallas_call(
        functools.partial(_sample_attn_kernel, n_tok=n_tok),
        grid=(batch,),
        in_specs=[pcol(0), pcol(1), pcol(2), tab, tab] + [cache_spec(d) for _, d in DIL_PATTERNS],
        out_specs=[out] * 9 + [pl.BlockSpec((1, SAMPLE_ROWS, W_C), lambda b: (b, 0, 0))],
        out_shape=[jax.ShapeDtypeStruct((2, batch * SAMPLE_ROWS, WG_C // 2), F32)] * 9
        + [jax.ShapeDtypeStruct((batch, SAMPLE_ROWS, W_C), F32)],
        compiler_params=_cparams(1),
        name="sample_attn",
    )(p3, p3, p3, cos_s, sin_s, *views)
    return list(res[:9]), res[9].reshape(batch * SAMPLE_ROWS, W_C)


def _post_kernel(x_ref, h_ref, o0, o1, o2, m0, m1, m2, d0, d1, d2, cz_ref, ng_ref, w_ao, w_az, w_bu, w_bv, w_bz, w_ga, w_gb,
                 w_gc, lg_ref, lb_ref, wsp_ref, bsp_ref, wb_ref, wo_ref, fg_ref, out_ref, *rest, nchunks, final,
                 emit_vn):
    if emit_vn:
        vn_out, vn_s, yb_s = rest
    else:
        vn_s, yb_s = rest
    nt = (((1,), (1,)), ((), ()))
    x = x_ref[...]
    r = lax.rsqrt(jnp.mean(x * x, axis=-1, keepdims=True) + EPS)
    xn = (x * r * ng_ref[...]).astype(BF16)
    proj = lambda w_ref: lax.dot_general(xn, w_ref[...], nt, preferred_element_type=F32)
    branch = lambda y, lo, hi: jnp.dot(y.astype(BF16), wb_ref[lo:hi, :], preferred_element_type=F32)

    ya = h_ref[...] * _sigmoid(proj(w_ao)) * _silu(proj(w_az))
    acc = _sigmoid(proj(w_ga)) * branch(ya, 0, W_A)

    v = proj(w_bv)
    mu = jnp.mean(v, axis=-1, keepdims=True)
    var = jnp.mean(jnp.square(v - mu), axis=-1, keepdims=True)
    vn = (v - mu) * lax.rsqrt(var + EPS) * lg_ref[...] + lb_ref[...]
    if emit_vn:
        vn_out[...] = vn
    vn_s[...] = vn.astype(BF16)
    gate_b = proj(w_bu) * _silu(proj(w_bz))
    tri = (lax.broadcasted_iota(jnp.int32, (CHUNK_B, CHUNK_B), 0)
           >= lax.broadcasted_iota(jnp.int32, (CHUNK_B, CHUNK_B), 1))
    for g in range(G_B):
        wg = jnp.where(tri, wsp_ref[g], 0.0).astype(BF16)
        cs = slice(g * DG_B, (g + 1) * DG_B)
        for c in range(nchunks):
            rs = slice(c * CHUNK_B, (c + 1) * CHUNK_B)
            yb_s[rs, cs] = jnp.dot(wg, vn_s[rs, cs], preferred_element_type=F32) + bsp_ref[g]
    acc = acc + _sigmoid(proj(w_gb)) * branch(gate_b * yb_s[...], W_A, W_A + W_B)

    unslab = lambda ref: jnp.concatenate([ref[0], ref[1]], axis=1)
    ms = [unslab(m0), unslab(m1), unslab(m2)]
    mx = jnp.maximum(jnp.maximum(ms[0], ms[1]), ms[2])
    es = [jnp.exp(m - mx) for m in ms]
    inv_tot = 1.0 / (es[0] * unslab(d0) + es[1] * unslab(d1) + es[2] * unslab(d2))
    pc = None
    for g, o_ref in enumerate((o0, o1, o2)):
        gs = slice(g * WG_C, (g + 1) * WG_C)
        yc = unslab(o_ref) * (es[g] * inv_tot) * _silu(cz_ref[:, gs])
        t = branch(yc, W_A + W_B + g * WG_C, W_A + W_B + (g + 1) * WG_C)
        pc = t if pc is None else pc + t
    acc = acc + _sigmoid(proj(w_gc)) * pc

    out = x + jnp.dot(acc.astype(BF16), wo_ref[...], preferred_element_type=F32)
    if final:
        rr = lax.rsqrt(jnp.mean(out * out, axis=-1, keepdims=True) + EPS)
        out = out * rr * fg_ref[...]
    out_ref[...] = out


def _post(x, p1, base, h, att, norm_g, w_main, ln_g, ln_b, w_sp, b_sp_col, w_branch, w_out, final_g, l, final,
          emit_vn):
    rows = x.shape[0]
    tm = min(rows, POST_TM)
    once = pl.Buffered(1)
    row = lambda w: pl.BlockSpec((tm, w), lambda i: (i, 0))
    slab = pl.BlockSpec((2, tm, WG_C // 2), lambda i: (0, i, 0))
    wblk = lambda b: pl.BlockSpec((None, WBLK, D_MODEL), lambda i, b=b: (l, b, 0), pipeline_mode=once)
    vec = lambda w: pl.BlockSpec((None, 1, w), lambda i: (l, 0, 0), pipeline_mode=once)
    in_specs = (
        [row(D_MODEL), row(W_A)] + [slab] * 9
        + [pl.BlockSpec((tm, W_C), lambda i: (i, (base + OFF_CZ - OFF_CQ) // W_C)), vec(D_MODEL)]
        + [wblk(b) for b in (BLK_AO, BLK_AZ, BLK_BU, BLK_BV, BLK_BZ, BLK_GA, BLK_GB, BLK_GC)]
        + [vec(W_B), vec(W_B),
           pl.BlockSpec((None, G_B, CHUNK_B, CHUNK_B), lambda i: (l, 0, 0, 0), pipeline_mode=once),
           pl.BlockSpec((None, G_B, CHUNK_B, 1), lambda i: (l, 0, 0, 0), pipeline_mode=once),
           pl.BlockSpec((None, W_A + W_B + W_C, D_MODEL), lambda i: (l, 0, 0), pipeline_mode=once),
           pl.BlockSpec((None, D_MODEL, D_MODEL), lambda i: (l, 0, 0), pipeline_mode=once),
           pl.BlockSpec((1, D_MODEL), lambda i: (0, 0), pipeline_mode=once)])
    out_specs = [row(D_MODEL)]
    out_shape = [jax.ShapeDtypeStruct((rows, D_MODEL), F32)]
    if emit_vn:
        out_specs.append(row(W_B))
        out_shape.append(jax.ShapeDtypeStruct((rows, W_B), F32))
    res = pl.pallas_call(
        functools.partial(_post_kernel, nchunks=tm // CHUNK_B, final=final, emit_vn=emit_vn),
        grid=(rows // tm,),
        in_specs=in_specs,
        out_specs=out_specs,
        out_shape=out_shape,
        scratch_shapes=[pltpu.VMEM((tm, W_B), BF16), pltpu.VMEM((tm, W_B), F32)],
        compiler_params=_cparams(1),
        name="post",
    )(x, h, *att, p1, norm_g, *([w_main] * 8), ln_g, ln_b, w_sp, b_sp_col, w_branch, w_out, final_g)
    return (res[0], res[1]) if emit_vn else (res[0], None)


def _rope_tables(pos):
    half = ROT_DIM // 2
    inv = ROPE_THETA ** (-jnp.arange(half, dtype=F32) / half)
    rest = DH_C - ROT_DIM
    inv_h = jnp.concatenate([inv, inv, jnp.zeros((rest,), F32)])
    sgn_h = jnp.concatenate([-jnp.ones((half,), F32), jnp.ones((half,), F32), jnp.zeros((rest,), F32)])
    inv_l = jnp.concatenate([inv_h, inv_h])[None, :]
    sgn_l = jnp.concatenate([sgn_h, sgn_h])[None, :]
    ang = pos.astype(F32)[:, None] * inv_l
    return jnp.cos(ang), jnp.sin(ang) * sgn_l


def _stack_kv(k_rows, v_rows):
    b, t, _ = k_rows.shape
    return jnp.stack([k_rows.reshape(b, t, HG_C, DH_C), v_rows.reshape(b, t, HG_C, DH_C)], axis=2)


def kernel(x_prompt, x_sample, state_C, state_n, state_m, state_conv, cache_kv_w128, cache_kv_w512, cache_kv_w2048,
           norm_g, w_in, b_igate, b_fgate, conv_w, conv_b, ln_v_g, ln_v_b, w_spatial, b_spatial, w_branch, w_out,
           final_norm_g):
    depth = w_in.shape[0]
    bp, seq, _ = x_prompt.shape
    bs, n_tok, _ = x_sample.shape
    assert bp == 1 and seq % ATT_TILE == 0 and n_tok <= SAMPLE_ROWS // 2 and n_tok >= CONV_W - 1
    caches = (cache_kv_w128, cache_kv_w512, cache_kv_w2048)
    pad_tok = SAMPLE_ROWS - n_tok
    rows_s = bs * SAMPLE_ROWS

    w_in_t = jnp.swapaxes(w_in, 1, 2)
    w_main = _pack_w_in(w_in_t)
    gate_bias = jnp.concatenate([b_igate, b_fgate], axis=1)
    bias_row = jnp.pad(gate_bias, ((0, 0), (0, GATE_LANES - 2 * H_A)))[:, None, :]
    bias_col = gate_bias[:, :, None]
    wb16 = w_branch.astype(BF16)
    wo16 = w_out.astype(BF16)
    b_sp_col = b_spatial[..., None]
    norm_g3, conv_b3 = norm_g[:, None, :], conv_b[:, None, :]
    ln_g3, ln_b3 = ln_v_g[:, None, :], ln_v_b[:, None, :]
    seqs_per_chunk = CHUNK_B // SAMPLE_ROWS
    w8 = jnp.pad(w_spatial[:, :, :n_tok, :n_tok], ((0, 0), (0, 0), (0, pad_tok), (0, pad_tok)))
    w_sp_s = jnp.einsum('ab,lgij->lgaibj', jnp.eye(seqs_per_chunk, dtype=F32), w8).reshape(
        depth, G_B, CHUNK_B, CHUNK_B)
    b_sp_s = jnp.tile(jnp.pad(b_spatial[:, :, :n_tok], ((0, 0), (0, 0), (0, pad_tok))),
                      (1, 1, seqs_per_chunk))[..., None]

    cos_p, sin_p = _rope_tables(jnp.arange(seq))
    cos_s, sin_s = _rope_tables(PAST_LEN + jnp.arange(SAMPLE_ROWS))

    hp = x_prompt.reshape(seq, D_MODEL)
    hs = jnp.pad(x_sample, ((0, 0), (0, pad_tok), (0, 0))).reshape(rows_s, D_MODEL)
    fg = final_norm_g[None, :]
    zeros_c = jnp.zeros((1, 1, H_A, DH_A, DH_A), F32)
    zeros_n = jnp.zeros((1, 1, H_A, 1, DH_A), F32)
    zeros_m = jnp.zeros((1, 1, H_A, 1, GATE_LANES), F32)
    zeros_tail = jnp.zeros((1, 1, 8, 2 * W_A), F32)
    n0_s = state_n[:, :, :, None, :]
    m0_s = jnp.broadcast_to(state_m[:, :, :, None, None], (depth, bs, H_A, 1, GATE_LANES))
    tail0_s = jnp.pad(state_conv, ((0, 0), (0, 0), (8 - (CONV_W - 1), 0), (0, 0)))
    views = _cache_views(caches)

    p_out = {k: [] for k in ('C', 'n', 'm', 'conv', 'kv0', 'kv1', 'kv2')}
    s_out = {k: [] for k in ('C', 'n', 'm', 'conv', 'chunk_v', 'kv0', 'kv1', 'kv2')}
    for l in range(depth):
        final = l == depth - 1

        ya, pp, c1, n1, m1, conv1 = _seq_a(hp, cos_p, sin_p, norm_g3, w_main, w_in_t, bias_row, bias_col, conv_w, conv_b3,
                                           zeros_c, zeros_n, zeros_m, zeros_tail, l, L=MLSTM_L)
        att = [[], [], []]
        for g, (win, d) in enumerate(DIL_PATTERNS):
            for dst, val in zip(att, _dil_attn(pp, 0, g, d)):
                dst.append(val)
            keep = min(win, seq)
            k_g = pp[seq - keep:, OFF_CK - OFF_CQ + g * WG_C:OFF_CK - OFF_CQ + (g + 1) * WG_C]
            v_g = pp[seq - keep:, OFF_CV - OFF_CQ + g * WG_C:OFF_CV - OFF_CQ + (g + 1) * WG_C]
            p_out['kv%d' % g].append(_stack_kv(k_g[None], v_g[None]))
        hp, _ = _post(hp, pp, 0, ya, att[0] + att[1] + att[2], norm_g3, w_main, ln_g3, ln_b3, w_spatial, b_sp_col, wb16, wo16, fg, l,
                      final, False)
        p_out['C'].append(c1)
        p_out['n'].append(n1)
        p_out['m'].append(m1)
        p_out['conv'].append(conv1)

        ps, gcol, grow = _inproj(hs, norm_g3, w_main, w_in_t, l)
        ya, c1, n1, m1, conv1 = _mlstm(ps, gcol, grow, bias_row, bias_col, conv_w, conv_b3,
                                       state_C, n0_s, m0_s, tail0_s, l, l, L=SAMPLE_ROWS, t_valid=n_tok)
        att_s, kr = _sample_attn(ps, cos_s, sin_s, views, l, n_tok)
        hs, vn = _post(hs, ps, OFF_CQ, ya, att_s, norm_g3, w_main, ln_g3, ln_b3, w_sp_s, b_sp_s, wb16, wo16, fg, l,
                       final, True)
        s_out['C'].append(c1)
        s_out['n'].append(n1)
        s_out['m'].append(m1)
        s_out['conv'].append(conv1)
        s_out['chunk_v'].append(vn.reshape(bs, SAMPLE_ROWS, W_B)[:, :n_tok])
        kr3 = kr.reshape(bs, SAMPLE_ROWS, W_C)[:, :n_tok]
        v3 = ps[:, OFF_CV:OFF_CV + W_C].reshape(bs, SAMPLE_ROWS, W_C)[:, :n_tok]
        for g in range(len(DIL_PATTERNS)):
            gs = slice(g * WG_C, (g + 1) * WG_C)
            s_out['kv%d' % g].append(_stack_kv(kr3[:, :, gs], v3[:, :, gs]))

    stk = lambda d, k: jnp.stack(d[k], axis=0)
    y_prompt = hp.reshape(bp, seq, D_MODEL)
    y_sample = hs.reshape(bs, SAMPLE_ROWS, D_MODEL)[:, :n_tok]
    return (y_prompt, y_sample,
            stk(p_out, 'C'), stk(p_out, 'n'), stk(p_out, 'm'), stk(p_out, 'conv'),
            stk(p_out, 'kv0'), stk(p_out, 'kv1'), stk(p_out, 'kv2'),
            stk(s_out, 'C'), stk(s_out, 'n'), stk(s_out, 'm'), stk(s_out, 'conv'), stk(s_out, 'chunk_v'),
            stk(s_out, 'kv0'), stk(s_out, 'kv1'), stk(s_out, 'kv2'))
```

```python
import functools

import jax
import jax.numpy as jnp
from jax import lax
from jax.experimental import pallas as pl
from jax.experimental.pallas import tpu as pltpu

F32 = jnp.float32
BF16 = jnp.bfloat16

D_MODEL = 1024
H_A = 4
DH_A = 256
W_A = H_A * DH_A
CONV_W = 4
G_B = 4
CHUNK_B = 128
W_B = 1024
DG_B = W_B // G_B
DIL_PATTERNS = ((128, 1), (512, 4), (2048, 16))
HG_C = 4
DH_C = 64
WG_C = HG_C * DH_C
W_C = len(DIL_PATTERNS) * WG_C
SPAN_C = 128
ROT_DIM = DH_C // 4
ROPE_THETA = 500000.0
EPS = 1e-6
PAST_LEN = 16384

N_PACK = 14336
WBLK = 1024
BLK_AO, BLK_AZ, BLK_BU, BLK_BV, BLK_BZ, BLK_C0, BLK_GA, BLK_GB, BLK_GC = 3, 4, 5, 6, 7, 8, 11, 12, 13
N_P1 = 6144
OFF_CQ, OFF_CK, OFF_CV, OFF_CZ = 3072, 3840, 4608, 5376
GATE_LANES = 128

INPROJ_TM = 2048
POST_TM = 512
MLSTM_BAND = 128
MLSTM_L = 256
SAMPLE_ROWS = 8
ATT_ROWS = 1024
VMEM_LIMIT = 56 * 1024 * 1024


def _cparams(n_axes):
    return pltpu.CompilerParams(dimension_semantics=("arbitrary",) * n_axes, vmem_limit_bytes=VMEM_LIMIT)


def _dot(a, b):
    return jnp.dot(a.astype(BF16), b.astype(BF16), preferred_element_type=F32)


def _dot_nt(a, b):
    return lax.dot_general(a.astype(BF16), b.astype(BF16), (((1,), (1,)), ((), ())), preferred_element_type=F32)


def _dot_tn(a, b):
    return lax.dot_general(a.astype(BF16), b.astype(BF16), (((0,), (0,)), ((), ())), preferred_element_type=F32)


def _sigmoid(x):
    return 0.5 * jnp.tanh(0.5 * x) + 0.5


def _silu(x):
    return x * _sigmoid(x)


def _inproj_kernel(x_ref, g_ref, w_ref, wg_ref, p_ref, gc_ref, gr_ref, xn_ref):
    nt = (((1,), (1,)), ((), ()))

    @pl.when(pl.program_id(1) == 0)
    def _():
        x = x_ref[...]
        r = lax.rsqrt(jnp.mean(x * x, axis=-1, keepdims=True) + EPS)
        xn = (x * r * g_ref[...]).astype(BF16)
        xn_ref[...] = xn
        wg = wg_ref[...].astype(BF16)
        gc_ref[...] = lax.dot_general(xn, wg, nt, preferred_element_type=F32)
        gr_ref[...] = lax.dot_general(wg[0:8, :], xn, nt, preferred_element_type=F32)

    p_ref[...] = lax.dot_general(xn_ref[...], w_ref[...], nt, preferred_element_type=F32)


def _inproj(x, norm_g, w_main, w_in_t, l):
    rows = x.shape[0]
    gate_blk = 4 * W_A // GATE_LANES
    tm = min(rows, INPROJ_TM)
    tn = WBLK
    n_a = 3 * W_A // tn
    wblk = lambda i, j: (l, jnp.where(j < n_a, j, j + (BLK_C0 - n_a)), 0)
    return pl.pallas_call(
        _inproj_kernel,
        grid=(rows // tm, N_P1 // tn),
        in_specs=[
            pl.BlockSpec((tm, D_MODEL), lambda i, j: (i, 0)),
            pl.BlockSpec((None, 1, D_MODEL), lambda i, j: (l, 0, 0)),
            pl.BlockSpec((None, tn, D_MODEL), wblk),
            pl.BlockSpec((None, GATE_LANES, D_MODEL), lambda i, j: (l, gate_blk, 0)),
        ],
        out_specs=[
            pl.BlockSpec((tm, tn), lambda i, j: (i, j)),
            pl.BlockSpec((tm, GATE_LANES), lambda i, j: (i, 0)),
            pl.BlockSpec((None, 8, tm), lambda i, j: (i, 0, 0)),
        ],
        out_shape=[
            jax.ShapeDtypeStruct((rows, N_P1), F32),
            jax.ShapeDtypeStruct((rows, GATE_LANES), F32),
            jax.ShapeDtypeStruct((rows // tm, 8, tm), F32),
        ],
        scratch_shapes=[pltpu.VMEM((tm, D_MODEL), BF16)],
        compiler_params=_cparams(2),
        name="inproj",
    )(x, norm_g, w_main, w_in_t)


def _pack_kernel(a_ref, b_ref, w_ref, *, first_shifted):
    j = pl.program_id(1)

    @pl.when(j < first_shifted)
    def _():
        w_ref[...] = a_ref[...].astype(BF16)

    @pl.when(j >= first_shifted)
    def _():
        w_ref[...] = jnp.concatenate([a_ref[2 * H_A:, :], b_ref[...]], axis=0).astype(BF16)


def _pack_w_in(w_in_t):
    depth = w_in_t.shape[0]
    tn = 2048
    gate_off = 4 * W_A
    assert gate_off % tn == 0 and w_in_t.shape[1] == N_PACK + 2 * H_A and 2 * H_A == 8
    kern = functools.partial(_pack_kernel, first_shifted=gate_off // tn)
    return pl.pallas_call(
        kern,
        grid=(depth, N_PACK // tn),
        in_specs=[
            pl.BlockSpec((None, tn, D_MODEL), lambda l, j: (l, j, 0)),
            pl.BlockSpec((None, 8, D_MODEL), lambda l, j: (l, (j + 1) * (tn // 8), 0)),
        ],
        out_specs=pl.BlockSpec((None, tn, D_MODEL), lambda l, j: (l, j, 0)),
        out_shape=jax.ShapeDtypeStruct((depth, N_PACK, D_MODEL), BF16),
        compiler_params=_cparams(2),
        name="pack_w_in",
    )(w_in_t, w_in_t)


def _conv_silu(ext_ref, cols, w, b, L):
    y = b
    for back in range(CONV_W):
        y = y + ext_ref[8 - back:8 - back + L, cols] * w[CONV_W - 1 - back:CONV_W - back]
    return _silu(y)


def _mlstm_core(g_col, g_row, get_v, ext_s, cw_ref, cb_ref, c_s, n_s, m_s, y_ref, L, t_valid, side_work=None):
    tick = side_work if side_work is not None else (lambda: None)
    neg_inf = jnp.float32(-jnp.inf)
    ig_col, lf_col = g_col, jax.nn.log_sigmoid(g_col)
    ig_row, lf_row = g_row, jax.nn.log_sigmoid(g_row)
    if t_valid < L:
        vc = lax.broadcasted_iota(jnp.int32, (L, GATE_LANES), 0) < t_valid
        vr = lax.broadcasted_iota(jnp.int32, (8, L), 1) < t_valid
        ig_col, lf_col = jnp.where(vc, ig_col, neg_inf), jnp.where(vc, lf_col, 0.0)
        ig_row, lf_row = jnp.where(vr, ig_row, neg_inf), jnp.where(vr, lf_row, 0.0)
    ti = lax.broadcasted_iota(jnp.int32, (L, L), 0)
    si = lax.broadcasted_iota(jnp.int32, (L, L), 1)
    causal = ti >= si
    b_col = jnp.dot(causal.astype(F32), lf_col, preferred_element_type=F32, precision=lax.Precision.HIGHEST)
    b_row = jnp.dot(lf_row, (ti <= si).astype(F32), preferred_element_type=F32, precision=lax.Precision.HIGHEST)
    last = t_valid - 1
    band = min(L, MLSTM_BAND)
    nt = (((1,), (1,)), ((), ()))

    for h in range(H_A):
        cs = slice(h * DH_A, (h + 1) * DH_A)
        ks = slice(W_A + h * DH_A, W_A + (h + 1) * DH_A)
        q = _conv_silu(ext_s, cs, cw_ref[:, cs], cb_ref[:, cs], L)
        tick()
        k = _conv_silu(ext_s, ks, cw_ref[:, ks], cb_ref[:, ks], L) * (DH_A ** -0.5)
        tick()
        v = get_v(h)
        q16, k16, v16 = q.astype(BF16), k.astype(BF16), v.astype(BF16)
        bc = b_col[:, H_A + h:H_A + h + 1]
        igc = ig_col[:, h:h + 1]
        br = b_row[H_A + h:H_A + h + 1, :]
        igr = ig_row[h:h + 1, :]
        m_prev = m_s[h][:, 0:1]
        c_prev = c_s[h]
        n_prev = n_s[h]
        c16 = c_prev.astype(BF16)

        for r in range(L // band):
            rs = slice(r * band, (r + 1) * band)
            kw = (r + 1) * band
            ti = lax.broadcasted_iota(jnp.int32, (band, kw), 0) + r * band
            si = lax.broadcasted_iota(jnp.int32, (band, kw), 1)
            logw = jnp.where(ti >= si, bc[rs] - br[:, 0:kw] + igr[:, 0:kw], neg_inf)
            inter = bc[rs] + m_prev
            m_t = jnp.maximum(inter, jnp.max(logw, axis=1, keepdims=True))
            w_intra = jnp.exp(logw - m_t)
            w_inter = jnp.exp(inter - m_t)
            s = w_intra * lax.dot_general(q16[rs], k16[0:kw], nt, preferred_element_type=F32)
            num = (jnp.dot(s.astype(BF16), v16[0:kw], preferred_element_type=F32)
                   + w_inter * lax.dot_general(q16[rs], c16, nt, preferred_element_type=F32))
            den = (jnp.sum(s, axis=1, keepdims=True)
                   + w_inter * jnp.sum(q[rs] * n_prev, axis=1, keepdims=True))
            y_ref[0, rs, cs] = num / jnp.maximum(jnp.abs(den), jnp.exp(-m_t))
            if r == last // band:
                m_new = m_t[last - r * band:last - r * band + 1, :]
            tick()

        b_last = bc[last:last + 1, :]
        decay = jnp.exp(b_last + m_prev - m_new)
        w_s = jnp.exp(b_last - bc + igc - m_new)
        c_s[h] = decay * c_prev + lax.dot_general((w_s * v).astype(BF16), k16, (((0,), (0,)), ((), ())),
                                                  preferred_element_type=F32)
        n_s[h] = decay * n_prev + jnp.sum(w_s * k, axis=0, keepdims=True)
        m_s[h] = jnp.broadcast_to(m_new, (1, GATE_LANES))
        tick()

    ext_s[0:8, :] = ext_s[t_valid:t_valid + 8, :]


def _mlstm_state_io(c0_ref, n0_ref, m0_ref, tail0_ref, c_s, n_s, m_s, ext_s):
    @pl.when(pl.program_id(1) == 0)
    def _():
        c_s[...] = c0_ref[0]
        n_s[...] = n0_ref[0]
        m_s[...] = m0_ref[0]
        ext_s[0:8, :] = tail0_ref[0]


def _mlstm_state_out(c_out, n_out, m_out, tail_out, c_s, n_s, m_s, ext_s):
    @pl.when(pl.program_id(1) == pl.num_programs(1) - 1)
    def _():
        c_out[0] = c_s[...]
        n_out[0] = n_s[...]
        m_out[0] = m_s[...]
        tail_out[0] = ext_s[0:8, :]


def _mlstm_kernel(p_ref, gc_ref, gr_ref, brow_ref, bcol_ref, cw_ref, cb_ref, c0_ref, n0_ref, m0_ref, tail0_ref,
                  y_ref, c_out, n_out, m_out, tail_out, c_s, n_s, m_s, ext_s, *, L, t_valid):
    _mlstm_state_io(c0_ref, n0_ref, m0_ref, tail0_ref, c_s, n_s, m_s, ext_s)
    ext_s[8:8 + L, :] = p_ref[0, :, 0:2 * W_A]
    get_v = lambda h: p_ref[0, :, 2 * W_A + h * DH_A:2 * W_A + (h + 1) * DH_A]
    _mlstm_core(gc_ref[0] + brow_ref[...], gr_ref[0] + bcol_ref[...], get_v, ext_s, cw_ref, cb_ref,
                c_s, n_s, m_s, y_ref, L, t_valid)
    _mlstm_state_out(c_out, n_out, m_out, tail_out, c_s, n_s, m_s, ext_s)


def _seq_a_kernel(x_ref, cos_ref, sin_ref, ng_ref, w_aq, w_ak, w_av, w_c0, w_c1, w_c2, wg_ref, brow_ref, bcol_ref,
                  cw_ref, cb_ref, c0_ref, n0_ref, m0_ref, tail0_ref, y_ref, pc_ref, c_out, n_out, m_out, tail_out,
                  c_s, n_s, m_s, ext_s, v_s, *, L):
    _mlstm_state_io(c0_ref, n0_ref, m0_ref, tail0_ref, c_s, n_s, m_s, ext_s)
    nt = (((1,), (1,)), ((), ()))
    x = x_ref[0]
    r = lax.rsqrt(jnp.mean(x * x, axis=-1, keepdims=True) + EPS)
    xn = (x * r * ng_ref[...]).astype(BF16)
    proj = lambda w_ref: lax.dot_general(xn, w_ref[...], nt, preferred_element_type=F32)
    wg = wg_ref[...].astype(BF16)
    g_col = lax.dot_general(xn, wg, nt, preferred_element_type=F32) + brow_ref[...]
    g_row = lax.dot_general(wg[0:8, :], xn, nt, preferred_element_type=F32) + bcol_ref[...]
    ext_s[8:8 + L, 0:W_A] = proj(w_aq)
    ext_s[8:8 + L, W_A:2 * W_A] = proj(w_ak)
    v_s[...] = proj(w_av)
    piece = 256
    todo = [(w_ref, j, lo) for j, w_ref in enumerate((w_c0, w_c1, w_c2)) for lo in range(0, WBLK, piece)]

    assert piece == WG_C

    def emit_piece():
        w_ref, j, lo = todo.pop(0)
        col = j * WBLK + lo
        y = lax.dot_general(xn, w_ref[lo:lo + piece, :], nt, preferred_element_type=F32)
        if col < 2 * W_C:
            cos = jnp.concatenate([cos_ref[0], cos_ref[0]], axis=1)
            sin = jnp.concatenate([sin_ref[0], sin_ref[0]], axis=1)
            y = _rope(y, cos, sin)
            if col < W_C:
                y = y * (DH_C ** -0.5)
        pc_ref[0, :, col:col + piece] = y

    side_work = lambda: emit_piece() if todo else None
    get_v = lambda h: v_s[:, h * DH_A:(h + 1) * DH_A]
    _mlstm_core(g_col, g_row, get_v, ext_s, cw_ref, cb_ref, c_s, n_s, m_s, y_ref, L, L, side_work)
    while todo:
        emit_piece()
    _mlstm_state_out(c_out, n_out, m_out, tail_out, c_s, n_s, m_s, ext_s)


def _seq_a(x, cos_t, sin_t, norm_g, w_main, w_in_t, bias_row, bias_col, conv_w, conv_b, c0, n0, m0, tail0, l, *, L):
    rows = x.shape[0]
    nch = rows // L
    x3 = x.reshape(nch, L, D_MODEL)
    gate_blk = 4 * W_A // GATE_LANES
    once = pl.Buffered(1)
    chunk = lambda b, c: (c, 0, 0)
    layer3 = lambda b, c: (l, 0, 0)
    wblk = lambda blk: pl.BlockSpec((None, WBLK, D_MODEL), lambda b, c, blk=blk: (l, blk, 0), pipeline_mode=once)
    st5 = lambda b, c: (0, 0, 0, 0, 0)
    st4 = lambda b, c: (0, 0, 0, 0)
    first4 = lambda b, c: (0, 0, 0, 0)
    first3 = lambda b, c: (0, 0, 0)
    y, pc, c1, n1, m1, tail1 = pl.pallas_call(
        functools.partial(_seq_a_kernel, L=L),
        grid=(1, nch),
        in_specs=[
            pl.BlockSpec((1, L, D_MODEL), chunk),
            pl.BlockSpec((1, L, 2 * DH_C), chunk),
            pl.BlockSpec((1, L, 2 * DH_C), chunk),
            pl.BlockSpec((None, 1, D_MODEL), layer3, pipeline_mode=once),
        ] + [wblk(b) for b in (0, 1, 2, BLK_C0, BLK_C0 + 1, BLK_C0 + 2)] + [
            pl.BlockSpec((None, GATE_LANES, D_MODEL), lambda b, c: (l, gate_blk, 0), pipeline_mode=once),
            pl.BlockSpec((None, 1, GATE_LANES), layer3, pipeline_mode=once),
            pl.BlockSpec((None, 8, 1), layer3, pipeline_mode=once),
            pl.BlockSpec((None, CONV_W, 2 * W_A), layer3, pipeline_mode=once),
            pl.BlockSpec((None, 1, 2 * W_A), layer3, pipeline_mode=once),
            pl.BlockSpec((None, 1, H_A, DH_A, DH_A), st5, pipeline_mode=once),
            pl.BlockSpec((None, 1, H_A, 1, DH_A), st5, pipeline_mode=once),
            pl.BlockSpec((None, 1, H_A, 1, GATE_LANES), st5, pipeline_mode=once),
            pl.BlockSpec((None, 1, 8, 2 * W_A), st4, pipeline_mode=once),
        ],
        out_specs=[
            pl.BlockSpec((1, L, W_A), chunk),
            pl.BlockSpec((1, L, 3 * WBLK), chunk),
            pl.BlockSpec((1, H_A, DH_A, DH_A), first4),
            pl.BlockSpec((1, H_A, 1, DH_A), first4),
            pl.BlockSpec((1, H_A, 1, GATE_LANES), first4),
            pl.BlockSpec((1, 8, 2 * W_A), first3),
        ],
        out_shape=[
            jax.ShapeDtypeStruct((nch, L, W_A), F32),
            jax.ShapeDtypeStruct((nch, L, 3 * WBLK), F32),
            jax.ShapeDtypeStruct((1, H_A, DH_A, DH_A), F32),
            jax.ShapeDtypeStruct((1, H_A, 1, DH_A), F32),
            jax.ShapeDtypeStruct((1, H_A, 1, GATE_LANES), F32),
            jax.ShapeDtypeStruct((1, 8, 2 * W_A), F32),
        ],
        scratch_shapes=[
            pltpu.VMEM((H_A, DH_A, DH_A), F32),
            pltpu.VMEM((H_A, 1, DH_A), F32),
            pltpu.VMEM((H_A, 1, GATE_LANES), F32),
            pltpu.VMEM((L + 8, 2 * W_A), F32),
            pltpu.VMEM((L, W_A), F32),
        ],
        compiler_params=_cparams(2),
        name="seq_a",
    )(x3, cos_t.reshape(nch, L, 2 * DH_C), sin_t.reshape(nch, L, 2 * DH_C), norm_g, *([w_main] * 6), w_in_t,
      bias_row, bias_col, conv_w, conv_b, c0, n0, m0, tail0)
    return (y.reshape(rows, W_A), pc.reshape(rows, 3 * WBLK), c1, n1[:, :, 0, :], m1[:, :, 0, 0],
            tail1[:, 8 - (CONV_W - 1):, :])


def _mlstm(p, gcol, grow, bias_row, bias_col, conv_w, conv_b, c0, n0, m0, tail0, l, ls, *, L, t_valid):
    batch = c0.shape[1]
    rows = p.shape[0]
    nch = rows // (batch * L)
    p3 = p.reshape(batch * nch, L, N_P1)
    gc3 = gcol.reshape(batch * nch, L, GATE_LANES)
    tm = grow.shape[2]
    if L % GATE_LANES == 0:
        per = tm // L
        gr3, gr_index = grow, (lambda b, c: ((b * nch + c) // per, 0, (b * nch + c) % per))
    else:
        gr3 = grow.transpose(1, 0, 2).reshape(8, batch * nch, L).transpose(1, 0, 2)
        gr_index = lambda b, c: (b * nch + c, 0, 0)
    kern = functools.partial(_mlstm_kernel, L=L, t_valid=t_valid)
    chunk = lambda b, c: (b * nch + c, 0, 0)
    layer3 = lambda b, c: (l, 0, 0)
    per_b4 = lambda b, c: (b, 0, 0, 0)
    per_b3 = lambda b, c: (b, 0, 0)
    st5 = lambda b, c: (ls, b, 0, 0, 0)
    st4 = lambda b, c: (ls, b, 0, 0)
    y, c1, n1, m1, tail1 = pl.pallas_call(
        kern,
        grid=(batch, nch),
        in_specs=[
            pl.BlockSpec((1, L, 3 * W_A), chunk),
            pl.BlockSpec((1, L, GATE_LANES), chunk),
            pl.BlockSpec((1, 8, L), gr_index),
            pl.BlockSpec((None, 1, GATE_LANES), layer3),
            pl.BlockSpec((None, 8, 1), layer3),
            pl.BlockSpec((None, CONV_W, 2 * W_A), layer3),
            pl.BlockSpec((None, 1, 2 * W_A), layer3),
            pl.BlockSpec((None, 1, H_A, DH_A, DH_A), st5),
            pl.BlockSpec((None, 1, H_A, 1, DH_A), st5),
            pl.BlockSpec((None, 1, H_A, 1, GATE_LANES), st5),
            pl.BlockSpec((None, 1, 8, 2 * W_A), st4),
        ],
        out_specs=[
            pl.BlockSpec((1, L, W_A), chunk),
            pl.BlockSpec((1, H_A, DH_A, DH_A), per_b4),
            pl.BlockSpec((1, H_A, 1, DH_A), per_b4),
            pl.BlockSpec((1, H_A, 1, GATE_LANES), per_b4),
            pl.BlockSpec((1, 8, 2 * W_A), per_b3),
        ],
        out_shape=[
            jax.ShapeDtypeStruct((batch * nch, L, W_A), F32),
            jax.ShapeDtypeStruct((batch, H_A, DH_A, DH_A), F32),
            jax.ShapeDtypeStruct((batch, H_A, 1, DH_A), F32),
            jax.ShapeDtypeStruct((batch, H_A, 1, GATE_LANES), F32),
            jax.ShapeDtypeStruct((batch, 8, 2 * W_A), F32),
        ],
        scratch_shapes=[
            pltpu.VMEM((H_A, DH_A, DH_A), F32),
            pltpu.VMEM((H_A, 1, DH_A), F32),
            pltpu.VMEM((H_A, 1, GATE_LANES), F32),
            pltpu.VMEM((L + 8, 2 * W_A), F32),
        ],
        compiler_params=_cparams(2),
        name="mlstm",
    )(p3, gc3, gr3, bias_row, bias_col, conv_w, conv_b, c0, n0, m0, tail0)
    return y.reshape(rows, W_A), c1, n1[:, :, 0, :], m1[:, :, 0, 0], tail1[:, 8 - (CONV_W - 1):, :]


def _rope(x, cos, sin):
    width = x.shape[1]
    lane = lax.broadcasted_iota(jnp.int32, (1, width), 1) % DH_C
    half = ROT_DIM // 2
    partner = jnp.where(lane < half, pltpu.roll(x, width - half, 1), pltpu.roll(x, half, 1))
    return x * cos + partner * sin


def _dil_attn_kernel(q_ref, k_ref, v_ref, o_ref, mx_ref, den_ref, kprev, vprev, bias):
    t = pl.program_id(1)
    neg_inf = jnp.float32(-jnp.inf)

    @pl.when(t == 0)
    def _():
        ii = lax.broadcasted_iota(jnp.int32, (SPAN_C, 2 * SPAN_C), 0)
        jj = lax.broadcasted_iota(jnp.int32, (SPAN_C, 2 * SPAN_C), 1)
        band = (jj >= ii) & (jj <= ii + SPAN_C)
        bias[1] = jnp.where(band, 0.0, neg_inf)
        bias[0] = jnp.where(band & (jj >= SPAN_C), 0.0, neg_inf)
        kprev[...] = jnp.zeros((SPAN_C, WG_C), F32)
        vprev[...] = jnp.zeros((SPAN_C, WG_C), F32)

    lane_head = lax.broadcasted_iota(jnp.int32, (SPAN_C, WG_C), 1) // DH_C
    hm = [(lane_head == h).astype(F32) for h in range(HG_C)]

    def pick(per_head):
        out = jnp.broadcast_to(per_head[0], (SPAN_C, WG_C))
        for h in range(1, HG_C):
            out = jnp.where(lane_head >= h, per_head[h], out)
        return out

    has_prev_tile = (t > 0).astype(jnp.int32)
    n_rows = q_ref.shape[0]
    for b in range(n_rows // SPAN_C):
        rs = slice(b * SPAN_C, (b + 1) * SPAN_C)
        qb = q_ref[rs, :]
        if b == 0:
            kb = jnp.concatenate([kprev[...], k_ref[rs, :]], axis=0)
            vb = jnp.concatenate([vprev[...], v_ref[rs, :]], axis=0)
            mask = bias[has_prev_tile]
        else:
            kb = k_ref[(b - 1) * SPAN_C:(b + 1) * SPAN_C, :]
            vb = v_ref[(b - 1) * SPAN_C:(b + 1) * SPAN_C, :]
            mask = bias[1]
        kb16, vb16 = kb.astype(BF16), vb.astype(BF16)
        pvs, mxs, dens = [], [], []
        for h0 in range(0, HG_C, 2):
            q2 = jnp.concatenate([qb * hm[h0], qb * hm[h0 + 1]], axis=0)
            s = _dot_nt(q2, kb16).reshape(2, SPAN_C, 2 * SPAN_C) + mask
            mx = jnp.max(s, axis=2, keepdims=True)
            p = jnp.exp(s - mx)
            den = jnp.sum(p, axis=2, keepdims=True)
            pv = _dot(p.reshape(2 * SPAN_C, 2 * SPAN_C), vb16).reshape(2, SPAN_C, WG_C)
            pvs += [pv[0], pv[1]]
            mxs += [mx[0], mx[1]]
            dens += [den[0], den[1]]
        o_ref[rs, :] = pick(pvs)
        mx_ref[rs, :] = pick(mxs)
        den_ref[rs, :] = pick(dens)

    kprev[...] = k_ref[n_rows - SPAN_C:n_rows, :]
    vprev[...] = v_ref[n_rows - SPAN_C:n_rows, :]


def _dil_attn(p, g, d):
    rows, width = p.shape
    sub = rows // d
    n_rows = min(ATT_ROWS, sub)
    assert sub % n_rows == 0 and n_rows % SPAN_C == 0 and width % WG_C == 0
    per_res = width // WG_C
    pv = p.reshape(sub, d * width)
    col = lambda off: pl.BlockSpec((n_rows, WG_C),
                                   lambda r, t, off=off: (t, r * per_res + (off - OFF_CQ) // WG_C + g))
    out = pl.BlockSpec((n_rows, WG_C), lambda r, t: (t, r))
    out_shape = jax.ShapeDtypeStruct((sub, d * WG_C), F32)
    res = pl.pallas_call(
        _dil_attn_kernel,
        grid=(d, sub // n_rows),
        in_specs=[col(OFF_CQ), col(OFF_CK), col(OFF_CV)],
        out_specs=[out, out, out],
        out_shape=[out_shape, out_shape, out_shape],
        scratch_shapes=[pltpu.VMEM((SPAN_C, WG_C), F32), pltpu.VMEM((SPAN_C, WG_C), F32),
                        pltpu.VMEM((2, SPAN_C, 2 * SPAN_C), F32)],
        compiler_params=_cparams(2),
        name="dil_attn_d%d" % d,
    )(pv, pv, pv)
    return [r.reshape(rows, WG_C) for r in res]


def _sample_attn_kernel(pc0, pc1, pc2, cos_ref, sin_ref, c0_ref, c1_ref, c2_ref,
                        o0, o1, o2, m0, m1, m2, d0, d1, d2, kr_ref, *, n_tok):
    pc = jnp.concatenate([pc0[0], pc1[0], pc2[0]], axis=1)
    reps = W_C // (2 * DH_C)
    cos = jnp.concatenate([cos_ref[...]] * reps, axis=1)
    sin = jnp.concatenate([sin_ref[...]] * reps, axis=1)
    qr = _rope(pc[:, 0:W_C], cos, sin) * (DH_C ** -0.5)
    kr = _rope(pc[:, W_C:2 * W_C], cos, sin)
    v = pc[:, 2 * W_C:3 * W_C]
    kr_ref[0] = kr
    nrow = HG_C * SAMPLE_ROWS
    rowh = lax.broadcasted_iota(jnp.int32, (nrow, WG_C), 0) // SAMPLE_ROWS
    laneh = lax.broadcasted_iota(jnp.int32, (nrow, WG_C), 1) // DH_C
    hm = (rowh == laneh).astype(F32)
    tok_col = lax.broadcasted_iota(jnp.int32, (nrow, 1), 0) % SAMPLE_ROWS
    out_row = lax.broadcasted_iota(jnp.int32, (SAMPLE_ROWS, WG_C), 0)
    neg_inf = jnp.float32(-jnp.inf)
    crefs = (c0_ref, c1_ref, c2_ref)
    orefs = (o0, o1, o2)
    mrefs = (m0, m1, m2)
    drefs = (d0, d1, d2)
    for g, (_, d) in enumerate(DIL_PATTERNS):
        gs = slice(g * WG_C, (g + 1) * WG_C)
        n_buf = SPAN_C * d
        kn = kr[:, gs]
        vn = v[:, gs]
        qm = jnp.concatenate([qr[:, gs]] * HG_C, axis=0) * hm
        s = _dot(qm, crefs[g][0, 0])
        tok = lax.broadcasted_iota(jnp.int32, (nrow, n_buf), 0) % SAMPLE_ROWS
        pos = lax.broadcasted_iota(jnp.int32, (nrow, n_buf), 1)
        attends = (pos >= tok) if d == 1 else ((pos & (d - 1)) == tok)
        s = jnp.where(attends | (tok >= n_tok), s, neg_inf)
        new_ok = [(u <= tok_col) if d == 1 else (u == tok_col) for u in range(n_tok)]
        s_new = [jnp.where(ok, jnp.sum(qm * kn[u:u + 1, :], axis=1, keepdims=True), neg_inf)
                 for u, ok in enumerate(new_ok)]
        mx = jnp.max(s, axis=1, keepdims=True)
        for sn in s_new:
            mx = jnp.maximum(mx, sn)
        p = jnp.exp(s - mx)
        den = jnp.sum(p, axis=1, keepdims=True)
        acc = _dot_nt(p, crefs[g][0, 1])
        for u, sn in enumerate(s_new):
            pn = jnp.exp(sn - mx)
            den = den + pn
            acc = acc + pn * vn[u:u + 1, :]
        stacked = (acc * hm, mx * hm, den * hm)
        for ref, val, pad in zip((orefs[g], mrefs[g], drefs[g]), stacked, (0.0, 0.0, 1.0)):
            v8 = val[0:SAMPLE_ROWS]
            for h in range(1, HG_C):
                v8 = v8 + val[h * SAMPLE_ROWS:(h + 1) * SAMPLE_ROWS]
            ref[...] = jnp.where(out_row < n_tok, v8, pad)


def _cache_views(caches):
    views = []
    for (win, d), c in zip(DIL_PATTERNS, caches):
        depth, batch, n_buf = c.shape[:3]
        assert n_buf == SPAN_C * d
        views.append(jnp.transpose(c, (0, 1, 3, 4, 5, 2)).reshape(depth, batch, 2, WG_C, n_buf))
    return views


def _sample_attn(p, cos_s, sin_s, views, l, n_tok):
    batch = p.shape[0] // SAMPLE_ROWS
    p3 = p.reshape(batch, SAMPLE_ROWS, N_P1)
    pcol = lambda k: pl.BlockSpec((1, SAMPLE_ROWS, 1024), lambda b, k=k: (b, 0, OFF_CQ // 1024 + k))
    tab = pl.BlockSpec((SAMPLE_ROWS, 2 * DH_C), lambda b: (0, 0))
    cache_spec = lambda d: pl.BlockSpec((None, 1, 2, WG_C, SPAN_C * d), lambda b: (l, b, 0, 0, 0))
    out = pl.BlockSpec((SAMPLE_ROWS, WG_C), lambda b: (b, 0))
    res = pl.pallas_call(
        functools.partial(_sample_attn_kernel, n_tok=n_tok),
        grid=(batch,),
        in_specs=[pcol(0), pcol(1), pcol(2), tab, tab] + [cache_spec(d) for _, d in DIL_PATTERNS],
        out_specs=[out] * 9 + [pl.BlockSpec((1, SAMPLE_ROWS, W_C), lambda b: (b, 0, 0))],
        out_shape=[jax.ShapeDtypeStruct((batch * SAMPLE_ROWS, WG_C), F32)] * 9
        + [jax.ShapeDtypeStruct((batch, SAMPLE_ROWS, W_C), F32)],
        compiler_params=_cparams(1),
        name="sample_attn",
    )(p3, p3, p3, cos_s, sin_s, *views)
    return list(res[:9]), res[9].reshape(batch * SAMPLE_ROWS, W_C)


def _post_kernel(x_ref, h_ref, o0, o1, o2, m0, m1, m2, d0, d1, d2, cz_ref, ng_ref, w_ao, w_az, w_bu, w_bv, w_bz, w_ga, w_gb,
                 w_gc, lg_ref, lb_ref, wsp_ref, bsp_ref, wb_ref, wo_ref, fg_ref, out_ref, *rest, nchunks, final,
                 emit_vn):
    if emit_vn:
        vn_out, vn_s, yb_s = rest
    else:
        vn_s, yb_s = rest
    nt = (((1,), (1,)), ((), ()))
    x = x_ref[...]
    r = lax.rsqrt(jnp.mean(x * x, axis=-1, keepdims=True) + EPS)
    xn = (x * r * ng_ref[...]).astype(BF16)
    proj = lambda w_ref: lax.dot_general(xn, w_ref[...], nt, preferred_element_type=F32)
    branch = lambda y, lo, hi: jnp.dot(y.astype(BF16), wb_ref[lo:hi, :], preferred_element_type=F32)

    ya = h_ref[...] * _sigmoid(proj(w_ao)) * _silu(proj(w_az))
    acc = _sigmoid(proj(w_ga)) * branch(ya, 0, W_A)

    v = proj(w_bv)
    mu = jnp.mean(v, axis=-1, keepdims=True)
    var = jnp.mean(jnp.square(v - mu), axis=-1, keepdims=True)
    vn = (v - mu) * lax.rsqrt(var + EPS) * lg_ref[...] + lb_ref[...]
    if emit_vn:
        vn_out[...] = vn
    vn_s[...] = vn.astype(BF16)
    gate_b = proj(w_bu) * _silu(proj(w_bz))
    tri = (lax.broadcasted_iota(jnp.int32, (CHUNK_B, CHUNK_B), 0)
           >= lax.broadcasted_iota(jnp.int32, (CHUNK_B, CHUNK_B), 1))
    for g in range(G_B):
        wg = jnp.where(tri, wsp_ref[g], 0.0).astype(BF16)
        cs = slice(g * DG_B, (g + 1) * DG_B)
        for c in range(nchunks):
            rs = slice(c * CHUNK_B, (c + 1) * CHUNK_B)
            yb_s[rs, cs] = jnp.dot(wg, vn_s[rs, cs], preferred_element_type=F32) + bsp_ref[g]
    acc = acc + _sigmoid(proj(w_gb)) * branch(gate_b * yb_s[...], W_A, W_A + W_B)

    ms = [m0[...], m1[...], m2[...]]
    mx = jnp.maximum(jnp.maximum(ms[0], ms[1]), ms[2])
    es = [jnp.exp(m - mx) for m in ms]
    inv_tot = 1.0 / (es[0] * d0[...] + es[1] * d1[...] + es[2] * d2[...])
    pc = None
    for g, o_ref in enumerate((o0, o1, o2)):
        gs = slice(g * WG_C, (g + 1) * WG_C)
        yc = o_ref[...] * (es[g] * inv_tot) * _silu(cz_ref[:, gs])
        t = branch(yc, W_A + W_B + g * WG_C, W_A + W_B + (g + 1) * WG_C)
        pc = t if pc is None else pc + t
    acc = acc + _sigmoid(proj(w_gc)) * pc

    out = x + jnp.dot(acc.astype(BF16), wo_ref[...], preferred_element_type=F32)
    if final:
        rr = lax.rsqrt(jnp.mean(out * out, axis=-1, keepdims=True) + EPS)
        out = out * rr * fg_ref[...]
    out_ref[...] = out


def _post(x, p1, base, h, att, norm_g, w_main, ln_g, ln_b, w_sp, b_sp_col, w_branch, w_out, final_g, l, final,
          emit_vn):
    rows = x.shape[0]
    tm = min(rows, POST_TM)
    once = pl.Buffered(1)
    row = lambda w: pl.BlockSpec((tm, w), lambda i: (i, 0))
    wblk = lambda b: pl.BlockSpec((None, WBLK, D_MODEL), lambda i, b=b: (l, b, 0), pipeline_mode=once)
    vec = lambda w: pl.BlockSpec((None, 1, w), lambda i: (l, 0, 0), pipeline_mode=once)
    in_specs = (
        [row(D_MODEL), row(W_A)] + [row(WG_C)] * 9
        + [pl.BlockSpec((tm, W_C), lambda i: (i, (base + OFF_CZ - OFF_CQ) // W_C)), vec(D_MODEL)]
        + [wblk(b) for b in (BLK_AO, BLK_AZ, BLK_BU, BLK_BV, BLK_BZ, BLK_GA, BLK_GB, BLK_GC)]
        + [vec(W_B), vec(W_B),
           pl.BlockSpec((None, G_B, CHUNK_B, CHUNK_B), lambda i: (l, 0, 0, 0), pipeline_mode=once),
           pl.BlockSpec((None, G_B, CHUNK_B, 1), lambda i: (l, 0, 0, 0), pipeline_mode=once),
           pl.BlockSpec((None, W_A + W_B + W_C, D_MODEL), lambda i: (l, 0, 0), pipeline_mode=once),
           pl.BlockSpec((None, D_MODEL, D_MODEL), lambda i: (l, 0, 0), pipeline_mode=once),
           pl.BlockSpec((1, D_MODEL), lambda i: (0, 0), pipeline_mode=once)])
    out_specs = [row(D_MODEL)]
    out_shape = [jax.ShapeDtypeStruct((rows, D_MODEL), F32)]
    if emit_vn:
        out_specs.append(row(W_B))
        out_shape.append(jax.ShapeDtypeStruct((rows, W_B), F32))
    res = pl.pallas_call(
        functools.partial(_post_kernel, nchunks=tm // CHUNK_B, final=final, emit_vn=emit_vn),
        grid=(rows // tm,),
        in_specs=in_specs,
        out_specs=out_specs,
        out_shape=out_shape,
        scratch_shapes=[pltpu.VMEM((tm, W_B), BF16), pltpu.VMEM((tm, W_B), F32)],
        compiler_params=_cparams(1),
        name="post",
    )(x, h, *att, p1, norm_g, *([w_main] * 8), ln_g, ln_b, w_sp, b_sp_col, w_branch, w_out, final_g)
    return (res[0], res[1]) if emit_vn else (res[0], None)


def _rope_tables(pos):
    half = ROT_DIM // 2
    inv = ROPE_THETA ** (-jnp.arange(half, dtype=F32) / half)
    rest = DH_C - ROT_DIM
    inv_h = jnp.concatenate([inv, inv, jnp.zeros((rest,), F32)])
    sgn_h = jnp.concatenate([-jnp.ones((half,), F32), jnp.ones((half,), F32), jnp.zeros((rest,), F32)])
    inv_l = jnp.concatenate([inv_h, inv_h])[None, :]
    sgn_l = jnp.concatenate([sgn_h, sgn_h])[None, :]
    ang = pos.astype(F32)[:, None] * inv_l
    return jnp.cos(ang), jnp.sin(ang) * sgn_l


def _stack_kv(k_rows, v_rows):
    b, t, _ = k_rows.shape
    return jnp.stack([k_rows.reshape(b, t, HG_C, DH_C), v_rows.reshape(b, t, HG_C, DH_C)], axis=2)


def kernel(x_prompt, x_sample, state_C, state_n, state_m, state_conv, cache_kv_w128, cache_kv_w512, cache_kv_w2048,
           norm_g, w_in, b_igate, b_fgate, conv_w, conv_b, ln_v_g, ln_v_b, w_spatial, b_spatial, w_branch, w_out,
           final_norm_g):
    depth = w_in.shape[0]
    bp, seq, _ = x_prompt.shape
    bs, n_tok, _ = x_sample.shape
    assert bp == 1 and seq % (SPAN_C * max(d for _, d in DIL_PATTERNS)) == 0 and n_tok <= SAMPLE_ROWS // 2 and n_tok >= CONV_W - 1
    caches = (cache_kv_w128, cache_kv_w512, cache_kv_w2048)
    pad_tok = SAMPLE_ROWS - n_tok
    rows_s = bs * SAMPLE_ROWS

    w_in_t = jnp.swapaxes(w_in, 1, 2)
    w_main = _pack_w_in(w_in_t)
    gate_bias = jnp.concatenate([b_igate, b_fgate], axis=1)
    bias_row = jnp.pad(gate_bias, ((0, 0), (0, GATE_LANES - 2 * H_A)))[:, None, :]
    bias_col = gate_bias[:, :, None]
    wb16 = w_branch.astype(BF16)
    wo16 = w_out.astype(BF16)
    b_sp_col = b_spatial[..., None]
    norm_g3, conv_b3 = norm_g[:, None, :], conv_b[:, None, :]
    ln_g3, ln_b3 = ln_v_g[:, None, :], ln_v_b[:, None, :]
    seqs_per_chunk = CHUNK_B // SAMPLE_ROWS
    w8 = jnp.pad(w_spatial[:, :, :n_tok, :n_tok], ((0, 0), (0, 0), (0, pad_tok), (0, pad_tok)))
    w_sp_s = jnp.einsum('ab,lgij->lgaibj', jnp.eye(seqs_per_chunk, dtype=F32), w8).reshape(
        depth, G_B, CHUNK_B, CHUNK_B)
    b_sp_s = jnp.tile(jnp.pad(b_spatial[:, :, :n_tok], ((0, 0), (0, 0), (0, pad_tok))),
                      (1, 1, seqs_per_chunk))[..., None]

    cos_p, sin_p = _rope_tables(jnp.arange(seq))
    cos_s, sin_s = _rope_tables(PAST_LEN + jnp.arange(SAMPLE_ROWS))

    hp = x_prompt.reshape(seq, D_MODEL)
    hs = jnp.pad(x_sample, ((0, 0), (0, pad_tok), (0, 0))).reshape(rows_s, D_MODEL)
    fg = final_norm_g[None, :]
    zeros_c = jnp.zeros((1, 1, H_A, DH_A, DH_A), F32)
    zeros_n = jnp.zeros((1, 1, H_A, 1, DH_A), F32)
    zeros_m = jnp.zeros((1, 1, H_A, 1, GATE_LANES), F32)
    zeros_tail = jnp.zeros((1, 1, 8, 2 * W_A), F32)
    n0_s = state_n[:, :, :, None, :]
    m0_s = jnp.broadcast_to(state_m[:, :, :, None, None], (depth, bs, H_A, 1, GATE_LANES))
    tail0_s = jnp.pad(state_conv, ((0, 0), (0, 0), (8 - (CONV_W - 1), 0), (0, 0)))
    views = _cache_views(caches)

    p_out = {k: [] for k in ('C', 'n', 'm', 'conv', 'kv0', 'kv1', 'kv2')}
    s_out = {k: [] for k in ('C', 'n', 'm', 'conv', 'chunk_v', 'kv0', 'kv1', 'kv2')}
    for l in range(depth):
        final = l == depth - 1

        ya, pp, c1, n1, m1, conv1 = _seq_a(hp, cos_p, sin_p, norm_g3, w_main, w_in_t, bias_row, bias_col, conv_w, conv_b3,
                                           zeros_c, zeros_n, zeros_m, zeros_tail, l, L=MLSTM_L)
        att = [[], [], []]
        for g, (win, d) in enumerate(DIL_PATTERNS):
            for dst, val in zip(att, _dil_attn(pp, g, d)):
                dst.append(val)
            keep = min(win, seq)
            k_g = pp[seq - keep:, OFF_CK - OFF_CQ + g * WG_C:OFF_CK - OFF_CQ + (g + 1) * WG_C]
            v_g = pp[seq - keep:, OFF_CV - OFF_CQ + g * WG_C:OFF_CV - OFF_CQ + (g + 1) * WG_C]
            p_out['kv%d' % g].append(_stack_kv(k_g[None], v_g[None]))
        hp, _ = _post(hp, pp, 0, ya, att[0] + att[1] + att[2], norm_g3, w_main, ln_g3, ln_b3, w_spatial, b_sp_col, wb16, wo16, fg, l,
                      final, False)
        p_out['C'].append(c1)
        p_out['n'].append(n1)
        p_out['m'].append(m1)
        p_out['conv'].append(conv1)

        ps, gcol, grow = _inproj(hs, norm_g3, w_main, w_in_t, l)
        ya, c1, n1, m1, conv1 = _mlstm(ps, gcol, grow, bias_row, bias_col, conv_w, conv_b3,
                                       state_C, n0_s, m0_s, tail0_s, l, l, L=SAMPLE_ROWS, t_valid=n_tok)
        att_s, kr = _sample_attn(ps, cos_s, sin_s, views, l, n_tok)
        hs, vn = _post(hs, ps, OFF_CQ, ya, att_s, norm_g3, w_main, ln_g3, ln_b3, w_sp_s, b_sp_s, wb16, wo16, fg, l,
                       final, True)
        s_out['C'].append(c1)
        s_out['n'].append(n1)
        s_out['m'].append(m1)
        s_out['conv'].append(conv1)
        s_out['chunk_v'].append(vn.reshape(bs, SAMPLE_ROWS, W_B)[:, :n_tok])
        kr3 = kr.reshape(bs, SAMPLE_ROWS, W_C)[:, :n_tok]
        v3 = ps[:, OFF_CV:OFF_CV + W_C].reshape(bs, SAMPLE_ROWS, W_C)[:, :n_tok]
        for g in range(len(DIL_PATTERNS)):
            gs = slice(g * WG_C, (g + 1) * WG_C)
            s_out['kv%d' % g].append(_stack_kv(kr3[:, :, gs], v3[:, :, gs]))

    stk = lambda d, k: jnp.stack(d[k], axis=0)
    y_prompt = hp.reshape(bp, seq, D_MODEL)
    y_sample = hs.reshape(bs, SAMPLE_ROWS, D_MODEL)[:, :n_tok]
    return (y_prompt, y_sample,
            stk(p_out, 'C'), stk(p_out, 'n'), stk(p_out, 'm'), stk(p_out, 'conv'),
            stk(p_out, 'kv0'), stk(p_out, 'kv1'), stk(p_out, 'kv2'),
            stk(s_out, 'C'), stk(s_out, 'n'), stk(s_out, 'm'), stk(s_out, 'conv'), stk(s_out, 'chunk_v'),
            stk(s_out, 'kv0'), stk(s_out, 'kv1'), stk(s_out, 'kv2'))
```

```python
import functools

import jax
import jax.numpy as jnp
from jax import lax
from jax.experimental import pallas as pl
from jax.experimental.pallas import tpu as pltpu

F32 = jnp.float32
BF16 = jnp.bfloat16

D_MODEL = 1024
H_A = 4
DH_A = 256
W_A = H_A * DH_A
CONV_W = 4
G_B = 4
CHUNK_B = 128
W_B = 1024
DG_B = W_B // G_B
DIL_PATTERNS = ((128, 1), (512, 4), (2048, 16))
HG_C = 4
DH_C = 64
WG_C = HG_C * DH_C
W_C = len(DIL_PATTERNS) * WG_C
SPAN_C = 128
ROT_DIM = DH_C // 4
ROPE_THETA = 500000.0
EPS = 1e-6
PAST_LEN = 16384

N_PACK = 14336
WBLK = 1024
BLK_AO, BLK_AZ, BLK_BU, BLK_BV, BLK_BZ, BLK_C0, BLK_GA, BLK_GB, BLK_GC = 3, 4, 5, 6, 7, 8, 11, 12, 13
N_P1 = 6144
OFF_CQ, OFF_CK, OFF_CV, OFF_CZ = 3072, 3840, 4608, 5376
GATE_LANES = 128

INPROJ_TM = 2048
POST_TM = 512
MLSTM_BAND = 128
MLSTM_L = 256
SAMPLE_ROWS = 8
ATT_TILE = 2048
VMEM_LIMIT = 56 * 1024 * 1024


def _cparams(n_axes):
    return pltpu.CompilerParams(dimension_semantics=("arbitrary",) * n_axes, vmem_limit_bytes=VMEM_LIMIT)


def _dot(a, b):
    return jnp.dot(a.astype(BF16), b.astype(BF16), preferred_element_type=F32)


def _dot_nt(a, b):
    return lax.dot_general(a.astype(BF16), b.astype(BF16), (((1,), (1,)), ((), ())), preferred_element_type=F32)


def _dot_tn(a, b):
    return lax.dot_general(a.astype(BF16), b.astype(BF16), (((0,), (0,)), ((), ())), preferred_element_type=F32)


def _sigmoid(x):
    return 0.5 * jnp.tanh(0.5 * x) + 0.5


def _silu(x):
    return x * _sigmoid(x)


def _inproj_kernel(x_ref, g_ref, w_ref, wg_ref, p_ref, gc_ref, gr_ref, xn_ref):
    nt = (((1,), (1,)), ((), ()))

    @pl.when(pl.program_id(1) == 0)
    def _():
        x = x_ref[...]
        r = lax.rsqrt(jnp.mean(x * x, axis=-1, keepdims=True) + EPS)
        xn = (x * r * g_ref[...]).astype(BF16)
        xn_ref[...] = xn
        wg = wg_ref[...].astype(BF16)
        gc_ref[...] = lax.dot_general(xn, wg, nt, preferred_element_type=F32)
        gr_ref[...] = lax.dot_general(wg[0:8, :], xn, nt, preferred_element_type=F32)

    p_ref[...] = lax.dot_general(xn_ref[...], w_ref[...], nt, preferred_element_type=F32)


def _inproj(x, norm_g, w_main, w_in_t, l):
    rows = x.shape[0]
    gate_blk = 4 * W_A // GATE_LANES
    tm = min(rows, INPROJ_TM)
    tn = WBLK
    n_a = 3 * W_A // tn
    wblk = lambda i, j: (l, jnp.where(j < n_a, j, j + (BLK_C0 - n_a)), 0)
    return pl.pallas_call(
        _inproj_kernel,
        grid=(rows // tm, N_P1 // tn),
        in_specs=[
            pl.BlockSpec((tm, D_MODEL), lambda i, j: (i, 0)),
            pl.BlockSpec((None, 1, D_MODEL), lambda i, j: (l, 0, 0)),
            pl.BlockSpec((None, tn, D_MODEL), wblk),
            pl.BlockSpec((None, GATE_LANES, D_MODEL), lambda i, j: (l, gate_blk, 0)),
        ],
        out_specs=[
            pl.BlockSpec((tm, tn), lambda i, j: (i, j)),
            pl.BlockSpec((tm, GATE_LANES), lambda i, j: (i, 0)),
            pl.BlockSpec((None, 8, tm), lambda i, j: (i, 0, 0)),
        ],
        out_shape=[
            jax.ShapeDtypeStruct((rows, N_P1), F32),
            jax.ShapeDtypeStruct((rows, GATE_LANES), F32),
            jax.ShapeDtypeStruct((rows // tm, 8, tm), F32),
        ],
        scratch_shapes=[pltpu.VMEM((tm, D_MODEL), BF16)],
        compiler_params=_cparams(2),
        name="inproj",
    )(x, norm_g, w_main, w_in_t)


def _pack_kernel(a_ref, b_ref, w_ref, *, first_shifted):
    j = pl.program_id(1)

    @pl.when(j < first_shifted)
    def _():
        w_ref[...] = a_ref[...].astype(BF16)

    @pl.when(j >= first_shifted)
    def _():
        w_ref[...] = jnp.concatenate([a_ref[2 * H_A:, :], b_ref[...]], axis=0).astype(BF16)


def _pack_w_in(w_in_t):
    depth = w_in_t.shape[0]
    tn = 2048
    gate_off = 4 * W_A
    assert gate_off % tn == 0 and w_in_t.shape[1] == N_PACK + 2 * H_A and 2 * H_A == 8
    kern = functools.partial(_pack_kernel, first_shifted=gate_off // tn)
    return pl.pallas_call(
        kern,
        grid=(depth, N_PACK // tn),
        in_specs=[
            pl.BlockSpec((None, tn, D_MODEL), lambda l, j: (l, j, 0)),
            pl.BlockSpec((None, 8, D_MODEL), lambda l, j: (l, (j + 1) * (tn // 8), 0)),
        ],
        out_specs=pl.BlockSpec((None, tn, D_MODEL), lambda l, j: (l, j, 0)),
        out_shape=jax.ShapeDtypeStruct((depth, N_PACK, D_MODEL), BF16),
        compiler_params=_cparams(2),
        name="pack_w_in",
    )(w_in_t, w_in_t)


def _conv_silu(ext_ref, cols, w, b, L):
    y = b
    for back in range(CONV_W):
        y = y + ext_ref[8 - back:8 - back + L, cols] * w[CONV_W - 1 - back:CONV_W - back]
    return _silu(y)


def _mlstm_core(g_col, g_row, get_v, ext_s, cw_ref, cb_ref, c_s, n_s, m_s, y_ref, L, t_valid, side_work=None):
    tick = side_work if side_work is not None else (lambda: None)
    neg_inf = jnp.float32(-jnp.inf)
    ig_col, lf_col = g_col, jax.nn.log_sigmoid(g_col)
    ig_row, lf_row = g_row, jax.nn.log_sigmoid(g_row)
    if t_valid < L:
        vc = lax.broadcasted_iota(jnp.int32, (L, GATE_LANES), 0) < t_valid
        vr = lax.broadcasted_iota(jnp.int32, (8, L), 1) < t_valid
        ig_col, lf_col = jnp.where(vc, ig_col, neg_inf), jnp.where(vc, lf_col, 0.0)
        ig_row, lf_row = jnp.where(vr, ig_row, neg_inf), jnp.where(vr, lf_row, 0.0)
    ti = lax.broadcasted_iota(jnp.int32, (L, L), 0)
    si = lax.broadcasted_iota(jnp.int32, (L, L), 1)
    causal = ti >= si
    b_col = jnp.dot(causal.astype(F32), lf_col, preferred_element_type=F32, precision=lax.Precision.HIGHEST)
    b_row = jnp.dot(lf_row, (ti <= si).astype(F32), preferred_element_type=F32, precision=lax.Precision.HIGHEST)
    last = t_valid - 1
    band = min(L, MLSTM_BAND)
    nt = (((1,), (1,)), ((), ()))

    for h in range(H_A):
        cs = slice(h * DH_A, (h + 1) * DH_A)
        ks = slice(W_A + h * DH_A, W_A + (h + 1) * DH_A)
        q = _conv_silu(ext_s, cs, cw_ref[:, cs], cb_ref[:, cs], L)
        tick()
        k = _conv_silu(ext_s, ks, cw_ref[:, ks], cb_ref[:, ks], L) * (DH_A ** -0.5)
        tick()
        v = get_v(h)
        q16, k16, v16 = q.astype(BF16), k.astype(BF16), v.astype(BF16)
        bc = b_col[:, H_A + h:H_A + h + 1]
        igc = ig_col[:, h:h + 1]
        br = b_row[H_A + h:H_A + h + 1, :]
        igr = ig_row[h:h + 1, :]
        m_prev = m_s[h][:, 0:1]
        c_prev = c_s[h]
        n_prev = n_s[h]
        c16 = c_prev.astype(BF16)

        for r in range(L // band):
            rs = slice(r * band, (r + 1) * band)
            kw = (r + 1) * band
            ti = lax.broadcasted_iota(jnp.int32, (band, kw), 0) + r * band
            si = lax.broadcasted_iota(jnp.int32, (band, kw), 1)
            logw = jnp.where(ti >= si, bc[rs] - br[:, 0:kw] + igr[:, 0:kw], neg_inf)
            inter = bc[rs] + m_prev
            m_t = jnp.maximum(inter, jnp.max(logw, axis=1, keepdims=True))
            w_intra = jnp.exp(logw - m_t)
            w_inter = jnp.exp(inter - m_t)
            s = w_intra * lax.dot_general(q16[rs], k16[0:kw], nt, preferred_element_type=F32)
            num = (jnp.dot(s.astype(BF16), v16[0:kw], preferred_element_type=F32)
                   + w_inter * lax.dot_general(q16[rs], c16, nt, preferred_element_type=F32))
            den = (jnp.sum(s, axis=1, keepdims=True)
                   + w_inter * jnp.sum(q[rs] * n_prev, axis=1, keepdims=True))
            y_ref[0, rs, cs] = num / jnp.maximum(jnp.abs(den), jnp.exp(-m_t))
            if r == last // band:
                m_new = m_t[last - r * band:last - r * band + 1, :]
            tick()

        b_last = bc[last:last + 1, :]
        decay = jnp.exp(b_last + m_prev - m_new)
        w_s = jnp.exp(b_last - bc + igc - m_new)
        c_s[h] = decay * c_prev + lax.dot_general((w_s * v).astype(BF16), k16, (((0,), (0,)), ((), ())),
                                                  preferred_element_type=F32)
        n_s[h] = decay * n_prev + jnp.sum(w_s * k, axis=0, keepdims=True)
        m_s[h] = jnp.broadcast_to(m_new, (1, GATE_LANES))
        tick()

    ext_s[0:8, :] = ext_s[t_valid:t_valid + 8, :]


def _mlstm_state_io(c0_ref, n0_ref, m0_ref, tail0_ref, c_s, n_s, m_s, ext_s):
    @pl.when(pl.program_id(1) == 0)
    def _():
        c_s[...] = c0_ref[0]
        n_s[...] = n0_ref[0]
        m_s[...] = m0_ref[0]
        ext_s[0:8, :] = tail0_ref[0]


def _mlstm_state_out(c_out, n_out, m_out, tail_out, c_s, n_s, m_s, ext_s, cprev_ref=None):
    @pl.when(pl.program_id(1) == pl.num_programs(1) - 1)
    def _():
        n_prev = c_out.shape[0] - 1
        if n_prev:
            c_out[0:n_prev] = cprev_ref[...]
        c_out[n_prev, 0] = c_s[...]
        n_out[0] = n_s[...]
        m_out[0] = m_s[...]
        tail_out[0] = ext_s[0:8, :]


def _mlstm_kernel(p_ref, gc_ref, gr_ref, brow_ref, bcol_ref, cw_ref, cb_ref, c0_ref, n0_ref, m0_ref, tail0_ref,
                  *rest, L, t_valid, n_prev):
    cprev_ref = rest[0] if n_prev else None
    y_ref, c_out, n_out, m_out, tail_out, c_s, n_s, m_s, ext_s = rest[1:] if n_prev else rest
    _mlstm_state_io(c0_ref, n0_ref, m0_ref, tail0_ref, c_s, n_s, m_s, ext_s)
    ext_s[8:8 + L, :] = p_ref[0, :, 0:2 * W_A]
    get_v = lambda h: p_ref[0, :, 2 * W_A + h * DH_A:2 * W_A + (h + 1) * DH_A]
    _mlstm_core(gc_ref[0] + brow_ref[...], gr_ref[0] + bcol_ref[...], get_v, ext_s, cw_ref, cb_ref,
                c_s, n_s, m_s, y_ref, L, t_valid)
    _mlstm_state_out(c_out, n_out, m_out, tail_out, c_s, n_s, m_s, ext_s, cprev_ref)


def _seq_a_kernel(x_ref, cos_ref, sin_ref, ng_ref, w_aq, w_ak, w_av, w_c0, w_c1, w_c2, wg_ref, brow_ref, bcol_ref,
                  cw_ref, cb_ref, c0_ref, n0_ref, m0_ref, tail0_ref, y_ref, pc_ref, c_out, n_out, m_out, tail_out,
                  c_s, n_s, m_s, ext_s, v_s, *, L):
    _mlstm_state_io(c0_ref, n0_ref, m0_ref, tail0_ref, c_s, n_s, m_s, ext_s)
    nt = (((1,), (1,)), ((), ()))
    x = x_ref[0]
    r = lax.rsqrt(jnp.mean(x * x, axis=-1, keepdims=True) + EPS)
    xn = (x * r * ng_ref[...]).astype(BF16)
    wg = wg_ref[...].astype(BF16)
    g_col = lax.dot_general(xn, wg, nt, preferred_element_type=F32) + brow_ref[...]
    g_row = lax.dot_general(wg[0:8, :], xn, nt, preferred_element_type=F32) + bcol_ref[...]
    proj = lambda w_ref: lax.dot_general(xn, w_ref[...], nt, preferred_element_type=F32)
    ext_s[8:8 + L, 0:W_A] = proj(w_aq)
    ext_s[8:8 + L, W_A:2 * W_A] = proj(w_ak)
    v_s[...] = proj(w_av)
    piece = 256
    todo = [(w_ref, j, lo) for j, w_ref in enumerate((w_c0, w_c1, w_c2)) for lo in range(0, WBLK, piece)]

    assert piece == WG_C

    def emit_piece():
        w_ref, j, lo = todo.pop(0)
        col = j * WBLK + lo
        y = lax.dot_general(xn, w_ref[lo:lo + piece, :], nt, preferred_element_type=F32)
        if col < 2 * W_C:
            cos = jnp.concatenate([cos_ref[0], cos_ref[0]], axis=1)
            sin = jnp.concatenate([sin_ref[0], sin_ref[0]], axis=1)
            y = _rope(y, cos, sin)
            if col < W_C:
                y = y * (DH_C ** -0.5)
        pc_ref[0, :, col:col + piece] = y

    side_work = lambda: emit_piece() if todo else None
    get_v = lambda h: v_s[:, h * DH_A:(h + 1) * DH_A]
    _mlstm_core(g_col, g_row, get_v, ext_s, cw_ref, cb_ref, c_s, n_s, m_s, y_ref, L, L, side_work)
    while todo:
        emit_piece()
    _mlstm_state_out(c_out, n_out, m_out, tail_out, c_s, n_s, m_s, ext_s)


def _seq_a(x, cos_t, sin_t, norm_g, w_main, w_in_t, bias_row, bias_col, conv_w, conv_b, c0, n0, m0, tail0, l, *, L):
    rows = x.shape[0]
    nch = rows // L
    x3 = x.reshape(nch, L, D_MODEL)
    gate_blk = 4 * W_A // GATE_LANES
    once = pl.Buffered(1)
    chunk = lambda b, c: (c, 0, 0)
    layer3 = lambda b, c: (l, 0, 0)
    wblk = lambda blk: pl.BlockSpec((None, WBLK, D_MODEL), lambda b, c, blk=blk: (l, blk, 0), pipeline_mode=once)
    st5 = lambda b, c: (0, 0, 0, 0, 0)
    st4 = lambda b, c: (0, 0, 0, 0)
    first4 = lambda b, c: (0, 0, 0, 0)
    first3 = lambda b, c: (0, 0, 0)
    y, pc, c1, n1, m1, tail1 = pl.pallas_call(
        functools.partial(_seq_a_kernel, L=L),
        grid=(1, nch),
        in_specs=[
            pl.BlockSpec((1, L, D_MODEL), chunk),
            pl.BlockSpec((1, L, 2 * DH_C), chunk),
            pl.BlockSpec((1, L, 2 * DH_C), chunk),
            pl.BlockSpec((None, 1, D_MODEL), layer3, pipeline_mode=once),
        ] + [wblk(b) for b in (0, 1, 2, BLK_C0, BLK_C0 + 1, BLK_C0 + 2)] + [
            pl.BlockSpec((None, GATE_LANES, D_MODEL), lambda b, c: (l, gate_blk, 0), pipeline_mode=once),
            pl.BlockSpec((None, 1, GATE_LANES), layer3, pipeline_mode=once),
            pl.BlockSpec((None, 8, 1), layer3, pipeline_mode=once),
            pl.BlockSpec((None, CONV_W, 2 * W_A), layer3, pipeline_mode=once),
            pl.BlockSpec((None, 1, 2 * W_A), layer3, pipeline_mode=once),
            pl.BlockSpec((None, 1, H_A, DH_A, DH_A), st5, pipeline_mode=once),
            pl.BlockSpec((None, 1, H_A, 1, DH_A), st5, pipeline_mode=once),
            pl.BlockSpec((None, 1, H_A, 1, GATE_LANES), st5, pipeline_mode=once),
            pl.BlockSpec((None, 1, 8, 2 * W_A), st4, pipeline_mode=once),
        ],
        out_specs=[
            pl.BlockSpec((1, L, W_A), chunk),
            pl.BlockSpec((1, L, 3 * WBLK), chunk),
            pl.BlockSpec((1, 1, H_A, DH_A, DH_A), lambda b, c: (0, 0, 0, 0, 0)),
            pl.BlockSpec((1, H_A, 1, DH_A), first4),
            pl.BlockSpec((1, H_A, 1, GATE_LANES), first4),
            pl.BlockSpec((1, 8, 2 * W_A), first3),
        ],
        out_shape=[
            jax.ShapeDtypeStruct((nch, L, W_A), F32),
            jax.ShapeDtypeStruct((nch, L, 3 * WBLK), F32),
            jax.ShapeDtypeStruct((1, 1, H_A, DH_A, DH_A), F32),
            jax.ShapeDtypeStruct((1, H_A, 1, DH_A), F32),
            jax.ShapeDtypeStruct((1, H_A, 1, GATE_LANES), F32),
            jax.ShapeDtypeStruct((1, 8, 2 * W_A), F32),
        ],
        scratch_shapes=[
            pltpu.VMEM((H_A, DH_A, DH_A), F32),
            pltpu.VMEM((H_A, 1, DH_A), F32),
            pltpu.VMEM((H_A, 1, GATE_LANES), F32),
            pltpu.VMEM((L + 8, 2 * W_A), F32),
            pltpu.VMEM((L, W_A), F32),
        ],
        compiler_params=_cparams(2),
        name="seq_a",
    )(x3, cos_t.reshape(nch, L, 2 * DH_C), sin_t.reshape(nch, L, 2 * DH_C), norm_g, *([w_main] * 6), w_in_t,
      bias_row, bias_col, conv_w, conv_b, c0, n0, m0, tail0)
    return (y.reshape(rows, W_A), pc.reshape(rows, 3 * WBLK), c1[0], n1[:, :, 0, :], m1[:, :, 0, 0],
            tail1[:, 8 - (CONV_W - 1):, :])


def _mlstm(p, gcol, grow, bias_row, bias_col, conv_w, conv_b, c0, n0, m0, tail0, l, ls, c_prev, *, L, t_valid):
    batch = c0.shape[1]
    rows = p.shape[0]
    nch = rows // (batch * L)
    n_prev = 0 if c_prev is None else c_prev.shape[0]
    p3 = p.reshape(batch * nch, L, N_P1)
    gc3 = gcol.reshape(batch * nch, L, GATE_LANES)
    tm = grow.shape[2]
    if L % GATE_LANES == 0:
        per = tm // L
        gr3, gr_index = grow, (lambda b, c: ((b * nch + c) // per, 0, (b * nch + c) % per))
    else:
        gr3 = grow.transpose(1, 0, 2).reshape(8, batch * nch, L).transpose(1, 0, 2)
        gr_index = lambda b, c: (b * nch + c, 0, 0)
    kern = functools.partial(_mlstm_kernel, L=L, t_valid=t_valid, n_prev=n_prev)
    chunk = lambda b, c: (b * nch + c, 0, 0)
    layer3 = lambda b, c: (l, 0, 0)
    stack5 = lambda b, c: (0, b, 0, 0, 0)
    per_b4 = lambda b, c: (b, 0, 0, 0)
    per_b3 = lambda b, c: (b, 0, 0)
    st5 = lambda b, c: (ls, b, 0, 0, 0)
    st4 = lambda b, c: (ls, b, 0, 0)
    y, c1, n1, m1, tail1 = pl.pallas_call(
        kern,
        grid=(batch, nch),
        in_specs=[
            pl.BlockSpec((1, L, 3 * W_A), chunk),
            pl.BlockSpec((1, L, GATE_LANES), chunk),
            pl.BlockSpec((1, 8, L), gr_index),
            pl.BlockSpec((None, 1, GATE_LANES), layer3),
            pl.BlockSpec((None, 8, 1), layer3),
            pl.BlockSpec((None, CONV_W, 2 * W_A), layer3),
            pl.BlockSpec((None, 1, 2 * W_A), layer3),
            pl.BlockSpec((None, 1, H_A, DH_A, DH_A), st5),
            pl.BlockSpec((None, 1, H_A, 1, DH_A), st5),
            pl.BlockSpec((None, 1, H_A, 1, GATE_LANES), st5),
            pl.BlockSpec((None, 1, 8, 2 * W_A), st4),
        ] + ([pl.BlockSpec((n_prev, 1, H_A, DH_A, DH_A), stack5)] if n_prev else []),
        out_specs=[
            pl.BlockSpec((1, L, W_A), chunk),
            pl.BlockSpec((n_prev + 1, 1, H_A, DH_A, DH_A), stack5),
            pl.BlockSpec((1, H_A, 1, DH_A), per_b4),
            pl.BlockSpec((1, H_A, 1, GATE_LANES), per_b4),
            pl.BlockSpec((1, 8, 2 * W_A), per_b3),
        ],
        out_shape=[
            jax.ShapeDtypeStruct((batch * nch, L, W_A), F32),
            jax.ShapeDtypeStruct((n_prev + 1, batch, H_A, DH_A, DH_A), F32),
            jax.ShapeDtypeStruct((batch, H_A, 1, DH_A), F32),
            jax.ShapeDtypeStruct((batch, H_A, 1, GATE_LANES), F32),
            jax.ShapeDtypeStruct((batch, 8, 2 * W_A), F32),
        ],
        scratch_shapes=[
            pltpu.VMEM((H_A, DH_A, DH_A), F32),
            pltpu.VMEM((H_A, 1, DH_A), F32),
            pltpu.VMEM((H_A, 1, GATE_LANES), F32),
            pltpu.VMEM((L + 8, 2 * W_A), F32),
        ],
        compiler_params=_cparams(2),
        name="mlstm",
    )(p3, gc3, gr3, bias_row, bias_col, conv_w, conv_b, c0, n0, m0, tail0, *([c_prev] if n_prev else []))
    return y.reshape(rows, W_A), c1, n1[:, :, 0, :], m1[:, :, 0, 0], tail1[:, 8 - (CONV_W - 1):, :]


def _rope(x, cos, sin):
    width = x.shape[1]
    lane = lax.broadcasted_iota(jnp.int32, (1, width), 1) % DH_C
    half = ROT_DIM // 2
    partner = jnp.where(lane < half, pltpu.roll(x, width - half, 1), pltpu.roll(x, half, 1))
    return x * cos + partner * sin


def _dil_attn_kernel(q_ref, k_ref, v_ref, o_ref, mx_ref, den_ref, qs, kbuf, vbuf, bias, *, d):
    i = pl.program_id(0)
    T = ATT_TILE
    HL = WG_C // 2
    nblk = T // SPAN_C
    nres = T // (SPAN_C * d)
    cur = (i % 2) * T
    prev = T - cur
    neg_inf = jnp.float32(-jnp.inf)

    def split(x):
        return x[:, 0:HL], x[:, HL:WG_C]

    def put(buf, lo, x):
        a, b = split(x)
        buf[0, pl.ds(lo, T), :] = a
        buf[1, pl.ds(lo, T), :] = b

    @pl.when(i == 0)
    def _():
        ii = lax.broadcasted_iota(jnp.int32, (SPAN_C, 2 * SPAN_C), 0)
        jj = lax.broadcasted_iota(jnp.int32, (SPAN_C, 2 * SPAN_C), 1)
        band = (jj >= ii) & (jj <= ii + SPAN_C)
        bias[1] = jnp.where(band, 0.0, neg_inf)
        bias[0] = jnp.where(band & (jj >= SPAN_C), 0.0, neg_inf)
        kbuf[:, T:2 * T, :] = jnp.zeros((2, T, HL), F32)
        vbuf[:, T:2 * T, :] = jnp.zeros((2, T, HL), F32)

    put(qs, 0, q_ref[...])
    put(kbuf, cur, k_ref[...])
    put(vbuf, cur, v_ref[...])

    lane_head = lax.broadcasted_iota(jnp.int32, (SPAN_C, WG_C), 1) // DH_C
    hm = [(lane_head == h).astype(F32) for h in range(HG_C)]

    def rows(start, size):
        return pl.ds(start, size) if d == 1 else pl.ds(start, size, stride=d)

    def take(buf, start, size):
        return jnp.concatenate([buf[0, rows(start, size), :], buf[1, rows(start, size), :]], axis=1)

    def pick(per_head):
        out = jnp.broadcast_to(per_head[0], (SPAN_C, WG_C))
        for h in range(1, HG_C):
            out = jnp.where(lane_head >= h, per_head[h], out)
        return out

    def body(blk, carry):
        r = blk % d
        c = blk // d
        qstart = r + SPAN_C * d * c
        in_tile = cur + qstart
        before = jnp.where(c > 0, in_tile - SPAN_C * d, prev + r + SPAN_C * d * (nres - 1))
        qb = take(qs, qstart, SPAN_C)
        kb = jnp.concatenate([take(kbuf, before, SPAN_C), take(kbuf, in_tile, SPAN_C)], axis=0)
        vb = jnp.concatenate([take(vbuf, before, SPAN_C), take(vbuf, in_tile, SPAN_C)], axis=0)
        has_prev = jnp.logical_or(i > 0, c > 0).astype(jnp.int32)
        kb16, vb16 = kb.astype(BF16), vb.astype(BF16)
        pvs, mxs, dens = [], [], []
        for h0 in range(0, HG_C, 2):
            q2 = jnp.concatenate([qb * hm[h0], qb * hm[h0 + 1]], axis=0)
            s = _dot_nt(q2, kb16).reshape(2, SPAN_C, 2 * SPAN_C) + bias[has_prev]
            mx = jnp.max(s, axis=2, keepdims=True)
            p = jnp.exp(s - mx)
            den = jnp.sum(p, axis=2, keepdims=True)
            pv = _dot(p.reshape(2 * SPAN_C, 2 * SPAN_C), vb16).reshape(2, SPAN_C, WG_C)
            pvs += [pv[0], pv[1]]
            mxs += [mx[0], mx[1]]
            dens += [den[0], den[1]]
        outs = (pick(pvs), pick(mxs), pick(dens))
        for ref, val in zip((o_ref, mx_ref, den_ref), outs):
            for half, part in enumerate(split(val)):
                ref[half, rows(qstart, SPAN_C), :] = part
        return carry

    lax.fori_loop(0, nblk, body, 0, unroll=8)


def _dil_attn(p, base, g, d):
    rows = p.shape[0]
    T = ATT_TILE
    kern = functools.partial(_dil_attn_kernel, d=d)
    col = lambda off: pl.BlockSpec((T, WG_C), lambda i, off=off: (i, (base + off - OFF_CQ) // WG_C + g))
    slab = pl.BlockSpec((2, T, WG_C // 2), lambda i: (0, i, 0))
    slab_shape = jax.ShapeDtypeStruct((2, rows, WG_C // 2), F32)
    return pl.pallas_call(
        kern,
        grid=(rows // T,),
        in_specs=[col(OFF_CQ), col(OFF_CK), col(OFF_CV)],
        out_specs=[slab, slab, slab],
        out_shape=[slab_shape, slab_shape, slab_shape],
        scratch_shapes=[pltpu.VMEM((2, T, WG_C // 2), F32), pltpu.VMEM((2, 2 * T, WG_C // 2), F32),
                        pltpu.VMEM((2, 2 * T, WG_C // 2), F32), pltpu.VMEM((2, SPAN_C, 2 * SPAN_C), F32)],
        compiler_params=_cparams(1),
        name="dil_attn_d%d" % d,
    )(p, p, p)


def _sample_attn_kernel(pc0, pc1, pc2, cos_ref, sin_ref, c0_ref, c1_ref, c2_ref,
                        o0, o1, o2, m0, m1, m2, d0, d1, d2, kr_ref, *, n_tok):
    pc = jnp.concatenate([pc0[0], pc1[0], pc2[0]], axis=1)
    reps = W_C // (2 * DH_C)
    cos = jnp.concatenate([cos_ref[...]] * reps, axis=1)
    sin = jnp.concatenate([sin_ref[...]] * reps, axis=1)
    qr = _rope(pc[:, 0:W_C], cos, sin) * (DH_C ** -0.5)
    kr = _rope(pc[:, W_C:2 * W_C], cos, sin)
    v = pc[:, 2 * W_C:3 * W_C]
    kr_ref[0] = kr
    nrow = HG_C * SAMPLE_ROWS
    rowh = lax.broadcasted_iota(jnp.int32, (nrow, WG_C), 0) // SAMPLE_ROWS
    laneh = lax.broadcasted_iota(jnp.int32, (nrow, WG_C), 1) // DH_C
    hm = (rowh == laneh).astype(F32)
    tok_col = lax.broadcasted_iota(jnp.int32, (nrow, 1), 0) % SAMPLE_ROWS
    out_row = lax.broadcasted_iota(jnp.int32, (SAMPLE_ROWS, WG_C), 0)
    neg_inf = jnp.float32(-jnp.inf)
    crefs = (c0_ref, c1_ref, c2_ref)
    orefs = (o0, o1, o2)
    mrefs = (m0, m1, m2)
    drefs = (d0, d1, d2)
    for g, (_, d) in enumerate(DIL_PATTERNS):
        gs = slice(g * WG_C, (g + 1) * WG_C)
        n_buf = SPAN_C * d
        kn = kr[:, gs]
        vn = v[:, gs]
        qm = jnp.concatenate([qr[:, gs]] * HG_C, axis=0) * hm
        s = _dot(qm, crefs[g][0, 0])
        tok = lax.broadcasted_iota(jnp.int32, (nrow, n_buf), 0) % SAMPLE_ROWS
        pos = lax.broadcasted_iota(jnp.int32, (nrow, n_buf), 1)
        attends = (pos >= tok) if d == 1 else ((pos & (d - 1)) == tok)
        s = jnp.where(attends | (tok >= n_tok), s, neg_inf)
        new_ok = [(u <= tok_col) if d == 1 else (u == tok_col) for u in range(n_tok)]
        s_new = [jnp.where(ok, jnp.sum(qm * kn[u:u + 1, :], axis=1, keepdims=True), neg_inf)
                 for u, ok in enumerate(new_ok)]
        mx = jnp.max(s, axis=1, keepdims=True)
        for sn in s_new:
            mx = jnp.maximum(mx, sn)
        p = jnp.exp(s - mx)
        den = jnp.sum(p, axis=1, keepdims=True)
        acc = _dot_nt(p, crefs[g][0, 1])
        for u, sn in enumerate(s_new):
            pn = jnp.exp(sn - mx)
            den = den + pn
            acc = acc + pn * vn[u:u + 1, :]
        stacked = (acc * hm, mx * hm, den * hm)
        for ref, val, pad in zip((orefs[g], mrefs[g], drefs[g]), stacked, (0.0, 0.0, 1.0)):
            v8 = val[0:SAMPLE_ROWS]
            for h in range(1, HG_C):
                v8 = v8 + val[h * SAMPLE_ROWS:(h + 1) * SAMPLE_ROWS]
            v8 = jnp.where(out_row < n_tok, v8, pad)
            for half in range(2):
                ref[half] = v8[:, half * (WG_C // 2):(half + 1) * (WG_C // 2)]


def _cache_views(caches):
    views = []
    for (win, d), c in zip(DIL_PATTERNS, caches):
        depth, batch, n_buf = c.shape[:3]
        assert n_buf == SPAN_C * d
        views.append(jnp.transpose(c, (0, 1, 3, 4, 5, 2)).reshape(depth, batch, 2, WG_C, n_buf))
    return views


def _sample_attn(p, cos_s, sin_s, views, l, n_tok):
    batch = p.shape[0] // SAMPLE_ROWS
    p3 = p.reshape(batch, SAMPLE_ROWS, N_P1)
    pcol = lambda k: pl.BlockSpec((1, SAMPLE_ROWS, 1024), lambda b, k=k: (b, 0, OFF_CQ // 1024 + k))
    tab = pl.BlockSpec((SAMPLE_ROWS, 2 * DH_C), lambda b: (0, 0))
    cache_spec = lambda d: pl.BlockSpec((None, 1, 2, WG_C, SPAN_C * d), lambda b: (l, b, 0, 0, 0))
    out = pl.BlockSpec((2, SAMPLE_ROWS, WG_C // 2), lambda b: (0, b, 0))
    res = pl.pallas_call(
        functools.partial(_sample_attn_kernel, n_tok=n_tok),
        grid=(batch,),
        in_specs=[pcol(0), pcol(1), pcol(2), tab, tab] + [cache_spec(d) for _, d in DIL_PATTERNS],
        out_specs=[out] * 9 + [pl.BlockSpec((1, SAMPLE_ROWS, W_C), lambda b: (b, 0, 0))],
        out_shape=[jax.ShapeDtypeStruct((2, batch * SAMPLE_ROWS, WG_C // 2), F32)] * 9
        + [jax.ShapeDtypeStruct((batch, SAMPLE_ROWS, W_C), F32)],
        compiler_params=_cparams(1),
        name="sample_attn",
    )(p3, p3, p3, cos_s, sin_s, *views)
    return list(res[:9]), res[9].reshape(batch * SAMPLE_ROWS, W_C)


def _post_kernel(x_ref, h_ref, o0, o1, o2, m0, m1, m2, d0, d1, d2, cz_ref, ng_ref, w_ao, w_az, w_bu, w_bv, w_bz, w_ga, w_gb,
                 w_gc, lg_ref, lb_ref, wsp_ref, bsp_ref, wb_ref, wo_ref, fg_ref, out_ref, *rest, nchunks, final,
                 emit_vn):
    if emit_vn:
        vn_out, vn_s, yb_s = rest
    else:
        vn_s, yb_s = rest
    nt = (((1,), (1,)), ((), ()))
    x = x_ref[...]
    r = lax.rsqrt(jnp.mean(x * x, axis=-1, keepdims=True) + EPS)
    xn = (x * r * ng_ref[...]).astype(BF16)
    proj = lambda w_ref: lax.dot_general(xn, w_ref[...], nt, preferred_element_type=F32)
    branch = lambda y, lo, hi: jnp.dot(y.astype(BF16), wb_ref[lo:hi, :], preferred_element_type=F32)

    ya = h_ref[...] * _sigmoid(proj(w_ao)) * _silu(proj(w_az))
    acc = _sigmoid(proj(w_ga)) * branch(ya, 0, W_A)

    v = proj(w_bv)
    mu = jnp.mean(v, axis=-1, keepdims=True)
    var = jnp.mean(jnp.square(v - mu), axis=-1, keepdims=True)
    vn = (v - mu) * lax.rsqrt(var + EPS) * lg_ref[...] + lb_ref[...]
    if emit_vn:
        vn_out[...] = vn
    vn_s[...] = vn.astype(BF16)
    gate_b = proj(w_bu) * _silu(proj(w_bz))
    tri = (lax.broadcasted_iota(jnp.int32, (CHUNK_B, CHUNK_B), 0)
           >= lax.broadcasted_iota(jnp.int32, (CHUNK_B, CHUNK_B), 1))
    for g in range(G_B):
        wg = jnp.where(tri, wsp_ref[g], 0.0).astype(BF16)
        cs = slice(g * DG_B, (g + 1) * DG_B)
        for c in range(nchunks):
            rs = slice(c * CHUNK_B, (c + 1) * CHUNK_B)
            yb_s[rs, cs] = jnp.dot(wg, vn_s[rs, cs], preferred_element_type=F32) + bsp_ref[g]
    acc = acc + _sigmoid(proj(w_gb)) * branch(gate_b * yb_s[...], W_A, W_A + W_B)

    unslab = lambda ref: jnp.concatenate([ref[0], ref[1]], axis=1)
    ms = [unslab(m0), unslab(m1), unslab(m2)]
    mx = jnp.maximum(jnp.maximum(ms[0], ms[1]), ms[2])
    es = [jnp.exp(m - mx) for m in ms]
    inv_tot = 1.0 / (es[0] * unslab(d0) + es[1] * unslab(d1) + es[2] * unslab(d2))
    pc = None
    for g, o_ref in enumerate((o0, o1, o2)):
        gs = slice(g * WG_C, (g + 1) * WG_C)
        yc = unslab(o_ref) * (es[g] * inv_tot) * _silu(cz_ref[:, gs])
        t = branch(yc, W_A + W_B + g * WG_C, W_A + W_B + (g + 1) * WG_C)
        pc = t if pc is None else pc + t
    acc = acc + _sigmoid(proj(w_gc)) * pc

    out = x + jnp.dot(acc.astype(BF16), wo_ref[...], preferred_element_type=F32)
    if final:
        rr = lax.rsqrt(jnp.mean(out * out, axis=-1, keepdims=True) + EPS)
        out = out * rr * fg_ref[...]
    out_ref[...] = out


def _post(x, p1, base, h, att, norm_g, w_main, ln_g, ln_b, w_sp, b_sp_col, w_branch, w_out, final_g, l, final,
          emit_vn):
    rows = x.shape[0]
    tm = min(rows, POST_TM)
    once = pl.Buffered(1)
    row = lambda w: pl.BlockSpec((tm, w), lambda i: (i, 0))
    slab = pl.BlockSpec((2, tm, WG_C // 2), lambda i: (0, i, 0))
    wblk = lambda b: pl.BlockSpec((None, WBLK, D_MODEL), lambda i, b=b: (l, b, 0), pipeline_mode=once)
    vec = lambda w: pl.BlockSpec((None, 1, w), lambda i: (l, 0, 0), pipeline_mode=once)
    in_specs = (
        [row(D_MODEL), row(W_A)] + [slab] * 9
        + [pl.BlockSpec((tm, W_C), lambda i: (i, (base + OFF_CZ - OFF_CQ) // W_C)), vec(D_MODEL)]
        + [wblk(b) for b in (BLK_AO, BLK_AZ, BLK_BU, BLK_BV, BLK_BZ, BLK_GA, BLK_GB, BLK_GC)]
        + [vec(W_B), vec(W_B),
           pl.BlockSpec((None, G_B, CHUNK_B, CHUNK_B), lambda i: (l, 0, 0, 0), pipeline_mode=once),
           pl.BlockSpec((None, G_B, CHUNK_B, 1), lambda i: (l, 0, 0, 0), pipeline_mode=once),
           pl.BlockSpec((None, W_A + W_B + W_C, D_MODEL), lambda i: (l, 0, 0), pipeline_mode=once),
           pl.BlockSpec((None, D_MODEL, D_MODEL), lambda i: (l, 0, 0), pipeline_mode=once),
           pl.BlockSpec((1, D_MODEL), lambda i: (0, 0), pipeline_mode=once)])
    out_specs = [row(D_MODEL)]
    out_shape = [jax.ShapeDtypeStruct((rows, D_MODEL), F32)]
    if emit_vn:
        out_specs.append(row(W_B))
        out_shape.append(jax.ShapeDtypeStruct((rows, W_B), F32))
    res = pl.pallas_call(
        functools.partial(_post_kernel, nchunks=tm // CHUNK_B, final=final, emit_vn=emit_vn),
        grid=(rows // tm,),
        in_specs=in_specs,
        out_specs=out_specs,
        out_shape=out_shape,
        scratch_shapes=[pltpu.VMEM((tm, W_B), BF16), pltpu.VMEM((tm, W_B), F32)],
        compiler_params=_cparams(1),
        name="post",
    )(x, h, *att, p1, norm_g, *([w_main] * 8), ln_g, ln_b, w_sp, b_sp_col, w_branch, w_out, final_g)
    return (res[0], res[1]) if emit_vn else (res[0], None)


def _rope_tables(pos):
    half = ROT_DIM // 2
    inv = ROPE_THETA ** (-jnp.arange(half, dtype=F32) / half)
    rest = DH_C - ROT_DIM
    inv_h = jnp.concatenate([inv, inv, jnp.zeros((rest,), F32)])
    sgn_h = jnp.concatenate([-jnp.ones((half,), F32), jnp.ones((half,), F32), jnp.zeros((rest,), F32)])
    inv_l = jnp.concatenate([inv_h, inv_h])[None, :]
    sgn_l = jnp.concatenate([sgn_h, sgn_h])[None, :]
    ang = pos.astype(F32)[:, None] * inv_l
    return jnp.cos(ang), jnp.sin(ang) * sgn_l


def _stack_kv(k_rows, v_rows):
    b, t, _ = k_rows.shape
    return jnp.stack([k_rows.reshape(b, t, HG_C, DH_C), v_rows.reshape(b, t, HG_C, DH_C)], axis=2)


def kernel(x_prompt, x_sample, state_C, state_n, state_m, state_conv, cache_kv_w128, cache_kv_w512, cache_kv_w2048,
           norm_g, w_in, b_igate, b_fgate, conv_w, conv_b, ln_v_g, ln_v_b, w_spatial, b_spatial, w_branch, w_out,
           final_norm_g):
    depth = w_in.shape[0]
    bp, seq, _ = x_prompt.shape
    bs, n_tok, _ = x_sample.shape
    assert bp == 1 and seq % ATT_TILE == 0 and n_tok <= SAMPLE_ROWS // 2 and n_tok >= CONV_W - 1
    caches = (cache_kv_w128, cache_kv_w512, cache_kv_w2048)
    pad_tok = SAMPLE_ROWS - n_tok
    rows_s = bs * SAMPLE_ROWS

    w_in_t = jnp.swapaxes(w_in, 1, 2)
    w_main = _pack_w_in(w_in_t)
    gate_bias = jnp.concatenate([b_igate, b_fgate], axis=1)
    bias_row = jnp.pad(gate_bias, ((0, 0), (0, GATE_LANES - 2 * H_A)))[:, None, :]
    bias_col = gate_bias[:, :, None]
    wb16 = w_branch.astype(BF16)
    wo16 = w_out.astype(BF16)
    b_sp_col = b_spatial[..., None]
    norm_g3, conv_b3 = norm_g[:, None, :], conv_b[:, None, :]
    ln_g3, ln_b3 = ln_v_g[:, None, :], ln_v_b[:, None, :]
    seqs_per_chunk = CHUNK_B // SAMPLE_ROWS
    w8 = jnp.pad(w_spatial[:, :, :n_tok, :n_tok], ((0, 0), (0, 0), (0, pad_tok), (0, pad_tok)))
    w_sp_s = jnp.einsum('ab,lgij->lgaibj', jnp.eye(seqs_per_chunk, dtype=F32), w8).reshape(
        depth, G_B, CHUNK_B, CHUNK_B)
    b_sp_s = jnp.tile(jnp.pad(b_spatial[:, :, :n_tok], ((0, 0), (0, 0), (0, pad_tok))),
                      (1, 1, seqs_per_chunk))[..., None]

    cos_p, sin_p = _rope_tables(jnp.arange(seq))
    cos_s, sin_s = _rope_tables(PAST_LEN + jnp.arange(SAMPLE_ROWS))

    hp = x_prompt.reshape(seq, D_MODEL)
    hs = jnp.pad(x_sample, ((0, 0), (0, pad_tok), (0, 0))).reshape(rows_s, D_MODEL)
    fg = final_norm_g[None, :]
    zeros_c = jnp.zeros((1, 1, H_A, DH_A, DH_A), F32)
    zeros_n = jnp.zeros((1, 1, H_A, 1, DH_A), F32)
    zeros_m = jnp.zeros((1, 1, H_A, 1, GATE_LANES), F32)
    zeros_tail = jnp.zeros((1, 1, 8, 2 * W_A), F32)
    n0_s = state_n[:, :, :, None, :]
    m0_s = jnp.broadcast_to(state_m[:, :, :, None, None], (depth, bs, H_A, 1, GATE_LANES))
    tail0_s = jnp.pad(state_conv, ((0, 0), (0, 0), (8 - (CONV_W - 1), 0), (0, 0)))
    views = _cache_views(caches)

    c_stack_s = None
    p_out = {k: [] for k in ('C', 'n', 'm', 'conv', 'kv0', 'kv1', 'kv2')}
    s_out = {k: [] for k in ('n', 'm', 'conv', 'chunk_v', 'kv0', 'kv1', 'kv2')}
    for l in range(depth):
        final = l == depth - 1

        ya, pp, c1, n1, m1, conv1 = _seq_a(hp, cos_p, sin_p, norm_g3, w_main, w_in_t, bias_row, bias_col, conv_w, conv_b3,
                                           zeros_c, zeros_n, zeros_m, zeros_tail, l, L=MLSTM_L)
        att = [[], [], []]
        for g, (win, d) in enumerate(DIL_PATTERNS):
            for dst, val in zip(att, _dil_attn(pp, 0, g, d)):
                dst.append(val)
            keep = min(win, seq)
            k_g = pp[seq - keep:, OFF_CK - OFF_CQ + g * WG_C:OFF_CK - OFF_CQ + (g + 1) * WG_C]
            v_g = pp[seq - keep:, OFF_CV - OFF_CQ + g * WG_C:OFF_CV - OFF_CQ + (g + 1) * WG_C]
            p_out['kv%d' % g].append(_stack_kv(k_g[None], v_g[None]))
        hp, _ = _post(hp, pp, 0, ya, att[0] + att[1] + att[2], norm_g3, w_main, ln_g3, ln_b3, w_spatial, b_sp_col, wb16, wo16, fg, l,
                      final, False)
        p_out['C'].append(c1)
        p_out['n'].append(n1)
        p_out['m'].append(m1)
        p_out['conv'].append(conv1)

        ps, gcol, grow = _inproj(hs, norm_g3, w_main, w_in_t, l)
        ya, c1, n1, m1, conv1 = _mlstm(ps, gcol, grow, bias_row, bias_col, conv_w, conv_b3,
                                       state_C, n0_s, m0_s, tail0_s, l, l, c_stack_s, L=SAMPLE_ROWS, t_valid=n_tok)
        c_stack_s = c1
        att_s, kr = _sample_attn(ps, cos_s, sin_s, views, l, n_tok)
        hs, vn = _post(hs, ps, OFF_CQ, ya, att_s, norm_g3, w_main, ln_g3, ln_b3, w_sp_s, b_sp_s, wb16, wo16, fg, l,
                       final, True)
        s_out['n'].append(n1)
        s_out['m'].append(m1)
        s_out['conv'].append(conv1)
        s_out['chunk_v'].append(vn.reshape(bs, SAMPLE_ROWS, W_B)[:, :n_tok])
        kr3 = kr.reshape(bs, SAMPLE_ROWS, W_C)[:, :n_tok]
        v3 = ps[:, OFF_CV:OFF_CV + W_C].reshape(bs, SAMPLE_ROWS, W_C)[:, :n_tok]
        for g in range(len(DIL_PATTERNS)):
            gs = slice(g * WG_C, (g + 1) * WG_C)
            s_out['kv%d' % g].append(_stack_kv(kr3[:, :, gs], v3[:, :, gs]))

    stk = lambda d, k: jnp.stack(d[k], axis=0)
    y_prompt = hp.reshape(bp, seq, D_MODEL)
    y_sample = hs.reshape(bs, SAMPLE_ROWS, D_MODEL)[:, :n_tok]
    return (y_prompt, y_sample,
            stk(p_out, 'C'), stk(p_out, 'n'), stk(p_out, 'm'), stk(p_out, 'conv'),
            stk(p_out, 'kv0'), stk(p_out, 'kv1'), stk(p_out, 'kv2'),
            c_stack_s, stk(s_out, 'n'), stk(s_out, 'm'), stk(s_out, 'conv'), stk(s_out, 'chunk_v'),
            stk(s_out, 'kv0'), stk(s_out, 'kv1'), stk(s_out, 'kv2'))
```

```python
import functools

import jax
import jax.numpy as jnp
from jax import lax
from jax.experimental import pallas as pl
from jax.experimental.pallas import tpu as pltpu

F32 = jnp.float32
BF16 = jnp.bfloat16

D_MODEL = 1024
H_A = 4
DH_A = 256
W_A = H_A * DH_A
CONV_W = 4
G_B = 4
CHUNK_B = 128
W_B = 1024
DG_B = W_B // G_B
DIL_PATTERNS = ((128, 1), (512, 4), (2048, 16))
HG_C = 4
DH_C = 64
WG_C = HG_C * DH_C
W_C = len(DIL_PATTERNS) * WG_C
SPAN_C = 128
ROT_DIM = DH_C // 4
ROPE_THETA = 500000.0
EPS = 1e-6
PAST_LEN = 16384

N_PACK = 14336
WBLK = 1024
BLK_AO, BLK_AZ, BLK_BU, BLK_BV, BLK_BZ, BLK_C0, BLK_GA, BLK_GB, BLK_GC = 3, 4, 5, 6, 7, 8, 11, 12, 13
N_P1 = 6144
OFF_CQ, OFF_CK, OFF_CV, OFF_CZ = 3072, 3840, 4608, 5376
GATE_LANES = 128

INPROJ_TM = 2048
POST_TM = 512
MLSTM_BAND = 128
MLSTM_L = 256
SAMPLE_ROWS = 8
SAMPLE_ATT_SEQS = 2
ATT_TILE = 2048
VMEM_LIMIT = 56 * 1024 * 1024


def _cparams(n_axes):
    return pltpu.CompilerParams(dimension_semantics=("arbitrary",) * n_axes, vmem_limit_bytes=VMEM_LIMIT)


def _dot(a, b):
    return jnp.dot(a.astype(BF16), b.astype(BF16), preferred_element_type=F32)


def _dot_nt(a, b):
    return lax.dot_general(a.astype(BF16), b.astype(BF16), (((1,), (1,)), ((), ())), preferred_element_type=F32)


def _dot_tn(a, b):
    return lax.dot_general(a.astype(BF16), b.astype(BF16), (((0,), (0,)), ((), ())), preferred_element_type=F32)


def _sigmoid(x):
    return 0.5 * jnp.tanh(0.5 * x) + 0.5


def _silu(x):
    return x * _sigmoid(x)


def _inproj_kernel(x_ref, g_ref, w_ref, wg_ref, p_ref, gc_ref, gr_ref, xn_ref):
    nt = (((1,), (1,)), ((), ()))

    @pl.when(pl.program_id(1) == 0)
    def _():
        x = x_ref[...]
        r = lax.rsqrt(jnp.mean(x * x, axis=-1, keepdims=True) + EPS)
        xn = (x * r * g_ref[...]).astype(BF16)
        xn_ref[...] = xn
        wg = wg_ref[...].astype(BF16)
        gc_ref[...] = lax.dot_general(xn, wg, nt, preferred_element_type=F32)
        gr_ref[...] = lax.dot_general(wg[0:8, :], xn, nt, preferred_element_type=F32)

    p_ref[...] = lax.dot_general(xn_ref[...], w_ref[...], nt, preferred_element_type=F32)


def _inproj(x, norm_g, w_main, w_in_t, l):
    rows = x.shape[0]
    gate_blk = 4 * W_A // GATE_LANES
    tm = min(rows, INPROJ_TM)
    tn = WBLK
    n_a = 3 * W_A // tn
    wblk = lambda i, j: (l, jnp.where(j < n_a, j, j + (BLK_C0 - n_a)), 0)
    return pl.pallas_call(
        _inproj_kernel,
        grid=(rows // tm, N_P1 // tn),
        in_specs=[
            pl.BlockSpec((tm, D_MODEL), lambda i, j: (i, 0)),
            pl.BlockSpec((None, 1, D_MODEL), lambda i, j: (l, 0, 0)),
            pl.BlockSpec((None, tn, D_MODEL), wblk),
            pl.BlockSpec((None, GATE_LANES, D_MODEL), lambda i, j: (l, gate_blk, 0)),
        ],
        out_specs=[
            pl.BlockSpec((tm, tn), lambda i, j: (i, j)),
            pl.BlockSpec((tm, GATE_LANES), lambda i, j: (i, 0)),
            pl.BlockSpec((None, 8, tm), lambda i, j: (i, 0, 0)),
        ],
        out_shape=[
            jax.ShapeDtypeStruct((rows, N_P1), F32),
            jax.ShapeDtypeStruct((rows, GATE_LANES), F32),
            jax.ShapeDtypeStruct((rows // tm, 8, tm), F32),
        ],
        scratch_shapes=[pltpu.VMEM((tm, D_MODEL), BF16)],
        compiler_params=_cparams(2),
        name="inproj",
    )(x, norm_g, w_main, w_in_t)


def _pack_kernel(a_ref, b_ref, w_ref, *, first_shifted):
    j = pl.program_id(1)

    @pl.when(j < first_shifted)
    def _():
        w_ref[...] = a_ref[...].astype(BF16)

    @pl.when(j >= first_shifted)
    def _():
        w_ref[...] = jnp.concatenate([a_ref[2 * H_A:, :], b_ref[...]], axis=0).astype(BF16)


def _pack_w_in(w_in_t):
    depth = w_in_t.shape[0]
    tn = 2048
    gate_off = 4 * W_A
    assert gate_off % tn == 0 and w_in_t.shape[1] == N_PACK + 2 * H_A and 2 * H_A == 8
    kern = functools.partial(_pack_kernel, first_shifted=gate_off // tn)
    return pl.pallas_call(
        kern,
        grid=(depth, N_PACK // tn),
        in_specs=[
            pl.BlockSpec((None, tn, D_MODEL), lambda l, j: (l, j, 0)),
            pl.BlockSpec((None, 8, D_MODEL), lambda l, j: (l, (j + 1) * (tn // 8), 0)),
        ],
        out_specs=pl.BlockSpec((None, tn, D_MODEL), lambda l, j: (l, j, 0)),
        out_shape=jax.ShapeDtypeStruct((depth, N_PACK, D_MODEL), BF16),
        compiler_params=_cparams(2),
        name="pack_w_in",
    )(w_in_t, w_in_t)


def _conv_silu(ext_ref, cols, w, b, L):
    y = b
    for back in range(CONV_W):
        y = y + ext_ref[8 - back:8 - back + L, cols] * w[CONV_W - 1 - back:CONV_W - back]
    return _silu(y)


def _mlstm_core(g_col, g_row, get_v, ext_s, cw_ref, cb_ref, c_s, n_s, m_s, y_ref, L, t_valid, side_work=None):
    tick = side_work if side_work is not None else (lambda: None)
    neg_inf = jnp.float32(-jnp.inf)
    ig_col, lf_col = g_col, jax.nn.log_sigmoid(g_col)
    ig_row, lf_row = g_row, jax.nn.log_sigmoid(g_row)
    if t_valid < L:
        vc = lax.broadcasted_iota(jnp.int32, (L, GATE_LANES), 0) < t_valid
        vr = lax.broadcasted_iota(jnp.int32, (8, L), 1) < t_valid
        ig_col, lf_col = jnp.where(vc, ig_col, neg_inf), jnp.where(vc, lf_col, 0.0)
        ig_row, lf_row = jnp.where(vr, ig_row, neg_inf), jnp.where(vr, lf_row, 0.0)
    ti = lax.broadcasted_iota(jnp.int32, (L, L), 0)
    si = lax.broadcasted_iota(jnp.int32, (L, L), 1)
    lower = (ti >= si).astype(F32)
    upper = (ti <= si).astype(F32)
    if L <= 16:
        b_col = lower[:, 0:1] * lf_col[0:1, :]
        b_row = lf_row[:, 0:1] * upper[0:1, :]
        for s in range(1, L):
            b_col = b_col + lower[:, s:s + 1] * lf_col[s:s + 1, :]
            b_row = b_row + lf_row[:, s:s + 1] * upper[s:s + 1, :]
    else:
        b_col = jnp.dot(lower, lf_col, preferred_element_type=F32, precision=lax.Precision.HIGHEST)
        b_row = jnp.dot(lf_row, upper, preferred_element_type=F32, precision=lax.Precision.HIGHEST)
    last = t_valid - 1
    band = min(L, MLSTM_BAND)
    nt = (((1,), (1,)), ((), ()))

    for h in range(H_A):
        cs = slice(h * DH_A, (h + 1) * DH_A)
        ks = slice(W_A + h * DH_A, W_A + (h + 1) * DH_A)
        q = _conv_silu(ext_s, cs, cw_ref[:, cs], cb_ref[:, cs], L)
        tick()
        k = _conv_silu(ext_s, ks, cw_ref[:, ks], cb_ref[:, ks], L) * (DH_A ** -0.5)
        tick()
        v = get_v(h)
        q16, k16, v16 = q.astype(BF16), k.astype(BF16), v.astype(BF16)
        bc = b_col[:, H_A + h:H_A + h + 1]
        igc = ig_col[:, h:h + 1]
        br = b_row[H_A + h:H_A + h + 1, :]
        igr = ig_row[h:h + 1, :]
        m_prev = m_s[h][:, 0:1]
        c_prev = c_s[h]
        n_prev = n_s[h]
        c16 = c_prev.astype(BF16)

        for r in range(L // band):
            rs = slice(r * band, (r + 1) * band)
            kw = (r + 1) * band
            ti = lax.broadcasted_iota(jnp.int32, (band, kw), 0) + r * band
            si = lax.broadcasted_iota(jnp.int32, (band, kw), 1)
            logw = jnp.where(ti >= si, bc[rs] - br[:, 0:kw] + igr[:, 0:kw], neg_inf)
            inter = bc[rs] + m_prev
            m_t = jnp.maximum(inter, jnp.max(logw, axis=1, keepdims=True))
            w_intra = jnp.exp(logw - m_t)
            w_inter = jnp.exp(inter - m_t)
            s = w_intra * lax.dot_general(q16[rs], k16[0:kw], nt, preferred_element_type=F32)
            num = (jnp.dot(s.astype(BF16), v16[0:kw], preferred_element_type=F32)
                   + w_inter * lax.dot_general(q16[rs], c16, nt, preferred_element_type=F32))
            den = (jnp.sum(s, axis=1, keepdims=True)
                   + w_inter * jnp.sum(q[rs] * n_prev, axis=1, keepdims=True))
            y_ref[0, rs, cs] = num / jnp.maximum(jnp.abs(den), jnp.exp(-m_t))
            if r == last // band:
                m_new = m_t[last - r * band:last - r * band + 1, :]
            tick()

        b_last = bc[last:last + 1, :]
        decay = jnp.exp(b_last + m_prev - m_new)
        w_s = jnp.exp(b_last - bc + igc - m_new)
        c_s[h] = decay * c_prev + lax.dot_general((w_s * v).astype(BF16), k16, (((0,), (0,)), ((), ())),
                                                  preferred_element_type=F32)
        n_s[h] = decay * n_prev + jnp.sum(w_s * k, axis=0, keepdims=True)
        m_s[h] = jnp.broadcast_to(m_new, (1, GATE_LANES))
        tick()

    ext_s[0:8, :] = ext_s[t_valid:t_valid + 8, :]


def _mlstm_state_io(c0_ref, n0_ref, m0_ref, tail0_ref, c_s, n_s, m_s, ext_s):
    @pl.when(pl.program_id(1) == 0)
    def _():
        c_s[...] = c0_ref[0]
        n_s[...] = n0_ref[0]
        m_s[...] = m0_ref[0]
        ext_s[0:8, :] = tail0_ref[0]


def _mlstm_state_out(c_out, n_out, m_out, tail_out, c_s, n_s, m_s, ext_s, cprev_ref=None):
    @pl.when(pl.program_id(1) == pl.num_programs(1) - 1)
    def _():
        n_prev = c_out.shape[0] - 1
        if n_prev:
            c_out[0:n_prev] = cprev_ref[...]
        c_out[n_prev, 0] = c_s[...]
        n_out[0] = n_s[...]
        m_out[0] = m_s[...]
        tail_out[0] = ext_s[0:8, :]


def _mlstm_kernel(p_ref, gc_ref, gr_ref, brow_ref, bcol_ref, cw_ref, cb_ref, c0_ref, n0_ref, m0_ref, tail0_ref,
                  *rest, L, t_valid, n_prev):
    cprev_ref = rest[0] if n_prev else None
    y_ref, c_out, n_out, m_out, tail_out, c_s, n_s, m_s, ext_s = rest[1:] if n_prev else rest
    _mlstm_state_io(c0_ref, n0_ref, m0_ref, tail0_ref, c_s, n_s, m_s, ext_s)
    ext_s[8:8 + L, :] = p_ref[0, :, 0:2 * W_A]
    get_v = lambda h: p_ref[0, :, 2 * W_A + h * DH_A:2 * W_A + (h + 1) * DH_A]
    _mlstm_core(gc_ref[0] + brow_ref[...], gr_ref[0] + bcol_ref[...], get_v, ext_s, cw_ref, cb_ref,
                c_s, n_s, m_s, y_ref, L, t_valid)
    _mlstm_state_out(c_out, n_out, m_out, tail_out, c_s, n_s, m_s, ext_s, cprev_ref)


def _seq_a_kernel(x_ref, cos_ref, sin_ref, ng_ref, w_aq, w_ak, w_av, w_c0, w_c1, w_c2, wg_ref, brow_ref, bcol_ref,
                  cw_ref, cb_ref, c0_ref, n0_ref, m0_ref, tail0_ref, y_ref, pc_ref, c_out, n_out, m_out, tail_out,
                  c_s, n_s, m_s, ext_s, v_s, *, L):
    _mlstm_state_io(c0_ref, n0_ref, m0_ref, tail0_ref, c_s, n_s, m_s, ext_s)
    nt = (((1,), (1,)), ((), ()))
    x = x_ref[0]
    r = lax.rsqrt(jnp.mean(x * x, axis=-1, keepdims=True) + EPS)
    xn = (x * r * ng_ref[...]).astype(BF16)
    wg = wg_ref[...].astype(BF16)
    g_col = lax.dot_general(xn, wg, nt, preferred_element_type=F32) + brow_ref[...]
    g_row = lax.dot_general(wg[0:8, :], xn, nt, preferred_element_type=F32) + bcol_ref[...]
    proj = lambda w_ref: lax.dot_general(xn, w_ref[...], nt, preferred_element_type=F32)
    ext_s[8:8 + L, 0:W_A] = proj(w_aq)
    ext_s[8:8 + L, W_A:2 * W_A] = proj(w_ak)
    v_s[...] = proj(w_av)
    piece = 256
    todo = [(w_ref, j, lo) for j, w_ref in enumerate((w_c0, w_c1, w_c2)) for lo in range(0, WBLK, piece)]

    assert piece == WG_C

    def emit_piece():
        w_ref, j, lo = todo.pop(0)
        col = j * WBLK + lo
        y = lax.dot_general(xn, w_ref[lo:lo + piece, :], nt, preferred_element_type=F32)
        if col < 2 * W_C:
            cos = jnp.concatenate([cos_ref[0], cos_ref[0]], axis=1)
            sin = jnp.concatenate([sin_ref[0], sin_ref[0]], axis=1)
            y = _rope(y, cos, sin)
            if col < W_C:
                y = y * (DH_C ** -0.5)
        pc_ref[0, :, col:col + piece] = y

    side_work = lambda: emit_piece() if todo else None
    get_v = lambda h: v_s[:, h * DH_A:(h + 1) * DH_A]
    _mlstm_core(g_col, g_row, get_v, ext_s, cw_ref, cb_ref, c_s, n_s, m_s, y_ref, L, L, side_work)
    while todo:
        emit_piece()
    _mlstm_state_out(c_out, n_out, m_out, tail_out, c_s, n_s, m_s, ext_s)


def _seq_a(x, cos_t, sin_t, norm_g, w_main, w_in_t, bias_row, bias_col, conv_w, conv_b, c0, n0, m0, tail0, l, *, L):
    rows = x.shape[0]
    nch = rows // L
    x3 = x.reshape(nch, L, D_MODEL)
    gate_blk = 4 * W_A // GATE_LANES
    once = pl.Buffered(1)
    chunk = lambda b, c: (c, 0, 0)
    layer3 = lambda b, c: (l, 0, 0)
    wblk = lambda blk: pl.BlockSpec((None, WBLK, D_MODEL), lambda b, c, blk=blk: (l, blk, 0), pipeline_mode=once)
    st5 = lambda b, c: (0, 0, 0, 0, 0)
    st4 = lambda b, c: (0, 0, 0, 0)
    first4 = lambda b, c: (0, 0, 0, 0)
    first3 = lambda b, c: (0, 0, 0)
    y, pc, c1, n1, m1, tail1 = pl.pallas_call(
        functools.partial(_seq_a_kernel, L=L),
        grid=(1, nch),
        in_specs=[
            pl.BlockSpec((1, L, D_MODEL), chunk),
            pl.BlockSpec((1, L, 2 * DH_C), chunk),
            pl.BlockSpec((1, L, 2 * DH_C), chunk),
            pl.BlockSpec((None, 1, D_MODEL), layer3, pipeline_mode=once),
        ] + [wblk(b) for b in (0, 1, 2, BLK_C0, BLK_C0 + 1, BLK_C0 + 2)] + [
            pl.BlockSpec((None, GATE_LANES, D_MODEL), lambda b, c: (l, gate_blk, 0), pipeline_mode=once),
            pl.BlockSpec((None, 1, GATE_LANES), layer3, pipeline_mode=once),
            pl.BlockSpec((None, 8, 1), layer3, pipeline_mode=once),
            pl.BlockSpec((None, CONV_W, 2 * W_A), layer3, pipeline_mode=once),
            pl.BlockSpec((None, 1, 2 * W_A), layer3, pipeline_mode=once),
            pl.BlockSpec((None, 1, H_A, DH_A, DH_A), st5, pipeline_mode=once),
            pl.BlockSpec((None, 1, H_A, 1, DH_A), st5, pipeline_mode=once),
            pl.BlockSpec((None, 1, H_A, 1, GATE_LANES), st5, pipeline_mode=once),
            pl.BlockSpec((None, 1, 8, 2 * W_A), st4, pipeline_mode=once),
        ],
        out_specs=[
            pl.BlockSpec((1, L, W_A), chunk),
            pl.BlockSpec((1, L, 3 * WBLK), chunk),
            pl.BlockSpec((1, 1, H_A, DH_A, DH_A), lambda b, c: (0, 0, 0, 0, 0)),
            pl.BlockSpec((1, H_A, 1, DH_A), first4),
            pl.BlockSpec((1, H_A, 1, GATE_LANES), first4),
            pl.BlockSpec((1, 8, 2 * W_A), first3),
        ],
        out_shape=[
            jax.ShapeDtypeStruct((nch, L, W_A), F32),
            jax.ShapeDtypeStruct((nch, L, 3 * WBLK), F32),
            jax.ShapeDtypeStruct((1, 1, H_A, DH_A, DH_A), F32),
            jax.ShapeDtypeStruct((1, H_A, 1, DH_A), F32),
            jax.ShapeDtypeStruct((1, H_A, 1, GATE_LANES), F32),
            jax.ShapeDtypeStruct((1, 8, 2 * W_A), F32),
        ],
        scratch_shapes=[
            pltpu.VMEM((H_A, DH_A, DH_A), F32),
            pltpu.VMEM((H_A, 1, DH_A), F32),
            pltpu.VMEM((H_A, 1, GATE_LANES), F32),
            pltpu.VMEM((L + 8, 2 * W_A), F32),
            pltpu.VMEM((L, W_A), F32),
        ],
        compiler_params=_cparams(2),
        name="seq_a",
    )(x3, cos_t.reshape(nch, L, 2 * DH_C), sin_t.reshape(nch, L, 2 * DH_C), norm_g, *([w_main] * 6), w_in_t,
      bias_row, bias_col, conv_w, conv_b, c0, n0, m0, tail0)
    return (y.reshape(rows, W_A), pc.reshape(rows, 3 * WBLK), c1[0], n1[:, :, 0, :], m1[:, :, 0, 0],
            tail1[:, 8 - (CONV_W - 1):, :])


def _mlstm(p, gcol, grow, bias_row, bias_col, conv_w, conv_b, c0, n0, m0, tail0, l, ls, c_prev, *, L, t_valid):
    batch = c0.shape[1]
    rows = p.shape[0]
    nch = rows // (batch * L)
    n_prev = 0 if c_prev is None else c_prev.shape[0]
    p3 = p.reshape(batch * nch, L, N_P1)
    gc3 = gcol.reshape(batch * nch, L, GATE_LANES)
    tm = grow.shape[2]
    if L % GATE_LANES == 0:
        per = tm // L
        gr3, gr_index = grow, (lambda b, c: ((b * nch + c) // per, 0, (b * nch + c) % per))
    else:
        gr3 = grow.transpose(1, 0, 2).reshape(8, batch * nch, L).transpose(1, 0, 2)
        gr_index = lambda b, c: (b * nch + c, 0, 0)
    kern = functools.partial(_mlstm_kernel, L=L, t_valid=t_valid, n_prev=n_prev)
    chunk = lambda b, c: (b * nch + c, 0, 0)
    layer3 = lambda b, c: (l, 0, 0)
    stack5 = lambda b, c: (0, b, 0, 0, 0)
    per_b4 = lambda b, c: (b, 0, 0, 0)
    per_b3 = lambda b, c: (b, 0, 0)
    st5 = lambda b, c: (ls, b, 0, 0, 0)
    st4 = lambda b, c: (ls, b, 0, 0)
    y, c1, n1, m1, tail1 = pl.pallas_call(
        kern,
        grid=(batch, nch),
        in_specs=[
            pl.BlockSpec((1, L, 3 * W_A), chunk),
            pl.BlockSpec((1, L, GATE_LANES), chunk),
            pl.BlockSpec((1, 8, L), gr_index),
            pl.BlockSpec((None, 1, GATE_LANES), layer3),
            pl.BlockSpec((None, 8, 1), layer3),
            pl.BlockSpec((None, CONV_W, 2 * W_A), layer3),
            pl.BlockSpec((None, 1, 2 * W_A), layer3),
            pl.BlockSpec((None, 1, H_A, DH_A, DH_A), st5),
            pl.BlockSpec((None, 1, H_A, 1, DH_A), st5),
            pl.BlockSpec((None, 1, H_A, 1, GATE_LANES), st5),
            pl.BlockSpec((None, 1, 8, 2 * W_A), st4),
        ] + ([pl.BlockSpec((n_prev, 1, H_A, DH_A, DH_A), stack5)] if n_prev else []),
        out_specs=[
            pl.BlockSpec((1, L, W_A), chunk),
            pl.BlockSpec((n_prev + 1, 1, H_A, DH_A, DH_A), stack5),
            pl.BlockSpec((1, H_A, 1, DH_A), per_b4),
            pl.BlockSpec((1, H_A, 1, GATE_LANES), per_b4),
            pl.BlockSpec((1, 8, 2 * W_A), per_b3),
        ],
        out_shape=[
            jax.ShapeDtypeStruct((batch * nch, L, W_A), F32),
            jax.ShapeDtypeStruct((n_prev + 1, batch, H_A, DH_A, DH_A), F32),
            jax.ShapeDtypeStruct((batch, H_A, 1, DH_A), F32),
            jax.ShapeDtypeStruct((batch, H_A, 1, GATE_LANES), F32),
            jax.ShapeDtypeStruct((batch, 8, 2 * W_A), F32),
        ],
        scratch_shapes=[
            pltpu.VMEM((H_A, DH_A, DH_A), F32),
            pltpu.VMEM((H_A, 1, DH_A), F32),
            pltpu.VMEM((H_A, 1, GATE_LANES), F32),
            pltpu.VMEM((L + 8, 2 * W_A), F32),
        ],
        compiler_params=_cparams(2),
        name="mlstm",
    )(p3, gc3, gr3, bias_row, bias_col, conv_w, conv_b, c0, n0, m0, tail0, *([c_prev] if n_prev else []))
    return y.reshape(rows, W_A), c1, n1[:, :, 0, :], m1[:, :, 0, 0], tail1[:, 8 - (CONV_W - 1):, :]


def _rope(x, cos, sin):
    width = x.shape[1]
    lane = lax.broadcasted_iota(jnp.int32, (1, width), 1) % DH_C
    half = ROT_DIM // 2
    partner = jnp.where(lane < half, pltpu.roll(x, width - half, 1), pltpu.roll(x, half, 1))
    return x * cos + partner * sin


def _dil_attn_kernel(q_ref, k_ref, v_ref, o_ref, mx_ref, den_ref, qs, kbuf, vbuf, bias, *, d):
    i = pl.program_id(0)
    T = ATT_TILE
    HL = WG_C // 2
    nblk = T // SPAN_C
    nres = T // (SPAN_C * d)
    cur = (i % 2) * T
    prev = T - cur
    neg_inf = jnp.float32(-jnp.inf)

    def split(x):
        return x[:, 0:HL], x[:, HL:WG_C]

    def put(buf, lo, x):
        a, b = split(x)
        buf[0, pl.ds(lo, T), :] = a
        buf[1, pl.ds(lo, T), :] = b

    @pl.when(i == 0)
    def _():
        ii = lax.broadcasted_iota(jnp.int32, (SPAN_C, 2 * SPAN_C), 0)
        jj = lax.broadcasted_iota(jnp.int32, (SPAN_C, 2 * SPAN_C), 1)
        band = (jj >= ii) & (jj <= ii + SPAN_C)
        bias[1] = jnp.where(band, 0.0, neg_inf)
        bias[0] = jnp.where(band & (jj >= SPAN_C), 0.0, neg_inf)
        kbuf[:, T:2 * T, :] = jnp.zeros((2, T, HL), F32)
        vbuf[:, T:2 * T, :] = jnp.zeros((2, T, HL), F32)

    put(qs, 0, q_ref[...])
    put(kbuf, cur, k_ref[...])
    put(vbuf, cur, v_ref[...])

    lane_head = lax.broadcasted_iota(jnp.int32, (SPAN_C, WG_C), 1) // DH_C
    hm = [(lane_head == h).astype(F32) for h in range(HG_C)]

    def rows(start, size):
        return pl.ds(start, size) if d == 1 else pl.ds(start, size, stride=d)

    def take(buf, start, size):
        return jnp.concatenate([buf[0, rows(start, size), :], buf[1, rows(start, size), :]], axis=1)

    def pick(per_head):
        out = jnp.broadcast_to(per_head[0], (SPAN_C, WG_C))
        for h in range(1, HG_C):
            out = jnp.where(lane_head >= h, per_head[h], out)
        return out

    def body(blk, carry):
        r = blk % d
        c = blk // d
        qstart = r + SPAN_C * d * c
        in_tile = cur + qstart
        before = jnp.where(c > 0, in_tile - SPAN_C * d, prev + r + SPAN_C * d * (nres - 1))
        qb = take(qs, qstart, SPAN_C)
        kb = jnp.concatenate([take(kbuf, before, SPAN_C), take(kbuf, in_tile, SPAN_C)], axis=0)
        vb = jnp.concatenate([take(vbuf, before, SPAN_C), take(vbuf, in_tile, SPAN_C)], axis=0)
        has_prev = jnp.logical_or(i > 0, c > 0).astype(jnp.int32)
        kb16, vb16 = kb.astype(BF16), vb.astype(BF16)
        pvs, mxs, dens = [], [], []
        for h0 in range(0, HG_C, 2):
            q2 = jnp.concatenate([qb * hm[h0], qb * hm[h0 + 1]], axis=0)
            s = _dot_nt(q2, kb16).reshape(2, SPAN_C, 2 * SPAN_C) + bias[has_prev]
            mx = jnp.max(s, axis=2, keepdims=True)
            p = jnp.exp(s - mx)
            den = jnp.sum(p, axis=2, keepdims=True)
            pv = _dot(p.reshape(2 * SPAN_C, 2 * SPAN_C), vb16).reshape(2, SPAN_C, WG_C)
            pvs += [pv[0], pv[1]]
            mxs += [mx[0], mx[1]]
            dens += [den[0], den[1]]
        outs = (pick(pvs), pick(mxs), pick(dens))
        for ref, val in zip((o_ref, mx_ref, den_ref), outs):
            for half, part in enumerate(split(val)):
                ref[half, rows(qstart, SPAN_C), :] = part
        return carry

    lax.fori_loop(0, nblk, body, 0, unroll=8)


def _dil_attn(p, base, g, d):
    rows = p.shape[0]
    T = ATT_TILE
    kern = functools.partial(_dil_attn_kernel, d=d)
    col = lambda off: pl.BlockSpec((T, WG_C), lambda i, off=off: (i, (base + off - OFF_CQ) // WG_C + g))
    slab = pl.BlockSpec((2, T, WG_C // 2), lambda i: (0, i, 0))
    slab_shape = jax.ShapeDtypeStruct((2, rows, WG_C // 2), F32)
    return pl.pallas_call(
        kern,
        grid=(rows // T,),
        in_specs=[col(OFF_CQ), col(OFF_CK), col(OFF_CV)],
        out_specs=[slab, slab, slab],
        out_shape=[slab_shape, slab_shape, slab_shape],
        scratch_shapes=[pltpu.VMEM((2, T, WG_C // 2), F32), pltpu.VMEM((2, 2 * T, WG_C // 2), F32),
                        pltpu.VMEM((2, 2 * T, WG_C // 2), F32), pltpu.VMEM((2, SPAN_C, 2 * SPAN_C), F32)],
        compiler_params=_cparams(1),
        name="dil_attn_d%d" % d,
    )(p, p, p)


def _sample_attn_kernel(pc0, pc1, pc2, cos_ref, sin_ref, c0_ref, c1_ref, c2_ref,
                        o0, o1, o2, m0, m1, m2, d0, d1, d2, kr_ref, *, n_tok, n_seq):
    reps = W_C // (2 * DH_C)
    cos = jnp.concatenate([cos_ref[...]] * reps, axis=1)
    sin = jnp.concatenate([sin_ref[...]] * reps, axis=1)
    nrow = HG_C * SAMPLE_ROWS
    rowh = lax.broadcasted_iota(jnp.int32, (nrow, WG_C), 0) // SAMPLE_ROWS
    laneh = lax.broadcasted_iota(jnp.int32, (nrow, WG_C), 1) // DH_C
    hm = (rowh == laneh).astype(F32)
    tok_col = lax.broadcasted_iota(jnp.int32, (nrow, 1), 0) % SAMPLE_ROWS
    out_row = lax.broadcasted_iota(jnp.int32, (SAMPLE_ROWS, WG_C), 0)
    neg_inf = jnp.float32(-jnp.inf)
    crefs = (c0_ref, c1_ref, c2_ref)
    orefs = (o0, o1, o2)
    mrefs = (m0, m1, m2)
    drefs = (d0, d1, d2)
    for e in range(n_seq):
        rows_e = slice(e * SAMPLE_ROWS, (e + 1) * SAMPLE_ROWS)
        pc = jnp.concatenate([pc0[e], pc1[e], pc2[e]], axis=1)
        qr = _rope(pc[:, 0:W_C], cos, sin) * (DH_C ** -0.5)
        kr = _rope(pc[:, W_C:2 * W_C], cos, sin)
        v = pc[:, 2 * W_C:3 * W_C]
        kr_ref[e] = kr
        for g, (_, d) in enumerate(DIL_PATTERNS):
            gs = slice(g * WG_C, (g + 1) * WG_C)
            n_buf = SPAN_C * d
            kn = kr[:, gs]
            vn = v[:, gs]
            qm = jnp.concatenate([qr[:, gs]] * HG_C, axis=0) * hm
            s = _dot(qm, crefs[g][e, 0])
            tok = lax.broadcasted_iota(jnp.int32, (nrow, n_buf), 0) % SAMPLE_ROWS
            pos = lax.broadcasted_iota(jnp.int32, (nrow, n_buf), 1)
            attends = (pos >= tok) if d == 1 else ((pos & (d - 1)) == tok)
            s = jnp.where(attends | (tok >= n_tok), s, neg_inf)
            new_ok = [(u <= tok_col) if d == 1 else (u == tok_col) for u in range(n_tok)]
            s_new = [jnp.where(ok, jnp.sum(qm * kn[u:u + 1, :], axis=1, keepdims=True), neg_inf)
                     for u, ok in enumerate(new_ok)]
            mx = jnp.max(s, axis=1, keepdims=True)
            for sn in s_new:
                mx = jnp.maximum(mx, sn)
            p = jnp.exp(s - mx)
            den = jnp.sum(p, axis=1, keepdims=True)
            acc = _dot_nt(p, crefs[g][e, 1])
            for u, sn in enumerate(s_new):
                pn = jnp.exp(sn - mx)
                den = den + pn
                acc = acc + pn * vn[u:u + 1, :]
            stacked = (acc * hm, mx * hm, den * hm)
            for ref, val, pad in zip((orefs[g], mrefs[g], drefs[g]), stacked, (0.0, 0.0, 1.0)):
                v8 = val[0:SAMPLE_ROWS]
                for h in range(1, HG_C):
                    v8 = v8 + val[h * SAMPLE_ROWS:(h + 1) * SAMPLE_ROWS]
                v8 = jnp.where(out_row < n_tok, v8, pad)
                for half in range(2):
                    ref[half, rows_e, :] = v8[:, half * (WG_C // 2):(half + 1) * (WG_C // 2)]


def _cache_views(caches):
    views = []
    for (win, d), c in zip(DIL_PATTERNS, caches):
        depth, batch, n_buf = c.shape[:3]
        assert n_buf == SPAN_C * d
        views.append(jnp.transpose(c, (0, 1, 3, 4, 5, 2)).reshape(depth, batch, 2, WG_C, n_buf))
    return views


def _sample_attn(p, cos_s, sin_s, views, l, n_tok):
    batch = p.shape[0] // SAMPLE_ROWS
    n_seq = SAMPLE_ATT_SEQS if batch % SAMPLE_ATT_SEQS == 0 else 1
    p3 = p.reshape(batch, SAMPLE_ROWS, N_P1)
    pcol = lambda k: pl.BlockSpec((n_seq, SAMPLE_ROWS, 1024), lambda b, k=k: (b, 0, OFF_CQ // 1024 + k))
    tab = pl.BlockSpec((SAMPLE_ROWS, 2 * DH_C), lambda b: (0, 0))
    cache_spec = lambda d: pl.BlockSpec((None, n_seq, 2, WG_C, SPAN_C * d), lambda b: (l, b, 0, 0, 0))
    out = pl.BlockSpec((2, n_seq * SAMPLE_ROWS, WG_C // 2), lambda b: (0, b, 0))
    res = pl.pallas_call(
        functools.partial(_sample_attn_kernel, n_tok=n_tok, n_seq=n_seq),
        grid=(batch // n_seq,),
        in_specs=[pcol(0), pcol(1), pcol(2), tab, tab] + [cache_spec(d) for _, d in DIL_PATTERNS],
        out_specs=[out] * 9 + [pl.BlockSpec((n_seq, SAMPLE_ROWS, W_C), lambda b: (b, 0, 0))],
        out_shape=[jax.ShapeDtypeStruct((2, batch * SAMPLE_ROWS, WG_C // 2), F32)] * 9
        + [jax.ShapeDtypeStruct((batch, SAMPLE_ROWS, W_C), F32)],
        compiler_params=_cparams(1),
        name="sample_attn",
    )(p3, p3, p3, cos_s, sin_s, *views)
    return list(res[:9]), res[9].reshape(batch * SAMPLE_ROWS, W_C)


def _post_kernel(x_ref, h_ref, o0, o1, o2, m0, m1, m2, d0, d1, d2, cz_ref, ng_ref, w_ao, w_az, w_bu, w_bv, w_bz, w_ga, w_gb,
                 w_gc, lg_ref, lb_ref, wsp_ref, bsp_ref, wb_ref, wo_ref, fg_ref, out_ref, *rest, nchunks, final,
                 emit_vn):
    if emit_vn:
        vn_out, vn_s, yb_s = rest
    else:
        vn_s, yb_s = rest
    nt = (((1,), (1,)), ((), ()))
    x = x_ref[...]
    r = lax.rsqrt(jnp.mean(x * x, axis=-1, keepdims=True) + EPS)
    xn = (x * r * ng_ref[...]).astype(BF16)
    proj = lambda w_ref: lax.dot_general(xn, w_ref[...], nt, preferred_element_type=F32)
    branch = lambda y, lo, hi: jnp.dot(y.astype(BF16), wb_ref[lo:hi, :], preferred_element_type=F32)

    ya = h_ref[...] * _sigmoid(proj(w_ao)) * _silu(proj(w_az))
    acc = _sigmoid(proj(w_ga)) * branch(ya, 0, W_A)

    v = proj(w_bv)
    mu = jnp.mean(v, axis=-1, keepdims=True)
    var = jnp.mean(jnp.square(v - mu), axis=-1, keepdims=True)
    vn = (v - mu) * lax.rsqrt(var + EPS) * lg_ref[...] + lb_ref[...]
    if emit_vn:
        vn_out[...] = vn
    vn_s[...] = vn.astype(BF16)
    gate_b = proj(w_bu) * _silu(proj(w_bz))
    tri = (lax.broadcasted_iota(jnp.int32, (CHUNK_B, CHUNK_B), 0)
           >= lax.broadcasted_iota(jnp.int32, (CHUNK_B, CHUNK_B), 1))
    for g in range(G_B):
        wg = jnp.where(tri, wsp_ref[g], 0.0).astype(BF16)
        cs = slice(g * DG_B, (g + 1) * DG_B)
        for c in range(nchunks):
            rs = slice(c * CHUNK_B, (c + 1) * CHUNK_B)
            yb_s[rs, cs] = jnp.dot(wg, vn_s[rs, cs], preferred_element_type=F32) + bsp_ref[g]
    acc = acc + _sigmoid(proj(w_gb)) * branch(gate_b * yb_s[...], W_A, W_A + W_B)

    unslab = lambda ref: jnp.concatenate([ref[0], ref[1]], axis=1)
    ms = [unslab(m0), unslab(m1), unslab(m2)]
    mx = jnp.maximum(jnp.maximum(ms[0], ms[1]), ms[2])
    es = [jnp.exp(m - mx) for m in ms]
    inv_tot = 1.0 / (es[0] * unslab(d0) + es[1] * unslab(d1) + es[2] * unslab(d2))
    pc = None
    for g, o_ref in enumerate((o0, o1, o2)):
        gs = slice(g * WG_C, (g + 1) * WG_C)
        yc = unslab(o_ref) * (es[g] * inv_tot) * _silu(cz_ref[:, gs])
        t = branch(yc, W_A + W_B + g * WG_C, W_A + W_B + (g + 1) * WG_C)
        pc = t if pc is None else pc + t
    acc = acc + _sigmoid(proj(w_gc)) * pc

    out = x + jnp.dot(acc.astype(BF16), wo_ref[...], preferred_element_type=F32)
    if final:
        rr = lax.rsqrt(jnp.mean(out * out, axis=-1, keepdims=True) + EPS)
        out = out * rr * fg_ref[...]
    out_ref[...] = out


def _post(x, p1, base, h, att, norm_g, w_main, ln_g, ln_b, w_sp, b_sp_col, w_branch, w_out, final_g, l, final,
          emit_vn):
    rows = x.shape[0]
    tm = min(rows, POST_TM)
    once = pl.Buffered(1)
    row = lambda w: pl.BlockSpec((tm, w), lambda i: (i, 0))
    slab = pl.BlockSpec((2, tm, WG_C // 2), lambda i: (0, i, 0))
    wblk = lambda b: pl.BlockSpec((None, WBLK, D_MODEL), lambda i, b=b: (l, b, 0), pipeline_mode=once)
    vec = lambda w: pl.BlockSpec((None, 1, w), lambda i: (l, 0, 0), pipeline_mode=once)
    in_specs = (
        [row(D_MODEL), row(W_A)] + [slab] * 9
        + [pl.BlockSpec((tm, W_C), lambda i: (i, (base + OFF_CZ - OFF_CQ) // W_C)), vec(D_MODEL)]
        + [wblk(b) for b in (BLK_AO, BLK_AZ, BLK_BU, BLK_BV, BLK_BZ, BLK_GA, BLK_GB, BLK_GC)]
        + [vec(W_B), vec(W_B),
           pl.BlockSpec((None, G_B, CHUNK_B, CHUNK_B), lambda i: (l, 0, 0, 0), pipeline_mode=once),
           pl.BlockSpec((None, G_B, CHUNK_B, 1), lambda i: (l, 0, 0, 0), pipeline_mode=once),
           pl.BlockSpec((None, W_A + W_B + W_C, D_MODEL), lambda i: (l, 0, 0), pipeline_mode=once),
           pl.BlockSpec((None, D_MODEL, D_MODEL), lambda i: (l, 0, 0), pipeline_mode=once),
           pl.BlockSpec((1, D_MODEL), lambda i: (0, 0), pipeline_mode=once)])
    out_specs = [row(D_MODEL)]
    out_shape = [jax.ShapeDtypeStruct((rows, D_MODEL), F32)]
    if emit_vn:
        out_specs.append(row(W_B))
        out_shape.append(jax.ShapeDtypeStruct((rows, W_B), F32))
    res = pl.pallas_call(
        functools.partial(_post_kernel, nchunks=tm // CHUNK_B, final=final, emit_vn=emit_vn),
        grid=(rows // tm,),
        in_specs=in_specs,
        out_specs=out_specs,
        out_shape=out_shape,
        scratch_shapes=[pltpu.VMEM((tm, W_B), BF16), pltpu.VMEM((tm, W_B), F32)],
        compiler_params=_cparams(1),
        name="post",
    )(x, h, *att, p1, norm_g, *([w_main] * 8), ln_g, ln_b, w_sp, b_sp_col, w_branch, w_out, final_g)
    return (res[0], res[1]) if emit_vn else (res[0], None)


def _rope_tables(pos):
    half = ROT_DIM // 2
    inv = ROPE_THETA ** (-jnp.arange(half, dtype=F32) / half)
    ang = inv[:, None] * pos.astype(F32)[None, :]
    lane = jnp.arange(2 * DH_C) % DH_C
    freq = jnp.arange(half)[:, None] == (lane % half)[None, :]
    rot = (lane < ROT_DIM)[None, :]
    sel_cos = (freq & rot).astype(F32)
    sel_sin = sel_cos * jnp.where(lane < half, -1.0, 1.0)[None, :]
    spread = lambda t, sel: lax.dot_general(t, sel, (((0,), (0,)), ((), ())), precision=lax.Precision.HIGHEST)
    return spread(jnp.cos(ang), sel_cos) + (~rot).astype(F32), spread(jnp.sin(ang), sel_sin)


def _stack_kv(k_rows, v_rows):
    b, t, _ = k_rows.shape
    return jnp.stack([k_rows.reshape(b, t, HG_C, DH_C), v_rows.reshape(b, t, HG_C, DH_C)], axis=2)


def kernel(x_prompt, x_sample, state_C, state_n, state_m, state_conv, cache_kv_w128, cache_kv_w512, cache_kv_w2048,
           norm_g, w_in, b_igate, b_fgate, conv_w, conv_b, ln_v_g, ln_v_b, w_spatial, b_spatial, w_branch, w_out,
           final_norm_g):
    depth = w_in.shape[0]
    bp, seq, _ = x_prompt.shape
    bs, n_tok, _ = x_sample.shape
    assert bp == 1 and seq % ATT_TILE == 0 and n_tok <= SAMPLE_ROWS // 2 and n_tok >= CONV_W - 1
    caches = (cache_kv_w128, cache_kv_w512, cache_kv_w2048)
    pad_tok = SAMPLE_ROWS - n_tok
    rows_s = bs * SAMPLE_ROWS

    w_in_t = jnp.swapaxes(w_in, 1, 2)
    w_main = _pack_w_in(w_in_t)
    gate_bias = jnp.concatenate([b_igate, b_fgate], axis=1)
    bias_row = jnp.pad(gate_bias, ((0, 0), (0, GATE_LANES - 2 * H_A)))[:, None, :]
    bias_col = gate_bias[:, :, None]
    wb16 = w_branch.astype(BF16)
    wo16 = w_out.astype(BF16)
    b_sp_col = b_spatial[..., None]
    norm_g3, conv_b3 = norm_g[:, None, :], conv_b[:, None, :]
    ln_g3, ln_b3 = ln_v_g[:, None, :], ln_v_b[:, None, :]
    seqs_per_chunk = CHUNK_B // SAMPLE_ROWS
    w8 = jnp.pad(w_spatial[:, :, :n_tok, :n_tok], ((0, 0), (0, 0), (0, pad_tok), (0, pad_tok)))
    w_sp_s = jnp.einsum('ab,lgij->lgaibj', jnp.eye(seqs_per_chunk, dtype=F32), w8).reshape(
        depth, G_B, CHUNK_B, CHUNK_B)
    b_sp_s = jnp.tile(jnp.pad(b_spatial[:, :, :n_tok], ((0, 0), (0, 0), (0, pad_tok))),
                      (1, 1, seqs_per_chunk))[..., None]

    cos_p, sin_p = _rope_tables(jnp.arange(seq))
    cos_s, sin_s = _rope_tables(PAST_LEN + jnp.arange(SAMPLE_ROWS))

    hp = x_prompt.reshape(seq, D_MODEL)
    hs = jnp.pad(x_sample, ((0, 0), (0, pad_tok), (0, 0))).reshape(rows_s, D_MODEL)
    fg = final_norm_g[None, :]
    zeros_c = jnp.zeros((1, 1, H_A, DH_A, DH_A), F32)
    zeros_n = jnp.zeros((1, 1, H_A, 1, DH_A), F32)
    zeros_m = jnp.zeros((1, 1, H_A, 1, GATE_LANES), F32)
    zeros_tail = jnp.zeros((1, 1, 8, 2 * W_A), F32)
    n0_s = state_n[:, :, :, None, :]
    m0_s = jnp.broadcast_to(state_m[:, :, :, None, None], (depth, bs, H_A, 1, GATE_LANES))
    tail0_s = jnp.pad(state_conv, ((0, 0), (0, 0), (8 - (CONV_W - 1), 0), (0, 0)))
    views = _cache_views(caches)

    c_stack_s = None
    p_out = {k: [] for k in ('C', 'n', 'm', 'conv', 'kv0', 'kv1', 'kv2')}
    s_out = {k: [] for k in ('n', 'm', 'conv', 'chunk_v', 'kv0', 'kv1', 'kv2')}
    for l in range(depth):
        final = l == depth - 1

        ya, pp, c1, n1, m1, conv1 = _seq_a(hp, cos_p, sin_p, norm_g3, w_main, w_in_t, bias_row, bias_col, conv_w, conv_b3,
                                           zeros_c, zeros_n, zeros_m, zeros_tail, l, L=MLSTM_L)
        att = [[], [], []]
        for g, (win, d) in enumerate(DIL_PATTERNS):
            for dst, val in zip(att, _dil_attn(pp, 0, g, d)):
                dst.append(val)
            keep = min(win, seq)
            k_g = pp[seq - keep:, OFF_CK - OFF_CQ + g * WG_C:OFF_CK - OFF_CQ + (g + 1) * WG_C]
            v_g = pp[seq - keep:, OFF_CV - OFF_CQ + g * WG_C:OFF_CV - OFF_CQ + (g + 1) * WG_C]
            p_out['kv%d' % g].append(_stack_kv(k_g[None], v_g[None]))
        hp, _ = _post(hp, pp, 0, ya, att[0] + att[1] + att[2], norm_g3, w_main, ln_g3, ln_b3, w_spatial, b_sp_col, wb16, wo16, fg, l,
                      final, False)
        p_out['C'].append(c1)
        p_out['n'].append(n1)
        p_out['m'].append(m1)
        p_out['conv'].append(conv1)

        ps, gcol, grow = _inproj(hs, norm_g3, w_main, w_in_t, l)
        ya, c1, n1, m1, conv1 = _mlstm(ps, gcol, grow, bias_row, bias_col, conv_w, conv_b3,
                                       state_C, n0_s, m0_s, tail0_s, l, l, c_stack_s, L=SAMPLE_ROWS, t_valid=n_tok)
        c_stack_s = c1
        att_s, kr = _sample_attn(ps, cos_s, sin_s, views, l, n_tok)
        hs, vn = _post(hs, ps, OFF_CQ, ya, att_s, norm_g3, w_main, ln_g3, ln_b3, w_sp_s, b_sp_s, wb16, wo16, fg, l,
                       final, True)
        s_out['n'].append(n1)
        s_out['m'].append(m1)
        s_out['conv'].append(conv1)
        s_out['chunk_v'].append(vn.reshape(bs, SAMPLE_ROWS, W_B)[:, :n_tok])
        kr3 = kr.reshape(bs, SAMPLE_ROWS, W_C)[:, :n_tok]
        v3 = ps[:, OFF_CV:OFF_CV + W_C].reshape(bs, SAMPLE_ROWS, W_C)[:, :n_tok]
        for g in range(len(DIL_PATTERNS)):
            gs = slice(g * WG_C, (g + 1) * WG_C)
            s_out['kv%d' % g].append(_stack_kv(kr3[:, :, gs], v3[:, :, gs]))

    stk = lambda d, k: jnp.stack(d[k], axis=0)
    y_prompt = hp.reshape(bp, seq, D_MODEL)
    y_sample = hs.reshape(bs, SAMPLE_ROWS, D_MODEL)[:, :n_tok]
    return (y_prompt, y_sample,
            stk(p_out, 'C'), stk(p_out, 'n'), stk(p_out, 'm'), stk(p_out, 'conv'),
            stk(p_out, 'kv0'), stk(p_out, 'kv1'), stk(p_out, 'kv2'),
            c_stack_s, stk(s_out, 'n'), stk(s_out, 'm'), stk(s_out, 'conv'), stk(s_out, 'chunk_v'),
            stk(s_out, 'kv0'), stk(s_out, 'kv1'), stk(s_out, 'kv2'))
```

```python
import functools

import jax
import jax.numpy as jnp
from jax import lax
from jax.experimental import pallas as pl
from jax.experimental.pallas import tpu as pltpu

F32 = jnp.float32
BF16 = jnp.bfloat16

D_MODEL = 1024
H_A = 4
DH_A = 256
W_A = H_A * DH_A
CONV_W = 4
G_B = 4
CHUNK_B = 128
W_B = 1024
DG_B = W_B // G_B
DIL_PATTERNS = ((128, 1), (512, 4), (2048, 16))
HG_C = 4
DH_C = 64
WG_C = HG_C * DH_C
W_C = len(DIL_PATTERNS) * WG_C
SPAN_C = 128
ROT_DIM = DH_C // 4
ROPE_THETA = 500000.0
EPS = 1e-6
PAST_LEN = 16384

N_PACK = 14336
WBLK = 1024
BLK_AO, BLK_AZ, BLK_BU, BLK_BV, BLK_BZ, BLK_C0, BLK_GA, BLK_GB, BLK_GC = 3, 4, 5, 6, 7, 8, 11, 12, 13
N_P1 = 6144
OFF_CQ, OFF_CK, OFF_CV, OFF_CZ = 3072, 3840, 4608, 5376
GATE_LANES = 128

INPROJ_TM = 2048
POST_TM = 512
MLSTM_BAND = 128
MLSTM_L = 256
SAMPLE_ROWS = 8
SAMPLE_ATT_SEQS = 4
ATT_TILE = 2048
VMEM_LIMIT = 56 * 1024 * 1024


def _cparams(n_axes):
    return pltpu.CompilerParams(dimension_semantics=("arbitrary",) * n_axes, vmem_limit_bytes=VMEM_LIMIT)


def _dot(a, b):
    return jnp.dot(a.astype(BF16), b.astype(BF16), preferred_element_type=F32)


def _dot_nt(a, b):
    return lax.dot_general(a.astype(BF16), b.astype(BF16), (((1,), (1,)), ((), ())), preferred_element_type=F32)


def _dot_tn(a, b):
    return lax.dot_general(a.astype(BF16), b.astype(BF16), (((0,), (0,)), ((), ())), preferred_element_type=F32)


def _sigmoid(x):
    return 0.5 * jnp.tanh(0.5 * x) + 0.5


def _silu(x):
    return x * _sigmoid(x)


def _inproj_kernel(x_ref, g_ref, w_ref, wg_ref, p_ref, gc_ref, gr_ref, xn_ref):
    nt = (((1,), (1,)), ((), ()))

    @pl.when(pl.program_id(1) == 0)
    def _():
        x = x_ref[...]
        r = lax.rsqrt(jnp.mean(x * x, axis=-1, keepdims=True) + EPS)
        xn = (x * r * g_ref[...]).astype(BF16)
        xn_ref[...] = xn
        wg = wg_ref[...].astype(BF16)
        gc_ref[...] = lax.dot_general(xn, wg, nt, preferred_element_type=F32)
        gr_ref[...] = lax.dot_general(wg[0:8, :], xn, nt, preferred_element_type=F32)

    p_ref[...] = lax.dot_general(xn_ref[...], w_ref[...], nt, preferred_element_type=F32)


def _inproj(x, norm_g, w_main, w_in_t, l):
    rows = x.shape[0]
    gate_blk = 4 * W_A // GATE_LANES
    tm = min(rows, INPROJ_TM)
    tn = WBLK
    n_a = 3 * W_A // tn
    wblk = lambda i, j: (l, jnp.where(j < n_a, j, j + (BLK_C0 - n_a)), 0)
    return pl.pallas_call(
        _inproj_kernel,
        grid=(rows // tm, N_P1 // tn),
        in_specs=[
            pl.BlockSpec((tm, D_MODEL), lambda i, j: (i, 0)),
            pl.BlockSpec((None, 1, D_MODEL), lambda i, j: (l, 0, 0)),
            pl.BlockSpec((None, tn, D_MODEL), wblk),
            pl.BlockSpec((None, GATE_LANES, D_MODEL), lambda i, j: (l, gate_blk, 0)),
        ],
        out_specs=[
            pl.BlockSpec((tm, tn), lambda i, j: (i, j)),
            pl.BlockSpec((tm, GATE_LANES), lambda i, j: (i, 0)),
            pl.BlockSpec((None, 8, tm), lambda i, j: (i, 0, 0)),
        ],
        out_shape=[
            jax.ShapeDtypeStruct((rows, N_P1), F32),
            jax.ShapeDtypeStruct((rows, GATE_LANES), F32),
            jax.ShapeDtypeStruct((rows // tm, 8, tm), F32),
        ],
        scratch_shapes=[pltpu.VMEM((tm, D_MODEL), BF16)],
        compiler_params=_cparams(2),
        name="inproj",
    )(x, norm_g, w_main, w_in_t)


def _pack_kernel(a_ref, b_ref, w_ref, *, first_shifted):
    j = pl.program_id(1)

    @pl.when(j < first_shifted)
    def _():
        w_ref[...] = a_ref[...].astype(BF16)

    @pl.when(j >= first_shifted)
    def _():
        w_ref[...] = jnp.concatenate([a_ref[2 * H_A:, :], b_ref[...]], axis=0).astype(BF16)


def _pack_w_in(w_in_t):
    depth = w_in_t.shape[0]
    tn = 2048
    gate_off = 4 * W_A
    assert gate_off % tn == 0 and w_in_t.shape[1] == N_PACK + 2 * H_A and 2 * H_A == 8
    kern = functools.partial(_pack_kernel, first_shifted=gate_off // tn)
    return pl.pallas_call(
        kern,
        grid=(depth, N_PACK // tn),
        in_specs=[
            pl.BlockSpec((None, tn, D_MODEL), lambda l, j: (l, j, 0)),
            pl.BlockSpec((None, 8, D_MODEL), lambda l, j: (l, (j + 1) * (tn // 8), 0)),
        ],
        out_specs=pl.BlockSpec((None, tn, D_MODEL), lambda l, j: (l, j, 0)),
        out_shape=jax.ShapeDtypeStruct((depth, N_PACK, D_MODEL), BF16),
        compiler_params=_cparams(2),
        name="pack_w_in",
    )(w_in_t, w_in_t)


def _conv_silu(ext_ref, cols, w, b, L):
    y = b
    for back in range(CONV_W):
        y = y + ext_ref[8 - back:8 - back + L, cols] * w[CONV_W - 1 - back:CONV_W - back]
    return _silu(y)


def _mlstm_core(g_col, g_row, get_v, ext_s, cw_ref, cb_ref, c_s, n_s, m_s, y_ref, L, t_valid, side_work=None):
    tick = side_work if side_work is not None else (lambda: None)
    neg_inf = jnp.float32(-jnp.inf)
    ig_col, lf_col = g_col, jax.nn.log_sigmoid(g_col)
    ig_row, lf_row = g_row, jax.nn.log_sigmoid(g_row)
    if t_valid < L:
        vc = lax.broadcasted_iota(jnp.int32, (L, GATE_LANES), 0) < t_valid
        vr = lax.broadcasted_iota(jnp.int32, (8, L), 1) < t_valid
        ig_col, lf_col = jnp.where(vc, ig_col, neg_inf), jnp.where(vc, lf_col, 0.0)
        ig_row, lf_row = jnp.where(vr, ig_row, neg_inf), jnp.where(vr, lf_row, 0.0)
    ti = lax.broadcasted_iota(jnp.int32, (L, L), 0)
    si = lax.broadcasted_iota(jnp.int32, (L, L), 1)
    lower = (ti >= si).astype(F32)
    upper = (ti <= si).astype(F32)
    if L <= 16:
        b_col = lower[:, 0:1] * lf_col[0:1, :]
        b_row = lf_row[:, 0:1] * upper[0:1, :]
        for s in range(1, L):
            b_col = b_col + lower[:, s:s + 1] * lf_col[s:s + 1, :]
            b_row = b_row + lf_row[:, s:s + 1] * upper[s:s + 1, :]
    else:
        b_col = jnp.dot(lower, lf_col, preferred_element_type=F32, precision=lax.Precision.HIGHEST)
        b_row = jnp.dot(lf_row, upper, preferred_element_type=F32, precision=lax.Precision.HIGHEST)
    last = t_valid - 1
    band = min(L, MLSTM_BAND)
    nt = (((1,), (1,)), ((), ()))

    for h in range(H_A):
        cs = slice(h * DH_A, (h + 1) * DH_A)
        ks = slice(W_A + h * DH_A, W_A + (h + 1) * DH_A)
        q = _conv_silu(ext_s, cs, cw_ref[:, cs], cb_ref[:, cs], L)
        tick()
        k = _conv_silu(ext_s, ks, cw_ref[:, ks], cb_ref[:, ks], L) * (DH_A ** -0.5)
        tick()
        v = get_v(h)
        q16, k16, v16 = q.astype(BF16), k.astype(BF16), v.astype(BF16)
        bc = b_col[:, H_A + h:H_A + h + 1]
        igc = ig_col[:, h:h + 1]
        br = b_row[H_A + h:H_A + h + 1, :]
        igr = ig_row[h:h + 1, :]
        m_prev = m_s[h][:, 0:1]
        c_prev = c_s[h]
        n_prev = n_s[h]
        c16 = c_prev.astype(BF16)

        for r in range(L // band):
            rs = slice(r * band, (r + 1) * band)
            kw = (r + 1) * band
            ti = lax.broadcasted_iota(jnp.int32, (band, kw), 0) + r * band
            si = lax.broadcasted_iota(jnp.int32, (band, kw), 1)
            logw = jnp.where(ti >= si, bc[rs] - br[:, 0:kw] + igr[:, 0:kw], neg_inf)
            inter = bc[rs] + m_prev
            m_t = jnp.maximum(inter, jnp.max(logw, axis=1, keepdims=True))
            w_intra = jnp.exp(logw - m_t)
            w_inter = jnp.exp(inter - m_t)
            s = w_intra * lax.dot_general(q16[rs], k16[0:kw], nt, preferred_element_type=F32)
            num = (jnp.dot(s.astype(BF16), v16[0:kw], preferred_element_type=F32)
                   + w_inter * lax.dot_general(q16[rs], c16, nt, preferred_element_type=F32))
            den = (jnp.sum(s, axis=1, keepdims=True)
                   + w_inter * jnp.sum(q[rs] * n_prev, axis=1, keepdims=True))
            y_ref[0, rs, cs] = num / jnp.maximum(jnp.abs(den), jnp.exp(-m_t))
            if r == last // band:
                m_new = m_t[last - r * band:last - r * band + 1, :]
            tick()

        b_last = bc[last:last + 1, :]
        decay = jnp.exp(b_last + m_prev - m_new)
        w_s = jnp.exp(b_last - bc + igc - m_new)
        c_s[h] = decay * c_prev + lax.dot_general((w_s * v).astype(BF16), k16, (((0,), (0,)), ((), ())),
                                                  preferred_element_type=F32)
        n_s[h] = decay * n_prev + jnp.sum(w_s * k, axis=0, keepdims=True)
        m_s[h] = jnp.broadcast_to(m_new, (1, GATE_LANES))
        tick()

    ext_s[0:8, :] = ext_s[t_valid:t_valid + 8, :]


def _mlstm_state_io(c0_ref, n0_ref, m0_ref, tail0_ref, c_s, n_s, m_s, ext_s):
    @pl.when(pl.program_id(1) == 0)
    def _():
        c_s[...] = c0_ref[0]
        n_s[...] = n0_ref[0]
        m_s[...] = m0_ref[0]
        ext_s[0:8, :] = tail0_ref[0]


def _mlstm_state_out(c_out, n_out, m_out, tail_out, c_s, n_s, m_s, ext_s, cprev_ref=None):
    @pl.when(pl.program_id(1) == pl.num_programs(1) - 1)
    def _():
        n_prev = c_out.shape[0] - 1
        if n_prev:
            c_out[0:n_prev] = cprev_ref[...]
        c_out[n_prev, 0] = c_s[...]
        n_out[0] = n_s[...]
        m_out[0] = m_s[...]
        tail_out[0] = ext_s[0:8, :]


def _mlstm_kernel(p_ref, gc_ref, gr_ref, brow_ref, bcol_ref, cw_ref, cb_ref, c0_ref, n0_ref, m0_ref, tail0_ref,
                  *rest, L, t_valid, n_prev):
    cprev_ref = rest[0] if n_prev else None
    y_ref, c_out, n_out, m_out, tail_out, c_s, n_s, m_s, ext_s = rest[1:] if n_prev else rest
    _mlstm_state_io(c0_ref, n0_ref, m0_ref, tail0_ref, c_s, n_s, m_s, ext_s)
    ext_s[8:8 + L, :] = p_ref[0, :, 0:2 * W_A]
    get_v = lambda h: p_ref[0, :, 2 * W_A + h * DH_A:2 * W_A + (h + 1) * DH_A]
    _mlstm_core(gc_ref[0] + brow_ref[...], gr_ref[0] + bcol_ref[...], get_v, ext_s, cw_ref, cb_ref,
                c_s, n_s, m_s, y_ref, L, t_valid)
    _mlstm_state_out(c_out, n_out, m_out, tail_out, c_s, n_s, m_s, ext_s, cprev_ref)


def _seq_a_kernel(x_ref, cos_ref, sin_ref, ng_ref, w_aq, w_ak, w_av, w_c0, w_c1, w_c2, wg_ref, brow_ref, bcol_ref,
                  cw_ref, cb_ref, c0_ref, n0_ref, m0_ref, tail0_ref, y_ref, pc_ref, c_out, n_out, m_out, tail_out,
                  c_s, n_s, m_s, ext_s, v_s, *, L):
    _mlstm_state_io(c0_ref, n0_ref, m0_ref, tail0_ref, c_s, n_s, m_s, ext_s)
    nt = (((1,), (1,)), ((), ()))
    x = x_ref[0]
    r = lax.rsqrt(jnp.mean(x * x, axis=-1, keepdims=True) + EPS)
    xn = (x * r * ng_ref[...]).astype(BF16)
    wg = wg_ref[...].astype(BF16)
    g_col = lax.dot_general(xn, wg, nt, preferred_element_type=F32) + brow_ref[...]
    g_row = lax.dot_general(wg[0:8, :], xn, nt, preferred_element_type=F32) + bcol_ref[...]
    proj = lambda w_ref: lax.dot_general(xn, w_ref[...], nt, preferred_element_type=F32)
    ext_s[8:8 + L, 0:W_A] = proj(w_aq)
    ext_s[8:8 + L, W_A:2 * W_A] = proj(w_ak)
    v_s[...] = proj(w_av)
    piece = 256
    todo = [(w_ref, j, lo) for j, w_ref in enumerate((w_c0, w_c1, w_c2)) for lo in range(0, WBLK, piece)]

    assert piece == WG_C

    def emit_piece():
        w_ref, j, lo = todo.pop(0)
        col = j * WBLK + lo
        y = lax.dot_general(xn, w_ref[lo:lo + piece, :], nt, preferred_element_type=F32)
        if col < 2 * W_C:
            cos = jnp.concatenate([cos_ref[0], cos_ref[0]], axis=1)
            sin = jnp.concatenate([sin_ref[0], sin_ref[0]], axis=1)
            y = _rope(y, cos, sin)
            if col < W_C:
                y = y * (DH_C ** -0.5)
        pc_ref[0, :, col:col + piece] = y

    side_work = lambda: emit_piece() if todo else None
    get_v = lambda h: v_s[:, h * DH_A:(h + 1) * DH_A]
    _mlstm_core(g_col, g_row, get_v, ext_s, cw_ref, cb_ref, c_s, n_s, m_s, y_ref, L, L, side_work)
    while todo:
        emit_piece()
    _mlstm_state_out(c_out, n_out, m_out, tail_out, c_s, n_s, m_s, ext_s)


def _seq_a(x, cos_t, sin_t, norm_g, w_main, w_in_t, bias_row, bias_col, conv_w, conv_b, c0, n0, m0, tail0, l, *, L):
    rows = x.shape[0]
    nch = rows // L
    x3 = x.reshape(nch, L, D_MODEL)
    gate_blk = 4 * W_A // GATE_LANES
    once = pl.Buffered(1)
    chunk = lambda b, c: (c, 0, 0)
    layer3 = lambda b, c: (l, 0, 0)
    wblk = lambda blk: pl.BlockSpec((None, WBLK, D_MODEL), lambda b, c, blk=blk: (l, blk, 0), pipeline_mode=once)
    st5 = lambda b, c: (0, 0, 0, 0, 0)
    st4 = lambda b, c: (0, 0, 0, 0)
    first4 = lambda b, c: (0, 0, 0, 0)
    first3 = lambda b, c: (0, 0, 0)
    y, pc, c1, n1, m1, tail1 = pl.pallas_call(
        functools.partial(_seq_a_kernel, L=L),
        grid=(1, nch),
        in_specs=[
            pl.BlockSpec((1, L, D_MODEL), chunk),
            pl.BlockSpec((1, L, 2 * DH_C), chunk),
            pl.BlockSpec((1, L, 2 * DH_C), chunk),
            pl.BlockSpec((None, 1, D_MODEL), layer3, pipeline_mode=once),
        ] + [wblk(b) for b in (0, 1, 2, BLK_C0, BLK_C0 + 1, BLK_C0 + 2)] + [
            pl.BlockSpec((None, GATE_LANES, D_MODEL), lambda b, c: (l, gate_blk, 0), pipeline_mode=once),
            pl.BlockSpec((None, 1, GATE_LANES), layer3, pipeline_mode=once),
            pl.BlockSpec((None, 8, 1), layer3, pipeline_mode=once),
            pl.BlockSpec((None, CONV_W, 2 * W_A), layer3, pipeline_mode=once),
            pl.BlockSpec((None, 1, 2 * W_A), layer3, pipeline_mode=once),
            pl.BlockSpec((None, 1, H_A, DH_A, DH_A), st5, pipeline_mode=once),
            pl.BlockSpec((None, 1, H_A, 1, DH_A), st5, pipeline_mode=once),
            pl.BlockSpec((None, 1, H_A, 1, GATE_LANES), st5, pipeline_mode=once),
            pl.BlockSpec((None, 1, 8, 2 * W_A), st4, pipeline_mode=once),
        ],
        out_specs=[
            pl.BlockSpec((1, L, W_A), chunk),
            pl.BlockSpec((1, L, 3 * WBLK), chunk),
            pl.BlockSpec((1, 1, H_A, DH_A, DH_A), lambda b, c: (0, 0, 0, 0, 0)),
            pl.BlockSpec((1, H_A, 1, DH_A), first4),
            pl.BlockSpec((1, H_A, 1, GATE_LANES), first4),
            pl.BlockSpec((1, 8, 2 * W_A), first3),
        ],
        out_shape=[
            jax.ShapeDtypeStruct((nch, L, W_A), F32),
            jax.ShapeDtypeStruct((nch, L, 3 * WBLK), F32),
            jax.ShapeDtypeStruct((1, 1, H_A, DH_A, DH_A), F32),
            jax.ShapeDtypeStruct((1, H_A, 1, DH_A), F32),
            jax.ShapeDtypeStruct((1, H_A, 1, GATE_LANES), F32),
            jax.ShapeDtypeStruct((1, 8, 2 * W_A), F32),
        ],
        scratch_shapes=[
            pltpu.VMEM((H_A, DH_A, DH_A), F32),
            pltpu.VMEM((H_A, 1, DH_A), F32),
            pltpu.VMEM((H_A, 1, GATE_LANES), F32),
            pltpu.VMEM((L + 8, 2 * W_A), F32),
            pltpu.VMEM((L, W_A), F32),
        ],
        compiler_params=_cparams(2),
        name="seq_a",
    )(x3, cos_t.reshape(nch, L, 2 * DH_C), sin_t.reshape(nch, L, 2 * DH_C), norm_g, *([w_main] * 6), w_in_t,
      bias_row, bias_col, conv_w, conv_b, c0, n0, m0, tail0)
    return (y.reshape(rows, W_A), pc.reshape(rows, 3 * WBLK), c1[0], n1[:, :, 0, :], m1[:, :, 0, 0],
            tail1[:, 8 - (CONV_W - 1):, :])


def _mlstm(p, gcol, grow, bias_row, bias_col, conv_w, conv_b, c0, n0, m0, tail0, l, ls, c_prev, *, L, t_valid):
    batch = c0.shape[1]
    rows = p.shape[0]
    nch = rows // (batch * L)
    n_prev = 0 if c_prev is None else c_prev.shape[0]
    p3 = p.reshape(batch * nch, L, N_P1)
    gc3 = gcol.reshape(batch * nch, L, GATE_LANES)
    tm = grow.shape[2]
    if L % GATE_LANES == 0:
        per = tm // L
        gr3, gr_index = grow, (lambda b, c: ((b * nch + c) // per, 0, (b * nch + c) % per))
    else:
        gr3 = grow.transpose(1, 0, 2).reshape(8, batch * nch, L).transpose(1, 0, 2)
        gr_index = lambda b, c: (b * nch + c, 0, 0)
    kern = functools.partial(_mlstm_kernel, L=L, t_valid=t_valid, n_prev=n_prev)
    chunk = lambda b, c: (b * nch + c, 0, 0)
    layer3 = lambda b, c: (l, 0, 0)
    stack5 = lambda b, c: (0, b, 0, 0, 0)
    per_b4 = lambda b, c: (b, 0, 0, 0)
    per_b3 = lambda b, c: (b, 0, 0)
    st5 = lambda b, c: (ls, b, 0, 0, 0)
    st4 = lambda b, c: (ls, b, 0, 0)
    y, c1, n1, m1, tail1 = pl.pallas_call(
        kern,
        grid=(batch, nch),
        in_specs=[
            pl.BlockSpec((1, L, 3 * W_A), chunk),
            pl.BlockSpec((1, L, GATE_LANES), chunk),
            pl.BlockSpec((1, 8, L), gr_index),
            pl.BlockSpec((None, 1, GATE_LANES), layer3),
            pl.BlockSpec((None, 8, 1), layer3),
            pl.BlockSpec((None, CONV_W, 2 * W_A), layer3),
            pl.BlockSpec((None, 1, 2 * W_A), layer3),
            pl.BlockSpec((None, 1, H_A, DH_A, DH_A), st5),
            pl.BlockSpec((None, 1, H_A, 1, DH_A), st5),
            pl.BlockSpec((None, 1, H_A, 1, GATE_LANES), st5),
            pl.BlockSpec((None, 1, 8, 2 * W_A), st4),
        ] + ([pl.BlockSpec((n_prev, 1, H_A, DH_A, DH_A), stack5)] if n_prev else []),
        out_specs=[
            pl.BlockSpec((1, L, W_A), chunk),
            pl.BlockSpec((n_prev + 1, 1, H_A, DH_A, DH_A), stack5),
            pl.BlockSpec((1, H_A, 1, DH_A), per_b4),
            pl.BlockSpec((1, H_A, 1, GATE_LANES), per_b4),
            pl.BlockSpec((1, 8, 2 * W_A), per_b3),
        ],
        out_shape=[
            jax.ShapeDtypeStruct((batch * nch, L, W_A), F32),
            jax.ShapeDtypeStruct((n_prev + 1, batch, H_A, DH_A, DH_A), F32),
            jax.ShapeDtypeStruct((batch, H_A, 1, DH_A), F32),
            jax.ShapeDtypeStruct((batch, H_A, 1, GATE_LANES), F32),
            jax.ShapeDtypeStruct((batch, 8, 2 * W_A), F32),
        ],
        scratch_shapes=[
            pltpu.VMEM((H_A, DH_A, DH_A), F32),
            pltpu.VMEM((H_A, 1, DH_A), F32),
            pltpu.VMEM((H_A, 1, GATE_LANES), F32),
            pltpu.VMEM((L + 8, 2 * W_A), F32),
        ],
        compiler_params=_cparams(2),
        name="mlstm",
    )(p3, gc3, gr3, bias_row, bias_col, conv_w, conv_b, c0, n0, m0, tail0, *([c_prev] if n_prev else []))
    return y.reshape(rows, W_A), c1, n1[:, :, 0, :], m1[:, :, 0, 0], tail1[:, 8 - (CONV_W - 1):, :]


def _rope(x, cos, sin):
    width = x.shape[1]
    lane = lax.broadcasted_iota(jnp.int32, (1, width), 1) % DH_C
    half = ROT_DIM // 2
    partner = jnp.where(lane < half, pltpu.roll(x, width - half, 1), pltpu.roll(x, half, 1))
    return x * cos + partner * sin


def _dil_attn_kernel(q_ref, k_ref, v_ref, o_ref, mx_ref, den_ref, qs, kbuf, vbuf, bias, *, d):
    i = pl.program_id(0)
    T = ATT_TILE
    HL = WG_C // 2
    nblk = T // SPAN_C
    nres = T // (SPAN_C * d)
    cur = (i % 2) * T
    prev = T - cur
    neg_inf = jnp.float32(-jnp.inf)

    def split(x):
        return x[:, 0:HL], x[:, HL:WG_C]

    def put(buf, lo, x):
        a, b = split(x)
        buf[0, pl.ds(lo, T), :] = a
        buf[1, pl.ds(lo, T), :] = b

    @pl.when(i == 0)
    def _():
        ii = lax.broadcasted_iota(jnp.int32, (SPAN_C, 2 * SPAN_C), 0)
        jj = lax.broadcasted_iota(jnp.int32, (SPAN_C, 2 * SPAN_C), 1)
        band = (jj >= ii) & (jj <= ii + SPAN_C)
        bias[1] = jnp.where(band, 0.0, neg_inf)
        bias[0] = jnp.where(band & (jj >= SPAN_C), 0.0, neg_inf)
        kbuf[:, T:2 * T, :] = jnp.zeros((2, T, HL), F32)
        vbuf[:, T:2 * T, :] = jnp.zeros((2, T, HL), F32)

    put(qs, 0, q_ref[...])
    put(kbuf, cur, k_ref[...])
    put(vbuf, cur, v_ref[...])

    lane_head = lax.broadcasted_iota(jnp.int32, (SPAN_C, WG_C), 1) // DH_C
    hm = [(lane_head == h).astype(F32) for h in range(HG_C)]

    def rows(start, size):
        return pl.ds(start, size) if d == 1 else pl.ds(start, size, stride=d)

    def take(buf, start, size):
        return jnp.concatenate([buf[0, rows(start, size), :], buf[1, rows(start, size), :]], axis=1)

    def pick(per_head):
        out = jnp.broadcast_to(per_head[0], (SPAN_C, WG_C))
        for h in range(1, HG_C):
            out = jnp.where(lane_head >= h, per_head[h], out)
        return out

    def body(blk, carry):
        r = blk % d
        c = blk // d
        qstart = r + SPAN_C * d * c
        in_tile = cur + qstart
        before = jnp.where(c > 0, in_tile - SPAN_C * d, prev + r + SPAN_C * d * (nres - 1))
        qb = take(qs, qstart, SPAN_C)
        kb = jnp.concatenate([take(kbuf, before, SPAN_C), take(kbuf, in_tile, SPAN_C)], axis=0)
        vb = jnp.concatenate([take(vbuf, before, SPAN_C), take(vbuf, in_tile, SPAN_C)], axis=0)
        has_prev = jnp.logical_or(i > 0, c > 0).astype(jnp.int32)
        kb16, vb16 = kb.astype(BF16), vb.astype(BF16)
        pvs, mxs, dens = [], [], []
        for h0 in range(0, HG_C, 2):
            q2 = jnp.concatenate([qb * hm[h0], qb * hm[h0 + 1]], axis=0)
            s = _dot_nt(q2, kb16).reshape(2, SPAN_C, 2 * SPAN_C) + bias[has_prev]
            mx = jnp.max(s, axis=2, keepdims=True)
            p = jnp.exp(s - mx)
            den = jnp.sum(p, axis=2, keepdims=True)
            pv = _dot(p.reshape(2 * SPAN_C, 2 * SPAN_C), vb16).reshape(2, SPAN_C, WG_C)
            pvs += [pv[0], pv[1]]
            mxs += [mx[0], mx[1]]
            dens += [den[0], den[1]]
        outs = (pick(pvs), pick(mxs), pick(dens))
        for ref, val in zip((o_ref, mx_ref, den_ref), outs):
            for half, part in enumerate(split(val)):
                ref[half, rows(qstart, SPAN_C), :] = part
        return carry

    lax.fori_loop(0, nblk, body, 0, unroll=16)


def _dil_attn(p, base, g, d):
    rows = p.shape[0]
    T = ATT_TILE
    kern = functools.partial(_dil_attn_kernel, d=d)
    col = lambda off: pl.BlockSpec((T, WG_C), lambda i, off=off: (i, (base + off - OFF_CQ) // WG_C + g))
    slab = pl.BlockSpec((2, T, WG_C // 2), lambda i: (0, i, 0))
    slab_shape = jax.ShapeDtypeStruct((2, rows, WG_C // 2), F32)
    return pl.pallas_call(
        kern,
        grid=(rows // T,),
        in_specs=[col(OFF_CQ), col(OFF_CK), col(OFF_CV)],
        out_specs=[slab, slab, slab],
        out_shape=[slab_shape, slab_shape, slab_shape],
        scratch_shapes=[pltpu.VMEM((2, T, WG_C // 2), F32), pltpu.VMEM((2, 2 * T, WG_C // 2), F32),
                        pltpu.VMEM((2, 2 * T, WG_C // 2), F32), pltpu.VMEM((2, SPAN_C, 2 * SPAN_C), F32)],
        compiler_params=_cparams(1),
        name="dil_attn_d%d" % d,
    )(p, p, p)


def _sample_attn_kernel(pc0, pc1, pc2, cos_ref, sin_ref, c0_ref, c1_ref, c2_ref,
                        o0, o1, o2, m0, m1, m2, d0, d1, d2, kr_ref, *, n_tok, n_seq):
    reps = W_C // (2 * DH_C)
    cos = jnp.concatenate([cos_ref[...]] * reps, axis=1)
    sin = jnp.concatenate([sin_ref[...]] * reps, axis=1)
    nrow = HG_C * SAMPLE_ROWS
    rowh = lax.broadcasted_iota(jnp.int32, (nrow, WG_C), 0) // SAMPLE_ROWS
    laneh = lax.broadcasted_iota(jnp.int32, (nrow, WG_C), 1) // DH_C
    hm = (rowh == laneh).astype(F32)
    tok_col = lax.broadcasted_iota(jnp.int32, (nrow, 1), 0) % SAMPLE_ROWS
    out_row = lax.broadcasted_iota(jnp.int32, (SAMPLE_ROWS, WG_C), 0)
    neg_inf = jnp.float32(-jnp.inf)
    crefs = (c0_ref, c1_ref, c2_ref)
    orefs = (o0, o1, o2)
    mrefs = (m0, m1, m2)
    drefs = (d0, d1, d2)
    for e in range(n_seq):
        rows_e = slice(e * SAMPLE_ROWS, (e + 1) * SAMPLE_ROWS)
        pc = jnp.concatenate([pc0[e], pc1[e], pc2[e]], axis=1)
        qr = _rope(pc[:, 0:W_C], cos, sin) * (DH_C ** -0.5)
        kr = _rope(pc[:, W_C:2 * W_C], cos, sin)
        v = pc[:, 2 * W_C:3 * W_C]
        kr_ref[e] = kr
        for g, (_, d) in enumerate(DIL_PATTERNS):
            gs = slice(g * WG_C, (g + 1) * WG_C)
            n_buf = SPAN_C * d
            kn = kr[:, gs]
            vn = v[:, gs]
            qm = jnp.concatenate([qr[:, gs]] * HG_C, axis=0) * hm
            s = _dot(qm, crefs[g][e, 0])
            tok = lax.broadcasted_iota(jnp.int32, (nrow, n_buf), 0) % SAMPLE_ROWS
            pos = lax.broadcasted_iota(jnp.int32, (nrow, n_buf), 1)
            attends = (pos >= tok) if d == 1 else ((pos & (d - 1)) == tok)
            s = jnp.where(attends | (tok >= n_tok), s, neg_inf)
            new_ok = [(u <= tok_col) if d == 1 else (u == tok_col) for u in range(n_tok)]
            s_new = [jnp.where(ok, jnp.sum(qm * kn[u:u + 1, :], axis=1, keepdims=True), neg_inf)
                     for u, ok in enumerate(new_ok)]
            mx = jnp.max(s, axis=1, keepdims=True)
            for sn in s_new:
                mx = jnp.maximum(mx, sn)
            p = jnp.exp(s - mx)
            den = jnp.sum(p, axis=1, keepdims=True)
            acc = _dot_nt(p, crefs[g][e, 1])
            for u, sn in enumerate(s_new):
                pn = jnp.exp(sn - mx)
                den = den + pn
                acc = acc + pn * vn[u:u + 1, :]
            stacked = (acc * hm, mx * hm, den * hm)
            for ref, val, pad in zip((orefs[g], mrefs[g], drefs[g]), stacked, (0.0, 0.0, 1.0)):
                v8 = val[0:SAMPLE_ROWS]
                for h in range(1, HG_C):
                    v8 = v8 + val[h * SAMPLE_ROWS:(h + 1) * SAMPLE_ROWS]
                v8 = jnp.where(out_row < n_tok, v8, pad)
                for half in range(2):
                    ref[half, rows_e, :] = v8[:, half * (WG_C // 2):(half + 1) * (WG_C // 2)]


def _cache_views(caches):
    views = []
    for (win, d), c in zip(DIL_PATTERNS, caches):
        depth, batch, n_buf = c.shape[:3]
        assert n_buf == SPAN_C * d
        views.append(jnp.transpose(c, (0, 1, 3, 4, 5, 2)).reshape(depth, batch, 2, WG_C, n_buf))
    return views


def _sample_attn(p, cos_s, sin_s, views, l, n_tok):
    batch = p.shape[0] // SAMPLE_ROWS
    n_seq = SAMPLE_ATT_SEQS if batch % SAMPLE_ATT_SEQS == 0 else 1
    p3 = p.reshape(batch, SAMPLE_ROWS, N_P1)
    pcol = lambda k: pl.BlockSpec((n_seq, SAMPLE_ROWS, 1024), lambda b, k=k: (b, 0, OFF_CQ // 1024 + k))
    tab = pl.BlockSpec((SAMPLE_ROWS, 2 * DH_C), lambda b: (0, 0))
    cache_spec = lambda d: pl.BlockSpec((None, n_seq, 2, WG_C, SPAN_C * d), lambda b: (l, b, 0, 0, 0))
    out = pl.BlockSpec((2, n_seq * SAMPLE_ROWS, WG_C // 2), lambda b: (0, b, 0))
    res = pl.pallas_call(
        functools.partial(_sample_attn_kernel, n_tok=n_tok, n_seq=n_seq),
        grid=(batch // n_seq,),
        in_specs=[pcol(0), pcol(1), pcol(2), tab, tab] + [cache_spec(d) for _, d in DIL_PATTERNS],
        out_specs=[out] * 9 + [pl.BlockSpec((n_seq, SAMPLE_ROWS, W_C), lambda b: (b, 0, 0))],
        out_shape=[jax.ShapeDtypeStruct((2, batch * SAMPLE_ROWS, WG_C // 2), F32)] * 9
        + [jax.ShapeDtypeStruct((batch, SAMPLE_ROWS, W_C), F32)],
        compiler_params=_cparams(1),
        name="sample_attn",
    )(p3, p3, p3, cos_s, sin_s, *views)
    return list(res[:9]), res[9].reshape(batch * SAMPLE_ROWS, W_C)


def _post_kernel(x_ref, h_ref, o0, o1, o2, m0, m1, m2, d0, d1, d2, cz_ref, ng_ref, w_ao, w_az, w_bu, w_bv, w_bz, w_ga, w_gb,
                 w_gc, lg_ref, lb_ref, wsp_ref, bsp_ref, wb_ref, wo_ref, fg_ref, out_ref, *rest, nchunks, final,
                 emit_vn):
    if emit_vn:
        vn_out, vn_s, yb_s = rest
    else:
        vn_s, yb_s = rest
    nt = (((1,), (1,)), ((), ()))
    x = x_ref[...]
    r = lax.rsqrt(jnp.mean(x * x, axis=-1, keepdims=True) + EPS)
    xn = (x * r * ng_ref[...]).astype(BF16)
    proj = lambda w_ref: lax.dot_general(xn, w_ref[...], nt, preferred_element_type=F32)
    branch = lambda y, lo, hi: jnp.dot(y.astype(BF16), wb_ref[lo:hi, :], preferred_element_type=F32)

    ya = h_ref[...] * _sigmoid(proj(w_ao)) * _silu(proj(w_az))
    acc = _sigmoid(proj(w_ga)) * branch(ya, 0, W_A)

    v = proj(w_bv)
    mu = jnp.mean(v, axis=-1, keepdims=True)
    var = jnp.mean(jnp.square(v - mu), axis=-1, keepdims=True)
    vn = (v - mu) * lax.rsqrt(var + EPS) * lg_ref[...] + lb_ref[...]
    if emit_vn:
        vn_out[...] = vn
    vn_s[...] = vn.astype(BF16)
    gate_b = proj(w_bu) * _silu(proj(w_bz))
    tri = (lax.broadcasted_iota(jnp.int32, (CHUNK_B, CHUNK_B), 0)
           >= lax.broadcasted_iota(jnp.int32, (CHUNK_B, CHUNK_B), 1))
    for g in range(G_B):
        wg = jnp.where(tri, wsp_ref[g], 0.0).astype(BF16)
        cs = slice(g * DG_B, (g + 1) * DG_B)
        for c in range(nchunks):
            rs = slice(c * CHUNK_B, (c + 1) * CHUNK_B)
            yb_s[rs, cs] = jnp.dot(wg, vn_s[rs, cs], preferred_element_type=F32) + bsp_ref[g]
    acc = acc + _sigmoid(proj(w_gb)) * branch(gate_b * yb_s[...], W_A, W_A + W_B)

    unslab = lambda ref: jnp.concatenate([ref[0], ref[1]], axis=1)
    ms = [unslab(m0), unslab(m1), unslab(m2)]
    mx = jnp.maximum(jnp.maximum(ms[0], ms[1]), ms[2])
    es = [jnp.exp(m - mx) for m in ms]
    inv_tot = 1.0 / (es[0] * unslab(d0) + es[1] * unslab(d1) + es[2] * unslab(d2))
    pc = None
    for g, o_ref in enumerate((o0, o1, o2)):
        gs = slice(g * WG_C, (g + 1) * WG_C)
        yc = unslab(o_ref) * (es[g] * inv_tot) * _silu(cz_ref[:, gs])
        t = branch(yc, W_A + W_B + g * WG_C, W_A + W_B + (g + 1) * WG_C)
        pc = t if pc is None else pc + t
    acc = acc + _sigmoid(proj(w_gc)) * pc

    out = x + jnp.dot(acc.astype(BF16), wo_ref[...], preferred_element_type=F32)
    if final:
        rr = lax.rsqrt(jnp.mean(out * out, axis=-1, keepdims=True) + EPS)
        out = out * rr * fg_ref[...]
    out_ref[...] = out


def _post(x, p1, base, h, att, norm_g, w_main, ln_g, ln_b, w_sp, b_sp_col, w_branch, w_out, final_g, l, final,
          emit_vn):
    rows = x.shape[0]
    tm = min(rows, POST_TM)
    once = pl.Buffered(1)
    row = lambda w: pl.BlockSpec((tm, w), lambda i: (i, 0))
    slab = pl.BlockSpec((2, tm, WG_C // 2), lambda i: (0, i, 0))
    wblk = lambda b: pl.BlockSpec((None, WBLK, D_MODEL), lambda i, b=b: (l, b, 0), pipeline_mode=once)
    vec = lambda w: pl.BlockSpec((None, 1, w), lambda i: (l, 0, 0), pipeline_mode=once)
    in_specs = (
        [row(D_MODEL), row(W_A)] + [slab] * 9
        + [pl.BlockSpec((tm, W_C), lambda i: (i, (base + OFF_CZ - OFF_CQ) // W_C)), vec(D_MODEL)]
        + [wblk(b) for b in (BLK_AO, BLK_AZ, BLK_BU, BLK_BV, BLK_BZ, BLK_GA, BLK_GB, BLK_GC)]
        + [vec(W_B), vec(W_B),
           pl.BlockSpec((None, G_B, CHUNK_B, CHUNK_B), lambda i: (l, 0, 0, 0), pipeline_mode=once),
           pl.BlockSpec((None, G_B, CHUNK_B, 1), lambda i: (l, 0, 0, 0), pipeline_mode=once),
           pl.BlockSpec((None, W_A + W_B + W_C, D_MODEL), lambda i: (l, 0, 0), pipeline_mode=once),
           pl.BlockSpec((None, D_MODEL, D_MODEL), lambda i: (l, 0, 0), pipeline_mode=once),
           pl.BlockSpec((1, D_MODEL), lambda i: (0, 0), pipeline_mode=once)])
    out_specs = [row(D_MODEL)]
    out_shape = [jax.ShapeDtypeStruct((rows, D_MODEL), F32)]
    if emit_vn:
        out_specs.append(row(W_B))
        out_shape.append(jax.ShapeDtypeStruct((rows, W_B), F32))
    res = pl.pallas_call(
        functools.partial(_post_kernel, nchunks=tm // CHUNK_B, final=final, emit_vn=emit_vn),
        grid=(rows // tm,),
        in_specs=in_specs,
        out_specs=out_specs,
        out_shape=out_shape,
        scratch_shapes=[pltpu.VMEM((tm, W_B), BF16), pltpu.VMEM((tm, W_B), F32)],
        compiler_params=_cparams(1),
        name="post",
    )(x, h, *att, p1, norm_g, *([w_main] * 8), ln_g, ln_b, w_sp, b_sp_col, w_branch, w_out, final_g)
    return (res[0], res[1]) if emit_vn else (res[0], None)


def _rope_tables(pos):
    half = ROT_DIM // 2
    inv = ROPE_THETA ** (-jnp.arange(half, dtype=F32) / half)
    ang = inv[:, None] * pos.astype(F32)[None, :]
    lane = jnp.arange(2 * DH_C) % DH_C
    freq = jnp.arange(half)[:, None] == (lane % half)[None, :]
    rot = (lane < ROT_DIM)[None, :]
    sel_cos = (freq & rot).astype(F32)
    sel_sin = sel_cos * jnp.where(lane < half, -1.0, 1.0)[None, :]
    spread = lambda t, sel: lax.dot_general(t, sel, (((0,), (0,)), ((), ())), precision=lax.Precision.HIGHEST)
    return spread(jnp.cos(ang), sel_cos) + (~rot).astype(F32), spread(jnp.sin(ang), sel_sin)


def _stack_kv(k_rows, v_rows):
    b, t, _ = k_rows.shape
    return jnp.stack([k_rows.reshape(b, t, HG_C, DH_C), v_rows.reshape(b, t, HG_C, DH_C)], axis=2)


def kernel(x_prompt, x_sample, state_C, state_n, state_m, state_conv, cache_kv_w128, cache_kv_w512, cache_kv_w2048,
           norm_g, w_in, b_igate, b_fgate, conv_w, conv_b, ln_v_g, ln_v_b, w_spatial, b_spatial, w_branch, w_out,
           final_norm_g):
    depth = w_in.shape[0]
    bp, seq, _ = x_prompt.shape
    bs, n_tok, _ = x_sample.shape
    assert bp == 1 and seq % ATT_TILE == 0 and n_tok <= SAMPLE_ROWS // 2 and n_tok >= CONV_W - 1
    caches = (cache_kv_w128, cache_kv_w512, cache_kv_w2048)
    pad_tok = SAMPLE_ROWS - n_tok
    rows_s = bs * SAMPLE_ROWS

    w_in_t = jnp.swapaxes(w_in, 1, 2)
    w_main = _pack_w_in(w_in_t)
    gate_bias = jnp.concatenate([b_igate, b_fgate], axis=1)
    bias_row = jnp.pad(gate_bias, ((0, 0), (0, GATE_LANES - 2 * H_A)))[:, None, :]
    bias_col = gate_bias[:, :, None]
    wb16 = w_branch.astype(BF16)
    wo16 = w_out.astype(BF16)
    b_sp_col = b_spatial[..., None]
    norm_g3, conv_b3 = norm_g[:, None, :], conv_b[:, None, :]
    ln_g3, ln_b3 = ln_v_g[:, None, :], ln_v_b[:, None, :]
    seqs_per_chunk = CHUNK_B // SAMPLE_ROWS
    w8 = jnp.pad(w_spatial[:, :, :n_tok, :n_tok], ((0, 0), (0, 0), (0, pad_tok), (0, pad_tok)))
    w_sp_s = jnp.einsum('ab,lgij->lgaibj', jnp.eye(seqs_per_chunk, dtype=F32), w8).reshape(
        depth, G_B, CHUNK_B, CHUNK_B)
    b_sp_s = jnp.tile(jnp.pad(b_spatial[:, :, :n_tok], ((0, 0), (0, 0), (0, pad_tok))),
                      (1, 1, seqs_per_chunk))[..., None]

    cos_p, sin_p = _rope_tables(jnp.arange(seq))
    cos_s, sin_s = _rope_tables(PAST_LEN + jnp.arange(SAMPLE_ROWS))

    hp = x_prompt.reshape(seq, D_MODEL)
    hs = jnp.pad(x_sample, ((0, 0), (0, pad_tok), (0, 0))).reshape(rows_s, D_MODEL)
    fg = final_norm_g[None, :]
    zeros_c = jnp.zeros((1, 1, H_A, DH_A, DH_A), F32)
    zeros_n = jnp.zeros((1, 1, H_A, 1, DH_A), F32)
    zeros_m = jnp.zeros((1, 1, H_A, 1, GATE_LANES), F32)
    zeros_tail = jnp.zeros((1, 1, 8, 2 * W_A), F32)
    n0_s = state_n[:, :, :, None, :]
    m0_s = jnp.broadcast_to(state_m[:, :, :, None, None], (depth, bs, H_A, 1, GATE_LANES))
    tail0_s = jnp.pad(state_conv, ((0, 0), (0, 0), (8 - (CONV_W - 1), 0), (0, 0)))
    views = _cache_views(caches)

    c_stack_s = None
    p_out = {k: [] for k in ('C', 'n', 'm', 'conv', 'kv0', 'kv1', 'kv2')}
    s_out = {k: [] for k in ('n', 'm', 'conv', 'chunk_v', 'kv0', 'kv1', 'kv2')}
    for l in range(depth):
        final = l == depth - 1

        ya, pp, c1, n1, m1, conv1 = _seq_a(hp, cos_p, sin_p, norm_g3, w_main, w_in_t, bias_row, bias_col, conv_w, conv_b3,
                                           zeros_c, zeros_n, zeros_m, zeros_tail, l, L=MLSTM_L)
        att = [[], [], []]
        for g, (win, d) in enumerate(DIL_PATTERNS):
            for dst, val in zip(att, _dil_attn(pp, 0, g, d)):
                dst.append(val)
            keep = min(win, seq)
            k_g = pp[seq - keep:, OFF_CK - OFF_CQ + g * WG_C:OFF_CK - OFF_CQ + (g + 1) * WG_C]
            v_g = pp[seq - keep:, OFF_CV - OFF_CQ + g * WG_C:OFF_CV - OFF_CQ + (g + 1) * WG_C]
            p_out['kv%d' % g].append(_stack_kv(k_g[None], v_g[None]))
        hp, _ = _post(hp, pp, 0, ya, att[0] + att[1] + att[2], norm_g3, w_main, ln_g3, ln_b3, w_spatial, b_sp_col, wb16, wo16, fg, l,
                      final, False)
        p_out['C'].append(c1)
        p_out['n'].append(n1)
        p_out['m'].append(m1)
        p_out['conv'].append(conv1)

        ps, gcol, grow = _inproj(hs, norm_g3, w_main, w_in_t, l)
        ya, c1, n1, m1, conv1 = _mlstm(ps, gcol, grow, bias_row, bias_col, conv_w, conv_b3,
                                       state_C, n0_s, m0_s, tail0_s, l, l, c_stack_s, L=SAMPLE_ROWS, t_valid=n_tok)
        c_stack_s = c1
        att_s, kr = _sample_attn(ps, cos_s, sin_s, views, l, n_tok)
        hs, vn = _post(hs, ps, OFF_CQ, ya, att_s, norm_g3, w_main, ln_g3, ln_b3, w_sp_s, b_sp_s, wb16, wo16, fg, l,
                       final, True)
        s_out['n'].append(n1)
        s_out['m'].append(m1)
        s_out['conv'].append(conv1)
        s_out['chunk_v'].append(vn.reshape(bs, SAMPLE_ROWS, W_B)[:, :n_tok])
        kr3 = kr.reshape(bs, SAMPLE_ROWS, W_C)[:, :n_tok]
        v3 = ps[:, OFF_CV:OFF_CV + W_C].reshape(bs, SAMPLE_ROWS, W_C)[:, :n_tok]
        for g in range(len(DIL_PATTERNS)):
            gs = slice(g * WG_C, (g + 1) * WG_C)
            s_out['kv%d' % g].append(_stack_kv(kr3[:, :, gs], v3[:, :, gs]))

    stk = lambda d, k: jnp.stack(d[k], axis=0)
    y_prompt = hp.reshape(bp, seq, D_MODEL)
    y_sample = hs.reshape(bs, SAMPLE_ROWS, D_MODEL)[:, :n_tok]
    return (y_prompt, y_sample,
            stk(p_out, 'C'), stk(p_out, 'n'), stk(p_out, 'm'), stk(p_out, 'conv'),
            stk(p_out, 'kv0'), stk(p_out, 'kv1'), stk(p_out, 'kv2'),
            c_stack_s, stk(s_out, 'n'), stk(s_out, 'm'), stk(s_out, 'conv'), stk(s_out, 'chunk_v'),
            stk(s_out, 'kv0'), stk(s_out, 'kv1'), stk(s_out, 'kv2'))
```

```python
import functools

import jax
import jax.numpy as jnp
from jax import lax
from jax.experimental import pallas as pl
from jax.experimental.pallas import tpu as pltpu

F32 = jnp.float32
BF16 = jnp.bfloat16

D_MODEL = 1024
H_A = 4
DH_A = 256
W_A = H_A * DH_A
CONV_W = 4
G_B = 4
CHUNK_B = 128
W_B = 1024
DG_B = W_B // G_B
DIL_PATTERNS = ((128, 1), (512, 4), (2048, 16))
HG_C = 4
DH_C = 64
WG_C = HG_C * DH_C
W_C = len(DIL_PATTERNS) * WG_C
SPAN_C = 128
ROT_DIM = DH_C // 4
ROPE_THETA = 500000.0
EPS = 1e-6
PAST_LEN = 16384

N_PACK = 14336
WBLK = 1024
BLK_AO, BLK_AZ, BLK_BU, BLK_BV, BLK_BZ, BLK_C0, BLK_GA, BLK_GB, BLK_GC = 3, 4, 5, 6, 7, 8, 11, 12, 13
N_P1 = 6144
OFF_CQ, OFF_CK, OFF_CV, OFF_CZ = 3072, 3840, 4608, 5376
GATE_LANES = 128

INPROJ_TM = 2048
POST_TM = 512
MLSTM_BAND = 128
MLSTM_L = 256
SAMPLE_ROWS = 8
MLSTM_SEQS = 4
SAMPLE_ATT_SEQS = 4
ATT_TILE = 2048
VMEM_LIMIT = 56 * 1024 * 1024


def _cparams(n_axes):
    return pltpu.CompilerParams(dimension_semantics=("arbitrary",) * n_axes, vmem_limit_bytes=VMEM_LIMIT)


def _dot(a, b):
    return jnp.dot(a.astype(BF16), b.astype(BF16), preferred_element_type=F32)


def _dot_nt(a, b):
    return lax.dot_general(a.astype(BF16), b.astype(BF16), (((1,), (1,)), ((), ())), preferred_element_type=F32)


def _dot_tn(a, b):
    return lax.dot_general(a.astype(BF16), b.astype(BF16), (((0,), (0,)), ((), ())), preferred_element_type=F32)


def _sigmoid(x):
    return 0.5 * jnp.tanh(0.5 * x) + 0.5


def _silu(x):
    return x * _sigmoid(x)


def _inproj_kernel(x_ref, g_ref, w_ref, wg_ref, p_ref, gc_ref, gr_ref, xn_ref):
    nt = (((1,), (1,)), ((), ()))

    @pl.when(pl.program_id(1) == 0)
    def _():
        x = x_ref[...]
        r = lax.rsqrt(jnp.mean(x * x, axis=-1, keepdims=True) + EPS)
        xn = (x * r * g_ref[...]).astype(BF16)
        xn_ref[...] = xn
        wg = wg_ref[...].astype(BF16)
        gc_ref[...] = lax.dot_general(xn, wg, nt, preferred_element_type=F32)
        gr_ref[...] = lax.dot_general(wg[0:8, :], xn, nt, preferred_element_type=F32)

    p_ref[...] = lax.dot_general(xn_ref[...], w_ref[...], nt, preferred_element_type=F32)


def _inproj(x, norm_g, w_main, w_in_t, l):
    rows = x.shape[0]
    gate_blk = 4 * W_A // GATE_LANES
    tm = min(rows, INPROJ_TM)
    tn = WBLK
    n_a = 3 * W_A // tn
    wblk = lambda i, j: (l, jnp.where(j < n_a, j, j + (BLK_C0 - n_a)), 0)
    return pl.pallas_call(
        _inproj_kernel,
        grid=(rows // tm, N_P1 // tn),
        in_specs=[
            pl.BlockSpec((tm, D_MODEL), lambda i, j: (i, 0)),
            pl.BlockSpec((None, 1, D_MODEL), lambda i, j: (l, 0, 0)),
            pl.BlockSpec((None, tn, D_MODEL), wblk),
            pl.BlockSpec((None, GATE_LANES, D_MODEL), lambda i, j: (l, gate_blk, 0)),
        ],
        out_specs=[
            pl.BlockSpec((tm, tn), lambda i, j: (i, j)),
            pl.BlockSpec((tm, GATE_LANES), lambda i, j: (i, 0)),
            pl.BlockSpec((None, 8, tm), lambda i, j: (i, 0, 0)),
        ],
        out_shape=[
            jax.ShapeDtypeStruct((rows, N_P1), F32),
            jax.ShapeDtypeStruct((rows, GATE_LANES), F32),
            jax.ShapeDtypeStruct((rows // tm, 8, tm), F32),
        ],
        scratch_shapes=[pltpu.VMEM((tm, D_MODEL), BF16)],
        compiler_params=_cparams(2),
        name="inproj",
    )(x, norm_g, w_main, w_in_t)


def _pack_kernel(a_ref, b_ref, w_ref, *, first_shifted):
    j = pl.program_id(1)

    @pl.when(j < first_shifted)
    def _():
        w_ref[...] = a_ref[...].astype(BF16)

    @pl.when(j >= first_shifted)
    def _():
        w_ref[...] = jnp.concatenate([a_ref[2 * H_A:, :], b_ref[...]], axis=0).astype(BF16)


def _pack_w_in(w_in_t):
    depth = w_in_t.shape[0]
    tn = 2048
    gate_off = 4 * W_A
    assert gate_off % tn == 0 and w_in_t.shape[1] == N_PACK + 2 * H_A and 2 * H_A == 8
    kern = functools.partial(_pack_kernel, first_shifted=gate_off // tn)
    return pl.pallas_call(
        kern,
        grid=(depth, N_PACK // tn),
        in_specs=[
            pl.BlockSpec((None, tn, D_MODEL), lambda l, j: (l, j, 0)),
            pl.BlockSpec((None, 8, D_MODEL), lambda l, j: (l, (j + 1) * (tn // 8), 0)),
        ],
        out_specs=pl.BlockSpec((None, tn, D_MODEL), lambda l, j: (l, j, 0)),
        out_shape=jax.ShapeDtypeStruct((depth, N_PACK, D_MODEL), BF16),
        compiler_params=_cparams(2),
        name="pack_w_in",
    )(w_in_t, w_in_t)


def _conv_silu(ext_ref, cols, w, b, L):
    y = b
    for back in range(CONV_W):
        y = y + ext_ref[8 - back:8 - back + L, cols] * w[CONV_W - 1 - back:CONV_W - back]
    return _silu(y)


def _mlstm_core(g_col, g_row, get_v, ext_s, cw_ref, cb_ref, c_s, n_s, m_s, y_ref, L, t_valid, side_work=None):
    tick = side_work if side_work is not None else (lambda: None)
    neg_inf = jnp.float32(-jnp.inf)
    ig_col, lf_col = g_col, jax.nn.log_sigmoid(g_col)
    ig_row, lf_row = g_row, jax.nn.log_sigmoid(g_row)
    if t_valid < L:
        vc = lax.broadcasted_iota(jnp.int32, (L, GATE_LANES), 0) < t_valid
        vr = lax.broadcasted_iota(jnp.int32, (8, L), 1) < t_valid
        ig_col, lf_col = jnp.where(vc, ig_col, neg_inf), jnp.where(vc, lf_col, 0.0)
        ig_row, lf_row = jnp.where(vr, ig_row, neg_inf), jnp.where(vr, lf_row, 0.0)
    ti = lax.broadcasted_iota(jnp.int32, (L, L), 0)
    si = lax.broadcasted_iota(jnp.int32, (L, L), 1)
    lower = (ti >= si).astype(F32)
    upper = (ti <= si).astype(F32)
    if L <= 16:
        b_col = lower[:, 0:1] * lf_col[0:1, :]
        b_row = lf_row[:, 0:1] * upper[0:1, :]
        for s in range(1, L):
            b_col = b_col + lower[:, s:s + 1] * lf_col[s:s + 1, :]
            b_row = b_row + lf_row[:, s:s + 1] * upper[s:s + 1, :]
    else:
        b_col = jnp.dot(lower, lf_col, preferred_element_type=F32, precision=lax.Precision.HIGHEST)
        b_row = jnp.dot(lf_row, upper, preferred_element_type=F32, precision=lax.Precision.HIGHEST)
    last = t_valid - 1
    band = min(L, MLSTM_BAND)
    nt = (((1,), (1,)), ((), ()))

    for h in range(H_A):
        cs = slice(h * DH_A, (h + 1) * DH_A)
        ks = slice(W_A + h * DH_A, W_A + (h + 1) * DH_A)
        q = _conv_silu(ext_s, cs, cw_ref[:, cs], cb_ref[:, cs], L)
        tick()
        k = _conv_silu(ext_s, ks, cw_ref[:, ks], cb_ref[:, ks], L) * (DH_A ** -0.5)
        tick()
        v = get_v(h)
        q16, k16, v16 = q.astype(BF16), k.astype(BF16), v.astype(BF16)
        bc = b_col[:, H_A + h:H_A + h + 1]
        igc = ig_col[:, h:h + 1]
        br = b_row[H_A + h:H_A + h + 1, :]
        igr = ig_row[h:h + 1, :]
        m_prev = m_s[h][:, 0:1]
        c_prev = c_s[h]
        n_prev = n_s[h]
        c16 = c_prev.astype(BF16)

        for r in range(L // band):
            rs = slice(r * band, (r + 1) * band)
            kw = (r + 1) * band
            ti = lax.broadcasted_iota(jnp.int32, (band, kw), 0) + r * band
            si = lax.broadcasted_iota(jnp.int32, (band, kw), 1)
            logw = jnp.where(ti >= si, bc[rs] - br[:, 0:kw] + igr[:, 0:kw], neg_inf)
            inter = bc[rs] + m_prev
            m_t = jnp.maximum(inter, jnp.max(logw, axis=1, keepdims=True))
            w_intra = jnp.exp(logw - m_t)
            w_inter = jnp.exp(inter - m_t)
            s = w_intra * lax.dot_general(q16[rs], k16[0:kw], nt, preferred_element_type=F32)
            num = (jnp.dot(s.astype(BF16), v16[0:kw], preferred_element_type=F32)
                   + w_inter * lax.dot_general(q16[rs], c16, nt, preferred_element_type=F32))
            den = (jnp.sum(s, axis=1, keepdims=True)
                   + w_inter * jnp.sum(q[rs] * n_prev, axis=1, keepdims=True))
            y_ref[0, rs, cs] = num / jnp.maximum(jnp.abs(den), jnp.exp(-m_t))
            if r == last // band:
                m_new = m_t[last - r * band:last - r * band + 1, :]
            tick()

        b_last = bc[last:last + 1, :]
        decay = jnp.exp(b_last + m_prev - m_new)
        w_s = jnp.exp(b_last - bc + igc - m_new)
        c_s[h] = decay * c_prev + lax.dot_general((w_s * v).astype(BF16), k16, (((0,), (0,)), ((), ())),
                                                  preferred_element_type=F32)
        n_s[h] = decay * n_prev + jnp.sum(w_s * k, axis=0, keepdims=True)
        m_s[h] = jnp.broadcast_to(m_new, (1, GATE_LANES))
        tick()

    ext_s[0:8, :] = ext_s[t_valid:t_valid + 8, :]


def _mlstm_state_io(c0_ref, n0_ref, m0_ref, tail0_ref, c_s, n_s, m_s, ext_s):
    @pl.when(pl.program_id(1) == 0)
    def _():
        c_s[...] = c0_ref[0]
        n_s[...] = n0_ref[0]
        m_s[...] = m0_ref[0]
        ext_s[0:8, :] = tail0_ref[0]


def _mlstm_state_out(c_out, n_out, m_out, tail_out, c_s, n_s, m_s, ext_s):
    @pl.when(pl.program_id(1) == pl.num_programs(1) - 1)
    def _():
        c_out[0, 0] = c_s[...]
        n_out[0] = n_s[...]
        m_out[0] = m_s[...]
        tail_out[0] = ext_s[0:8, :]


def _mlstm_kernel(p_ref, gc_ref, gr_ref, brow_ref, bcol_ref, cw_ref, cb_ref, c0_ref, n0_ref, m0_ref, tail0_ref,
                  *rest, L, t_valid, n_prev, n_seq):
    cprev_ref = rest[0] if n_prev else None
    y_ref, c_out, n_out, m_out, tail_out, c_s, n_s, m_s, ext_s = rest[1:] if n_prev else rest
    first = pl.program_id(1) == 0
    last = pl.program_id(1) == pl.num_programs(1) - 1
    for e in range(n_seq):
        ce, ne, me, xe = c_s.at[e], n_s.at[e], m_s.at[e], ext_s.at[e]

        @pl.when(first)
        def _(e=e, ce=ce, ne=ne, me=me, xe=xe):
            ce[...] = c0_ref[e]
            ne[...] = n0_ref[e]
            me[...] = m0_ref[e]
            xe[0:8, :] = tail0_ref[e]

        xe[8:8 + L, :] = p_ref[e, :, 0:2 * W_A]
        get_v = lambda h, e=e: p_ref[e, :, 2 * W_A + h * DH_A:2 * W_A + (h + 1) * DH_A]
        _mlstm_core(gc_ref[e] + brow_ref[...], gr_ref[e] + bcol_ref[...], get_v, xe, cw_ref, cb_ref,
                    ce, ne, me, y_ref.at[pl.ds(e, 1)], L, t_valid)

        @pl.when(last)
        def _(e=e, ce=ce, ne=ne, me=me, xe=xe):
            c_out[n_prev, e] = ce[...]
            n_out[e] = ne[...]
            m_out[e] = me[...]
            tail_out[e] = xe[0:8, :]

    if n_prev:
        @pl.when(last)
        def _():
            c_out[0:n_prev] = cprev_ref[...]


def _seq_a_kernel(x_ref, cos_ref, sin_ref, ng_ref, w_aq, w_ak, w_av, w_c0, w_c1, w_c2, wg_ref, brow_ref, bcol_ref,
                  cw_ref, cb_ref, c0_ref, n0_ref, m0_ref, tail0_ref, y_ref, pc_ref, c_out, n_out, m_out, tail_out,
                  c_s, n_s, m_s, ext_s, v_s, *, L):
    _mlstm_state_io(c0_ref, n0_ref, m0_ref, tail0_ref, c_s, n_s, m_s, ext_s)
    nt = (((1,), (1,)), ((), ()))
    x = x_ref[0]
    r = lax.rsqrt(jnp.mean(x * x, axis=-1, keepdims=True) + EPS)
    xn = (x * r * ng_ref[...]).astype(BF16)
    wg = wg_ref[...].astype(BF16)
    g_col = lax.dot_general(xn, wg, nt, preferred_element_type=F32) + brow_ref[...]
    g_row = lax.dot_general(wg[0:8, :], xn, nt, preferred_element_type=F32) + bcol_ref[...]
    proj = lambda w_ref: lax.dot_general(xn, w_ref[...], nt, preferred_element_type=F32)
    ext_s[8:8 + L, 0:W_A] = proj(w_aq)
    ext_s[8:8 + L, W_A:2 * W_A] = proj(w_ak)
    v_s[...] = proj(w_av)
    piece = 256
    todo = [(w_ref, j, lo) for j, w_ref in enumerate((w_c0, w_c1, w_c2)) for lo in range(0, WBLK, piece)]

    assert piece == WG_C

    def emit_piece():
        w_ref, j, lo = todo.pop(0)
        col = j * WBLK + lo
        y = lax.dot_general(xn, w_ref[lo:lo + piece, :], nt, preferred_element_type=F32)
        if col < 2 * W_C:
            cos = jnp.concatenate([cos_ref[0], cos_ref[0]], axis=1)
            sin = jnp.concatenate([sin_ref[0], sin_ref[0]], axis=1)
            y = _rope(y, cos, sin)
            if col < W_C:
                y = y * (DH_C ** -0.5)
        pc_ref[0, :, col:col + piece] = y

    side_work = lambda: emit_piece() if todo else None
    get_v = lambda h: v_s[:, h * DH_A:(h + 1) * DH_A]
    _mlstm_core(g_col, g_row, get_v, ext_s, cw_ref, cb_ref, c_s, n_s, m_s, y_ref, L, L, side_work)
    while todo:
        emit_piece()
    _mlstm_state_out(c_out, n_out, m_out, tail_out, c_s, n_s, m_s, ext_s)


def _seq_a(x, cos_t, sin_t, norm_g, w_main, w_in_t, bias_row, bias_col, conv_w, conv_b, c0, n0, m0, tail0, l, *, L):
    rows = x.shape[0]
    nch = rows // L
    x3 = x.reshape(nch, L, D_MODEL)
    gate_blk = 4 * W_A // GATE_LANES
    once = pl.Buffered(1)
    chunk = lambda b, c: (c, 0, 0)
    layer3 = lambda b, c: (l, 0, 0)
    wblk = lambda blk: pl.BlockSpec((None, WBLK, D_MODEL), lambda b, c, blk=blk: (l, blk, 0), pipeline_mode=once)
    st5 = lambda b, c: (0, 0, 0, 0, 0)
    st4 = lambda b, c: (0, 0, 0, 0)
    first4 = lambda b, c: (0, 0, 0, 0)
    first3 = lambda b, c: (0, 0, 0)
    y, pc, c1, n1, m1, tail1 = pl.pallas_call(
        functools.partial(_seq_a_kernel, L=L),
        grid=(1, nch),
        in_specs=[
            pl.BlockSpec((1, L, D_MODEL), chunk),
            pl.BlockSpec((1, L, 2 * DH_C), chunk),
            pl.BlockSpec((1, L, 2 * DH_C), chunk),
            pl.BlockSpec((None, 1, D_MODEL), layer3, pipeline_mode=once),
        ] + [wblk(b) for b in (0, 1, 2, BLK_C0, BLK_C0 + 1, BLK_C0 + 2)] + [
            pl.BlockSpec((None, GATE_LANES, D_MODEL), lambda b, c: (l, gate_blk, 0), pipeline_mode=once),
            pl.BlockSpec((None, 1, GATE_LANES), layer3, pipeline_mode=once),
            pl.BlockSpec((None, 8, 1), layer3, pipeline_mode=once),
            pl.BlockSpec((None, CONV_W, 2 * W_A), layer3, pipeline_mode=once),
            pl.BlockSpec((None, 1, 2 * W_A), layer3, pipeline_mode=once),
            pl.BlockSpec((None, 1, H_A, DH_A, DH_A), st5, pipeline_mode=once),
            pl.BlockSpec((None, 1, H_A, 1, DH_A), st5, pipeline_mode=once),
            pl.BlockSpec((None, 1, H_A, 1, GATE_LANES), st5, pipeline_mode=once),
            pl.BlockSpec((None, 1, 8, 2 * W_A), st4, pipeline_mode=once),
        ],
        out_specs=[
            pl.BlockSpec((1, L, W_A), chunk),
            pl.BlockSpec((1, L, 3 * WBLK), chunk),
            pl.BlockSpec((1, 1, H_A, DH_A, DH_A), lambda b, c: (0, 0, 0, 0, 0)),
            pl.BlockSpec((1, H_A, 1, DH_A), first4),
            pl.BlockSpec((1, H_A, 1, GATE_LANES), first4),
            pl.BlockSpec((1, 8, 2 * W_A), first3),
        ],
        out_shape=[
            jax.ShapeDtypeStruct((nch, L, W_A), F32),
            jax.ShapeDtypeStruct((nch, L, 3 * WBLK), F32),
            jax.ShapeDtypeStruct((1, 1, H_A, DH_A, DH_A), F32),
            jax.ShapeDtypeStruct((1, H_A, 1, DH_A), F32),
            jax.ShapeDtypeStruct((1, H_A, 1, GATE_LANES), F32),
            jax.ShapeDtypeStruct((1, 8, 2 * W_A), F32),
        ],
        scratch_shapes=[
            pltpu.VMEM((H_A, DH_A, DH_A), F32),
            pltpu.VMEM((H_A, 1, DH_A), F32),
            pltpu.VMEM((H_A, 1, GATE_LANES), F32),
            pltpu.VMEM((L + 8, 2 * W_A), F32),
            pltpu.VMEM((L, W_A), F32),
        ],
        compiler_params=_cparams(2),
        name="seq_a",
    )(x3, cos_t.reshape(nch, L, 2 * DH_C), sin_t.reshape(nch, L, 2 * DH_C), norm_g, *([w_main] * 6), w_in_t,
      bias_row, bias_col, conv_w, conv_b, c0, n0, m0, tail0)
    return (y.reshape(rows, W_A), pc.reshape(rows, 3 * WBLK), c1[0], n1[:, :, 0, :], m1[:, :, 0, 0],
            tail1[:, 8 - (CONV_W - 1):, :])


def _mlstm(p, gcol, grow, bias_row, bias_col, conv_w, conv_b, c0, n0, m0, tail0, l, ls, c_prev, *, L, t_valid):
    batch = c0.shape[1]
    rows = p.shape[0]
    nch = rows // (batch * L)
    n_prev = 0 if c_prev is None else c_prev.shape[0]
    n_seq = MLSTM_SEQS if (nch == 1 and batch % MLSTM_SEQS == 0 and L % GATE_LANES != 0) else 1
    p3 = p.reshape(batch * nch, L, N_P1)
    gc3 = gcol.reshape(batch * nch, L, GATE_LANES)
    tm = grow.shape[2]
    if L % GATE_LANES == 0:
        per = tm // L
        gr3, gr_index = grow, (lambda b, c: ((b * nch + c) // per, 0, (b * nch + c) % per))
    else:
        gr3 = grow.transpose(1, 0, 2).reshape(8, batch * nch, L).transpose(1, 0, 2)
        gr_index = lambda b, c: (b * nch + c, 0, 0)
    kern = functools.partial(_mlstm_kernel, L=L, t_valid=t_valid, n_prev=n_prev, n_seq=n_seq)
    chunk = lambda b, c: (b * nch + c, 0, 0)
    layer3 = lambda b, c: (l, 0, 0)
    stack5 = lambda b, c: (0, b, 0, 0, 0)
    per_b4 = lambda b, c: (b, 0, 0, 0)
    per_b3 = lambda b, c: (b, 0, 0)
    st5 = lambda b, c: (ls, b, 0, 0, 0)
    st4 = lambda b, c: (ls, b, 0, 0)
    y, c1, n1, m1, tail1 = pl.pallas_call(
        kern,
        grid=(batch // n_seq, nch),
        in_specs=[
            pl.BlockSpec((n_seq, L, 3 * W_A), chunk),
            pl.BlockSpec((n_seq, L, GATE_LANES), chunk),
            pl.BlockSpec((n_seq, 8, L), gr_index),
            pl.BlockSpec((None, 1, GATE_LANES), layer3),
            pl.BlockSpec((None, 8, 1), layer3),
            pl.BlockSpec((None, CONV_W, 2 * W_A), layer3),
            pl.BlockSpec((None, 1, 2 * W_A), layer3),
            pl.BlockSpec((None, n_seq, H_A, DH_A, DH_A), st5),
            pl.BlockSpec((None, n_seq, H_A, 1, DH_A), st5),
            pl.BlockSpec((None, n_seq, H_A, 1, GATE_LANES), st5),
            pl.BlockSpec((None, n_seq, 8, 2 * W_A), st4),
        ] + ([pl.BlockSpec((n_prev, n_seq, H_A, DH_A, DH_A), stack5)] if n_prev else []),
        out_specs=[
            pl.BlockSpec((n_seq, L, W_A), chunk),
            pl.BlockSpec((n_prev + 1, n_seq, H_A, DH_A, DH_A), stack5),
            pl.BlockSpec((n_seq, H_A, 1, DH_A), per_b4),
            pl.BlockSpec((n_seq, H_A, 1, GATE_LANES), per_b4),
            pl.BlockSpec((n_seq, 8, 2 * W_A), per_b3),
        ],
        out_shape=[
            jax.ShapeDtypeStruct((batch * nch, L, W_A), F32),
            jax.ShapeDtypeStruct((n_prev + 1, batch, H_A, DH_A, DH_A), F32),
            jax.ShapeDtypeStruct((batch, H_A, 1, DH_A), F32),
            jax.ShapeDtypeStruct((batch, H_A, 1, GATE_LANES), F32),
            jax.ShapeDtypeStruct((batch, 8, 2 * W_A), F32),
        ],
        scratch_shapes=[
            pltpu.VMEM((n_seq, H_A, DH_A, DH_A), F32),
            pltpu.VMEM((n_seq, H_A, 1, DH_A), F32),
            pltpu.VMEM((n_seq, H_A, 1, GATE_LANES), F32),
            pltpu.VMEM((n_seq, L + 8, 2 * W_A), F32),
        ],
        compiler_params=_cparams(2),
        name="mlstm",
    )(p3, gc3, gr3, bias_row, bias_col, conv_w, conv_b, c0, n0, m0, tail0, *([c_prev] if n_prev else []))
    return y.reshape(rows, W_A), c1, n1[:, :, 0, :], m1[:, :, 0, 0], tail1[:, 8 - (CONV_W - 1):, :]


def _rope(x, cos, sin):
    width = x.shape[1]
    lane = lax.broadcasted_iota(jnp.int32, (1, width), 1) % DH_C
    half = ROT_DIM // 2
    partner = jnp.where(lane < half, pltpu.roll(x, width - half, 1), pltpu.roll(x, half, 1))
    return x * cos + partner * sin


def _dil_attn_kernel(q_ref, k_ref, v_ref, o_ref, mx_ref, den_ref, qs, kbuf, vbuf, bias, *, d):
    i = pl.program_id(0)
    T = ATT_TILE
    HL = WG_C // 2
    nblk = T // SPAN_C
    nres = T // (SPAN_C * d)
    cur = (i % 2) * T
    prev = T - cur
    neg_inf = jnp.float32(-jnp.inf)

    def split(x):
        return x[:, 0:HL], x[:, HL:WG_C]

    def put(buf, lo, x):
        a, b = split(x)
        buf[0, pl.ds(lo, T), :] = a
        buf[1, pl.ds(lo, T), :] = b

    @pl.when(i == 0)
    def _():
        ii = lax.broadcasted_iota(jnp.int32, (SPAN_C, 2 * SPAN_C), 0)
        jj = lax.broadcasted_iota(jnp.int32, (SPAN_C, 2 * SPAN_C), 1)
        band = (jj >= ii) & (jj <= ii + SPAN_C)
        bias[1] = jnp.where(band, 0.0, neg_inf)
        bias[0] = jnp.where(band & (jj >= SPAN_C), 0.0, neg_inf)
        kbuf[:, T:2 * T, :] = jnp.zeros((2, T, HL), F32)
        vbuf[:, T:2 * T, :] = jnp.zeros((2, T, HL), F32)

    put(qs, 0, q_ref[...])
    put(kbuf, cur, k_ref[...])
    put(vbuf, cur, v_ref[...])

    lane_head = lax.broadcasted_iota(jnp.int32, (SPAN_C, WG_C), 1) // DH_C
    hm = [(lane_head == h).astype(F32) for h in range(HG_C)]

    def rows(start, size):
        return pl.ds(start, size) if d == 1 else pl.ds(start, size, stride=d)

    def take(buf, start, size):
        return jnp.concatenate([buf[0, rows(start, size), :], buf[1, rows(start, size), :]], axis=1)

    def pick(per_head):
        out = jnp.broadcast_to(per_head[0], (SPAN_C, WG_C))
        for h in range(1, HG_C):
            out = jnp.where(lane_head >= h, per_head[h], out)
        return out

    def body(blk, carry):
        r = blk % d
        c = blk // d
        qstart = r + SPAN_C * d * c
        in_tile = cur + qstart
        before = jnp.where(c > 0, in_tile - SPAN_C * d, prev + r + SPAN_C * d * (nres - 1))
        qb = take(qs, qstart, SPAN_C)
        kb = jnp.concatenate([take(kbuf, before, SPAN_C), take(kbuf, in_tile, SPAN_C)], axis=0)
        vb = jnp.concatenate([take(vbuf, before, SPAN_C), take(vbuf, in_tile, SPAN_C)], axis=0)
        has_prev = jnp.logical_or(i > 0, c > 0).astype(jnp.int32)
        kb16, vb16 = kb.astype(BF16), vb.astype(BF16)
        pvs, mxs, dens = [], [], []
        for h0 in range(0, HG_C, 2):
            q2 = jnp.concatenate([qb * hm[h0], qb * hm[h0 + 1]], axis=0)
            s = _dot_nt(q2, kb16).reshape(2, SPAN_C, 2 * SPAN_C) + bias[has_prev]
            mx = jnp.max(s, axis=2, keepdims=True)
            p = jnp.exp(s - mx)
            den = jnp.sum(p, axis=2, keepdims=True)
            pv = _dot(p.reshape(2 * SPAN_C, 2 * SPAN_C), vb16).reshape(2, SPAN_C, WG_C)
            pvs += [pv[0], pv[1]]
            mxs += [mx[0], mx[1]]
            dens += [den[0], den[1]]
        outs = (pick(pvs), pick(mxs), pick(dens))
        for ref, val in zip((o_ref, mx_ref, den_ref), outs):
            for half, part in enumerate(split(val)):
                ref[half, rows(qstart, SPAN_C), :] = part
        return carry

    lax.fori_loop(0, nblk, body, 0, unroll=16)


def _dil_attn(p, base, g, d):
    rows = p.shape[0]
    T = ATT_TILE
    kern = functools.partial(_dil_attn_kernel, d=d)
    col = lambda off: pl.BlockSpec((T, WG_C), lambda i, off=off: (i, (base + off - OFF_CQ) // WG_C + g))
    slab = pl.BlockSpec((2, T, WG_C // 2), lambda i: (0, i, 0))
    slab_shape = jax.ShapeDtypeStruct((2, rows, WG_C // 2), F32)
    return pl.pallas_call(
        kern,
        grid=(rows // T,),
        in_specs=[col(OFF_CQ), col(OFF_CK), col(OFF_CV)],
        out_specs=[slab, slab, slab],
        out_shape=[slab_shape, slab_shape, slab_shape],
        scratch_shapes=[pltpu.VMEM((2, T, WG_C // 2), F32), pltpu.VMEM((2, 2 * T, WG_C // 2), F32),
                        pltpu.VMEM((2, 2 * T, WG_C // 2), F32), pltpu.VMEM((2, SPAN_C, 2 * SPAN_C), F32)],
        compiler_params=_cparams(1),
        name="dil_attn_d%d" % d,
    )(p, p, p)


def _sample_attn_kernel(pc0, pc1, pc2, cos_ref, sin_ref, c0_ref, c1_ref, c2_ref,
                        o0, o1, o2, m0, m1, m2, d0, d1, d2, kr_ref, *, n_tok, n_seq):
    reps = W_C // (2 * DH_C)
    cos = jnp.concatenate([cos_ref[...]] * reps, axis=1)
    sin = jnp.concatenate([sin_ref[...]] * reps, axis=1)
    nrow = HG_C * SAMPLE_ROWS
    rowh = lax.broadcasted_iota(jnp.int32, (nrow, WG_C), 0) // SAMPLE_ROWS
    laneh = lax.broadcasted_iota(jnp.int32, (nrow, WG_C), 1) // DH_C
    hm = (rowh == laneh).astype(F32)
    tok_col = lax.broadcasted_iota(jnp.int32, (nrow, 1), 0) % SAMPLE_ROWS
    out_row = lax.broadcasted_iota(jnp.int32, (SAMPLE_ROWS, WG_C), 0)
    neg_inf = jnp.float32(-jnp.inf)
    crefs = (c0_ref, c1_ref, c2_ref)
    orefs = (o0, o1, o2)
    mrefs = (m0, m1, m2)
    drefs = (d0, d1, d2)
    for e in range(n_seq):
        rows_e = slice(e * SAMPLE_ROWS, (e + 1) * SAMPLE_ROWS)
        pc = jnp.concatenate([pc0[e], pc1[e], pc2[e]], axis=1)
        qr = _rope(pc[:, 0:W_C], cos, sin) * (DH_C ** -0.5)
        kr = _rope(pc[:, W_C:2 * W_C], cos, sin)
        v = pc[:, 2 * W_C:3 * W_C]
        kr_ref[e] = kr
        for g, (_, d) in enumerate(DIL_PATTERNS):
            gs = slice(g * WG_C, (g + 1) * WG_C)
            n_buf = SPAN_C * d
            kn = kr[:, gs]
            vn = v[:, gs]
            qm = jnp.concatenate([qr[:, gs]] * HG_C, axis=0) * hm
            s = _dot(qm, crefs[g][e, 0])
            tok = lax.broadcasted_iota(jnp.int32, (nrow, n_buf), 0) % SAMPLE_ROWS
            pos = lax.broadcasted_iota(jnp.int32, (nrow, n_buf), 1)
            attends = (pos >= tok) if d == 1 else ((pos & (d - 1)) == tok)
            s = jnp.where(attends | (tok >= n_tok), s, neg_inf)
            new_ok = [(u <= tok_col) if d == 1 else (u == tok_col) for u in range(n_tok)]
            s_new = [jnp.where(ok, jnp.sum(qm * kn[u:u + 1, :], axis=1, keepdims=True), neg_inf)
                     for u, ok in enumerate(new_ok)]
            mx = jnp.max(s, axis=1, keepdims=True)
            for sn in s_new:
                mx = jnp.maximum(mx, sn)
            p = jnp.exp(s - mx)
            den = jnp.sum(p, axis=1, keepdims=True)
            acc = _dot_nt(p, crefs[g][e, 1])
            for u, sn in enumerate(s_new):
                pn = jnp.exp(sn - mx)
                den = den + pn
                acc = acc + pn * vn[u:u + 1, :]
            stacked = (acc * hm, mx * hm, den * hm)
            for ref, val, pad in zip((orefs[g], mrefs[g], drefs[g]), stacked, (0.0, 0.0, 1.0)):
                v8 = val[0:SAMPLE_ROWS]
                for h in range(1, HG_C):
                    v8 = v8 + val[h * SAMPLE_ROWS:(h + 1) * SAMPLE_ROWS]
                v8 = jnp.where(out_row < n_tok, v8, pad)
                for half in range(2):
                    ref[half, rows_e, :] = v8[:, half * (WG_C // 2):(half + 1) * (WG_C // 2)]


def _cache_views(caches):
    views = []
    for (win, d), c in zip(DIL_PATTERNS, caches):
        depth, batch, n_buf = c.shape[:3]
        assert n_buf == SPAN_C * d
        views.append(jnp.transpose(c, (0, 1, 3, 4, 5, 2)).reshape(depth, batch, 2, WG_C, n_buf))
    return views


def _sample_attn(p, cos_s, sin_s, views, l, n_tok):
    batch = p.shape[0] // SAMPLE_ROWS
    n_seq = SAMPLE_ATT_SEQS if batch % SAMPLE_ATT_SEQS == 0 else 1
    p3 = p.reshape(batch, SAMPLE_ROWS, N_P1)
    pcol = lambda k: pl.BlockSpec((n_seq, SAMPLE_ROWS, 1024), lambda b, k=k: (b, 0, OFF_CQ // 1024 + k))
    tab = pl.BlockSpec((SAMPLE_ROWS, 2 * DH_C), lambda b: (0, 0))
    cache_spec = lambda d: pl.BlockSpec((None, n_seq, 2, WG_C, SPAN_C * d), lambda b: (l, b, 0, 0, 0))
    out = pl.BlockSpec((2, n_seq * SAMPLE_ROWS, WG_C // 2), lambda b: (0, b, 0))
    res = pl.pallas_call(
        functools.partial(_sample_attn_kernel, n_tok=n_tok, n_seq=n_seq),
        grid=(batch // n_seq,),
        in_specs=[pcol(0), pcol(1), pcol(2), tab, tab] + [cache_spec(d) for _, d in DIL_PATTERNS],
        out_specs=[out] * 9 + [pl.BlockSpec((n_seq, SAMPLE_ROWS, W_C), lambda b: (b, 0, 0))],
        out_shape=[jax.ShapeDtypeStruct((2, batch * SAMPLE_ROWS, WG_C // 2), F32)] * 9
        + [jax.ShapeDtypeStruct((batch, SAMPLE_ROWS, W_C), F32)],
        compiler_params=_cparams(1),
        name="sample_attn",
    )(p3, p3, p3, cos_s, sin_s, *views)
    return list(res[:9]), res[9].reshape(batch * SAMPLE_ROWS, W_C)


def _post_kernel(x_ref, h_ref, o0, o1, o2, m0, m1, m2, d0, d1, d2, cz_ref, ng_ref, w_ao, w_az, w_bu, w_bv, w_bz, w_ga, w_gb,
                 w_gc, lg_ref, lb_ref, wsp_ref, bsp_ref, wb_ref, wo_ref, fg_ref, out_ref, *rest, nchunks, final,
                 emit_vn):
    if emit_vn:
        vn_out, vn_s, yb_s = rest
    else:
        vn_s, yb_s = rest
    nt = (((1,), (1,)), ((), ()))
    x = x_ref[...]
    r = lax.rsqrt(jnp.mean(x * x, axis=-1, keepdims=True) + EPS)
    xn = (x * r * ng_ref[...]).astype(BF16)
    proj = lambda w_ref: lax.dot_general(xn, w_ref[...], nt, preferred_element_type=F32)
    branch = lambda y, lo, hi: jnp.dot(y.astype(BF16), wb_ref[lo:hi, :], preferred_element_type=F32)

    ya = h_ref[...] * _sigmoid(proj(w_ao)) * _silu(proj(w_az))
    acc = _sigmoid(proj(w_ga)) * branch(ya, 0, W_A)

    v = proj(w_bv)
    mu = jnp.mean(v, axis=-1, keepdims=True)
    var = jnp.mean(jnp.square(v - mu), axis=-1, keepdims=True)
    vn = (v - mu) * lax.rsqrt(var + EPS) * lg_ref[...] + lb_ref[...]
    if emit_vn:
        vn_out[...] = vn
    vn_s[...] = vn.astype(BF16)
    gate_b = proj(w_bu) * _silu(proj(w_bz))
    tri = (lax.broadcasted_iota(jnp.int32, (CHUNK_B, CHUNK_B), 0)
           >= lax.broadcasted_iota(jnp.int32, (CHUNK_B, CHUNK_B), 1))
    for g in range(G_B):
        wg = jnp.where(tri, wsp_ref[g], 0.0).astype(BF16)
        cs = slice(g * DG_B, (g + 1) * DG_B)
        for c in range(nchunks):
            rs = slice(c * CHUNK_B, (c + 1) * CHUNK_B)
            yb_s[rs, cs] = jnp.dot(wg, vn_s[rs, cs], preferred_element_type=F32) + bsp_ref[g]
    acc = acc + _sigmoid(proj(w_gb)) * branch(gate_b * yb_s[...], W_A, W_A + W_B)

    unslab = lambda ref: jnp.concatenate([ref[0], ref[1]], axis=1)
    ms = [unslab(m0), unslab(m1), unslab(m2)]
    mx = jnp.maximum(jnp.maximum(ms[0], ms[1]), ms[2])
    es = [jnp.exp(m - mx) for m in ms]
    inv_tot = 1.0 / (es[0] * unslab(d0) + es[1] * unslab(d1) + es[2] * unslab(d2))
    pc = None
    for g, o_ref in enumerate((o0, o1, o2)):
        gs = slice(g * WG_C, (g + 1) * WG_C)
        yc = unslab(o_ref) * (es[g] * inv_tot) * _silu(cz_ref[:, gs])
        t = branch(yc, W_A + W_B + g * WG_C, W_A + W_B + (g + 1) * WG_C)
        pc = t if pc is None else pc + t
    acc = acc + _sigmoid(proj(w_gc)) * pc

    out = x + jnp.dot(acc.astype(BF16), wo_ref[...], preferred_element_type=F32)
    if final:
        rr = lax.rsqrt(jnp.mean(out * out, axis=-1, keepdims=True) + EPS)
        out = out * rr * fg_ref[...]
    out_ref[...] = out


def _post(x, p1, base, h, att, norm_g, w_main, ln_g, ln_b, w_sp, b_sp_col, w_branch, w_out, final_g, l, final,
          emit_vn):
    rows = x.shape[0]
    tm = min(rows, POST_TM)
    once = pl.Buffered(1)
    row = lambda w: pl.BlockSpec((tm, w), lambda i: (i, 0))
    slab = pl.BlockSpec((2, tm, WG_C // 2), lambda i: (0, i, 0))
    wblk = lambda b: pl.BlockSpec((None, WBLK, D_MODEL), lambda i, b=b: (l, b, 0), pipeline_mode=once)
    vec = lambda w: pl.BlockSpec((None, 1, w), lambda i: (l, 0, 0), pipeline_mode=once)
    in_specs = (
        [row(D_MODEL), row(W_A)] + [slab] * 9
        + [pl.BlockSpec((tm, W_C), lambda i: (i, (base + OFF_CZ - OFF_CQ) // W_C)), vec(D_MODEL)]
        + [wblk(b) for b in (BLK_AO, BLK_AZ, BLK_BU, BLK_BV, BLK_BZ, BLK_GA, BLK_GB, BLK_GC)]
        + [vec(W_B), vec(W_B),
           pl.BlockSpec((None, G_B, CHUNK_B, CHUNK_B), lambda i: (l, 0, 0, 0), pipeline_mode=once),
           pl.BlockSpec((None, G_B, CHUNK_B, 1), lambda i: (l, 0, 0, 0), pipeline_mode=once),
           pl.BlockSpec((None, W_A + W_B + W_C, D_MODEL), lambda i: (l, 0, 0), pipeline_mode=once),
           pl.BlockSpec((None, D_MODEL, D_MODEL), lambda i: (l, 0, 0), pipeline_mode=once),
           pl.BlockSpec((1, D_MODEL), lambda i: (0, 0), pipeline_mode=once)])
    out_specs = [row(D_MODEL)]
    out_shape = [jax.ShapeDtypeStruct((rows, D_MODEL), F32)]
    if emit_vn:
        out_specs.append(row(W_B))
        out_shape.append(jax.ShapeDtypeStruct((rows, W_B), F32))
    res = pl.pallas_call(
        functools.partial(_post_kernel, nchunks=tm // CHUNK_B, final=final, emit_vn=emit_vn),
        grid=(rows // tm,),
        in_specs=in_specs,
        out_specs=out_specs,
        out_shape=out_shape,
        scratch_shapes=[pltpu.VMEM((tm, W_B), BF16), pltpu.VMEM((tm, W_B), F32)],
        compiler_params=_cparams(1),
        name="post",
    )(x, h, *att, p1, norm_g, *([w_main] * 8), ln_g, ln_b, w_sp, b_sp_col, w_branch, w_out, final_g)
    return (res[0], res[1]) if emit_vn else (res[0], None)


def _rope_tables(pos):
    half = ROT_DIM // 2
    inv = ROPE_THETA ** (-jnp.arange(half, dtype=F32) / half)
    ang = inv[:, None] * pos.astype(F32)[None, :]
    lane = jnp.arange(2 * DH_C) % DH_C
    freq = jnp.arange(half)[:, None] == (lane % half)[None, :]
    rot = (lane < ROT_DIM)[None, :]
    sel_cos = (freq & rot).astype(F32)
    sel_sin = sel_cos * jnp.where(lane < half, -1.0, 1.0)[None, :]
    spread = lambda t, sel: lax.dot_general(t, sel, (((0,), (0,)), ((), ())), precision=lax.Precision.HIGHEST)
    return spread(jnp.cos(ang), sel_cos) + (~rot).astype(F32), spread(jnp.sin(ang), sel_sin)


def _stack_kv(k_rows, v_rows):
    b, t, _ = k_rows.shape
    return jnp.stack([k_rows.reshape(b, t, HG_C, DH_C), v_rows.reshape(b, t, HG_C, DH_C)], axis=2)


def kernel(x_prompt, x_sample, state_C, state_n, state_m, state_conv, cache_kv_w128, cache_kv_w512, cache_kv_w2048,
           norm_g, w_in, b_igate, b_fgate, conv_w, conv_b, ln_v_g, ln_v_b, w_spatial, b_spatial, w_branch, w_out,
           final_norm_g):
    depth = w_in.shape[0]
    bp, seq, _ = x_prompt.shape
    bs, n_tok, _ = x_sample.shape
    assert bp == 1 and seq % ATT_TILE == 0 and n_tok <= SAMPLE_ROWS // 2 and n_tok >= CONV_W - 1
    caches = (cache_kv_w128, cache_kv_w512, cache_kv_w2048)
    pad_tok = SAMPLE_ROWS - n_tok
    rows_s = bs * SAMPLE_ROWS

    w_in_t = jnp.swapaxes(w_in, 1, 2)
    w_main = _pack_w_in(w_in_t)
    gate_bias = jnp.concatenate([b_igate, b_fgate], axis=1)
    bias_row = jnp.pad(gate_bias, ((0, 0), (0, GATE_LANES - 2 * H_A)))[:, None, :]
    bias_col = gate_bias[:, :, None]
    wb16 = w_branch.astype(BF16)
    wo16 = w_out.astype(BF16)
    b_sp_col = b_spatial[..., None]
    norm_g3, conv_b3 = norm_g[:, None, :], conv_b[:, None, :]
    ln_g3, ln_b3 = ln_v_g[:, None, :], ln_v_b[:, None, :]
    seqs_per_chunk = CHUNK_B // SAMPLE_ROWS
    w8 = jnp.pad(w_spatial[:, :, :n_tok, :n_tok], ((0, 0), (0, 0), (0, pad_tok), (0, pad_tok)))
    w_sp_s = jnp.einsum('ab,lgij->lgaibj', jnp.eye(seqs_per_chunk, dtype=F32), w8).reshape(
        depth, G_B, CHUNK_B, CHUNK_B)
    b_sp_s = jnp.tile(jnp.pad(b_spatial[:, :, :n_tok], ((0, 0), (0, 0), (0, pad_tok))),
                      (1, 1, seqs_per_chunk))[..., None]

    cos_p, sin_p = _rope_tables(jnp.arange(seq))
    cos_s, sin_s = _rope_tables(PAST_LEN + jnp.arange(SAMPLE_ROWS))

    hp = x_prompt.reshape(seq, D_MODEL)
    hs = jnp.pad(x_sample, ((0, 0), (0, pad_tok), (0, 0))).reshape(rows_s, D_MODEL)
    fg = final_norm_g[None, :]
    zeros_c = jnp.zeros((1, 1, H_A, DH_A, DH_A), F32)
    zeros_n = jnp.zeros((1, 1, H_A, 1, DH_A), F32)
    zeros_m = jnp.zeros((1, 1, H_A, 1, GATE_LANES), F32)
    zeros_tail = jnp.zeros((1, 1, 8, 2 * W_A), F32)
    n0_s = state_n[:, :, :, None, :]
    m0_s = jnp.broadcast_to(state_m[:, :, :, None, None], (depth, bs, H_A, 1, GATE_LANES))
    tail0_s = jnp.pad(state_conv, ((0, 0), (0, 0), (8 - (CONV_W - 1), 0), (0, 0)))
    views = _cache_views(caches)

    c_stack_s = None
    p_out = {k: [] for k in ('C', 'n', 'm', 'conv', 'kv0', 'kv1', 'kv2')}
    s_out = {k: [] for k in ('n', 'm', 'conv', 'chunk_v', 'kv0', 'kv1', 'kv2')}
    for l in range(depth):
        final = l == depth - 1

        ya, pp, c1, n1, m1, conv1 = _seq_a(hp, cos_p, sin_p, norm_g3, w_main, w_in_t, bias_row, bias_col, conv_w, conv_b3,
                                           zeros_c, zeros_n, zeros_m, zeros_tail, l, L=MLSTM_L)
        att = [[], [], []]
        for g, (win, d) in enumerate(DIL_PATTERNS):
            for dst, val in zip(att, _dil_attn(pp, 0, g, d)):
                dst.append(val)
            keep = min(win, seq)
            k_g = pp[seq - keep:, OFF_CK - OFF_CQ + g * WG_C:OFF_CK - OFF_CQ + (g + 1) * WG_C]
            v_g = pp[seq - keep:, OFF_CV - OFF_CQ + g * WG_C:OFF_CV - OFF_CQ + (g + 1) * WG_C]
            p_out['kv%d' % g].append(_stack_kv(k_g[None], v_g[None]))
        hp, _ = _post(hp, pp, 0, ya, att[0] + att[1] + att[2], norm_g3, w_main, ln_g3, ln_b3, w_spatial, b_sp_col, wb16, wo16, fg, l,
                      final, False)
        p_out['C'].append(c1)
        p_out['n'].append(n1)
        p_out['m'].append(m1)
        p_out['conv'].append(conv1)

        ps, gcol, grow = _inproj(hs, norm_g3, w_main, w_in_t, l)
        ya, c1, n1, m1, conv1 = _mlstm(ps, gcol, grow, bias_row, bias_col, conv_w, conv_b3,
                                       state_C, n0_s, m0_s, tail0_s, l, l, c_stack_s, L=SAMPLE_ROWS, t_valid=n_tok)
        c_stack_s = c1
        att_s, kr = _sample_attn(ps, cos_s, sin_s, views, l, n_tok)
        hs, vn = _post(hs, ps, OFF_CQ, ya, att_s, norm_g3, w_main, ln_g3, ln_b3, w_sp_s, b_sp_s, wb16, wo16, fg, l,
                       final, True)
        s_out['n'].append(n1)
        s_out['m'].append(m1)
        s_out['conv'].append(conv1)
        s_out['chunk_v'].append(vn.reshape(bs, SAMPLE_ROWS, W_B)[:, :n_tok])
        kr3 = kr.reshape(bs, SAMPLE_ROWS, W_C)[:, :n_tok]
        v3 = ps[:, OFF_CV:OFF_CV + W_C].reshape(bs, SAMPLE_ROWS, W_C)[:, :n_tok]
        for g in range(len(DIL_PATTERNS)):
            gs = slice(g * WG_C, (g + 1) * WG_C)
            s_out['kv%d' % g].append(_stack_kv(kr3[:, :, gs], v3[:, :, gs]))

    stk = lambda d, k: jnp.stack(d[k], axis=0)
    y_prompt = hp.reshape(bp, seq, D_MODEL)
    y_sample = hs.reshape(bs, SAMPLE_ROWS, D_MODEL)[:, :n_tok]
    return (y_prompt, y_sample,
            stk(p_out, 'C'), stk(p_out, 'n'), stk(p_out, 'm'), stk(p_out, 'conv'),
            stk(p_out, 'kv0'), stk(p_out, 'kv1'), stk(p_out, 'kv2'),
            c_stack_s, stk(s_out, 'n'), stk(s_out, 'm'), stk(s_out, 'conv'), stk(s_out, 'chunk_v'),
            stk(s_out, 'kv0'), stk(s_out, 'kv1'), stk(s_out, 'kv2'))
```

```python
import functools

import jax
import jax.numpy as jnp
from jax import lax
from jax.experimental import pallas as pl
from jax.experimental.pallas import tpu as pltpu

F32 = jnp.float32
BF16 = jnp.bfloat16

D_MODEL = 1024
H_A = 4
DH_A = 256
W_A = H_A * DH_A
CONV_W = 4
G_B = 4
CHUNK_B = 128
W_B = 1024
DG_B = W_B // G_B
DIL_PATTERNS = ((128, 1), (512, 4), (2048, 16))
HG_C = 4
DH_C = 64
WG_C = HG_C * DH_C
W_C = len(DIL_PATTERNS) * WG_C
SPAN_C = 128
ROT_DIM = DH_C // 4
ROPE_THETA = 500000.0
EPS = 1e-6
PAST_LEN = 16384

N_PACK = 14336
WBLK = 1024
BLK_AO, BLK_AZ, BLK_BU, BLK_BV, BLK_BZ, BLK_C0, BLK_GA, BLK_GB, BLK_GC = 3, 4, 5, 6, 7, 8, 11, 12, 13
N_P1 = 6144
OFF_CQ, OFF_CK, OFF_CV, OFF_CZ = 3072, 3840, 4608, 5376
GATE_LANES = 128

INPROJ_TM = 2048
POST_TM = 512
MLSTM_BAND = 128
MLSTM_L = 256
SAMPLE_ROWS = 8
MLSTM_SEQS = 4
SAMPLE_ATT_SEQS = 4
ATT_TILE = 2048
VMEM_LIMIT = 56 * 1024 * 1024


def _cparams(n_axes):
    return pltpu.CompilerParams(dimension_semantics=("arbitrary",) * n_axes, vmem_limit_bytes=VMEM_LIMIT)


def _dot(a, b):
    return jnp.dot(a.astype(BF16), b.astype(BF16), preferred_element_type=F32)


def _dot_nt(a, b):
    return lax.dot_general(a.astype(BF16), b.astype(BF16), (((1,), (1,)), ((), ())), preferred_element_type=F32)


def _sigmoid(x):
    return 0.5 * jnp.tanh(0.5 * x) + 0.5


def _silu(x):
    return x * _sigmoid(x)


def _inproj_kernel(x_ref, g_ref, w_ref, wg_ref, p_ref, gc_ref, gr_ref, xn_ref):
    nt = (((1,), (1,)), ((), ()))

    @pl.when(pl.program_id(1) == 0)
    def _():
        x = x_ref[...]
        r = lax.rsqrt(jnp.mean(x * x, axis=-1, keepdims=True) + EPS)
        xn = (x * r * g_ref[...]).astype(BF16)
        xn_ref[...] = xn
        wg = wg_ref[...].astype(BF16)
        gc_ref[...] = lax.dot_general(xn, wg, nt, preferred_element_type=F32)
        gr_ref[...] = lax.dot_general(wg[0:8, :], xn, nt, preferred_element_type=F32)

    p_ref[...] = lax.dot_general(xn_ref[...], w_ref[...], nt, preferred_element_type=F32)


def _inproj(x, norm_g, w_main, w_in_t, l):
    rows = x.shape[0]
    gate_blk = 4 * W_A // GATE_LANES
    tm = min(rows, INPROJ_TM)
    tn = WBLK
    n_a = 3 * W_A // tn
    wblk = lambda i, j: (l, jnp.where(j < n_a, j, j + (BLK_C0 - n_a)), 0)
    return pl.pallas_call(
        _inproj_kernel,
        grid=(rows // tm, N_P1 // tn),
        in_specs=[
            pl.BlockSpec((tm, D_MODEL), lambda i, j: (i, 0)),
            pl.BlockSpec((None, 1, D_MODEL), lambda i, j: (l, 0, 0)),
            pl.BlockSpec((None, tn, D_MODEL), wblk),
            pl.BlockSpec((None, GATE_LANES, D_MODEL), lambda i, j: (l, gate_blk, 0)),
        ],
        out_specs=[
            pl.BlockSpec((tm, tn), lambda i, j: (i, j)),
            pl.BlockSpec((tm, GATE_LANES), lambda i, j: (i, 0)),
            pl.BlockSpec((None, 8, tm), lambda i, j: (i, 0, 0)),
        ],
        out_shape=[
            jax.ShapeDtypeStruct((rows, N_P1), F32),
            jax.ShapeDtypeStruct((rows, GATE_LANES), F32),
            jax.ShapeDtypeStruct((rows // tm, 8, tm), F32),
        ],
        scratch_shapes=[pltpu.VMEM((tm, D_MODEL), BF16)],
        compiler_params=_cparams(2),
        name="inproj",
    )(x, norm_g, w_main, w_in_t)


def _pack_kernel(a_ref, b_ref, w_ref, *, first_shifted):
    j = pl.program_id(1)

    @pl.when(j < first_shifted)
    def _():
        w_ref[...] = a_ref[...].astype(BF16)

    @pl.when(j >= first_shifted)
    def _():
        w_ref[...] = jnp.concatenate([a_ref[2 * H_A:, :], b_ref[...]], axis=0).astype(BF16)


def _pack_w_in(w_in_t):
    depth = w_in_t.shape[0]
    tn = 2048
    gate_off = 4 * W_A
    assert gate_off % tn == 0 and w_in_t.shape[1] == N_PACK + 2 * H_A and 2 * H_A == 8
    kern = functools.partial(_pack_kernel, first_shifted=gate_off // tn)
    return pl.pallas_call(
        kern,
        grid=(depth, N_PACK // tn),
        in_specs=[
            pl.BlockSpec((None, tn, D_MODEL), lambda l, j: (l, j, 0)),
            pl.BlockSpec((None, 8, D_MODEL), lambda l, j: (l, (j + 1) * (tn // 8), 0)),
        ],
        out_specs=pl.BlockSpec((None, tn, D_MODEL), lambda l, j: (l, j, 0)),
        out_shape=jax.ShapeDtypeStruct((depth, N_PACK, D_MODEL), BF16),
        compiler_params=_cparams(2),
        name="pack_w_in",
    )(w_in_t, w_in_t)


def _conv_silu(ext_ref, cols, w, b, L):
    y = b
    for back in range(CONV_W):
        y = y + ext_ref[8 - back:8 - back + L, cols] * w[CONV_W - 1 - back:CONV_W - back]
    return _silu(y)


def _mlstm_core(g_col, g_row, get_v, ext_s, cw_ref, cb_ref, c_s, n_s, m_s, y_ref, L, t_valid, side_work=None):
    tick = side_work if side_work is not None else (lambda: None)
    neg_inf = jnp.float32(-jnp.inf)
    ig_col, lf_col = g_col, jax.nn.log_sigmoid(g_col)
    ig_row, lf_row = g_row, jax.nn.log_sigmoid(g_row)
    if t_valid < L:
        vc = lax.broadcasted_iota(jnp.int32, (L, GATE_LANES), 0) < t_valid
        vr = lax.broadcasted_iota(jnp.int32, (8, L), 1) < t_valid
        ig_col, lf_col = jnp.where(vc, ig_col, neg_inf), jnp.where(vc, lf_col, 0.0)
        ig_row, lf_row = jnp.where(vr, ig_row, neg_inf), jnp.where(vr, lf_row, 0.0)
    ti = lax.broadcasted_iota(jnp.int32, (L, L), 0)
    si = lax.broadcasted_iota(jnp.int32, (L, L), 1)
    lower = (ti >= si).astype(F32)
    upper = (ti <= si).astype(F32)
    if L <= 16:
        b_col = lower[:, 0:1] * lf_col[0:1, :]
        b_row = lf_row[:, 0:1] * upper[0:1, :]
        for s in range(1, L):
            b_col = b_col + lower[:, s:s + 1] * lf_col[s:s + 1, :]
            b_row = b_row + lf_row[:, s:s + 1] * upper[s:s + 1, :]
    else:
        b_col = jnp.dot(lower, lf_col, preferred_element_type=F32, precision=lax.Precision.HIGHEST)
        b_row = jnp.dot(lf_row, upper, preferred_element_type=F32, precision=lax.Precision.HIGHEST)
    last = t_valid - 1
    band = min(L, MLSTM_BAND)
    nt = (((1,), (1,)), ((), ()))

    for h in range(H_A):
        cs = slice(h * DH_A, (h + 1) * DH_A)
        ks = slice(W_A + h * DH_A, W_A + (h + 1) * DH_A)
        q = _conv_silu(ext_s, cs, cw_ref[:, cs], cb_ref[:, cs], L)
        tick()
        k = _conv_silu(ext_s, ks, cw_ref[:, ks], cb_ref[:, ks], L) * (DH_A ** -0.5)
        tick()
        v = get_v(h)
        q16, k16, v16 = q.astype(BF16), k.astype(BF16), v.astype(BF16)
        bc = b_col[:, H_A + h:H_A + h + 1]
        igc = ig_col[:, h:h + 1]
        br = b_row[H_A + h:H_A + h + 1, :]
        igr = ig_row[h:h + 1, :]
        m_prev = m_s[h][:, 0:1]
        c_prev = c_s[h]
        n_prev = n_s[h]
        c16 = c_prev.astype(BF16)

        for r in range(L // band):
            rs = slice(r * band, (r + 1) * band)
            kw = (r + 1) * band
            ti = lax.broadcasted_iota(jnp.int32, (band, kw), 0) + r * band
            si = lax.broadcasted_iota(jnp.int32, (band, kw), 1)
            logw = jnp.where(ti >= si, bc[rs] - br[:, 0:kw] + igr[:, 0:kw], neg_inf)
            inter = bc[rs] + m_prev
            m_t = jnp.maximum(inter, jnp.max(logw, axis=1, keepdims=True))
            w_intra = jnp.exp(logw - m_t)
            w_inter = jnp.exp(inter - m_t)
            s = w_intra * lax.dot_general(q16[rs], k16[0:kw], nt, preferred_element_type=F32)
            num = (jnp.dot(s.astype(BF16), v16[0:kw], preferred_element_type=F32)
                   + w_inter * lax.dot_general(q16[rs], c16, nt, preferred_element_type=F32))
            den = (jnp.sum(s, axis=1, keepdims=True)
                   + w_inter * jnp.sum(q[rs] * n_prev, axis=1, keepdims=True))
            y_ref[0, rs, cs] = num / jnp.maximum(jnp.abs(den), jnp.exp(-m_t))
            if r == last // band:
                m_new = m_t[last - r * band:last - r * band + 1, :]
            tick()

        b_last = bc[last:last + 1, :]
        decay = jnp.exp(b_last + m_prev - m_new)
        w_s = jnp.exp(b_last - bc + igc - m_new)
        c_s[h] = decay * c_prev + lax.dot_general((w_s * v).astype(BF16), k16, (((0,), (0,)), ((), ())),
                                                  preferred_element_type=F32)
        n_s[h] = decay * n_prev + jnp.sum(w_s * k, axis=0, keepdims=True)
        m_s[h] = jnp.broadcast_to(m_new, (1, GATE_LANES))
        tick()

    ext_s[0:8, :] = ext_s[t_valid:t_valid + 8, :]


def _mlstm_state_io(c0_ref, n0_ref, m0_ref, tail0_ref, c_s, n_s, m_s, ext_s):
    @pl.when(pl.program_id(1) == 0)
    def _():
        c_s[...] = c0_ref[0]
        n_s[...] = n0_ref[0]
        m_s[...] = m0_ref[0]
        ext_s[0:8, :] = tail0_ref[0]


def _mlstm_state_out(c_out, n_out, m_out, tail_out, c_s, n_s, m_s, ext_s):
    @pl.when(pl.program_id(1) == pl.num_programs(1) - 1)
    def _():
        c_out[0, 0] = c_s[...]
        n_out[0] = n_s[...]
        m_out[0] = m_s[...]
        tail_out[0] = ext_s[0:8, :]


def _mlstm_kernel(p_ref, gc_ref, gr_ref, brow_ref, bcol_ref, cw_ref, cb_ref, c0_ref, n0_ref, m0_ref, tail0_ref,
                  *rest, L, t_valid, n_prev, n_seq):
    cprev_ref = rest[0] if n_prev else None
    y_ref, c_out, n_out, m_out, tail_out, c_s, n_s, m_s, ext_s = rest[1:] if n_prev else rest
    first = pl.program_id(1) == 0
    last = pl.program_id(1) == pl.num_programs(1) - 1
    for e in range(n_seq):
        ce, ne, me, xe = c_s.at[e], n_s.at[e], m_s.at[e], ext_s.at[e]

        @pl.when(first)
        def _(e=e, ce=ce, ne=ne, me=me, xe=xe):
            ce[...] = c0_ref[e]
            ne[...] = n0_ref[e]
            me[...] = m0_ref[e]
            xe[0:8, :] = tail0_ref[e]

        xe[8:8 + L, :] = p_ref[e, :, 0:2 * W_A]
        get_v = lambda h, e=e: p_ref[e, :, 2 * W_A + h * DH_A:2 * W_A + (h + 1) * DH_A]
        _mlstm_core(gc_ref[e] + brow_ref[...], gr_ref[e] + bcol_ref[...], get_v, xe, cw_ref, cb_ref,
                    ce, ne, me, y_ref.at[pl.ds(e, 1)], L, t_valid)

        @pl.when(last)
        def _(e=e, ce=ce, ne=ne, me=me, xe=xe):
            c_out[n_prev, e] = ce[...]
            n_out[e] = ne[...]
            m_out[e] = me[...]
            tail_out[e] = xe[0:8, :]

    if n_prev:
        @pl.when(last)
        def _():
            c_out[0:n_prev] = cprev_ref[...]


def _seq_a_kernel(x_ref, cos_ref, sin_ref, ng_ref, w_aq, w_ak, w_av, w_c0, w_c1, w_c2, wg_ref, brow_ref, bcol_ref,
                  cw_ref, cb_ref, c0_ref, n0_ref, m0_ref, tail0_ref, y_ref, pc_ref, c_out, n_out, m_out, tail_out,
                  c_s, n_s, m_s, ext_s, v_s, *, L):
    _mlstm_state_io(c0_ref, n0_ref, m0_ref, tail0_ref, c_s, n_s, m_s, ext_s)
    nt = (((1,), (1,)), ((), ()))
    x = x_ref[0]
    r = lax.rsqrt(jnp.mean(x * x, axis=-1, keepdims=True) + EPS)
    xn = (x * r * ng_ref[...]).astype(BF16)
    wg = wg_ref[...].astype(BF16)
    g_col = lax.dot_general(xn, wg, nt, preferred_element_type=F32) + brow_ref[...]
    g_row = lax.dot_general(wg[0:8, :], xn, nt, preferred_element_type=F32) + bcol_ref[...]
    proj = lambda w_ref: lax.dot_general(xn, w_ref[...], nt, preferred_element_type=F32)
    ext_s[8:8 + L, 0:W_A] = proj(w_aq)
    ext_s[8:8 + L, W_A:2 * W_A] = proj(w_ak)
    v_s[...] = proj(w_av)
    piece = 256
    todo = [(w_ref, j, lo) for j, w_ref in enumerate((w_c0, w_c1, w_c2)) for lo in range(0, WBLK, piece)]

    assert piece == WG_C

    def emit_piece():
        w_ref, j, lo = todo.pop(0)
        col = j * WBLK + lo
        y = lax.dot_general(xn, w_ref[lo:lo + piece, :], nt, preferred_element_type=F32)
        if col < 2 * W_C:
            cos = jnp.concatenate([cos_ref[0], cos_ref[0]], axis=1)
            sin = jnp.concatenate([sin_ref[0], sin_ref[0]], axis=1)
            y = _rope(y, cos, sin)
            if col < W_C:
                y = y * (DH_C ** -0.5)
        pc_ref[0, :, col:col + piece] = y

    side_work = lambda: emit_piece() if todo else None
    get_v = lambda h: v_s[:, h * DH_A:(h + 1) * DH_A]
    _mlstm_core(g_col, g_row, get_v, ext_s, cw_ref, cb_ref, c_s, n_s, m_s, y_ref, L, L, side_work)
    while todo:
        emit_piece()
    _mlstm_state_out(c_out, n_out, m_out, tail_out, c_s, n_s, m_s, ext_s)


def _seq_a(x, cos_t, sin_t, norm_g, w_main, w_in_t, bias_row, bias_col, conv_w, conv_b, c0, n0, m0, tail0, l, *, L):
    rows = x.shape[0]
    nch = rows // L
    x3 = x.reshape(nch, L, D_MODEL)
    gate_blk = 4 * W_A // GATE_LANES
    once = pl.Buffered(1)
    chunk = lambda b, c: (c, 0, 0)
    layer3 = lambda b, c: (l, 0, 0)
    wblk = lambda blk: pl.BlockSpec((None, WBLK, D_MODEL), lambda b, c, blk=blk: (l, blk, 0), pipeline_mode=once)
    st5 = lambda b, c: (0, 0, 0, 0, 0)
    st4 = lambda b, c: (0, 0, 0, 0)
    first4 = lambda b, c: (0, 0, 0, 0)
    first3 = lambda b, c: (0, 0, 0)
    y, pc, c1, n1, m1, tail1 = pl.pallas_call(
        functools.partial(_seq_a_kernel, L=L),
        grid=(1, nch),
        in_specs=[
            pl.BlockSpec((1, L, D_MODEL), chunk),
            pl.BlockSpec((1, L, 2 * DH_C), chunk),
            pl.BlockSpec((1, L, 2 * DH_C), chunk),
            pl.BlockSpec((None, 1, D_MODEL), layer3, pipeline_mode=once),
        ] + [wblk(b) for b in (0, 1, 2, BLK_C0, BLK_C0 + 1, BLK_C0 + 2)] + [
            pl.BlockSpec((None, GATE_LANES, D_MODEL), lambda b, c: (l, gate_blk, 0), pipeline_mode=once),
            pl.BlockSpec((None, 1, GATE_LANES), layer3, pipeline_mode=once),
            pl.BlockSpec((None, 8, 1), layer3, pipeline_mode=once),
            pl.BlockSpec((None, CONV_W, 2 * W_A), layer3, pipeline_mode=once),
            pl.BlockSpec((None, 1, 2 * W_A), layer3, pipeline_mode=once),
            pl.BlockSpec((None, 1, H_A, DH_A, DH_A), st5, pipeline_mode=once),
            pl.BlockSpec((None, 1, H_A, 1, DH_A), st5, pipeline_mode=once),
            pl.BlockSpec((None, 1, H_A, 1, GATE_LANES), st5, pipeline_mode=once),
            pl.BlockSpec((None, 1, 8, 2 * W_A), st4, pipeline_mode=once),
        ],
        out_specs=[
            pl.BlockSpec((1, L, W_A), chunk),
            pl.BlockSpec((1, L, 3 * WBLK), chunk),
            pl.BlockSpec((1, 1, H_A, DH_A, DH_A), lambda b, c: (0, 0, 0, 0, 0)),
            pl.BlockSpec((1, H_A, 1, DH_A), first4),
            pl.BlockSpec((1, H_A, 1, GATE_LANES), first4),
            pl.BlockSpec((1, 8, 2 * W_A), first3),
        ],
        out_shape=[
            jax.ShapeDtypeStruct((nch, L, W_A), F32),
            jax.ShapeDtypeStruct((nch, L, 3 * WBLK), F32),
            jax.ShapeDtypeStruct((1, 1, H_A, DH_A, DH_A), F32),
            jax.ShapeDtypeStruct((1, H_A, 1, DH_A), F32),
            jax.ShapeDtypeStruct((1, H_A, 1, GATE_LANES), F32),
            jax.ShapeDtypeStruct((1, 8, 2 * W_A), F32),
        ],
        scratch_shapes=[
            pltpu.VMEM((H_A, DH_A, DH_A), F32),
            pltpu.VMEM((H_A, 1, DH_A), F32),
            pltpu.VMEM((H_A, 1, GATE_LANES), F32),
            pltpu.VMEM((L + 8, 2 * W_A), F32),
            pltpu.VMEM((L, W_A), F32),
        ],
        compiler_params=_cparams(2),
        name="seq_a",
    )(x3, cos_t.reshape(nch, L, 2 * DH_C), sin_t.reshape(nch, L, 2 * DH_C), norm_g, *([w_main] * 6), w_in_t,
      bias_row, bias_col, conv_w, conv_b, c0, n0, m0, tail0)
    return (y.reshape(rows, W_A), pc.reshape(rows, 3 * WBLK), c1[0], n1[:, :, 0, :], m1[:, :, 0, 0],
            tail1[:, 8 - (CONV_W - 1):, :])


def _mlstm(p, gcol, grow, bias_row, bias_col, conv_w, conv_b, c0, n0, m0, tail0, l, ls, c_prev, *, L, t_valid):
    batch = c0.shape[1]
    rows = p.shape[0]
    nch = rows // (batch * L)
    n_prev = 0 if c_prev is None else c_prev.shape[0]
    n_seq = MLSTM_SEQS if (nch == 1 and batch % MLSTM_SEQS == 0 and L % GATE_LANES != 0) else 1
    p3 = p.reshape(batch * nch, L, N_P1)
    gc3 = gcol.reshape(batch * nch, L, GATE_LANES)
    tm = grow.shape[2]
    if L % GATE_LANES == 0:
        per = tm // L
        gr3, gr_index = grow, (lambda b, c: ((b * nch + c) // per, 0, (b * nch + c) % per))
    else:
        gr3 = grow.transpose(1, 0, 2).reshape(8, batch * nch, L).transpose(1, 0, 2)
        gr_index = lambda b, c: (b * nch + c, 0, 0)
    kern = functools.partial(_mlstm_kernel, L=L, t_valid=t_valid, n_prev=n_prev, n_seq=n_seq)
    chunk = lambda b, c: (b * nch + c, 0, 0)
    layer3 = lambda b, c: (l, 0, 0)
    stack5 = lambda b, c: (0, b, 0, 0, 0)
    per_b4 = lambda b, c: (b, 0, 0, 0)
    per_b3 = lambda b, c: (b, 0, 0)
    st5 = lambda b, c: (ls, b, 0, 0, 0)
    st4 = lambda b, c: (ls, b, 0, 0)
    y, c1, n1, m1, tail1 = pl.pallas_call(
        kern,
        grid=(batch // n_seq, nch),
        in_specs=[
            pl.BlockSpec((n_seq, L, 3 * W_A), chunk),
            pl.BlockSpec((n_seq, L, GATE_LANES), chunk),
            pl.BlockSpec((n_seq, 8, L), gr_index),
            pl.BlockSpec((None, 1, GATE_LANES), layer3),
            pl.BlockSpec((None, 8, 1), layer3),
            pl.BlockSpec((None, CONV_W, 2 * W_A), layer3),
            pl.BlockSpec((None, 1, 2 * W_A), layer3),
            pl.BlockSpec((None, n_seq, H_A, DH_A, DH_A), st5),
            pl.BlockSpec((None, n_seq, H_A, 1, DH_A), st5),
            pl.BlockSpec((None, n_seq, H_A, 1, GATE_LANES), st5),
            pl.BlockSpec((None, n_seq, 8, 2 * W_A), st4),
        ] + ([pl.BlockSpec((n_prev, n_seq, H_A, DH_A, DH_A), stack5)] if n_prev else []),
        out_specs=[
            pl.BlockSpec((n_seq, L, W_A), chunk),
            pl.BlockSpec((n_prev + 1, n_seq, H_A, DH_A, DH_A), stack5),
            pl.BlockSpec((n_seq, H_A, 1, DH_A), per_b4),
            pl.BlockSpec((n_seq, H_A, 1, GATE_LANES), per_b4),
            pl.BlockSpec((n_seq, 8, 2 * W_A), per_b3),
        ],
        out_shape=[
            jax.ShapeDtypeStruct((batch * nch, L, W_A), F32),
            jax.ShapeDtypeStruct((n_prev + 1, batch, H_A, DH_A, DH_A), F32),
            jax.ShapeDtypeStruct((batch, H_A, 1, DH_A), F32),
            jax.ShapeDtypeStruct((batch, H_A, 1, GATE_LANES), F32),
            jax.ShapeDtypeStruct((batch, 8, 2 * W_A), F32),
        ],
        scratch_shapes=[
            pltpu.VMEM((n_seq, H_A, DH_A, DH_A), F32),
            pltpu.VMEM((n_seq, H_A, 1, DH_A), F32),
            pltpu.VMEM((n_seq, H_A, 1, GATE_LANES), F32),
            pltpu.VMEM((n_seq, L + 8, 2 * W_A), F32),
        ],
        compiler_params=_cparams(2),
        name="mlstm",
    )(p3, gc3, gr3, bias_row, bias_col, conv_w, conv_b, c0, n0, m0, tail0, *([c_prev] if n_prev else []))
    return y.reshape(rows, W_A), c1, n1[:, :, 0, :], m1[:, :, 0, 0], tail1[:, 8 - (CONV_W - 1):, :]


def _rope(x, cos, sin):
    width = x.shape[1]
    lane = lax.broadcasted_iota(jnp.int32, (1, width), 1) % DH_C
    half = ROT_DIM // 2
    partner = jnp.where(lane < half, pltpu.roll(x, width - half, 1), pltpu.roll(x, half, 1))
    return x * cos + partner * sin


def _dil_attn_kernel(q_ref, k_ref, v_ref, o_ref, mx_ref, den_ref, qs, kbuf, vbuf, bias, *, d):
    i = pl.program_id(0)
    T = ATT_TILE
    HL = WG_C // 2
    nblk = T // SPAN_C
    nres = T // (SPAN_C * d)
    cur = (i % 2) * T
    prev = T - cur
    neg_inf = jnp.float32(-jnp.inf)

    def split(x):
        return x[:, 0:HL], x[:, HL:WG_C]

    def put(buf, lo, x):
        a, b = split(x)
        buf[0, pl.ds(lo, T), :] = a
        buf[1, pl.ds(lo, T), :] = b

    @pl.when(i == 0)
    def _():
        ii = lax.broadcasted_iota(jnp.int32, (SPAN_C, 2 * SPAN_C), 0)
        jj = lax.broadcasted_iota(jnp.int32, (SPAN_C, 2 * SPAN_C), 1)
        band = (jj >= ii) & (jj <= ii + SPAN_C)
        bias[1] = jnp.where(band, 0.0, neg_inf)
        bias[0] = jnp.where(band & (jj >= SPAN_C), 0.0, neg_inf)
        kbuf[:, T:2 * T, :] = jnp.zeros((2, T, HL), F32)
        vbuf[:, T:2 * T, :] = jnp.zeros((2, T, HL), F32)

    put(qs, 0, q_ref[...])
    put(kbuf, cur, k_ref[...])
    put(vbuf, cur, v_ref[...])

    lane_head = lax.broadcasted_iota(jnp.int32, (SPAN_C, WG_C), 1) // DH_C
    hm = [(lane_head == h).astype(F32) for h in range(HG_C)]

    def rows(start, size):
        return pl.ds(start, size) if d == 1 else pl.ds(start, size, stride=d)

    def take(buf, start, size):
        return jnp.concatenate([buf[0, rows(start, size), :], buf[1, rows(start, size), :]], axis=1)

    def pick(per_head):
        out = jnp.broadcast_to(per_head[0], (SPAN_C, WG_C))
        for h in range(1, HG_C):
            out = jnp.where(lane_head >= h, per_head[h], out)
        return out

    def body(blk, carry):
        r = blk % d
        c = blk // d
        qstart = r + SPAN_C * d * c
        in_tile = cur + qstart
        before = jnp.where(c > 0, in_tile - SPAN_C * d, prev + r + SPAN_C * d * (nres - 1))
        qb = take(qs, qstart, SPAN_C)
        kb = jnp.concatenate([take(kbuf, before, SPAN_C), take(kbuf, in_tile, SPAN_C)], axis=0)
        vb = jnp.concatenate([take(vbuf, before, SPAN_C), take(vbuf, in_tile, SPAN_C)], axis=0)
        has_prev = jnp.logical_or(i > 0, c > 0).astype(jnp.int32)
        kb16, vb16 = kb.astype(BF16), vb.astype(BF16)
        pvs, mxs, dens = [], [], []
        for h0 in range(0, HG_C, 2):
            q2 = jnp.concatenate([qb * hm[h0], qb * hm[h0 + 1]], axis=0)
            s = _dot_nt(q2, kb16).reshape(2, SPAN_C, 2 * SPAN_C) + bias[has_prev]
            mx = jnp.max(s, axis=2, keepdims=True)
            p = jnp.exp(s - mx)
            den = jnp.sum(p, axis=2, keepdims=True)
            pv = _dot(p.reshape(2 * SPAN_C, 2 * SPAN_C), vb16).reshape(2, SPAN_C, WG_C)
            pvs += [pv[0], pv[1]]
            mxs += [mx[0], mx[1]]
            dens += [den[0], den[1]]
        outs = (pick(pvs), pick(mxs), pick(dens))
        for ref, val in zip((o_ref, mx_ref, den_ref), outs):
            for half, part in enumerate(split(val)):
                ref[half, rows(qstart, SPAN_C), :] = part
        return carry

    lax.fori_loop(0, nblk, body, 0, unroll=16)


def _dil_attn(p, base, g, d):
    rows = p.shape[0]
    T = ATT_TILE
    kern = functools.partial(_dil_attn_kernel, d=d)
    col = lambda off: pl.BlockSpec((T, WG_C), lambda i, off=off: (i, (base + off - OFF_CQ) // WG_C + g))
    slab = pl.BlockSpec((2, T, WG_C // 2), lambda i: (0, i, 0))
    slab_shape = jax.ShapeDtypeStruct((2, rows, WG_C // 2), F32)
    return pl.pallas_call(
        kern,
        grid=(rows // T,),
        in_specs=[col(OFF_CQ), col(OFF_CK), col(OFF_CV)],
        out_specs=[slab, slab, slab],
        out_shape=[slab_shape, slab_shape, slab_shape],
        scratch_shapes=[pltpu.VMEM((2, T, WG_C // 2), F32), pltpu.VMEM((2, 2 * T, WG_C // 2), F32),
                        pltpu.VMEM((2, 2 * T, WG_C // 2), F32), pltpu.VMEM((2, SPAN_C, 2 * SPAN_C), F32)],
        compiler_params=_cparams(1),
        name="dil_attn_d%d" % d,
    )(p, p, p)


def _sample_attn_kernel(pc0, pc1, pc2, cos_ref, sin_ref, c0_ref, c1_ref, c2_ref,
                        o0, o1, o2, m0, m1, m2, d0, d1, d2, kr_ref, *, n_tok, n_seq):
    reps = W_C // (2 * DH_C)
    cos = jnp.concatenate([cos_ref[...]] * reps, axis=1)
    sin = jnp.concatenate([sin_ref[...]] * reps, axis=1)
    nrow = HG_C * SAMPLE_ROWS
    rowh = lax.broadcasted_iota(jnp.int32, (nrow, WG_C), 0) // SAMPLE_ROWS
    laneh = lax.broadcasted_iota(jnp.int32, (nrow, WG_C), 1) // DH_C
    hm = (rowh == laneh).astype(F32)
    tok_col = lax.broadcasted_iota(jnp.int32, (nrow, 1), 0) % SAMPLE_ROWS
    out_row = lax.broadcasted_iota(jnp.int32, (SAMPLE_ROWS, WG_C), 0)
    neg_inf = jnp.float32(-jnp.inf)
    crefs = (c0_ref, c1_ref, c2_ref)
    orefs = (o0, o1, o2)
    mrefs = (m0, m1, m2)
    drefs = (d0, d1, d2)
    for e in range(n_seq):
        rows_e = slice(e * SAMPLE_ROWS, (e + 1) * SAMPLE_ROWS)
        pc = jnp.concatenate([pc0[e], pc1[e], pc2[e]], axis=1)
        qr = _rope(pc[:, 0:W_C], cos, sin) * (DH_C ** -0.5)
        kr = _rope(pc[:, W_C:2 * W_C], cos, sin)
        v = pc[:, 2 * W_C:3 * W_C]
        kr_ref[e] = kr
        for g, (_, d) in enumerate(DIL_PATTERNS):
            gs = slice(g * WG_C, (g + 1) * WG_C)
            n_buf = SPAN_C * d
            kn = kr[:, gs]
            vn = v[:, gs]
            qm = jnp.concatenate([qr[:, gs]] * HG_C, axis=0) * hm
            s = _dot(qm, crefs[g][e, 0])
            tok = lax.broadcasted_iota(jnp.int32, (nrow, n_buf), 0) % SAMPLE_ROWS
            pos = lax.broadcasted_iota(jnp.int32, (nrow, n_buf), 1)
            attends = (pos >= tok) if d == 1 else ((pos & (d - 1)) == tok)
            s = jnp.where(attends | (tok >= n_tok), s, neg_inf)
            new_ok = [(u <= tok_col) if d == 1 else (u == tok_col) for u in range(n_tok)]
            s_new = [jnp.where(ok, jnp.sum(qm * kn[u:u + 1, :], axis=1, keepdims=True), neg_inf)
                     for u, ok in enumerate(new_ok)]
            mx = jnp.max(s, axis=1, keepdims=True)
            for sn in s_new:
                mx = jnp.maximum(mx, sn)
            p = jnp.exp(s - mx)
            den = jnp.sum(p, axis=1, keepdims=True)
            acc = _dot_nt(p, crefs[g][e, 1])
            for u, sn in enumerate(s_new):
                pn = jnp.exp(sn - mx)
                den = den + pn
                acc = acc + pn * vn[u:u + 1, :]
            stacked = (acc * hm, mx * hm, den * hm)
            for ref, val, pad in zip((orefs[g], mrefs[g], drefs[g]), stacked, (0.0, 0.0, 1.0)):
                v8 = val[0:SAMPLE_ROWS]
                for h in range(1, HG_C):
                    v8 = v8 + val[h * SAMPLE_ROWS:(h + 1) * SAMPLE_ROWS]
                v8 = jnp.where(out_row < n_tok, v8, pad)
                for half in range(2):
                    ref[half, rows_e, :] = v8[:, half * (WG_C // 2):(half + 1) * (WG_C // 2)]


def _cache_views(caches):
    views = []
    for (win, d), c in zip(DIL_PATTERNS, caches):
        depth, batch, n_buf = c.shape[:3]
        assert n_buf == SPAN_C * d
        views.append(jnp.transpose(c, (0, 1, 3, 4, 5, 2)).reshape(depth, batch, 2, WG_C, n_buf))
    return views


def _sample_attn(p, cos_s, sin_s, views, l, n_tok):
    batch = p.shape[0] // SAMPLE_ROWS
    n_seq = SAMPLE_ATT_SEQS if batch % SAMPLE_ATT_SEQS == 0 else 1
    p3 = p.reshape(batch, SAMPLE_ROWS, N_P1)
    pcol = lambda k: pl.BlockSpec((n_seq, SAMPLE_ROWS, WBLK), lambda b, k=k: (b, 0, OFF_CQ // WBLK + k))
    tab = pl.BlockSpec((SAMPLE_ROWS, 2 * DH_C), lambda b: (0, 0))
    cache_spec = lambda d: pl.BlockSpec((None, n_seq, 2, WG_C, SPAN_C * d), lambda b: (l, b, 0, 0, 0))
    out = pl.BlockSpec((2, n_seq * SAMPLE_ROWS, WG_C // 2), lambda b: (0, b, 0))
    res = pl.pallas_call(
        functools.partial(_sample_attn_kernel, n_tok=n_tok, n_seq=n_seq),
        grid=(batch // n_seq,),
        in_specs=[pcol(0), pcol(1), pcol(2), tab, tab] + [cache_spec(d) for _, d in DIL_PATTERNS],
        out_specs=[out] * 9 + [pl.BlockSpec((n_seq, SAMPLE_ROWS, W_C), lambda b: (b, 0, 0))],
        out_shape=[jax.ShapeDtypeStruct((2, batch * SAMPLE_ROWS, WG_C // 2), F32)] * 9
        + [jax.ShapeDtypeStruct((batch, SAMPLE_ROWS, W_C), F32)],
        compiler_params=_cparams(1),
        name="sample_attn",
    )(p3, p3, p3, cos_s, sin_s, *views)
    return list(res[:9]), res[9].reshape(batch * SAMPLE_ROWS, W_C)


def _post_kernel(x_ref, h_ref, o0, o1, o2, m0, m1, m2, d0, d1, d2, cz_ref, ng_ref, w_ao, w_az, w_bu, w_bv, w_bz, w_ga, w_gb,
                 w_gc, lg_ref, lb_ref, wsp_ref, bsp_ref, wb_ref, wo_ref, fg_ref, out_ref, *rest, nchunks, final,
                 emit_vn):
    if emit_vn:
        vn_out, vn_s, yb_s = rest
    else:
        vn_s, yb_s = rest
    nt = (((1,), (1,)), ((), ()))
    x = x_ref[...]
    r = lax.rsqrt(jnp.mean(x * x, axis=-1, keepdims=True) + EPS)
    xn = (x * r * ng_ref[...]).astype(BF16)
    proj = lambda w_ref: lax.dot_general(xn, w_ref[...], nt, preferred_element_type=F32)
    branch = lambda y, lo, hi: jnp.dot(y.astype(BF16), wb_ref[lo:hi, :], preferred_element_type=F32)

    ya = h_ref[...] * _sigmoid(proj(w_ao)) * _silu(proj(w_az))
    acc = _sigmoid(proj(w_ga)) * branch(ya, 0, W_A)

    v = proj(w_bv)
    mu = jnp.mean(v, axis=-1, keepdims=True)
    var = jnp.mean(jnp.square(v - mu), axis=-1, keepdims=True)
    vn = (v - mu) * lax.rsqrt(var + EPS) * lg_ref[...] + lb_ref[...]
    if emit_vn:
        vn_out[...] = vn
    vn_s[...] = vn.astype(BF16)
    gate_b = proj(w_bu) * _silu(proj(w_bz))
    tri = (lax.broadcasted_iota(jnp.int32, (CHUNK_B, CHUNK_B), 0)
           >= lax.broadcasted_iota(jnp.int32, (CHUNK_B, CHUNK_B), 1))
    for g in range(G_B):
        wg = jnp.where(tri, wsp_ref[g], 0.0).astype(BF16)
        cs = slice(g * DG_B, (g + 1) * DG_B)
        for c in range(nchunks):
            rs = slice(c * CHUNK_B, (c + 1) * CHUNK_B)
            yb_s[rs, cs] = jnp.dot(wg, vn_s[rs, cs], preferred_element_type=F32) + bsp_ref[g]
    acc = acc + _sigmoid(proj(w_gb)) * branch(gate_b * yb_s[...], W_A, W_A + W_B)

    unslab = lambda ref: jnp.concatenate([ref[0], ref[1]], axis=1)
    ms = [unslab(m0), unslab(m1), unslab(m2)]
    mx = jnp.maximum(jnp.maximum(ms[0], ms[1]), ms[2])
    es = [jnp.exp(m - mx) for m in ms]
    inv_tot = 1.0 / (es[0] * unslab(d0) + es[1] * unslab(d1) + es[2] * unslab(d2))
    pc = None
    for g, o_ref in enumerate((o0, o1, o2)):
        gs = slice(g * WG_C, (g + 1) * WG_C)
        yc = unslab(o_ref) * (es[g] * inv_tot) * _silu(cz_ref[:, gs])
        t = branch(yc, W_A + W_B + g * WG_C, W_A + W_B + (g + 1) * WG_C)
        pc = t if pc is None else pc + t
    acc = acc + _sigmoid(proj(w_gc)) * pc

    out = x + jnp.dot(acc.astype(BF16), wo_ref[...], preferred_element_type=F32)
    if final:
        rr = lax.rsqrt(jnp.mean(out * out, axis=-1, keepdims=True) + EPS)
        out = out * rr * fg_ref[...]
    out_ref[...] = out


def _post(x, p1, base, h, att, norm_g, w_main, ln_g, ln_b, w_sp, b_sp_col, w_branch, w_out, final_g, l, final,
          emit_vn):
    rows = x.shape[0]
    tm = min(rows, POST_TM)
    once = pl.Buffered(1)
    row = lambda w: pl.BlockSpec((tm, w), lambda i: (i, 0))
    slab = pl.BlockSpec((2, tm, WG_C // 2), lambda i: (0, i, 0))
    wblk = lambda b: pl.BlockSpec((None, WBLK, D_MODEL), lambda i, b=b: (l, b, 0), pipeline_mode=once)
    vec = lambda w: pl.BlockSpec((None, 1, w), lambda i: (l, 0, 0), pipeline_mode=once)
    in_specs = (
        [row(D_MODEL), row(W_A)] + [slab] * 9
        + [pl.BlockSpec((tm, W_C), lambda i: (i, (base + OFF_CZ - OFF_CQ) // W_C)), vec(D_MODEL)]
        + [wblk(b) for b in (BLK_AO, BLK_AZ, BLK_BU, BLK_BV, BLK_BZ, BLK_GA, BLK_GB, BLK_GC)]
        + [vec(W_B), vec(W_B),
           pl.BlockSpec((None, G_B, CHUNK_B, CHUNK_B), lambda i: (l, 0, 0, 0), pipeline_mode=once),
           pl.BlockSpec((None, G_B, CHUNK_B, 1), lambda i: (l, 0, 0, 0), pipeline_mode=once),
           pl.BlockSpec((None, W_A + W_B + W_C, D_MODEL), lambda i: (l, 0, 0), pipeline_mode=once),
           pl.BlockSpec((None, D_MODEL, D_MODEL), lambda i: (l, 0, 0), pipeline_mode=once),
           pl.BlockSpec((1, D_MODEL), lambda i: (0, 0), pipeline_mode=once)])
    out_specs = [row(D_MODEL)]
    out_shape = [jax.ShapeDtypeStruct((rows, D_MODEL), F32)]
    if emit_vn:
        out_specs.append(row(W_B))
        out_shape.append(jax.ShapeDtypeStruct((rows, W_B), F32))
    res = pl.pallas_call(
        functools.partial(_post_kernel, nchunks=tm // CHUNK_B, final=final, emit_vn=emit_vn),
        grid=(rows // tm,),
        in_specs=in_specs,
        out_specs=out_specs,
        out_shape=out_shape,
        scratch_shapes=[pltpu.VMEM((tm, W_B), BF16), pltpu.VMEM((tm, W_B), F32)],
        compiler_params=_cparams(1),
        name="post",
    )(x, h, *att, p1, norm_g, *([w_main] * 8), ln_g, ln_b, w_sp, b_sp_col, w_branch, w_out, final_g)
    return (res[0], res[1]) if emit_vn else (res[0], None)


def _rope_tables(pos):
    half = ROT_DIM // 2
    inv = ROPE_THETA ** (-jnp.arange(half, dtype=F32) / half)
    ang = inv[:, None] * pos.astype(F32)[None, :]
    lane = jnp.arange(2 * DH_C) % DH_C
    freq = jnp.arange(half)[:, None] == (lane % half)[None, :]
    rot = (lane < ROT_DIM)[None, :]
    sel_cos = (freq & rot).astype(F32)
    sel_sin = sel_cos * jnp.where(lane < half, -1.0, 1.0)[None, :]
    spread = lambda t, sel: lax.dot_general(t, sel, (((0,), (0,)), ((), ())), precision=lax.Precision.HIGHEST)
    return spread(jnp.cos(ang), sel_cos) + (~rot).astype(F32), spread(jnp.sin(ang), sel_sin)


def _stack_kv(k_rows, v_rows):
    b, t, _ = k_rows.shape
    return jnp.stack([k_rows.reshape(b, t, HG_C, DH_C), v_rows.reshape(b, t, HG_C, DH_C)], axis=2)


def kernel(x_prompt, x_sample, state_C, state_n, state_m, state_conv, cache_kv_w128, cache_kv_w512, cache_kv_w2048,
           norm_g, w_in, b_igate, b_fgate, conv_w, conv_b, ln_v_g, ln_v_b, w_spatial, b_spatial, w_branch, w_out,
           final_norm_g):
    depth = w_in.shape[0]
    bp, seq, _ = x_prompt.shape
    bs, n_tok, _ = x_sample.shape
    assert bp == 1 and seq % ATT_TILE == 0 and n_tok <= SAMPLE_ROWS // 2 and n_tok >= CONV_W - 1
    caches = (cache_kv_w128, cache_kv_w512, cache_kv_w2048)
    pad_tok = SAMPLE_ROWS - n_tok
    rows_s = bs * SAMPLE_ROWS

    w_in_t = jnp.swapaxes(w_in, 1, 2)
    w_main = _pack_w_in(w_in_t)
    gate_bias = jnp.concatenate([b_igate, b_fgate], axis=1)
    bias_row = jnp.pad(gate_bias, ((0, 0), (0, GATE_LANES - 2 * H_A)))[:, None, :]
    bias_col = gate_bias[:, :, None]
    wb16 = w_branch.astype(BF16)
    wo16 = w_out.astype(BF16)
    b_sp_col = b_spatial[..., None]
    norm_g3, conv_b3 = norm_g[:, None, :], conv_b[:, None, :]
    ln_g3, ln_b3 = ln_v_g[:, None, :], ln_v_b[:, None, :]
    seqs_per_chunk = CHUNK_B // SAMPLE_ROWS
    w8 = jnp.pad(w_spatial[:, :, :n_tok, :n_tok], ((0, 0), (0, 0), (0, pad_tok), (0, pad_tok)))
    w_sp_s = jnp.einsum('ab,lgij->lgaibj', jnp.eye(seqs_per_chunk, dtype=F32), w8).reshape(
        depth, G_B, CHUNK_B, CHUNK_B)
    b_sp_s = jnp.tile(jnp.pad(b_spatial[:, :, :n_tok], ((0, 0), (0, 0), (0, pad_tok))),
                      (1, 1, seqs_per_chunk))[..., None]

    cos_p, sin_p = _rope_tables(jnp.arange(seq))
    cos_s, sin_s = _rope_tables(PAST_LEN + jnp.arange(SAMPLE_ROWS))

    hp = x_prompt.reshape(seq, D_MODEL)
    hs = jnp.pad(x_sample, ((0, 0), (0, pad_tok), (0, 0))).reshape(rows_s, D_MODEL)
    fg = final_norm_g[None, :]
    zeros_c = jnp.zeros((1, 1, H_A, DH_A, DH_A), F32)
    zeros_n = jnp.zeros((1, 1, H_A, 1, DH_A), F32)
    zeros_m = jnp.zeros((1, 1, H_A, 1, GATE_LANES), F32)
    zeros_tail = jnp.zeros((1, 1, 8, 2 * W_A), F32)
    n0_s = state_n[:, :, :, None, :]
    m0_s = jnp.broadcast_to(state_m[:, :, :, None, None], (depth, bs, H_A, 1, GATE_LANES))
    tail0_s = jnp.pad(state_conv, ((0, 0), (0, 0), (8 - (CONV_W - 1), 0), (0, 0)))
    views = _cache_views(caches)

    c_stack_s = None
    p_out = {k: [] for k in ('C', 'n', 'm', 'conv', 'kv0', 'kv1', 'kv2')}
    s_out = {k: [] for k in ('n', 'm', 'conv', 'chunk_v', 'kv0', 'kv1', 'kv2')}
    for l in range(depth):
        final = l == depth - 1

        ya, pp, c1, n1, m1, conv1 = _seq_a(hp, cos_p, sin_p, norm_g3, w_main, w_in_t, bias_row, bias_col, conv_w, conv_b3,
                                           zeros_c, zeros_n, zeros_m, zeros_tail, l, L=MLSTM_L)
        att = [[], [], []]
        for g, (win, d) in enumerate(DIL_PATTERNS):
            for dst, val in zip(att, _dil_attn(pp, 0, g, d)):
                dst.append(val)
            keep = min(win, seq)
            k_g = pp[seq - keep:, OFF_CK - OFF_CQ + g * WG_C:OFF_CK - OFF_CQ + (g + 1) * WG_C]
            v_g = pp[seq - keep:, OFF_CV - OFF_CQ + g * WG_C:OFF_CV - OFF_CQ + (g + 1) * WG_C]
            p_out['kv%d' % g].append(_stack_kv(k_g[None], v_g[None]))
        hp, _ = _post(hp, pp, 0, ya, att[0] + att[1] + att[2], norm_g3, w_main, ln_g3, ln_b3, w_spatial, b_sp_col, wb16, wo16, fg, l,
                      final, False)
        p_out['C'].append(c1)
        p_out['n'].append(n1)
        p_out['m'].append(m1)
        p_out['conv'].append(conv1)

        ps, gcol, grow = _inproj(hs, norm_g3, w_main, w_in_t, l)
        ya, c1, n1, m1, conv1 = _mlstm(ps, gcol, grow, bias_row, bias_col, conv_w, conv_b3,
                                       state_C, n0_s, m0_s, tail0_s, l, l, c_stack_s, L=SAMPLE_ROWS, t_valid=n_tok)
        c_stack_s = c1
        att_s, kr = _sample_attn(ps, cos_s, sin_s, views, l, n_tok)
        hs, vn = _post(hs, ps, OFF_CQ, ya, att_s, norm_g3, w_main, ln_g3, ln_b3, w_sp_s, b_sp_s, wb16, wo16, fg, l,
                       final, True)
        s_out['n'].append(n1)
        s_out['m'].append(m1)
        s_out['conv'].append(conv1)
        s_out['chunk_v'].append(vn.reshape(bs, SAMPLE_ROWS, W_B)[:, :n_tok])
        kr3 = kr.reshape(bs, SAMPLE_ROWS, W_C)[:, :n_tok]
        v3 = ps[:, OFF_CV:OFF_CV + W_C].reshape(bs, SAMPLE_ROWS, W_C)[:, :n_tok]
        for g in range(len(DIL_PATTERNS)):
            gs = slice(g * WG_C, (g + 1) * WG_C)
            s_out['kv%d' % g].append(_stack_kv(kr3[:, :, gs], v3[:, :, gs]))

    stk = lambda d, k: jnp.stack(d[k], axis=0)
    y_prompt = hp.reshape(bp, seq, D_MODEL)
    y_sample = hs.reshape(bs, SAMPLE_ROWS, D_MODEL)[:, :n_tok]
    return (y_prompt, y_sample,
            stk(p_out, 'C'), stk(p_out, 'n'), stk(p_out, 'm'), stk(p_out, 'conv'),
            stk(p_out, 'kv0'), stk(p_out, 'kv1'), stk(p_out, 'kv2'),
            c_stack_s, stk(s_out, 'n'), stk(s_out, 'm'), stk(s_out, 'conv'), stk(s_out, 'chunk_v'),
            stk(s_out, 'kv0'), stk(s_out, 'kv1'), stk(s_out, 'kv2'))
```

```python
import functools

import jax
import jax.numpy as jnp
from jax import lax
from jax.experimental import pallas as pl
from jax.experimental.pallas import tpu as pltpu

F32 = jnp.float32
BF16 = jnp.bfloat16

D_MODEL = 1024
H_A = 4
DH_A = 256
W_A = H_A * DH_A
CONV_W = 4
G_B = 4
CHUNK_B = 128
W_B = 1024
DG_B = W_B // G_B
DIL_PATTERNS = ((128, 1), (512, 4), (2048, 16))
HG_C = 4
DH_C = 64
WG_C = HG_C * DH_C
W_C = len(DIL_PATTERNS) * WG_C
SPAN_C = 128
ROT_DIM = DH_C // 4
ROPE_THETA = 500000.0
EPS = 1e-6
PAST_LEN = 16384

N_PACK = 14336
WBLK = 1024
BLK_AO, BLK_AZ, BLK_BU, BLK_BV, BLK_BZ, BLK_C0, BLK_GA, BLK_GB, BLK_GC = 3, 4, 5, 6, 7, 8, 11, 12, 13
N_P1 = 6144
OFF_CQ, OFF_CK, OFF_CV, OFF_CZ = 3072, 3840, 4608, 5376
GATE_LANES = 128

INPROJ_TM = 2048
POST_TM = 512
MLSTM_BAND = 128
MLSTM_L = 256
SAMPLE_ROWS = 8
MLSTM_SEQS = 4
SAMPLE_ATT_SEQS = 4
N_SLABS = 3 * W_C // 128
ATT_TILE = 2048
VMEM_LIMIT = 56 * 1024 * 1024


def _cparams(n_axes):
    return pltpu.CompilerParams(dimension_semantics=("arbitrary",) * n_axes, vmem_limit_bytes=VMEM_LIMIT)


def _dot(a, b):
    return jnp.dot(a.astype(BF16), b.astype(BF16), preferred_element_type=F32)


def _dot_nt(a, b):
    return lax.dot_general(a.astype(BF16), b.astype(BF16), (((1,), (1,)), ((), ())), preferred_element_type=F32)


def _sigmoid(x):
    return 0.5 * jnp.tanh(0.5 * x) + 0.5


def _silu(x):
    return x * _sigmoid(x)


def _inproj_kernel(x_ref, g_ref, w_ref, wg_ref, p_ref, gc_ref, gr_ref, xn_ref):
    nt = (((1,), (1,)), ((), ()))

    @pl.when(pl.program_id(1) == 0)
    def _():
        x = x_ref[...]
        r = lax.rsqrt(jnp.mean(x * x, axis=-1, keepdims=True) + EPS)
        xn = (x * r * g_ref[...]).astype(BF16)
        xn_ref[...] = xn
        wg = wg_ref[...].astype(BF16)
        gc_ref[...] = lax.dot_general(xn, wg, nt, preferred_element_type=F32)
        gr_ref[...] = lax.dot_general(wg[0:8, :], xn, nt, preferred_element_type=F32)

    p_ref[...] = lax.dot_general(xn_ref[...], w_ref[...], nt, preferred_element_type=F32)


def _inproj(x, norm_g, w_main, w_in_t, l):
    rows = x.shape[0]
    gate_blk = 4 * W_A // GATE_LANES
    tm = min(rows, INPROJ_TM)
    tn = WBLK
    n_a = 3 * W_A // tn
    wblk = lambda i, j: (l, jnp.where(j < n_a, j, j + (BLK_C0 - n_a)), 0)
    return pl.pallas_call(
        _inproj_kernel,
        grid=(rows // tm, N_P1 // tn),
        in_specs=[
            pl.BlockSpec((tm, D_MODEL), lambda i, j: (i, 0)),
            pl.BlockSpec((None, 1, D_MODEL), lambda i, j: (l, 0, 0)),
            pl.BlockSpec((None, tn, D_MODEL), wblk),
            pl.BlockSpec((None, GATE_LANES, D_MODEL), lambda i, j: (l, gate_blk, 0)),
        ],
        out_specs=[
            pl.BlockSpec((tm, tn), lambda i, j: (i, j)),
            pl.BlockSpec((tm, GATE_LANES), lambda i, j: (i, 0)),
            pl.BlockSpec((None, 8, tm), lambda i, j: (i, 0, 0)),
        ],
        out_shape=[
            jax.ShapeDtypeStruct((rows, N_P1), F32),
            jax.ShapeDtypeStruct((rows, GATE_LANES), F32),
            jax.ShapeDtypeStruct((rows // tm, 8, tm), F32),
        ],
        scratch_shapes=[pltpu.VMEM((tm, D_MODEL), BF16)],
        compiler_params=_cparams(2),
        name="inproj",
    )(x, norm_g, w_main, w_in_t)


def _pack_kernel(a_ref, b_ref, w_ref, *, first_shifted):
    j = pl.program_id(1)

    @pl.when(j < first_shifted)
    def _():
        w_ref[...] = a_ref[...].astype(BF16)

    @pl.when(j >= first_shifted)
    def _():
        w_ref[...] = jnp.concatenate([a_ref[2 * H_A:, :], b_ref[...]], axis=0).astype(BF16)


def _pack_w_in(w_in_t):
    depth = w_in_t.shape[0]
    tn = 2048
    gate_off = 4 * W_A
    assert gate_off % tn == 0 and w_in_t.shape[1] == N_PACK + 2 * H_A and 2 * H_A == 8
    kern = functools.partial(_pack_kernel, first_shifted=gate_off // tn)
    return pl.pallas_call(
        kern,
        grid=(depth, N_PACK // tn),
        in_specs=[
            pl.BlockSpec((None, tn, D_MODEL), lambda l, j: (l, j, 0)),
            pl.BlockSpec((None, 8, D_MODEL), lambda l, j: (l, (j + 1) * (tn // 8), 0)),
        ],
        out_specs=pl.BlockSpec((None, tn, D_MODEL), lambda l, j: (l, j, 0)),
        out_shape=jax.ShapeDtypeStruct((depth, N_PACK, D_MODEL), BF16),
        compiler_params=_cparams(2),
        name="pack_w_in",
    )(w_in_t, w_in_t)


def _conv_silu(ext_ref, cols, w, b, L):
    y = b
    for back in range(CONV_W):
        y = y + ext_ref[8 - back:8 - back + L, cols] * w[CONV_W - 1 - back:CONV_W - back]
    return _silu(y)


def _mlstm_core(g_col, g_row, get_v, ext_s, cw_ref, cb_ref, c_s, n_s, m_s, y_ref, L, t_valid, side_work=None):
    tick = side_work if side_work is not None else (lambda: None)
    neg_inf = jnp.float32(-jnp.inf)
    ig_col, lf_col = g_col, jax.nn.log_sigmoid(g_col)
    ig_row, lf_row = g_row, jax.nn.log_sigmoid(g_row)
    if t_valid < L:
        vc = lax.broadcasted_iota(jnp.int32, (L, GATE_LANES), 0) < t_valid
        vr = lax.broadcasted_iota(jnp.int32, (8, L), 1) < t_valid
        ig_col, lf_col = jnp.where(vc, ig_col, neg_inf), jnp.where(vc, lf_col, 0.0)
        ig_row, lf_row = jnp.where(vr, ig_row, neg_inf), jnp.where(vr, lf_row, 0.0)
    ti = lax.broadcasted_iota(jnp.int32, (L, L), 0)
    si = lax.broadcasted_iota(jnp.int32, (L, L), 1)
    lower = (ti >= si).astype(F32)
    upper = (ti <= si).astype(F32)
    if L <= 16:
        b_col = lower[:, 0:1] * lf_col[0:1, :]
        b_row = lf_row[:, 0:1] * upper[0:1, :]
        for s in range(1, L):
            b_col = b_col + lower[:, s:s + 1] * lf_col[s:s + 1, :]
            b_row = b_row + lf_row[:, s:s + 1] * upper[s:s + 1, :]
    else:
        b_col = jnp.dot(lower, lf_col, preferred_element_type=F32, precision=lax.Precision.HIGHEST)
        b_row = jnp.dot(lf_row, upper, preferred_element_type=F32, precision=lax.Precision.HIGHEST)
    last = t_valid - 1
    band = min(L, MLSTM_BAND)
    nt = (((1,), (1,)), ((), ()))

    for h in range(H_A):
        cs = slice(h * DH_A, (h + 1) * DH_A)
        ks = slice(W_A + h * DH_A, W_A + (h + 1) * DH_A)
        q = _conv_silu(ext_s, cs, cw_ref[:, cs], cb_ref[:, cs], L)
        tick()
        k = _conv_silu(ext_s, ks, cw_ref[:, ks], cb_ref[:, ks], L) * (DH_A ** -0.5)
        tick()
        v = get_v(h)
        q16, k16, v16 = q.astype(BF16), k.astype(BF16), v.astype(BF16)
        bc = b_col[:, H_A + h:H_A + h + 1]
        igc = ig_col[:, h:h + 1]
        br = b_row[H_A + h:H_A + h + 1, :]
        igr = ig_row[h:h + 1, :]
        m_prev = m_s[h][:, 0:1]
        c_prev = c_s[h]
        n_prev = n_s[h]
        c16 = c_prev.astype(BF16)

        for r in range(L // band):
            rs = slice(r * band, (r + 1) * band)
            kw = (r + 1) * band
            ti = lax.broadcasted_iota(jnp.int32, (band, kw), 0) + r * band
            si = lax.broadcasted_iota(jnp.int32, (band, kw), 1)
            logw = jnp.where(ti >= si, bc[rs] - br[:, 0:kw] + igr[:, 0:kw], neg_inf)
            inter = bc[rs] + m_prev
            m_t = jnp.maximum(inter, jnp.max(logw, axis=1, keepdims=True))
            w_intra = jnp.exp(logw - m_t)
            w_inter = jnp.exp(inter - m_t)
            s = w_intra * lax.dot_general(q16[rs], k16[0:kw], nt, preferred_element_type=F32)
            num = (jnp.dot(s.astype(BF16), v16[0:kw], preferred_element_type=F32)
                   + w_inter * lax.dot_general(q16[rs], c16, nt, preferred_element_type=F32))
            den = (jnp.sum(s, axis=1, keepdims=True)
                   + w_inter * jnp.sum(q[rs] * n_prev, axis=1, keepdims=True))
            y_ref[0, rs, cs] = num / jnp.maximum(jnp.abs(den), jnp.exp(-m_t))
            if r == last // band:
                m_new = m_t[last - r * band:last - r * band + 1, :]
            tick()

        b_last = bc[last:last + 1, :]
        decay = jnp.exp(b_last + m_prev - m_new)
        w_s = jnp.exp(b_last - bc + igc - m_new)
        c_s[h] = decay * c_prev + lax.dot_general((w_s * v).astype(BF16), k16, (((0,), (0,)), ((), ())),
                                                  preferred_element_type=F32)
        n_s[h] = decay * n_prev + jnp.sum(w_s * k, axis=0, keepdims=True)
        m_s[h] = jnp.broadcast_to(m_new, (1, GATE_LANES))
        tick()

    ext_s[0:8, :] = ext_s[t_valid:t_valid + 8, :]


def _mlstm_state_io(c0_ref, n0_ref, m0_ref, tail0_ref, c_s, n_s, m_s, ext_s):
    @pl.when(pl.program_id(1) == 0)
    def _():
        c_s[...] = c0_ref[0]
        n_s[...] = n0_ref[0]
        m_s[...] = m0_ref[0]
        ext_s[0:8, :] = tail0_ref[0]


def _mlstm_state_out(c_out, n_out, m_out, tail_out, c_s, n_s, m_s, ext_s):
    @pl.when(pl.program_id(1) == pl.num_programs(1) - 1)
    def _():
        c_out[0, 0] = c_s[...]
        n_out[0] = n_s[...]
        m_out[0] = m_s[...]
        tail_out[0] = ext_s[0:8, :]


def _mlstm_kernel(p_ref, gc_ref, gr_ref, brow_ref, bcol_ref, cw_ref, cb_ref, c0_ref, n0_ref, m0_ref, tail0_ref,
                  *rest, L, t_valid, n_prev, n_seq):
    cprev_ref = rest[0] if n_prev else None
    y_ref, c_out, n_out, m_out, tail_out, c_s, n_s, m_s, ext_s = rest[1:] if n_prev else rest
    first = pl.program_id(1) == 0
    last = pl.program_id(1) == pl.num_programs(1) - 1
    for e in range(n_seq):
        ce, ne, me, xe = c_s.at[e], n_s.at[e], m_s.at[e], ext_s.at[e]

        @pl.when(first)
        def _(e=e, ce=ce, ne=ne, me=me, xe=xe):
            ce[...] = c0_ref[e]
            ne[...] = n0_ref[e]
            me[...] = m0_ref[e]
            xe[0:8, :] = tail0_ref[e]

        xe[8:8 + L, :] = p_ref[e, :, 0:2 * W_A]
        get_v = lambda h, e=e: p_ref[e, :, 2 * W_A + h * DH_A:2 * W_A + (h + 1) * DH_A]
        _mlstm_core(gc_ref[e] + brow_ref[...], gr_ref[e] + bcol_ref[...], get_v, xe, cw_ref, cb_ref,
                    ce, ne, me, y_ref.at[pl.ds(e, 1)], L, t_valid)

        @pl.when(last)
        def _(e=e, ce=ce, ne=ne, me=me, xe=xe):
            c_out[n_prev, e] = ce[...]
            n_out[e] = ne[...]
            m_out[e] = me[...]
            tail_out[e] = xe[0:8, :]

    if n_prev:
        @pl.when(last)
        def _():
            c_out[0:n_prev] = cprev_ref[...]


def _seq_a_kernel(x_ref, cos_ref, sin_ref, ng_ref, w_aq, w_ak, w_av, w_c0, w_c1, w_c2, wg_ref, brow_ref, bcol_ref,
                  cw_ref, cb_ref, c0_ref, n0_ref, m0_ref, tail0_ref, y_ref, slab_ref, cz_ref, c_out, n_out, m_out, tail_out,
                  c_s, n_s, m_s, ext_s, v_s, *, L):
    _mlstm_state_io(c0_ref, n0_ref, m0_ref, tail0_ref, c_s, n_s, m_s, ext_s)
    nt = (((1,), (1,)), ((), ()))
    x = x_ref[0]
    r = lax.rsqrt(jnp.mean(x * x, axis=-1, keepdims=True) + EPS)
    xn = (x * r * ng_ref[...]).astype(BF16)
    wg = wg_ref[...].astype(BF16)
    g_col = lax.dot_general(xn, wg, nt, preferred_element_type=F32) + brow_ref[...]
    g_row = lax.dot_general(wg[0:8, :], xn, nt, preferred_element_type=F32) + bcol_ref[...]
    proj = lambda w_ref: lax.dot_general(xn, w_ref[...], nt, preferred_element_type=F32)
    ext_s[8:8 + L, 0:W_A] = proj(w_aq)
    ext_s[8:8 + L, W_A:2 * W_A] = proj(w_ak)
    v_s[...] = proj(w_av)
    piece = 256
    todo = [(w_ref, j, lo) for j, w_ref in enumerate((w_c0, w_c1, w_c2)) for lo in range(0, WBLK, piece)]

    assert piece == WG_C

    def emit_piece():
        w_ref, j, lo = todo.pop(0)
        col = j * WBLK + lo
        y = lax.dot_general(xn, w_ref[lo:lo + piece, :], nt, preferred_element_type=F32)
        if col < 2 * W_C:
            cos = jnp.concatenate([cos_ref[0], cos_ref[0]], axis=1)
            sin = jnp.concatenate([sin_ref[0], sin_ref[0]], axis=1)
            y = _rope(y, cos, sin)
            if col < W_C:
                y = y * (DH_C ** -0.5)
        if col < 3 * W_C:
            p = col // piece
            slab_ref[2 * p] = y[:, 0:piece // 2]
            slab_ref[2 * p + 1] = y[:, piece // 2:piece]
        else:
            cz_ref[0, :, col - 3 * W_C:col - 3 * W_C + piece] = y

    side_work = lambda: emit_piece() if todo else None
    get_v = lambda h: v_s[:, h * DH_A:(h + 1) * DH_A]
    _mlstm_core(g_col, g_row, get_v, ext_s, cw_ref, cb_ref, c_s, n_s, m_s, y_ref, L, L, side_work)
    while todo:
        emit_piece()
    _mlstm_state_out(c_out, n_out, m_out, tail_out, c_s, n_s, m_s, ext_s)


def _seq_a(x, cos_t, sin_t, norm_g, w_main, w_in_t, bias_row, bias_col, conv_w, conv_b, c0, n0, m0, tail0, l, *, L):
    rows = x.shape[0]
    nch = rows // L
    x3 = x.reshape(nch, L, D_MODEL)
    gate_blk = 4 * W_A // GATE_LANES
    once = pl.Buffered(1)
    chunk = lambda b, c: (c, 0, 0)
    layer3 = lambda b, c: (l, 0, 0)
    wblk = lambda blk: pl.BlockSpec((None, WBLK, D_MODEL), lambda b, c, blk=blk: (l, blk, 0), pipeline_mode=once)
    st5 = lambda b, c: (0, 0, 0, 0, 0)
    st4 = lambda b, c: (0, 0, 0, 0)
    first4 = lambda b, c: (0, 0, 0, 0)
    first3 = lambda b, c: (0, 0, 0)
    y, slabs, cz, c1, n1, m1, tail1 = pl.pallas_call(
        functools.partial(_seq_a_kernel, L=L),
        grid=(1, nch),
        in_specs=[
            pl.BlockSpec((1, L, D_MODEL), chunk),
            pl.BlockSpec((1, L, 2 * DH_C), chunk),
            pl.BlockSpec((1, L, 2 * DH_C), chunk),
            pl.BlockSpec((None, 1, D_MODEL), layer3, pipeline_mode=once),
        ] + [wblk(b) for b in (0, 1, 2, BLK_C0, BLK_C0 + 1, BLK_C0 + 2)] + [
            pl.BlockSpec((None, GATE_LANES, D_MODEL), lambda b, c: (l, gate_blk, 0), pipeline_mode=once),
            pl.BlockSpec((None, 1, GATE_LANES), layer3, pipeline_mode=once),
            pl.BlockSpec((None, 8, 1), layer3, pipeline_mode=once),
            pl.BlockSpec((None, CONV_W, 2 * W_A), layer3, pipeline_mode=once),
            pl.BlockSpec((None, 1, 2 * W_A), layer3, pipeline_mode=once),
            pl.BlockSpec((None, 1, H_A, DH_A, DH_A), st5, pipeline_mode=once),
            pl.BlockSpec((None, 1, H_A, 1, DH_A), st5, pipeline_mode=once),
            pl.BlockSpec((None, 1, H_A, 1, GATE_LANES), st5, pipeline_mode=once),
            pl.BlockSpec((None, 1, 8, 2 * W_A), st4, pipeline_mode=once),
        ],
        out_specs=[
            pl.BlockSpec((1, L, W_A), chunk),
            pl.BlockSpec((N_SLABS, L, WG_C // 2), lambda b, c: (0, c, 0)),
            pl.BlockSpec((1, L, W_C), chunk),
            pl.BlockSpec((1, 1, H_A, DH_A, DH_A), lambda b, c: (0, 0, 0, 0, 0)),
            pl.BlockSpec((1, H_A, 1, DH_A), first4),
            pl.BlockSpec((1, H_A, 1, GATE_LANES), first4),
            pl.BlockSpec((1, 8, 2 * W_A), first3),
        ],
        out_shape=[
            jax.ShapeDtypeStruct((nch, L, W_A), F32),
            jax.ShapeDtypeStruct((N_SLABS, rows, WG_C // 2), F32),
            jax.ShapeDtypeStruct((nch, L, W_C), F32),
            jax.ShapeDtypeStruct((1, 1, H_A, DH_A, DH_A), F32),
            jax.ShapeDtypeStruct((1, H_A, 1, DH_A), F32),
            jax.ShapeDtypeStruct((1, H_A, 1, GATE_LANES), F32),
            jax.ShapeDtypeStruct((1, 8, 2 * W_A), F32),
        ],
        scratch_shapes=[
            pltpu.VMEM((H_A, DH_A, DH_A), F32),
            pltpu.VMEM((H_A, 1, DH_A), F32),
            pltpu.VMEM((H_A, 1, GATE_LANES), F32),
            pltpu.VMEM((L + 8, 2 * W_A), F32),
            pltpu.VMEM((L, W_A), F32),
        ],
        compiler_params=_cparams(2),
        name="seq_a",
    )(x3, cos_t.reshape(nch, L, 2 * DH_C), sin_t.reshape(nch, L, 2 * DH_C), norm_g, *([w_main] * 6), w_in_t,
      bias_row, bias_col, conv_w, conv_b, c0, n0, m0, tail0)
    return (y.reshape(rows, W_A), slabs, cz.reshape(rows, W_C), c1[0], n1[:, :, 0, :], m1[:, :, 0, 0],
            tail1[:, 8 - (CONV_W - 1):, :])


def _mlstm(p, gcol, grow, bias_row, bias_col, conv_w, conv_b, c0, n0, m0, tail0, l, ls, c_prev, *, L, t_valid):
    batch = c0.shape[1]
    rows = p.shape[0]
    nch = rows // (batch * L)
    n_prev = 0 if c_prev is None else c_prev.shape[0]
    n_seq = MLSTM_SEQS if (nch == 1 and batch % MLSTM_SEQS == 0 and L % GATE_LANES != 0) else 1
    p3 = p.reshape(batch * nch, L, N_P1)
    gc3 = gcol.reshape(batch * nch, L, GATE_LANES)
    tm = grow.shape[2]
    if L % GATE_LANES == 0:
        per = tm // L
        gr3, gr_index = grow, (lambda b, c: ((b * nch + c) // per, 0, (b * nch + c) % per))
    else:
        gr3 = grow.transpose(1, 0, 2).reshape(8, batch * nch, L).transpose(1, 0, 2)
        gr_index = lambda b, c: (b * nch + c, 0, 0)
    kern = functools.partial(_mlstm_kernel, L=L, t_valid=t_valid, n_prev=n_prev, n_seq=n_seq)
    chunk = lambda b, c: (b * nch + c, 0, 0)
    layer3 = lambda b, c: (l, 0, 0)
    stack5 = lambda b, c: (0, b, 0, 0, 0)
    per_b4 = lambda b, c: (b, 0, 0, 0)
    per_b3 = lambda b, c: (b, 0, 0)
    st5 = lambda b, c: (ls, b, 0, 0, 0)
    st4 = lambda b, c: (ls, b, 0, 0)
    y, c1, n1, m1, tail1 = pl.pallas_call(
        kern,
        grid=(batch // n_seq, nch),
        in_specs=[
            pl.BlockSpec((n_seq, L, 3 * W_A), chunk),
            pl.BlockSpec((n_seq, L, GATE_LANES), chunk),
            pl.BlockSpec((n_seq, 8, L), gr_index),
            pl.BlockSpec((None, 1, GATE_LANES), layer3),
            pl.BlockSpec((None, 8, 1), layer3),
            pl.BlockSpec((None, CONV_W, 2 * W_A), layer3),
            pl.BlockSpec((None, 1, 2 * W_A), layer3),
            pl.BlockSpec((None, n_seq, H_A, DH_A, DH_A), st5),
            pl.BlockSpec((None, n_seq, H_A, 1, DH_A), st5),
            pl.BlockSpec((None, n_seq, H_A, 1, GATE_LANES), st5),
            pl.BlockSpec((None, n_seq, 8, 2 * W_A), st4),
        ] + ([pl.BlockSpec((n_prev, n_seq, H_A, DH_A, DH_A), stack5)] if n_prev else []),
        out_specs=[
            pl.BlockSpec((n_seq, L, W_A), chunk),
            pl.BlockSpec((n_prev + 1, n_seq, H_A, DH_A, DH_A), stack5),
            pl.BlockSpec((n_seq, H_A, 1, DH_A), per_b4),
            pl.BlockSpec((n_seq, H_A, 1, GATE_LANES), per_b4),
            pl.BlockSpec((n_seq, 8, 2 * W_A), per_b3),
        ],
        out_shape=[
            jax.ShapeDtypeStruct((batch * nch, L, W_A), F32),
            jax.ShapeDtypeStruct((n_prev + 1, batch, H_A, DH_A, DH_A), F32),
            jax.ShapeDtypeStruct((batch, H_A, 1, DH_A), F32),
            jax.ShapeDtypeStruct((batch, H_A, 1, GATE_LANES), F32),
            jax.ShapeDtypeStruct((batch, 8, 2 * W_A), F32),
        ],
        scratch_shapes=[
            pltpu.VMEM((n_seq, H_A, DH_A, DH_A), F32),
            pltpu.VMEM((n_seq, H_A, 1, DH_A), F32),
            pltpu.VMEM((n_seq, H_A, 1, GATE_LANES), F32),
            pltpu.VMEM((n_seq, L + 8, 2 * W_A), F32),
        ],
        compiler_params=_cparams(2),
        name="mlstm",
    )(p3, gc3, gr3, bias_row, bias_col, conv_w, conv_b, c0, n0, m0, tail0, *([c_prev] if n_prev else []))
    return y.reshape(rows, W_A), c1, n1[:, :, 0, :], m1[:, :, 0, 0], tail1[:, 8 - (CONV_W - 1):, :]


def _rope(x, cos, sin):
    width = x.shape[1]
    lane = lax.broadcasted_iota(jnp.int32, (1, width), 1) % DH_C
    half = ROT_DIM // 2
    partner = jnp.where(lane < half, pltpu.roll(x, width - half, 1), pltpu.roll(x, half, 1))
    return x * cos + partner * sin


def _dil_attn_kernel(q_ref, k_ref, kp_ref, v_ref, vp_ref, o_ref, mx_ref, den_ref, bias, *, d):
    i = pl.program_id(0)
    T = ATT_TILE
    HL = WG_C // 2
    nres = T // (SPAN_C * d)
    neg_inf = jnp.float32(-jnp.inf)

    @pl.when(i == 0)
    def _():
        ii = lax.broadcasted_iota(jnp.int32, (SPAN_C, 2 * SPAN_C), 0)
        jj = lax.broadcasted_iota(jnp.int32, (SPAN_C, 2 * SPAN_C), 1)
        band = (jj >= ii) & (jj <= ii + SPAN_C)
        bias[1] = jnp.where(band, 0.0, neg_inf)
        bias[0] = jnp.where(band & (jj >= SPAN_C), 0.0, neg_inf)

    lane_head = lax.broadcasted_iota(jnp.int32, (SPAN_C, WG_C), 1) // DH_C
    hm = [(lane_head == h).astype(F32) for h in range(HG_C)]

    def rows(start):
        return pl.ds(start, SPAN_C) if d == 1 else pl.ds(start, SPAN_C, stride=d)

    def take(ref, start):
        return jnp.concatenate([ref[0, rows(start), :], ref[1, rows(start), :]], axis=1)

    def pick(per_head):
        out = jnp.broadcast_to(per_head[0], (SPAN_C, WG_C))
        for h in range(1, HG_C):
            out = jnp.where(lane_head >= h, per_head[h], out)
        return out

    has_prev_tile = (i > 0).astype(jnp.int32)
    for blk in range(T // SPAN_C):
        r, c = blk % d, blk // d
        qstart = r + SPAN_C * d * c
        qb = take(q_ref, qstart)
        if c > 0:
            k_before, v_before = take(k_ref, qstart - SPAN_C * d), take(v_ref, qstart - SPAN_C * d)
            mask = bias[1]
        else:
            last_blk = r + SPAN_C * d * (nres - 1)
            k_before, v_before = take(kp_ref, last_blk), take(vp_ref, last_blk)
            mask = bias[has_prev_tile]
        kb16 = jnp.concatenate([k_before, take(k_ref, qstart)], axis=0).astype(BF16)
        vb16 = jnp.concatenate([v_before, take(v_ref, qstart)], axis=0).astype(BF16)
        pvs, mxs, dens = [], [], []
        for h0 in range(0, HG_C, 2):
            q2 = jnp.concatenate([qb * hm[h0], qb * hm[h0 + 1]], axis=0)
            s = _dot_nt(q2, kb16).reshape(2, SPAN_C, 2 * SPAN_C) + mask
            mx = jnp.max(s, axis=2, keepdims=True)
            p = jnp.exp(s - mx)
            den = jnp.sum(p, axis=2, keepdims=True)
            pv = _dot(p.reshape(2 * SPAN_C, 2 * SPAN_C), vb16).reshape(2, SPAN_C, WG_C)
            pvs += [pv[0], pv[1]]
            mxs += [mx[0], mx[1]]
            dens += [den[0], den[1]]
        for ref, val in zip((o_ref, mx_ref, den_ref), (pick(pvs), pick(mxs), pick(dens))):
            ref[0, rows(qstart), :] = val[:, 0:HL]
            ref[1, rows(qstart), :] = val[:, HL:WG_C]


def _dil_attn(slabs, g, d):
    rows = slabs.shape[1]
    T = ATT_TILE
    per_seg = W_C // WG_C
    cur = lambda seg: pl.BlockSpec((2, T, WG_C // 2), lambda i, seg=seg: (seg * per_seg + g, i, 0))
    prev = lambda seg: pl.BlockSpec((2, T, WG_C // 2),
                                    lambda i, seg=seg: (seg * per_seg + g, jnp.maximum(i - 1, 0), 0))
    slab = pl.BlockSpec((2, T, WG_C // 2), lambda i: (0, i, 0))
    slab_shape = jax.ShapeDtypeStruct((2, rows, WG_C // 2), F32)
    return pl.pallas_call(
        functools.partial(_dil_attn_kernel, d=d),
        grid=(rows // T,),
        in_specs=[cur(0), cur(1), prev(1), cur(2), prev(2)],
        out_specs=[slab, slab, slab],
        out_shape=[slab_shape, slab_shape, slab_shape],
        scratch_shapes=[pltpu.VMEM((2, SPAN_C, 2 * SPAN_C), F32)],
        compiler_params=_cparams(1),
        name="dil_attn_d%d" % d,
    )(slabs, slabs, slabs, slabs, slabs)


def _sample_attn_kernel(pc0, pc1, pc2, cos_ref, sin_ref, c0_ref, c1_ref, c2_ref,
                        o0, o1, o2, m0, m1, m2, d0, d1, d2, kr_ref, *, n_tok, n_seq):
    reps = W_C // (2 * DH_C)
    cos = jnp.concatenate([cos_ref[...]] * reps, axis=1)
    sin = jnp.concatenate([sin_ref[...]] * reps, axis=1)
    nrow = HG_C * SAMPLE_ROWS
    rowh = lax.broadcasted_iota(jnp.int32, (nrow, WG_C), 0) // SAMPLE_ROWS
    laneh = lax.broadcasted_iota(jnp.int32, (nrow, WG_C), 1) // DH_C
    hm = (rowh == laneh).astype(F32)
    tok_col = lax.broadcasted_iota(jnp.int32, (nrow, 1), 0) % SAMPLE_ROWS
    out_row = lax.broadcasted_iota(jnp.int32, (SAMPLE_ROWS, WG_C), 0)
    neg_inf = jnp.float32(-jnp.inf)
    crefs = (c0_ref, c1_ref, c2_ref)
    orefs = (o0, o1, o2)
    mrefs = (m0, m1, m2)
    drefs = (d0, d1, d2)
    for e in range(n_seq):
        rows_e = slice(e * SAMPLE_ROWS, (e + 1) * SAMPLE_ROWS)
        pc = jnp.concatenate([pc0[e], pc1[e], pc2[e]], axis=1)
        qr = _rope(pc[:, 0:W_C], cos, sin) * (DH_C ** -0.5)
        kr = _rope(pc[:, W_C:2 * W_C], cos, sin)
        v = pc[:, 2 * W_C:3 * W_C]
        kr_ref[e] = kr
        for g, (_, d) in enumerate(DIL_PATTERNS):
            gs = slice(g * WG_C, (g + 1) * WG_C)
            n_buf = SPAN_C * d
            kn = kr[:, gs]
            vn = v[:, gs]
            qm = jnp.concatenate([qr[:, gs]] * HG_C, axis=0) * hm
            s = _dot(qm, crefs[g][e, 0])
            tok = lax.broadcasted_iota(jnp.int32, (nrow, n_buf), 0) % SAMPLE_ROWS
            pos = lax.broadcasted_iota(jnp.int32, (nrow, n_buf), 1)
            attends = (pos >= tok) if d == 1 else ((pos & (d - 1)) == tok)
            s = jnp.where(attends | (tok >= n_tok), s, neg_inf)
            new_ok = [(u <= tok_col) if d == 1 else (u == tok_col) for u in range(n_tok)]
            s_new = [jnp.where(ok, jnp.sum(qm * kn[u:u + 1, :], axis=1, keepdims=True), neg_inf)
                     for u, ok in enumerate(new_ok)]
            mx = jnp.max(s, axis=1, keepdims=True)
            for sn in s_new:
                mx = jnp.maximum(mx, sn)
            p = jnp.exp(s - mx)
            den = jnp.sum(p, axis=1, keepdims=True)
            acc = _dot_nt(p, crefs[g][e, 1])
            for u, sn in enumerate(s_new):
                pn = jnp.exp(sn - mx)
                den = den + pn
                acc = acc + pn * vn[u:u + 1, :]
            stacked = (acc * hm, mx * hm, den * hm)
            for ref, val, pad in zip((orefs[g], mrefs[g], drefs[g]), stacked, (0.0, 0.0, 1.0)):
                v8 = val[0:SAMPLE_ROWS]
                for h in range(1, HG_C):
                    v8 = v8 + val[h * SAMPLE_ROWS:(h + 1) * SAMPLE_ROWS]
                v8 = jnp.where(out_row < n_tok, v8, pad)
                for half in range(2):
                    ref[half, rows_e, :] = v8[:, half * (WG_C // 2):(half + 1) * (WG_C // 2)]


def _cache_views(caches):
    views = []
    for (win, d), c in zip(DIL_PATTERNS, caches):
        depth, batch, n_buf = c.shape[:3]
        assert n_buf == SPAN_C * d
        views.append(jnp.transpose(c, (0, 1, 3, 4, 5, 2)).reshape(depth, batch, 2, WG_C, n_buf))
    return views


def _sample_attn(p, cos_s, sin_s, views, l, n_tok):
    batch = p.shape[0] // SAMPLE_ROWS
    n_seq = SAMPLE_ATT_SEQS if batch % SAMPLE_ATT_SEQS == 0 else 1
    p3 = p.reshape(batch, SAMPLE_ROWS, N_P1)
    pcol = lambda k: pl.BlockSpec((n_seq, SAMPLE_ROWS, WBLK), lambda b, k=k: (b, 0, OFF_CQ // WBLK + k))
    tab = pl.BlockSpec((SAMPLE_ROWS, 2 * DH_C), lambda b: (0, 0))
    cache_spec = lambda d: pl.BlockSpec((None, n_seq, 2, WG_C, SPAN_C * d), lambda b: (l, b, 0, 0, 0))
    out = pl.BlockSpec((2, n_seq * SAMPLE_ROWS, WG_C // 2), lambda b: (0, b, 0))
    res = pl.pallas_call(
        functools.partial(_sample_attn_kernel, n_tok=n_tok, n_seq=n_seq),
        grid=(batch // n_seq,),
        in_specs=[pcol(0), pcol(1), pcol(2), tab, tab] + [cache_spec(d) for _, d in DIL_PATTERNS],
        out_specs=[out] * 9 + [pl.BlockSpec((n_seq, SAMPLE_ROWS, W_C), lambda b: (b, 0, 0))],
        out_shape=[jax.ShapeDtypeStruct((2, batch * SAMPLE_ROWS, WG_C // 2), F32)] * 9
        + [jax.ShapeDtypeStruct((batch, SAMPLE_ROWS, W_C), F32)],
        compiler_params=_cparams(1),
        name="sample_attn",
    )(p3, p3, p3, cos_s, sin_s, *views)
    return list(res[:9]), res[9].reshape(batch * SAMPLE_ROWS, W_C)


def _post_kernel(x_ref, h_ref, o0, o1, o2, m0, m1, m2, d0, d1, d2, cz_ref, ng_ref, w_ao, w_az, w_bu, w_bv, w_bz, w_ga, w_gb,
                 w_gc, lg_ref, lb_ref, wsp_ref, bsp_ref, wb_ref, wo_ref, fg_ref, out_ref, *rest, nchunks, final,
                 emit_vn):
    if emit_vn:
        vn_out, vn_s, yb_s = rest
    else:
        vn_s, yb_s = rest
    nt = (((1,), (1,)), ((), ()))
    x = x_ref[...]
    r = lax.rsqrt(jnp.mean(x * x, axis=-1, keepdims=True) + EPS)
    xn = (x * r * ng_ref[...]).astype(BF16)
    proj = lambda w_ref: lax.dot_general(xn, w_ref[...], nt, preferred_element_type=F32)
    branch = lambda y, lo, hi: jnp.dot(y.astype(BF16), wb_ref[lo:hi, :], preferred_element_type=F32)

    ya = h_ref[...] * _sigmoid(proj(w_ao)) * _silu(proj(w_az))
    acc = _sigmoid(proj(w_ga)) * branch(ya, 0, W_A)

    v = proj(w_bv)
    mu = jnp.mean(v, axis=-1, keepdims=True)
    var = jnp.mean(jnp.square(v - mu), axis=-1, keepdims=True)
    vn = (v - mu) * lax.rsqrt(var + EPS) * lg_ref[...] + lb_ref[...]
    if emit_vn:
        vn_out[...] = vn
    vn_s[...] = vn.astype(BF16)
    gate_b = proj(w_bu) * _silu(proj(w_bz))
    tri = (lax.broadcasted_iota(jnp.int32, (CHUNK_B, CHUNK_B), 0)
           >= lax.broadcasted_iota(jnp.int32, (CHUNK_B, CHUNK_B), 1))
    for g in range(G_B):
        wg = jnp.where(tri, wsp_ref[g], 0.0).astype(BF16)
        cs = slice(g * DG_B, (g + 1) * DG_B)
        for c in range(nchunks):
            rs = slice(c * CHUNK_B, (c + 1) * CHUNK_B)
            yb_s[rs, cs] = jnp.dot(wg, vn_s[rs, cs], preferred_element_type=F32) + bsp_ref[g]
    acc = acc + _sigmoid(proj(w_gb)) * branch(gate_b * yb_s[...], W_A, W_A + W_B)

    unslab = lambda ref: jnp.concatenate([ref[0], ref[1]], axis=1)
    ms = [unslab(m0), unslab(m1), unslab(m2)]
    mx = jnp.maximum(jnp.maximum(ms[0], ms[1]), ms[2])
    es = [jnp.exp(m - mx) for m in ms]
    inv_tot = 1.0 / (es[0] * unslab(d0) + es[1] * unslab(d1) + es[2] * unslab(d2))
    pc = None
    for g, o_ref in enumerate((o0, o1, o2)):
        gs = slice(g * WG_C, (g + 1) * WG_C)
        yc = unslab(o_ref) * (es[g] * inv_tot) * _silu(cz_ref[:, gs])
        t = branch(yc, W_A + W_B + g * WG_C, W_A + W_B + (g + 1) * WG_C)
        pc = t if pc is None else pc + t
    acc = acc + _sigmoid(proj(w_gc)) * pc

    out = x + jnp.dot(acc.astype(BF16), wo_ref[...], preferred_element_type=F32)
    if final:
        rr = lax.rsqrt(jnp.mean(out * out, axis=-1, keepdims=True) + EPS)
        out = out * rr * fg_ref[...]
    out_ref[...] = out


def _post(x, p1, cz_blk, h, att, norm_g, w_main, ln_g, ln_b, w_sp, b_sp_col, w_branch, w_out, final_g, l, final,
          emit_vn):
    rows = x.shape[0]
    tm = min(rows, POST_TM)
    once = pl.Buffered(1)
    row = lambda w: pl.BlockSpec((tm, w), lambda i: (i, 0))
    slab = pl.BlockSpec((2, tm, WG_C // 2), lambda i: (0, i, 0))
    wblk = lambda b: pl.BlockSpec((None, WBLK, D_MODEL), lambda i, b=b: (l, b, 0), pipeline_mode=once)
    vec = lambda w: pl.BlockSpec((None, 1, w), lambda i: (l, 0, 0), pipeline_mode=once)
    in_specs = (
        [row(D_MODEL), row(W_A)] + [slab] * 9
        + [pl.BlockSpec((tm, W_C), lambda i: (i, cz_blk)), vec(D_MODEL)]
        + [wblk(b) for b in (BLK_AO, BLK_AZ, BLK_BU, BLK_BV, BLK_BZ, BLK_GA, BLK_GB, BLK_GC)]
        + [vec(W_B), vec(W_B),
           pl.BlockSpec((None, G_B, CHUNK_B, CHUNK_B), lambda i: (l, 0, 0, 0), pipeline_mode=once),
           pl.BlockSpec((None, G_B, CHUNK_B, 1), lambda i: (l, 0, 0, 0), pipeline_mode=once),
           pl.BlockSpec((None, W_A + W_B + W_C, D_MODEL), lambda i: (l, 0, 0), pipeline_mode=once),
           pl.BlockSpec((None, D_MODEL, D_MODEL), lambda i: (l, 0, 0), pipeline_mode=once),
           pl.BlockSpec((1, D_MODEL), lambda i: (0, 0), pipeline_mode=once)])
    out_specs = [row(D_MODEL)]
    out_shape = [jax.ShapeDtypeStruct((rows, D_MODEL), F32)]
    if emit_vn:
        out_specs.append(row(W_B))
        out_shape.append(jax.ShapeDtypeStruct((rows, W_B), F32))
    res = pl.pallas_call(
        functools.partial(_post_kernel, nchunks=tm // CHUNK_B, final=final, emit_vn=emit_vn),
        grid=(rows // tm,),
        in_specs=in_specs,
        out_specs=out_specs,
        out_shape=out_shape,
        scratch_shapes=[pltpu.VMEM((tm, W_B), BF16), pltpu.VMEM((tm, W_B), F32)],
        compiler_params=_cparams(1),
        name="post",
    )(x, h, *att, p1, norm_g, *([w_main] * 8), ln_g, ln_b, w_sp, b_sp_col, w_branch, w_out, final_g)
    return (res[0], res[1]) if emit_vn else (res[0], None)


def _rope_tables(pos):
    half = ROT_DIM // 2
    inv = ROPE_THETA ** (-jnp.arange(half, dtype=F32) / half)
    ang = inv[:, None] * pos.astype(F32)[None, :]
    lane = jnp.arange(2 * DH_C) % DH_C
    freq = jnp.arange(half)[:, None] == (lane % half)[None, :]
    rot = (lane < ROT_DIM)[None, :]
    sel_cos = (freq & rot).astype(F32)
    sel_sin = sel_cos * jnp.where(lane < half, -1.0, 1.0)[None, :]
    spread = lambda t, sel: lax.dot_general(t, sel, (((0,), (0,)), ((), ())), precision=lax.Precision.HIGHEST)
    return spread(jnp.cos(ang), sel_cos) + (~rot).astype(F32), spread(jnp.sin(ang), sel_sin)


def _stack_kv(k_rows, v_rows):
    b, t, _ = k_rows.shape
    return jnp.stack([k_rows.reshape(b, t, HG_C, DH_C), v_rows.reshape(b, t, HG_C, DH_C)], axis=2)


def kernel(x_prompt, x_sample, state_C, state_n, state_m, state_conv, cache_kv_w128, cache_kv_w512, cache_kv_w2048,
           norm_g, w_in, b_igate, b_fgate, conv_w, conv_b, ln_v_g, ln_v_b, w_spatial, b_spatial, w_branch, w_out,
           final_norm_g):
    depth = w_in.shape[0]
    bp, seq, _ = x_prompt.shape
    bs, n_tok, _ = x_sample.shape
    assert bp == 1 and seq % ATT_TILE == 0 and n_tok <= SAMPLE_ROWS // 2 and n_tok >= CONV_W - 1
    caches = (cache_kv_w128, cache_kv_w512, cache_kv_w2048)
    pad_tok = SAMPLE_ROWS - n_tok
    rows_s = bs * SAMPLE_ROWS

    w_in_t = jnp.swapaxes(w_in, 1, 2)
    w_main = _pack_w_in(w_in_t)
    gate_bias = jnp.concatenate([b_igate, b_fgate], axis=1)
    bias_row = jnp.pad(gate_bias, ((0, 0), (0, GATE_LANES - 2 * H_A)))[:, None, :]
    bias_col = gate_bias[:, :, None]
    wb16 = w_branch.astype(BF16)
    wo16 = w_out.astype(BF16)
    b_sp_col = b_spatial[..., None]
    norm_g3, conv_b3 = norm_g[:, None, :], conv_b[:, None, :]
    ln_g3, ln_b3 = ln_v_g[:, None, :], ln_v_b[:, None, :]
    seqs_per_chunk = CHUNK_B // SAMPLE_ROWS
    w8 = jnp.pad(w_spatial[:, :, :n_tok, :n_tok], ((0, 0), (0, 0), (0, pad_tok), (0, pad_tok)))
    w_sp_s = jnp.einsum('ab,lgij->lgaibj', jnp.eye(seqs_per_chunk, dtype=F32), w8).reshape(
        depth, G_B, CHUNK_B, CHUNK_B)
    b_sp_s = jnp.tile(jnp.pad(b_spatial[:, :, :n_tok], ((0, 0), (0, 0), (0, pad_tok))),
                      (1, 1, seqs_per_chunk))[..., None]

    cos_p, sin_p = _rope_tables(jnp.arange(seq))
    cos_s, sin_s = _rope_tables(PAST_LEN + jnp.arange(SAMPLE_ROWS))

    hp = x_prompt.reshape(seq, D_MODEL)
    hs = jnp.pad(x_sample, ((0, 0), (0, pad_tok), (0, 0))).reshape(rows_s, D_MODEL)
    fg = final_norm_g[None, :]
    zeros_c = jnp.zeros((1, 1, H_A, DH_A, DH_A), F32)
    zeros_n = jnp.zeros((1, 1, H_A, 1, DH_A), F32)
    zeros_m = jnp.zeros((1, 1, H_A, 1, GATE_LANES), F32)
    zeros_tail = jnp.zeros((1, 1, 8, 2 * W_A), F32)
    n0_s = state_n[:, :, :, None, :]
    m0_s = jnp.broadcast_to(state_m[:, :, :, None, None], (depth, bs, H_A, 1, GATE_LANES))
    tail0_s = jnp.pad(state_conv, ((0, 0), (0, 0), (8 - (CONV_W - 1), 0), (0, 0)))
    views = _cache_views(caches)

    c_stack_s = None
    p_out = {k: [] for k in ('C', 'n', 'm', 'conv', 'kv0', 'kv1', 'kv2')}
    s_out = {k: [] for k in ('n', 'm', 'conv', 'chunk_v', 'kv0', 'kv1', 'kv2')}
    for l in range(depth):
        final = l == depth - 1

        ya, slabs, pz, c1, n1, m1, conv1 = _seq_a(hp, cos_p, sin_p, norm_g3, w_main, w_in_t, bias_row, bias_col, conv_w, conv_b3,
                                           zeros_c, zeros_n, zeros_m, zeros_tail, l, L=MLSTM_L)
        att = [[], [], []]
        for g, (win, d) in enumerate(DIL_PATTERNS):
            for dst, val in zip(att, _dil_attn(slabs, g, d)):
                dst.append(val)
            keep = min(win, seq)
            n_pairs = len(DIL_PATTERNS)
            unslab = lambda p: jnp.concatenate([slabs[2 * p, seq - keep:], slabs[2 * p + 1, seq - keep:]], axis=1)
            k_g, v_g = unslab(n_pairs + g), unslab(2 * n_pairs + g)
            p_out['kv%d' % g].append(_stack_kv(k_g[None], v_g[None]))
        hp, _ = _post(hp, pz, 0, ya, att[0] + att[1] + att[2], norm_g3, w_main, ln_g3, ln_b3, w_spatial, b_sp_col, wb16, wo16, fg, l,
                      final, False)
        p_out['C'].append(c1)
        p_out['n'].append(n1)
        p_out['m'].append(m1)
        p_out['conv'].append(conv1)

        ps, gcol, grow = _inproj(hs, norm_g3, w_main, w_in_t, l)
        ya, c1, n1, m1, conv1 = _mlstm(ps, gcol, grow, bias_row, bias_col, conv_w, conv_b3,
                                       state_C, n0_s, m0_s, tail0_s, l, l, c_stack_s, L=SAMPLE_ROWS, t_valid=n_tok)
        c_stack_s = c1
        att_s, kr = _sample_attn(ps, cos_s, sin_s, views, l, n_tok)
        hs, vn = _post(hs, ps, OFF_CZ // W_C, ya, att_s, norm_g3, w_main, ln_g3, ln_b3, w_sp_s, b_sp_s, wb16, wo16, fg, l,
                       final, True)
        s_out['n'].append(n1)
        s_out['m'].append(m1)
        s_out['conv'].append(conv1)
        s_out['chunk_v'].append(vn.reshape(bs, SAMPLE_ROWS, W_B)[:, :n_tok])
        kr3 = kr.reshape(bs, SAMPLE_ROWS, W_C)[:, :n_tok]
        v3 = ps[:, OFF_CV:OFF_CV + W_C].reshape(bs, SAMPLE_ROWS, W_C)[:, :n_tok]
        for g in range(len(DIL_PATTERNS)):
            gs = slice(g * WG_C, (g + 1) * WG_C)
            s_out['kv%d' % g].append(_stack_kv(kr3[:, :, gs], v3[:, :, gs]))

    stk = lambda d, k: jnp.stack(d[k], axis=0)
    y_prompt = hp.reshape(bp, seq, D_MODEL)
    y_sample = hs.reshape(bs, SAMPLE_ROWS, D_MODEL)[:, :n_tok]
    return (y_prompt, y_sample,
            stk(p_out, 'C'), stk(p_out, 'n'), stk(p_out, 'm'), stk(p_out, 'conv'),
            stk(p_out, 'kv0'), stk(p_out, 'kv1'), stk(p_out, 'kv2'),
            c_stack_s, stk(s_out, 'n'), stk(s_out, 'm'), stk(s_out, 'conv'), stk(s_out, 'chunk_v'),
            stk(s_out, 'kv0'), stk(s_out, 'kv1'), stk(s_out, 'kv2'))
```

```python
import functools

import jax
import jax.numpy as jnp
from jax import lax
from jax.experimental import pallas as pl
from jax.experimental.pallas import tpu as pltpu

F32 = jnp.float32
BF16 = jnp.bfloat16

D_MODEL = 1024
H_A = 4
DH_A = 256
W_A = H_A * DH_A
CONV_W = 4
G_B = 4
CHUNK_B = 128
W_B = 1024
DG_B = W_B // G_B
DIL_PATTERNS = ((128, 1), (512, 4), (2048, 16))
HG_C = 4
DH_C = 64
WG_C = HG_C * DH_C
W_C = len(DIL_PATTERNS) * WG_C
SPAN_C = 128
ROT_DIM = DH_C // 4
ROPE_THETA = 500000.0
EPS = 1e-6
PAST_LEN = 16384

N_PACK = 14336
WBLK = 1024
BLK_AO, BLK_AZ, BLK_BU, BLK_BV, BLK_BZ, BLK_C0, BLK_GA, BLK_GB, BLK_GC = 3, 4, 5, 6, 7, 8, 11, 12, 13
N_P1 = 6144
OFF_CQ, OFF_CK, OFF_CV, OFF_CZ = 3072, 3840, 4608, 5376
GATE_LANES = 128

INPROJ_TM = 2048
POST_TM = 512
MLSTM_BAND = 128
MLSTM_L = 256
SAMPLE_ROWS = 8
MLSTM_SEQS = 4
SAMPLE_ATT_SEQS = 4
ATT_TILE = 2048
VMEM_LIMIT = 56 * 1024 * 1024


def _cparams(n_axes):
    return pltpu.CompilerParams(dimension_semantics=("arbitrary",) * n_axes, vmem_limit_bytes=VMEM_LIMIT)


def _dot(a, b):
    return jnp.dot(a.astype(BF16), b.astype(BF16), preferred_element_type=F32)


def _dot_nt(a, b):
    return lax.dot_general(a.astype(BF16), b.astype(BF16), (((1,), (1,)), ((), ())), preferred_element_type=F32)


def _sigmoid(x):
    return 0.5 * jnp.tanh(0.5 * x) + 0.5


def _silu(x):
    return x * _sigmoid(x)


def _inproj_kernel(x_ref, g_ref, w_ref, wg_ref, p_ref, gc_ref, gr_ref, xn_ref):
    nt = (((1,), (1,)), ((), ()))

    @pl.when(pl.program_id(1) == 0)
    def _():
        x = x_ref[...]
        r = lax.rsqrt(jnp.mean(x * x, axis=-1, keepdims=True) + EPS)
        xn = (x * r * g_ref[...]).astype(BF16)
        xn_ref[...] = xn
        wg = wg_ref[...].astype(BF16)
        gc_ref[...] = lax.dot_general(xn, wg, nt, preferred_element_type=F32)
        gr_ref[...] = lax.dot_general(wg[0:8, :], xn, nt, preferred_element_type=F32)

    p_ref[...] = lax.dot_general(xn_ref[...], w_ref[...], nt, preferred_element_type=F32)


def _inproj(x, norm_g, w_main, w_in_t, l):
    rows = x.shape[0]
    gate_blk = 4 * W_A // GATE_LANES
    tm = min(rows, INPROJ_TM)
    tn = WBLK
    n_a = 3 * W_A // tn
    wblk = lambda i, j: (l, jnp.where(j < n_a, j, j + (BLK_C0 - n_a)), 0)
    return pl.pallas_call(
        _inproj_kernel,
        grid=(rows // tm, N_P1 // tn),
        in_specs=[
            pl.BlockSpec((tm, D_MODEL), lambda i, j: (i, 0)),
            pl.BlockSpec((None, 1, D_MODEL), lambda i, j: (l, 0, 0)),
            pl.BlockSpec((None, tn, D_MODEL), wblk),
            pl.BlockSpec((None, GATE_LANES, D_MODEL), lambda i, j: (l, gate_blk, 0)),
        ],
        out_specs=[
            pl.BlockSpec((tm, tn), lambda i, j: (i, j)),
            pl.BlockSpec((tm, GATE_LANES), lambda i, j: (i, 0)),
            pl.BlockSpec((None, 8, tm), lambda i, j: (i, 0, 0)),
        ],
        out_shape=[
            jax.ShapeDtypeStruct((rows, N_P1), F32),
            jax.ShapeDtypeStruct((rows, GATE_LANES), F32),
            jax.ShapeDtypeStruct((rows // tm, 8, tm), F32),
        ],
        scratch_shapes=[pltpu.VMEM((tm, D_MODEL), BF16)],
        compiler_params=_cparams(2),
        name="inproj",
    )(x, norm_g, w_main, w_in_t)


def _pack_kernel(a_ref, b_ref, w_ref, *, first_shifted):
    j = pl.program_id(1)

    @pl.when(j < first_shifted)
    def _():
        w_ref[...] = a_ref[...].astype(BF16)

    @pl.when(j >= first_shifted)
    def _():
        w_ref[...] = jnp.concatenate([a_ref[2 * H_A:, :], b_ref[...]], axis=0).astype(BF16)


def _pack_w_in(w_in_t):
    depth = w_in_t.shape[0]
    tn = 2048
    gate_off = 4 * W_A
    assert gate_off % tn == 0 and w_in_t.shape[1] == N_PACK + 2 * H_A and 2 * H_A == 8
    kern = functools.partial(_pack_kernel, first_shifted=gate_off // tn)
    return pl.pallas_call(
        kern,
        grid=(depth, N_PACK // tn),
        in_specs=[
            pl.BlockSpec((None, tn, D_MODEL), lambda l, j: (l, j, 0)),
            pl.BlockSpec((None, 8, D_MODEL), lambda l, j: (l, (j + 1) * (tn // 8), 0)),
        ],
        out_specs=pl.BlockSpec((None, tn, D_MODEL), lambda l, j: (l, j, 0)),
        out_shape=jax.ShapeDtypeStruct((depth, N_PACK, D_MODEL), BF16),
        compiler_params=_cparams(2),
        name="pack_w_in",
    )(w_in_t, w_in_t)


def _conv_silu(ext_ref, cols, w, b, L):
    y = b
    for back in range(CONV_W):
        y = y + ext_ref[8 - back:8 - back + L, cols] * w[CONV_W - 1 - back:CONV_W - back]
    return _silu(y)


def _mlstm_core(g_col, g_row, get_v, ext_s, cw_ref, cb_ref, c_s, n_s, m_s, y_ref, L, t_valid, side_work=None):
    tick = side_work if side_work is not None else (lambda: None)
    neg_inf = jnp.float32(-jnp.inf)
    ig_col, lf_col = g_col, jax.nn.log_sigmoid(g_col)
    ig_row, lf_row = g_row, jax.nn.log_sigmoid(g_row)
    if t_valid < L:
        vc = lax.broadcasted_iota(jnp.int32, (L, GATE_LANES), 0) < t_valid
        vr = lax.broadcasted_iota(jnp.int32, (8, L), 1) < t_valid
        ig_col, lf_col = jnp.where(vc, ig_col, neg_inf), jnp.where(vc, lf_col, 0.0)
        ig_row, lf_row = jnp.where(vr, ig_row, neg_inf), jnp.where(vr, lf_row, 0.0)
    ti = lax.broadcasted_iota(jnp.int32, (L, L), 0)
    si = lax.broadcasted_iota(jnp.int32, (L, L), 1)
    lower = (ti >= si).astype(F32)
    upper = (ti <= si).astype(F32)
    if L <= 16:
        b_col = lower[:, 0:1] * lf_col[0:1, :]
        b_row = lf_row[:, 0:1] * upper[0:1, :]
        for s in range(1, L):
            b_col = b_col + lower[:, s:s + 1] * lf_col[s:s + 1, :]
            b_row = b_row + lf_row[:, s:s + 1] * upper[s:s + 1, :]
    else:
        b_col = jnp.dot(lower, lf_col, preferred_element_type=F32, precision=lax.Precision.HIGHEST)
        b_row = jnp.dot(lf_row, upper, preferred_element_type=F32, precision=lax.Precision.HIGHEST)
    last = t_valid - 1
    band = min(L, MLSTM_BAND)
    nt = (((1,), (1,)), ((), ()))

    for h in range(H_A):
        cs = slice(h * DH_A, (h + 1) * DH_A)
        ks = slice(W_A + h * DH_A, W_A + (h + 1) * DH_A)
        q = _conv_silu(ext_s, cs, cw_ref[:, cs], cb_ref[:, cs], L)
        tick()
        k = _conv_silu(ext_s, ks, cw_ref[:, ks], cb_ref[:, ks], L) * (DH_A ** -0.5)
        tick()
        v = get_v(h)
        q16, k16, v16 = q.astype(BF16), k.astype(BF16), v.astype(BF16)
        bc = b_col[:, H_A + h:H_A + h + 1]
        igc = ig_col[:, h:h + 1]
        br = b_row[H_A + h:H_A + h + 1, :]
        igr = ig_row[h:h + 1, :]
        m_prev = m_s[h][:, 0:1]
        c_prev = c_s[h]
        n_prev = n_s[h]
        c16 = c_prev.astype(BF16)

        for r in range(L // band):
            rs = slice(r * band, (r + 1) * band)
            kw = (r + 1) * band
            ti = lax.broadcasted_iota(jnp.int32, (band, kw), 0) + r * band
            si = lax.broadcasted_iota(jnp.int32, (band, kw), 1)
            logw = jnp.where(ti >= si, bc[rs] - br[:, 0:kw] + igr[:, 0:kw], neg_inf)
            inter = bc[rs] + m_prev
            m_t = jnp.maximum(inter, jnp.max(logw, axis=1, keepdims=True))
            w_intra = jnp.exp(logw - m_t)
            w_inter = jnp.exp(inter - m_t)
            s = w_intra * lax.dot_general(q16[rs], k16[0:kw], nt, preferred_element_type=F32)
            num = (jnp.dot(s.astype(BF16), v16[0:kw], preferred_element_type=F32)
                   + w_inter * lax.dot_general(q16[rs], c16, nt, preferred_element_type=F32))
            den = (jnp.sum(s, axis=1, keepdims=True)
                   + w_inter * jnp.sum(q[rs] * n_prev, axis=1, keepdims=True))
            y_ref[0, rs, cs] = num / jnp.maximum(jnp.abs(den), jnp.exp(-m_t))
            if r == last // band:
                m_new = m_t[last - r * band:last - r * band + 1, :]
            tick()

        b_last = bc[last:last + 1, :]
        decay = jnp.exp(b_last + m_prev - m_new)
        w_s = jnp.exp(b_last - bc + igc - m_new)
        c_s[h] = decay * c_prev + lax.dot_general((w_s * v).astype(BF16), k16, (((0,), (0,)), ((), ())),
                                                  preferred_element_type=F32)
        n_s[h] = decay * n_prev + jnp.sum(w_s * k, axis=0, keepdims=True)
        m_s[h] = jnp.broadcast_to(m_new, (1, GATE_LANES))
        tick()

    ext_s[0:8, :] = ext_s[t_valid:t_valid + 8, :]


def _mlstm_state_io(c0_ref, n0_ref, m0_ref, tail0_ref, c_s, n_s, m_s, ext_s):
    @pl.when(pl.program_id(1) == 0)
    def _():
        c_s[...] = c0_ref[0]
        n_s[...] = n0_ref[0]
        m_s[...] = m0_ref[0]
        ext_s[0:8, :] = tail0_ref[0]


def _mlstm_state_out(c_out, n_out, m_out, tail_out, c_s, n_s, m_s, ext_s):
    @pl.when(pl.program_id(1) == pl.num_programs(1) - 1)
    def _():
        c_out[0, 0] = c_s[...]
        n_out[0] = n_s[...]
        m_out[0] = m_s[...]
        tail_out[0] = ext_s[0:8, :]


def _mlstm_kernel(p_ref, gc_ref, gr_ref, brow_ref, bcol_ref, cw_ref, cb_ref, c0_ref, n0_ref, m0_ref, tail0_ref,
                  *rest, L, t_valid, n_prev, n_seq):
    cprev_ref = rest[0] if n_prev else None
    y_ref, c_out, n_out, m_out, tail_out, c_s, n_s, m_s, ext_s = rest[1:] if n_prev else rest
    first = pl.program_id(1) == 0
    last = pl.program_id(1) == pl.num_programs(1) - 1
    for e in range(n_seq):
        ce, ne, me, xe = c_s.at[e], n_s.at[e], m_s.at[e], ext_s.at[e]

        @pl.when(first)
        def _(e=e, ce=ce, ne=ne, me=me, xe=xe):
            ce[...] = c0_ref[e]
            ne[...] = n0_ref[e]
            me[...] = m0_ref[e]
            xe[0:8, :] = tail0_ref[e]

        xe[8:8 + L, :] = p_ref[e, :, 0:2 * W_A]
        get_v = lambda h, e=e: p_ref[e, :, 2 * W_A + h * DH_A:2 * W_A + (h + 1) * DH_A]
        _mlstm_core(gc_ref[e] + brow_ref[...], gr_ref[e] + bcol_ref[...], get_v, xe, cw_ref, cb_ref,
                    ce, ne, me, y_ref.at[pl.ds(e, 1)], L, t_valid)

        @pl.when(last)
        def _(e=e, ce=ce, ne=ne, me=me, xe=xe):
            c_out[n_prev, e] = ce[...]
            n_out[e] = ne[...]
            m_out[e] = me[...]
            tail_out[e] = xe[0:8, :]

    if n_prev:
        @pl.when(last)
        def _():
            c_out[0:n_prev] = cprev_ref[...]


def _seq_a_kernel(x_ref, cos_ref, sin_ref, ng_ref, w_aq, w_ak, w_av, w_c0, w_c1, w_c2, wg_ref, brow_ref, bcol_ref,
                  cw_ref, cb_ref, c0_ref, n0_ref, m0_ref, tail0_ref, y_ref, pc_ref, c_out, n_out, m_out, tail_out,
                  c_s, n_s, m_s, ext_s, v_s, *, L):
    _mlstm_state_io(c0_ref, n0_ref, m0_ref, tail0_ref, c_s, n_s, m_s, ext_s)
    nt = (((1,), (1,)), ((), ()))
    x = x_ref[0]
    r = lax.rsqrt(jnp.mean(x * x, axis=-1, keepdims=True) + EPS)
    xn = (x * r * ng_ref[...]).astype(BF16)
    wg = wg_ref[...].astype(BF16)
    g_col = lax.dot_general(xn, wg, nt, preferred_element_type=F32) + brow_ref[...]
    g_row = lax.dot_general(wg[0:8, :], xn, nt, preferred_element_type=F32) + bcol_ref[...]
    proj = lambda w_ref: lax.dot_general(xn, w_ref[...], nt, preferred_element_type=F32)
    ext_s[8:8 + L, 0:W_A] = proj(w_aq)
    ext_s[8:8 + L, W_A:2 * W_A] = proj(w_ak)
    v_s[...] = proj(w_av)
    piece = 256
    todo = [(w_ref, j, lo) for j, w_ref in enumerate((w_c0, w_c1, w_c2)) for lo in range(0, WBLK, piece)]

    assert piece == WG_C

    def emit_piece():
        w_ref, j, lo = todo.pop(0)
        col = j * WBLK + lo
        y = lax.dot_general(xn, w_ref[lo:lo + piece, :], nt, preferred_element_type=F32)
        if col < 2 * W_C:
            cos = jnp.concatenate([cos_ref[0], cos_ref[0]], axis=1)
            sin = jnp.concatenate([sin_ref[0], sin_ref[0]], axis=1)
            y = _rope(y, cos, sin)
            if col < W_C:
                y = y * (DH_C ** -0.5)
        pc_ref[0, :, col:col + piece] = y

    side_work = lambda: emit_piece() if todo else None
    get_v = lambda h: v_s[:, h * DH_A:(h + 1) * DH_A]
    _mlstm_core(g_col, g_row, get_v, ext_s, cw_ref, cb_ref, c_s, n_s, m_s, y_ref, L, L, side_work)
    while todo:
        emit_piece()
    _mlstm_state_out(c_out, n_out, m_out, tail_out, c_s, n_s, m_s, ext_s)


def _seq_a(x, cos_t, sin_t, norm_g, w_main, w_in_t, bias_row, bias_col, conv_w, conv_b, c0, n0, m0, tail0, l, *, L):
    rows = x.shape[0]
    nch = rows // L
    x3 = x.reshape(nch, L, D_MODEL)
    gate_blk = 4 * W_A // GATE_LANES
    once = pl.Buffered(1)
    chunk = lambda b, c: (c, 0, 0)
    layer3 = lambda b, c: (l, 0, 0)
    wblk = lambda blk: pl.BlockSpec((None, WBLK, D_MODEL), lambda b, c, blk=blk: (l, blk, 0), pipeline_mode=once)
    st5 = lambda b, c: (0, 0, 0, 0, 0)
    st4 = lambda b, c: (0, 0, 0, 0)
    first4 = lambda b, c: (0, 0, 0, 0)
    first3 = lambda b, c: (0, 0, 0)
    y, pc, c1, n1, m1, tail1 = pl.pallas_call(
        functools.partial(_seq_a_kernel, L=L),
        grid=(1, nch),
        in_specs=[
            pl.BlockSpec((1, L, D_MODEL), chunk),
            pl.BlockSpec((1, L, 2 * DH_C), chunk),
            pl.BlockSpec((1, L, 2 * DH_C), chunk),
            pl.BlockSpec((None, 1, D_MODEL), layer3, pipeline_mode=once),
        ] + [wblk(b) for b in (0, 1, 2, BLK_C0, BLK_C0 + 1, BLK_C0 + 2)] + [
            pl.BlockSpec((None, GATE_LANES, D_MODEL), lambda b, c: (l, gate_blk, 0), pipeline_mode=once),
            pl.BlockSpec((None, 1, GATE_LANES), layer3, pipeline_mode=once),
            pl.BlockSpec((None, 8, 1), layer3, pipeline_mode=once),
            pl.BlockSpec((None, CONV_W, 2 * W_A), layer3, pipeline_mode=once),
            pl.BlockSpec((None, 1, 2 * W_A), layer3, pipeline_mode=once),
            pl.BlockSpec((None, 1, H_A, DH_A, DH_A), st5, pipeline_mode=once),
            pl.BlockSpec((None, 1, H_A, 1, DH_A), st5, pipeline_mode=once),
            pl.BlockSpec((None, 1, H_A, 1, GATE_LANES), st5, pipeline_mode=once),
            pl.BlockSpec((None, 1, 8, 2 * W_A), st4, pipeline_mode=once),
        ],
        out_specs=[
            pl.BlockSpec((1, L, W_A), chunk),
            pl.BlockSpec((1, L, 3 * WBLK), chunk),
            pl.BlockSpec((1, 1, H_A, DH_A, DH_A), lambda b, c: (0, 0, 0, 0, 0)),
            pl.BlockSpec((1, H_A, 1, DH_A), first4),
            pl.BlockSpec((1, H_A, 1, GATE_LANES), first4),
            pl.BlockSpec((1, 8, 2 * W_A), first3),
        ],
        out_shape=[
            jax.ShapeDtypeStruct((nch, L, W_A), F32),
            jax.ShapeDtypeStruct((nch, L, 3 * WBLK), F32),
            jax.ShapeDtypeStruct((1, 1, H_A, DH_A, DH_A), F32),
            jax.ShapeDtypeStruct((1, H_A, 1, DH_A), F32),
            jax.ShapeDtypeStruct((1, H_A, 1, GATE_LANES), F32),
            jax.ShapeDtypeStruct((1, 8, 2 * W_A), F32),
        ],
        scratch_shapes=[
            pltpu.VMEM((H_A, DH_A, DH_A), F32),
            pltpu.VMEM((H_A, 1, DH_A), F32),
            pltpu.VMEM((H_A, 1, GATE_LANES), F32),
            pltpu.VMEM((L + 8, 2 * W_A), F32),
            pltpu.VMEM((L, W_A), F32),
        ],
        compiler_params=_cparams(2),
        name="seq_a",
    )(x3, cos_t.reshape(nch, L, 2 * DH_C), sin_t.reshape(nch, L, 2 * DH_C), norm_g, *([w_main] * 6), w_in_t,
      bias_row, bias_col, conv_w, conv_b, c0, n0, m0, tail0)
    return (y.reshape(rows, W_A), pc.reshape(rows, 3 * WBLK), c1[0], n1[:, :, 0, :], m1[:, :, 0, 0],
            tail1[:, 8 - (CONV_W - 1):, :])


def _mlstm(p, gcol, grow, bias_row, bias_col, conv_w, conv_b, c0, n0, m0, tail0, l, ls, c_prev, *, L, t_valid):
    batch = c0.shape[1]
    rows = p.shape[0]
    nch = rows // (batch * L)
    n_prev = 0 if c_prev is None else c_prev.shape[0]
    n_seq = MLSTM_SEQS if (nch == 1 and batch % MLSTM_SEQS == 0 and L % GATE_LANES != 0) else 1
    p3 = p.reshape(batch * nch, L, N_P1)
    gc3 = gcol.reshape(batch * nch, L, GATE_LANES)
    tm = grow.shape[2]
    if L % GATE_LANES == 0:
        per = tm // L
        gr3, gr_index = grow, (lambda b, c: ((b * nch + c) // per, 0, (b * nch + c) % per))
    else:
        gr3 = grow.transpose(1, 0, 2).reshape(8, batch * nch, L).transpose(1, 0, 2)
        gr_index = lambda b, c: (b * nch + c, 0, 0)
    kern = functools.partial(_mlstm_kernel, L=L, t_valid=t_valid, n_prev=n_prev, n_seq=n_seq)
    chunk = lambda b, c: (b * nch + c, 0, 0)
    layer3 = lambda b, c: (l, 0, 0)
    stack5 = lambda b, c: (0, b, 0, 0, 0)
    per_b4 = lambda b, c: (b, 0, 0, 0)
    per_b3 = lambda b, c: (b, 0, 0)
    st5 = lambda b, c: (ls, b, 0, 0, 0)
    st4 = lambda b, c: (ls, b, 0, 0)
    y, c1, n1, m1, tail1 = pl.pallas_call(
        kern,
        grid=(batch // n_seq, nch),
        in_specs=[
            pl.BlockSpec((n_seq, L, 3 * W_A), chunk),
            pl.BlockSpec((n_seq, L, GATE_LANES), chunk),
            pl.BlockSpec((n_seq, 8, L), gr_index),
            pl.BlockSpec((None, 1, GATE_LANES), layer3),
            pl.BlockSpec((None, 8, 1), layer3),
            pl.BlockSpec((None, CONV_W, 2 * W_A), layer3),
            pl.BlockSpec((None, 1, 2 * W_A), layer3),
            pl.BlockSpec((None, n_seq, H_A, DH_A, DH_A), st5),
            pl.BlockSpec((None, n_seq, H_A, 1, DH_A), st5),
            pl.BlockSpec((None, n_seq, H_A, 1, GATE_LANES), st5),
            pl.BlockSpec((None, n_seq, 8, 2 * W_A), st4),
        ] + ([pl.BlockSpec((n_prev, n_seq, H_A, DH_A, DH_A), stack5)] if n_prev else []),
        out_specs=[
            pl.BlockSpec((n_seq, L, W_A), chunk),
            pl.BlockSpec((n_prev + 1, n_seq, H_A, DH_A, DH_A), stack5),
            pl.BlockSpec((n_seq, H_A, 1, DH_A), per_b4),
            pl.BlockSpec((n_seq, H_A, 1, GATE_LANES), per_b4),
            pl.BlockSpec((n_seq, 8, 2 * W_A), per_b3),
        ],
        out_shape=[
            jax.ShapeDtypeStruct((batch * nch, L, W_A), F32),
            jax.ShapeDtypeStruct((n_prev + 1, batch, H_A, DH_A, DH_A), F32),
            jax.ShapeDtypeStruct((batch, H_A, 1, DH_A), F32),
            jax.ShapeDtypeStruct((batch, H_A, 1, GATE_LANES), F32),
            jax.ShapeDtypeStruct((batch, 8, 2 * W_A), F32),
        ],
        scratch_shapes=[
            pltpu.VMEM((n_seq, H_A, DH_A, DH_A), F32),
            pltpu.VMEM((n_seq, H_A, 1, DH_A), F32),
            pltpu.VMEM((n_seq, H_A, 1, GATE_LANES), F32),
            pltpu.VMEM((n_seq, L + 8, 2 * W_A), F32),
        ],
        compiler_params=_cparams(2),
        name="mlstm",
    )(p3, gc3, gr3, bias_row, bias_col, conv_w, conv_b, c0, n0, m0, tail0, *([c_prev] if n_prev else []))
    return y.reshape(rows, W_A), c1, n1[:, :, 0, :], m1[:, :, 0, 0], tail1[:, 8 - (CONV_W - 1):, :]


def _rope(x, cos, sin):
    width = x.shape[1]
    lane = lax.broadcasted_iota(jnp.int32, (1, width), 1) % DH_C
    half = ROT_DIM // 2
    partner = jnp.where(lane < half, pltpu.roll(x, width - half, 1), pltpu.roll(x, half, 1))
    return x * cos + partner * sin


def _dil_attn_kernel(q_ref, k_ref, v_ref, o_ref, mx_ref, den_ref, qs, kbuf, vbuf, stage, bias, *, d):
    i = pl.program_id(0)
    T = ATT_TILE
    HL = WG_C // 2
    nblk = T // SPAN_C
    nres = T // (SPAN_C * d)
    cur = (i % 2) * T
    prev = T - cur
    neg_inf = jnp.float32(-jnp.inf)

    def split(x):
        return x[:, 0:HL], x[:, HL:WG_C]

    def put(buf, lo, x):
        a, b = split(x)
        buf[0, pl.ds(lo, T), :] = a
        buf[1, pl.ds(lo, T), :] = b

    @pl.when(i == 0)
    def _():
        ii = lax.broadcasted_iota(jnp.int32, (SPAN_C, 2 * SPAN_C), 0)
        jj = lax.broadcasted_iota(jnp.int32, (SPAN_C, 2 * SPAN_C), 1)
        band = (jj >= ii) & (jj <= ii + SPAN_C)
        bias[1] = jnp.where(band, 0.0, neg_inf)
        bias[0] = jnp.where(band & (jj >= SPAN_C), 0.0, neg_inf)
        kbuf[:, T:2 * T, :] = jnp.zeros((2, T, HL), F32)
        vbuf[:, T:2 * T, :] = jnp.zeros((2, T, HL), F32)

    def put_residue_major(buf, lo, x):
        if d == 1:
            put(buf, lo, x)
            return
        put(stage, 0, x)
        for r in range(d):
            for half in range(2):
                buf[half, pl.ds(lo + r * (T // d), T // d), :] = stage[half, pl.ds(r, T // d, stride=d), :]

    put(qs, 0, q_ref[...])
    put_residue_major(kbuf, cur, k_ref[...])
    put_residue_major(vbuf, cur, v_ref[...])

    lane_head = lax.broadcasted_iota(jnp.int32, (SPAN_C, WG_C), 1) // DH_C
    hm = [(lane_head == h).astype(F32) for h in range(HG_C)]

    def rows(start, size):
        return pl.ds(start, size) if d == 1 else pl.ds(start, size, stride=d)

    def take(buf, start, size):
        return jnp.concatenate([buf[0, rows(start, size), :], buf[1, rows(start, size), :]], axis=1)

    def pick(per_head):
        out = jnp.broadcast_to(per_head[0], (SPAN_C, WG_C))
        for h in range(1, HG_C):
            out = jnp.where(lane_head >= h, per_head[h], out)
        return out

    def body(blk, carry):
        r = blk % d
        c = blk // d
        qstart = r + SPAN_C * d * c
        in_tile = cur + SPAN_C * (r * nres + c)
        before = jnp.where(c > 0, in_tile - SPAN_C, prev + SPAN_C * (r * nres + nres - 1))
        in_tile, before = pl.multiple_of(in_tile, SPAN_C), pl.multiple_of(before, SPAN_C)

        def keys(buf, start):
            return jnp.concatenate([buf[0, pl.ds(start, SPAN_C), :], buf[1, pl.ds(start, SPAN_C), :]], axis=1)

        qb = take(qs, qstart, SPAN_C)
        kb = jnp.concatenate([keys(kbuf, before), keys(kbuf, in_tile)], axis=0)
        vb = jnp.concatenate([keys(vbuf, before), keys(vbuf, in_tile)], axis=0)
        has_prev = jnp.logical_or(i > 0, c > 0).astype(jnp.int32)
        kb16, vb16 = kb.astype(BF16), vb.astype(BF16)
        pvs, mxs, dens = [], [], []
        for h0 in range(0, HG_C, 2):
            q2 = jnp.concatenate([qb * hm[h0], qb * hm[h0 + 1]], axis=0)
            s = _dot_nt(q2, kb16).reshape(2, SPAN_C, 2 * SPAN_C) + bias[has_prev]
            mx = jnp.max(s, axis=2, keepdims=True)
            p = jnp.exp(s - mx)
            den = jnp.sum(p, axis=2, keepdims=True)
            pv = _dot(p.reshape(2 * SPAN_C, 2 * SPAN_C), vb16).reshape(2, SPAN_C, WG_C)
            pvs += [pv[0], pv[1]]
            mxs += [mx[0], mx[1]]
            dens += [den[0], den[1]]
        outs = (pick(pvs), pick(mxs), pick(dens))
        for ref, val in zip((o_ref, mx_ref, den_ref), outs):
            for half, part in enumerate(split(val)):
                ref[half, rows(qstart, SPAN_C), :] = part
        return carry

    lax.fori_loop(0, nblk, body, 0, unroll=16)


def _dil_attn(p, base, g, d):
    rows = p.shape[0]
    T = ATT_TILE
    kern = functools.partial(_dil_attn_kernel, d=d)
    col = lambda off: pl.BlockSpec((T, WG_C), lambda i, off=off: (i, (base + off - OFF_CQ) // WG_C + g))
    slab = pl.BlockSpec((2, T, WG_C // 2), lambda i: (0, i, 0))
    slab_shape = jax.ShapeDtypeStruct((2, rows, WG_C // 2), F32)
    return pl.pallas_call(
        kern,
        grid=(rows // T,),
        in_specs=[col(OFF_CQ), col(OFF_CK), col(OFF_CV)],
        out_specs=[slab, slab, slab],
        out_shape=[slab_shape, slab_shape, slab_shape],
        scratch_shapes=[pltpu.VMEM((2, T, WG_C // 2), F32), pltpu.VMEM((2, 2 * T, WG_C // 2), F32),
                        pltpu.VMEM((2, 2 * T, WG_C // 2), F32), pltpu.VMEM((2, T, WG_C // 2), F32),
                        pltpu.VMEM((2, SPAN_C, 2 * SPAN_C), F32)],
        compiler_params=_cparams(1),
        name="dil_attn_d%d" % d,
    )(p, p, p)


def _sample_attn_kernel(pc0, pc1, pc2, cos_ref, sin_ref, c0_ref, c1_ref, c2_ref,
                        o0, o1, o2, m0, m1, m2, d0, d1, d2, kr_ref, *, n_tok, n_seq):
    reps = W_C // (2 * DH_C)
    cos = jnp.concatenate([cos_ref[...]] * reps, axis=1)
    sin = jnp.concatenate([sin_ref[...]] * reps, axis=1)
    nrow = HG_C * SAMPLE_ROWS
    rowh = lax.broadcasted_iota(jnp.int32, (nrow, WG_C), 0) // SAMPLE_ROWS
    laneh = lax.broadcasted_iota(jnp.int32, (nrow, WG_C), 1) // DH_C
    hm = (rowh == laneh).astype(F32)
    tok_col = lax.broadcasted_iota(jnp.int32, (nrow, 1), 0) % SAMPLE_ROWS
    out_row = lax.broadcasted_iota(jnp.int32, (SAMPLE_ROWS, WG_C), 0)
    neg_inf = jnp.float32(-jnp.inf)
    crefs = (c0_ref, c1_ref, c2_ref)
    orefs = (o0, o1, o2)
    mrefs = (m0, m1, m2)
    drefs = (d0, d1, d2)
    for e in range(n_seq):
        rows_e = slice(e * SAMPLE_ROWS, (e + 1) * SAMPLE_ROWS)
        pc = jnp.concatenate([pc0[e], pc1[e], pc2[e]], axis=1)
        qr = _rope(pc[:, 0:W_C], cos, sin) * (DH_C ** -0.5)
        kr = _rope(pc[:, W_C:2 * W_C], cos, sin)
        v = pc[:, 2 * W_C:3 * W_C]
        kr_ref[e] = kr
        for g, (_, d) in enumerate(DIL_PATTERNS):
            gs = slice(g * WG_C, (g + 1) * WG_C)
            n_buf = SPAN_C * d
            kn = kr[:, gs]
            vn = v[:, gs]
            qm = jnp.concatenate([qr[:, gs]] * HG_C, axis=0) * hm
            s = _dot(qm, crefs[g][e, 0])
            tok = lax.broadcasted_iota(jnp.int32, (nrow, n_buf), 0) % SAMPLE_ROWS
            pos = lax.broadcasted_iota(jnp.int32, (nrow, n_buf), 1)
            attends = (pos >= tok) if d == 1 else ((pos & (d - 1)) == tok)
            s = jnp.where(attends | (tok >= n_tok), s, neg_inf)
            new_ok = [(u <= tok_col) if d == 1 else (u == tok_col) for u in range(n_tok)]
            s_new = [jnp.where(ok, jnp.sum(qm * kn[u:u + 1, :], axis=1, keepdims=True), neg_inf)
                     for u, ok in enumerate(new_ok)]
            mx = jnp.max(s, axis=1, keepdims=True)
            for sn in s_new:
                mx = jnp.maximum(mx, sn)
            p = jnp.exp(s - mx)
            den = jnp.sum(p, axis=1, keepdims=True)
            acc = _dot_nt(p, crefs[g][e, 1])
            for u, sn in enumerate(s_new):
                pn = jnp.exp(sn - mx)
                den = den + pn
                acc = acc + pn * vn[u:u + 1, :]
            stacked = (acc * hm, mx * hm, den * hm)
            for ref, val, pad in zip((orefs[g], mrefs[g], drefs[g]), stacked, (0.0, 0.0, 1.0)):
                v8 = val[0:SAMPLE_ROWS]
                for h in range(1, HG_C):
                    v8 = v8 + val[h * SAMPLE_ROWS:(h + 1) * SAMPLE_ROWS]
                v8 = jnp.where(out_row < n_tok, v8, pad)
                for half in range(2):
                    ref[half, rows_e, :] = v8[:, half * (WG_C // 2):(half + 1) * (WG_C // 2)]


def _cache_views(caches):
    views = []
    for (win, d), c in zip(DIL_PATTERNS, caches):
        depth, batch, n_buf = c.shape[:3]
        assert n_buf == SPAN_C * d
        views.append(jnp.transpose(c, (0, 1, 3, 4, 5, 2)).reshape(depth, batch, 2, WG_C, n_buf))
    return views


def _sample_attn(p, cos_s, sin_s, views, l, n_tok):
    batch = p.shape[0] // SAMPLE_ROWS
    n_seq = SAMPLE_ATT_SEQS if batch % SAMPLE_ATT_SEQS == 0 else 1
    p3 = p.reshape(batch, SAMPLE_ROWS, N_P1)
    pcol = lambda k: pl.BlockSpec((n_seq, SAMPLE_ROWS, WBLK), lambda b, k=k: (b, 0, OFF_CQ // WBLK + k))
    tab = pl.BlockSpec((SAMPLE_ROWS, 2 * DH_C), lambda b: (0, 0))
    cache_spec = lambda d: pl.BlockSpec((None, n_seq, 2, WG_C, SPAN_C * d), lambda b: (l, b, 0, 0, 0))
    out = pl.BlockSpec((2, n_seq * SAMPLE_ROWS, WG_C // 2), lambda b: (0, b, 0))
    res = pl.pallas_call(
        functools.partial(_sample_attn_kernel, n_tok=n_tok, n_seq=n_seq),
        grid=(batch // n_seq,),
        in_specs=[pcol(0), pcol(1), pcol(2), tab, tab] + [cache_spec(d) for _, d in DIL_PATTERNS],
        out_specs=[out] * 9 + [pl.BlockSpec((n_seq, SAMPLE_ROWS, W_C), lambda b: (b, 0, 0))],
        out_shape=[jax.ShapeDtypeStruct((2, batch * SAMPLE_ROWS, WG_C // 2), F32)] * 9
        + [jax.ShapeDtypeStruct((batch, SAMPLE_ROWS, W_C), F32)],
        compiler_params=_cparams(1),
        name="sample_attn",
    )(p3, p3, p3, cos_s, sin_s, *views)
    return list(res[:9]), res[9].reshape(batch * SAMPLE_ROWS, W_C)


def _post_kernel(x_ref, h_ref, o0, o1, o2, m0, m1, m2, d0, d1, d2, cz_ref, ng_ref, w_ao, w_az, w_bu, w_bv, w_bz, w_ga, w_gb,
                 w_gc, lg_ref, lb_ref, wsp_ref, bsp_ref, wb_ref, wo_ref, fg_ref, out_ref, *rest, nchunks, final,
                 emit_vn):
    if emit_vn:
        vn_out, vn_s, yb_s = rest
    else:
        vn_s, yb_s = rest
    nt = (((1,), (1,)), ((), ()))
    x = x_ref[...]
    r = lax.rsqrt(jnp.mean(x * x, axis=-1, keepdims=True) + EPS)
    xn = (x * r * ng_ref[...]).astype(BF16)
    proj = lambda w_ref: lax.dot_general(xn, w_ref[...], nt, preferred_element_type=F32)
    branch = lambda y, lo, hi: jnp.dot(y.astype(BF16), wb_ref[lo:hi, :], preferred_element_type=F32)

    ya = h_ref[...] * _sigmoid(proj(w_ao)) * _silu(proj(w_az))
    acc = _sigmoid(proj(w_ga)) * branch(ya, 0, W_A)

    v = proj(w_bv)
    mu = jnp.mean(v, axis=-1, keepdims=True)
    var = jnp.mean(jnp.square(v - mu), axis=-1, keepdims=True)
    vn = (v - mu) * lax.rsqrt(var + EPS) * lg_ref[...] + lb_ref[...]
    if emit_vn:
        vn_out[...] = vn
    vn_s[...] = vn.astype(BF16)
    gate_b = proj(w_bu) * _silu(proj(w_bz))
    tri = (lax.broadcasted_iota(jnp.int32, (CHUNK_B, CHUNK_B), 0)
           >= lax.broadcasted_iota(jnp.int32, (CHUNK_B, CHUNK_B), 1))
    for g in range(G_B):
        wg = jnp.where(tri, wsp_ref[g], 0.0).astype(BF16)
        cs = slice(g * DG_B, (g + 1) * DG_B)
        for c in range(nchunks):
            rs = slice(c * CHUNK_B, (c + 1) * CHUNK_B)
            yb_s[rs, cs] = jnp.dot(wg, vn_s[rs, cs], preferred_element_type=F32) + bsp_ref[g]
    acc = acc + _sigmoid(proj(w_gb)) * branch(gate_b * yb_s[...], W_A, W_A + W_B)

    unslab = lambda ref: jnp.concatenate([ref[0], ref[1]], axis=1)
    ms = [unslab(m0), unslab(m1), unslab(m2)]
    mx = jnp.maximum(jnp.maximum(ms[0], ms[1]), ms[2])
    es = [jnp.exp(m - mx) for m in ms]
    inv_tot = 1.0 / (es[0] * unslab(d0) + es[1] * unslab(d1) + es[2] * unslab(d2))
    pc = None
    for g, o_ref in enumerate((o0, o1, o2)):
        gs = slice(g * WG_C, (g + 1) * WG_C)
        yc = unslab(o_ref) * (es[g] * inv_tot) * _silu(cz_ref[:, gs])
        t = branch(yc, W_A + W_B + g * WG_C, W_A + W_B + (g + 1) * WG_C)
        pc = t if pc is None else pc + t
    acc = acc + _sigmoid(proj(w_gc)) * pc

    out = x + jnp.dot(acc.astype(BF16), wo_ref[...], preferred_element_type=F32)
    if final:
        rr = lax.rsqrt(jnp.mean(out * out, axis=-1, keepdims=True) + EPS)
        out = out * rr * fg_ref[...]
    out_ref[...] = out


def _post(x, p1, base, h, att, norm_g, w_main, ln_g, ln_b, w_sp, b_sp_col, w_branch, w_out, final_g, l, final,
          emit_vn):
    rows = x.shape[0]
    tm = min(rows, POST_TM)
    once = pl.Buffered(1)
    row = lambda w: pl.BlockSpec((tm, w), lambda i: (i, 0))
    slab = pl.BlockSpec((2, tm, WG_C // 2), lambda i: (0, i, 0))
    wblk = lambda b: pl.BlockSpec((None, WBLK, D_MODEL), lambda i, b=b: (l, b, 0), pipeline_mode=once)
    vec = lambda w: pl.BlockSpec((None, 1, w), lambda i: (l, 0, 0), pipeline_mode=once)
    in_specs = (
        [row(D_MODEL), row(W_A)] + [slab] * 9
        + [pl.BlockSpec((tm, W_C), lambda i: (i, (base + OFF_CZ - OFF_CQ) // W_C)), vec(D_MODEL)]
        + [wblk(b) for b in (BLK_AO, BLK_AZ, BLK_BU, BLK_BV, BLK_BZ, BLK_GA, BLK_GB, BLK_GC)]
        + [vec(W_B), vec(W_B),
           pl.BlockSpec((None, G_B, CHUNK_B, CHUNK_B), lambda i: (l, 0, 0, 0), pipeline_mode=once),
           pl.BlockSpec((None, G_B, CHUNK_B, 1), lambda i: (l, 0, 0, 0), pipeline_mode=once),
           pl.BlockSpec((None, W_A + W_B + W_C, D_MODEL), lambda i: (l, 0, 0), pipeline_mode=once),
           pl.BlockSpec((None, D_MODEL, D_MODEL), lambda i: (l, 0, 0), pipeline_mode=once),
           pl.BlockSpec((1, D_MODEL), lambda i: (0, 0), pipeline_mode=once)])
    out_specs = [row(D_MODEL)]
    out_shape = [jax.ShapeDtypeStruct((rows, D_MODEL), F32)]
    if emit_vn:
        out_specs.append(row(W_B))
        out_shape.append(jax.ShapeDtypeStruct((rows, W_B), F32))
    res = pl.pallas_call(
        functools.partial(_post_kernel, nchunks=tm // CHUNK_B, final=final, emit_vn=emit_vn),
        grid=(rows // tm,),
        in_specs=in_specs,
        out_specs=out_specs,
        out_shape=out_shape,
        scratch_shapes=[pltpu.VMEM((tm, W_B), BF16), pltpu.VMEM((tm, W_B), F32)],
        compiler_params=_cparams(1),
        name="post",
    )(x, h, *att, p1, norm_g, *([w_main] * 8), ln_g, ln_b, w_sp, b_sp_col, w_branch, w_out, final_g)
    return (res[0], res[1]) if emit_vn else (res[0], None)


def _rope_tables(pos):
    half = ROT_DIM // 2
    inv = ROPE_THETA ** (-jnp.arange(half, dtype=F32) / half)
    ang = inv[:, None] * pos.astype(F32)[None, :]
    lane = jnp.arange(2 * DH_C) % DH_C
    freq = jnp.arange(half)[:, None] == (lane % half)[None, :]
    rot = (lane < ROT_DIM)[None, :]
    sel_cos = (freq & rot).astype(F32)
    sel_sin = sel_cos * jnp.where(lane < half, -1.0, 1.0)[None, :]
    spread = lambda t, sel: lax.dot_general(t, sel, (((0,), (0,)), ((), ())), precision=lax.Precision.HIGHEST)
    return spread(jnp.cos(ang), sel_cos) + (~rot).astype(F32), spread(jnp.sin(ang), sel_sin)


def _stack_kv(k_rows, v_rows):
    b, t, _ = k_rows.shape
    return jnp.stack([k_rows.reshape(b, t, HG_C, DH_C), v_rows.reshape(b, t, HG_C, DH_C)], axis=2)


def kernel(x_prompt, x_sample, state_C, state_n, state_m, state_conv, cache_kv_w128, cache_kv_w512, cache_kv_w2048,
           norm_g, w_in, b_igate, b_fgate, conv_w, conv_b, ln_v_g, ln_v_b, w_spatial, b_spatial, w_branch, w_out,
           final_norm_g):
    depth = w_in.shape[0]
    bp, seq, _ = x_prompt.shape
    bs, n_tok, _ = x_sample.shape
    assert bp == 1 and seq % ATT_TILE == 0 and n_tok <= SAMPLE_ROWS // 2 and n_tok >= CONV_W - 1
    caches = (cache_kv_w128, cache_kv_w512, cache_kv_w2048)
    pad_tok = SAMPLE_ROWS - n_tok
    rows_s = bs * SAMPLE_ROWS

    w_in_t = jnp.swapaxes(w_in, 1, 2)
    w_main = _pack_w_in(w_in_t)
    gate_bias = jnp.concatenate([b_igate, b_fgate], axis=1)
    bias_row = jnp.pad(gate_bias, ((0, 0), (0, GATE_LANES - 2 * H_A)))[:, None, :]
    bias_col = gate_bias[:, :, None]
    wb16 = w_branch.astype(BF16)
    wo16 = w_out.astype(BF16)
    b_sp_col = b_spatial[..., None]
    norm_g3, conv_b3 = norm_g[:, None, :], conv_b[:, None, :]
    ln_g3, ln_b3 = ln_v_g[:, None, :], ln_v_b[:, None, :]
    seqs_per_chunk = CHUNK_B // SAMPLE_ROWS
    w8 = jnp.pad(w_spatial[:, :, :n_tok, :n_tok], ((0, 0), (0, 0), (0, pad_tok), (0, pad_tok)))
    w_sp_s = jnp.einsum('ab,lgij->lgaibj', jnp.eye(seqs_per_chunk, dtype=F32), w8).reshape(
        depth, G_B, CHUNK_B, CHUNK_B)
    b_sp_s = jnp.tile(jnp.pad(b_spatial[:, :, :n_tok], ((0, 0), (0, 0), (0, pad_tok))),
                      (1, 1, seqs_per_chunk))[..., None]

    cos_p, sin_p = _rope_tables(jnp.arange(seq))
    cos_s, sin_s = _rope_tables(PAST_LEN + jnp.arange(SAMPLE_ROWS))

    hp = x_prompt.reshape(seq, D_MODEL)
    hs = jnp.pad(x_sample, ((0, 0), (0, pad_tok), (0, 0))).reshape(rows_s, D_MODEL)
    fg = final_norm_g[None, :]
    zeros_c = jnp.zeros((1, 1, H_A, DH_A, DH_A), F32)
    zeros_n = jnp.zeros((1, 1, H_A, 1, DH_A), F32)
    zeros_m = jnp.zeros((1, 1, H_A, 1, GATE_LANES), F32)
    zeros_tail = jnp.zeros((1, 1, 8, 2 * W_A), F32)
    n0_s = state_n[:, :, :, None, :]
    m0_s = jnp.broadcast_to(state_m[:, :, :, None, None], (depth, bs, H_A, 1, GATE_LANES))
    tail0_s = jnp.pad(state_conv, ((0, 0), (0, 0), (8 - (CONV_W - 1), 0), (0, 0)))
    views = _cache_views(caches)

    c_stack_s = None
    p_out = {k: [] for k in ('C', 'n', 'm', 'conv', 'kv0', 'kv1', 'kv2')}
    s_out = {k: [] for k in ('n', 'm', 'conv', 'chunk_v', 'kv0', 'kv1', 'kv2')}
    for l in range(depth):
        final = l == depth - 1

        ya, pp, c1, n1, m1, conv1 = _seq_a(hp, cos_p, sin_p, norm_g3, w_main, w_in_t, bias_row, bias_col, conv_w, conv_b3,
                                           zeros_c, zeros_n, zeros_m, zeros_tail, l, L=MLSTM_L)
        att = [[], [], []]
        for g, (win, d) in enumerate(DIL_PATTERNS):
            for dst, val in zip(att, _dil_attn(pp, 0, g, d)):
                dst.append(val)
            keep = min(win, seq)
            k_g = pp[seq - keep:, OFF_CK - OFF_CQ + g * WG_C:OFF_CK - OFF_CQ + (g + 1) * WG_C]
            v_g = pp[seq - keep:, OFF_CV - OFF_CQ + g * WG_C:OFF_CV - OFF_CQ + (g + 1) * WG_C]
            p_out['kv%d' % g].append(_stack_kv(k_g[None], v_g[None]))
        hp, _ = _post(hp, pp, 0, ya, att[0] + att[1] + att[2], norm_g3, w_main, ln_g3, ln_b3, w_spatial, b_sp_col, wb16, wo16, fg, l,
                      final, False)
        p_out['C'].append(c1)
        p_out['n'].append(n1)
        p_out['m'].append(m1)
        p_out['conv'].append(conv1)

        ps, gcol, grow = _inproj(hs, norm_g3, w_main, w_in_t, l)
        ya, c1, n1, m1, conv1 = _mlstm(ps, gcol, grow, bias_row, bias_col, conv_w, conv_b3,
                                       state_C, n0_s, m0_s, tail0_s, l, l, c_stack_s, L=SAMPLE_ROWS, t_valid=n_tok)
        c_stack_s = c1
        att_s, kr = _sample_attn(ps, cos_s, sin_s, views, l, n_tok)
        hs, vn = _post(hs, ps, OFF_CQ, ya, att_s, norm_g3, w_main, ln_g3, ln_b3, w_sp_s, b_sp_s, wb16, wo16, fg, l,
                       final, True)
        s_out['n'].append(n1)
        s_out['m'].append(m1)
        s_out['conv'].append(conv1)
        s_out['chunk_v'].append(vn.reshape(bs, SAMPLE_ROWS, W_B)[:, :n_tok])
        kr3 = kr.reshape(bs, SAMPLE_ROWS, W_C)[:, :n_tok]
        v3 = ps[:, OFF_CV:OFF_CV + W_C].reshape(bs, SAMPLE_ROWS, W_C)[:, :n_tok]
        for g in range(len(DIL_PATTERNS)):
            gs = slice(g * WG_C, (g + 1) * WG_C)
            s_out['kv%d' % g].append(_stack_kv(kr3[:, :, gs], v3[:, :, gs]))

    stk = lambda d, k: jnp.stack(d[k], axis=0)
    y_prompt = hp.reshape(bp, seq, D_MODEL)
    y_sample = hs.reshape(bs, SAMPLE_ROWS, D_MODEL)[:, :n_tok]
    return (y_prompt, y_sample,
            stk(p_out, 'C'), stk(p_out, 'n'), stk(p_out, 'm'), stk(p_out, 'conv'),
            stk(p_out, 'kv0'), stk(p_out, 'kv1'), stk(p_out, 'kv2'),
            c_stack_s, stk(s_out, 'n'), stk(s_out, 'm'), stk(s_out, 'conv'), stk(s_out, 'chunk_v'),
            stk(s_out, 'kv0'), stk(s_out, 'kv1'), stk(s_out, 'kv2'))
```

```python
import functools

import jax
import jax.numpy as jnp
from jax import lax
from jax.experimental import pallas as pl
from jax.experimental.pallas import tpu as pltpu

F32 = jnp.float32
BF16 = jnp.bfloat16

D_MODEL = 1024
H_A = 4
DH_A = 256
W_A = H_A * DH_A
CONV_W = 4
G_B = 4
CHUNK_B = 128
W_B = 1024
DG_B = W_B // G_B
DIL_PATTERNS = ((128, 1), (512, 4), (2048, 16))
HG_C = 4
DH_C = 64
WG_C = HG_C * DH_C
W_C = len(DIL_PATTERNS) * WG_C
SPAN_C = 128
ROT_DIM = DH_C // 4
ROPE_THETA = 500000.0
EPS = 1e-6
PAST_LEN = 16384

N_PACK = 14336
WBLK = 1024
BLK_AO, BLK_AZ, BLK_BU, BLK_BV, BLK_BZ, BLK_C0, BLK_GA, BLK_GB, BLK_GC = 3, 4, 5, 6, 7, 8, 11, 12, 13
N_P1 = 6144
OFF_CQ, OFF_CK, OFF_CV, OFF_CZ = 3072, 3840, 4608, 5376
GATE_LANES = 128

INPROJ_TM = 2048
POST_TM = 512
MLSTM_BAND = 128
MLSTM_L = 256
SAMPLE_ROWS = 8
MLSTM_SEQS = 4
SAMPLE_ATT_SEQS = 4
ATT_TILE = 2048
VMEM_LIMIT = 56 * 1024 * 1024


def _cparams(n_axes):
    return pltpu.CompilerParams(dimension_semantics=("arbitrary",) * n_axes, vmem_limit_bytes=VMEM_LIMIT)


def _dot(a, b):
    return jnp.dot(a.astype(BF16), b.astype(BF16), preferred_element_type=F32)


def _dot_nt(a, b):
    return lax.dot_general(a.astype(BF16), b.astype(BF16), (((1,), (1,)), ((), ())), preferred_element_type=F32)


def _sigmoid(x):
    return 0.5 * jnp.tanh(0.5 * x) + 0.5


def _silu(x):
    return x * _sigmoid(x)


def _inproj_kernel(x_ref, g_ref, w_ref, wg_ref, p_ref, gc_ref, gr_ref, xn_ref):
    nt = (((1,), (1,)), ((), ()))

    @pl.when(pl.program_id(1) == 0)
    def _():
        x = x_ref[...]
        r = lax.rsqrt(jnp.mean(x * x, axis=-1, keepdims=True) + EPS)
        xn = (x * r * g_ref[...]).astype(BF16)
        xn_ref[...] = xn
        wg = wg_ref[...].astype(BF16)
        gc_ref[...] = lax.dot_general(xn, wg, nt, preferred_element_type=F32)
        gr_ref[...] = lax.dot_general(wg[0:8, :], xn, nt, preferred_element_type=F32)

    p_ref[...] = lax.dot_general(xn_ref[...], w_ref[...], nt, preferred_element_type=F32)


def _inproj(x, norm_g, w_main, w_in_t, l):
    rows = x.shape[0]
    gate_blk = 4 * W_A // GATE_LANES
    tm = min(rows, INPROJ_TM)
    tn = WBLK
    n_a = 3 * W_A // tn
    wblk = lambda i, j: (l, jnp.where(j < n_a, j, j + (BLK_C0 - n_a)), 0)
    return pl.pallas_call(
        _inproj_kernel,
        grid=(rows // tm, N_P1 // tn),
        in_specs=[
            pl.BlockSpec((tm, D_MODEL), lambda i, j: (i, 0)),
            pl.BlockSpec((None, 1, D_MODEL), lambda i, j: (l, 0, 0)),
            pl.BlockSpec((None, tn, D_MODEL), wblk),
            pl.BlockSpec((None, GATE_LANES, D_MODEL), lambda i, j: (l, gate_blk, 0)),
        ],
        out_specs=[
            pl.BlockSpec((tm, tn), lambda i, j: (i, j)),
            pl.BlockSpec((tm, GATE_LANES), lambda i, j: (i, 0)),
            pl.BlockSpec((None, 8, tm), lambda i, j: (i, 0, 0)),
        ],
        out_shape=[
            jax.ShapeDtypeStruct((rows, N_P1), F32),
            jax.ShapeDtypeStruct((rows, GATE_LANES), F32),
            jax.ShapeDtypeStruct((rows // tm, 8, tm), F32),
        ],
        scratch_shapes=[pltpu.VMEM((tm, D_MODEL), BF16)],
        compiler_params=_cparams(2),
        name="inproj",
    )(x, norm_g, w_main, w_in_t)


def _pack_kernel(a_ref, b_ref, w_ref, *, first_shifted):
    j = pl.program_id(1)

    @pl.when(j < first_shifted)
    def _():
        w_ref[...] = a_ref[...].astype(BF16)

    @pl.when(j >= first_shifted)
    def _():
        w_ref[...] = jnp.concatenate([a_ref[2 * H_A:, :], b_ref[...]], axis=0).astype(BF16)


def _pack_w_in(w_in_t):
    depth = w_in_t.shape[0]
    tn = 2048
    gate_off = 4 * W_A
    assert gate_off % tn == 0 and w_in_t.shape[1] == N_PACK + 2 * H_A and 2 * H_A == 8
    kern = functools.partial(_pack_kernel, first_shifted=gate_off // tn)
    return pl.pallas_call(
        kern,
        grid=(depth, N_PACK // tn),
        in_specs=[
            pl.BlockSpec((None, tn, D_MODEL), lambda l, j: (l, j, 0)),
            pl.BlockSpec((None, 8, D_MODEL), lambda l, j: (l, (j + 1) * (tn // 8), 0)),
        ],
        out_specs=pl.BlockSpec((None, tn, D_MODEL), lambda l, j: (l, j, 0)),
        out_shape=jax.ShapeDtypeStruct((depth, N_PACK, D_MODEL), BF16),
        compiler_params=_cparams(2),
        name="pack_w_in",
    )(w_in_t, w_in_t)


def _conv_silu(ext_ref, cols, w, b, L):
    y = b
    for back in range(CONV_W):
        y = y + ext_ref[8 - back:8 - back + L, cols] * w[CONV_W - 1 - back:CONV_W - back]
    return _silu(y)


def _mlstm_core(g_col, g_row, get_v, ext_s, cw_ref, cb_ref, c_s, n_s, m_s, y_ref, L, t_valid, side_work=None):
    tick = side_work if side_work is not None else (lambda: None)
    neg_inf = jnp.float32(-jnp.inf)
    ig_col, lf_col = g_col, jax.nn.log_sigmoid(g_col)
    ig_row, lf_row = g_row, jax.nn.log_sigmoid(g_row)
    if t_valid < L:
        vc = lax.broadcasted_iota(jnp.int32, (L, GATE_LANES), 0) < t_valid
        vr = lax.broadcasted_iota(jnp.int32, (8, L), 1) < t_valid
        ig_col, lf_col = jnp.where(vc, ig_col, neg_inf), jnp.where(vc, lf_col, 0.0)
        ig_row, lf_row = jnp.where(vr, ig_row, neg_inf), jnp.where(vr, lf_row, 0.0)
    ti = lax.broadcasted_iota(jnp.int32, (L, L), 0)
    si = lax.broadcasted_iota(jnp.int32, (L, L), 1)
    lower = (ti >= si).astype(F32)
    upper = (ti <= si).astype(F32)
    if L <= 16:
        b_col = lower[:, 0:1] * lf_col[0:1, :]
        b_row = lf_row[:, 0:1] * upper[0:1, :]
        for s in range(1, L):
            b_col = b_col + lower[:, s:s + 1] * lf_col[s:s + 1, :]
            b_row = b_row + lf_row[:, s:s + 1] * upper[s:s + 1, :]
    else:
        b_col = jnp.dot(lower, lf_col, preferred_element_type=F32, precision=lax.Precision.HIGHEST)
        b_row = jnp.dot(lf_row, upper, preferred_element_type=F32, precision=lax.Precision.HIGHEST)
    last = t_valid - 1
    band = min(L, MLSTM_BAND)
    nt = (((1,), (1,)), ((), ()))

    for h in range(H_A):
        cs = slice(h * DH_A, (h + 1) * DH_A)
        ks = slice(W_A + h * DH_A, W_A + (h + 1) * DH_A)
        q = _conv_silu(ext_s, cs, cw_ref[:, cs], cb_ref[:, cs], L)
        tick()
        k = _conv_silu(ext_s, ks, cw_ref[:, ks], cb_ref[:, ks], L) * (DH_A ** -0.5)
        tick()
        v = get_v(h)
        q16, k16, v16 = q.astype(BF16), k.astype(BF16), v.astype(BF16)
        bc = b_col[:, H_A + h:H_A + h + 1]
        igc = ig_col[:, h:h + 1]
        br = b_row[H_A + h:H_A + h + 1, :]
        igr = ig_row[h:h + 1, :]
        m_prev = m_s[h][:, 0:1]
        c_prev = c_s[h]
        n_prev = n_s[h]
        c16 = c_prev.astype(BF16)

        for r in range(L // band):
            rs = slice(r * band, (r + 1) * band)
            kw = (r + 1) * band
            ti = lax.broadcasted_iota(jnp.int32, (band, kw), 0) + r * band
            si = lax.broadcasted_iota(jnp.int32, (band, kw), 1)
            logw = jnp.where(ti >= si, bc[rs] - br[:, 0:kw] + igr[:, 0:kw], neg_inf)
            inter = bc[rs] + m_prev
            m_t = jnp.maximum(inter, jnp.max(logw, axis=1, keepdims=True))
            w_intra = jnp.exp(logw - m_t)
            w_inter = jnp.exp(inter - m_t)
            s = w_intra * lax.dot_general(q16[rs], k16[0:kw], nt, preferred_element_type=F32)
            num = (jnp.dot(s.astype(BF16), v16[0:kw], preferred_element_type=F32)
                   + w_inter * lax.dot_general(q16[rs], c16, nt, preferred_element_type=F32))
            den = (jnp.sum(s, axis=1, keepdims=True)
                   + w_inter * jnp.sum(q[rs] * n_prev, axis=1, keepdims=True))
            y_ref[0, rs, cs] = num / jnp.maximum(jnp.abs(den), jnp.exp(-m_t))
            if r == last // band:
                m_new = m_t[last - r * band:last - r * band + 1, :]
            tick()

        b_last = bc[last:last + 1, :]
        decay = jnp.exp(b_last + m_prev - m_new)
        w_s = jnp.exp(b_last - bc + igc - m_new)
        c_s[h] = decay * c_prev + lax.dot_general((w_s * v).astype(BF16), k16, (((0,), (0,)), ((), ())),
                                                  preferred_element_type=F32)
        n_s[h] = decay * n_prev + jnp.sum(w_s * k, axis=0, keepdims=True)
        m_s[h] = jnp.broadcast_to(m_new, (1, GATE_LANES))
        tick()

    ext_s[0:8, :] = ext_s[t_valid:t_valid + 8, :]


def _mlstm_state_io(c0_ref, n0_ref, m0_ref, tail0_ref, c_s, n_s, m_s, ext_s):
    @pl.when(pl.program_id(1) == 0)
    def _():
        c_s[...] = c0_ref[0]
        n_s[...] = n0_ref[0]
        m_s[...] = m0_ref[0]
        ext_s[0:8, :] = tail0_ref[0]


def _mlstm_state_out(c_out, n_out, m_out, tail_out, c_s, n_s, m_s, ext_s):
    @pl.when(pl.program_id(1) == pl.num_programs(1) - 1)
    def _():
        c_out[0, 0] = c_s[...]
        n_out[0] = n_s[...]
        m_out[0] = m_s[...]
        tail_out[0] = ext_s[0:8, :]


def _mlstm_kernel(p_ref, gc_ref, gr_ref, brow_ref, bcol_ref, cw_ref, cb_ref, c0_ref, n0_ref, m0_ref, tail0_ref,
                  *rest, L, t_valid, n_prev, n_seq):
    cprev_ref = rest[0] if n_prev else None
    y_ref, c_out, n_out, m_out, tail_out, c_s, n_s, m_s, ext_s = rest[1:] if n_prev else rest
    first = pl.program_id(1) == 0
    last = pl.program_id(1) == pl.num_programs(1) - 1
    for e in range(n_seq):
        ce, ne, me, xe = c_s.at[e], n_s.at[e], m_s.at[e], ext_s.at[e]

        @pl.when(first)
        def _(e=e, ce=ce, ne=ne, me=me, xe=xe):
            ce[...] = c0_ref[e]
            ne[...] = n0_ref[e]
            me[...] = m0_ref[e]
            xe[0:8, :] = tail0_ref[e]

        xe[8:8 + L, :] = p_ref[e, :, 0:2 * W_A]
        get_v = lambda h, e=e: p_ref[e, :, 2 * W_A + h * DH_A:2 * W_A + (h + 1) * DH_A]
        _mlstm_core(gc_ref[e] + brow_ref[...], gr_ref[e] + bcol_ref[...], get_v, xe, cw_ref, cb_ref,
                    ce, ne, me, y_ref.at[pl.ds(e, 1)], L, t_valid)

        @pl.when(last)
        def _(e=e, ce=ce, ne=ne, me=me, xe=xe):
            c_out[n_prev, e] = ce[...]
            n_out[e] = ne[...]
            m_out[e] = me[...]
            tail_out[e] = xe[0:8, :]

    if n_prev:
        @pl.when(last)
        def _():
            c_out[0:n_prev] = cprev_ref[...]


def _seq_a_kernel(x_ref, cos_ref, sin_ref, ng_ref, w_aq, w_ak, w_av, w_c0, w_c1, w_c2, wg_ref, brow_ref, bcol_ref,
                  cw_ref, cb_ref, c0_ref, n0_ref, m0_ref, tail0_ref, y_ref, pc_ref, c_out, n_out, m_out, tail_out,
                  c_s, n_s, m_s, ext_s, v_s, *, L):
    _mlstm_state_io(c0_ref, n0_ref, m0_ref, tail0_ref, c_s, n_s, m_s, ext_s)
    nt = (((1,), (1,)), ((), ()))
    x = x_ref[0]
    r = lax.rsqrt(jnp.mean(x * x, axis=-1, keepdims=True) + EPS)
    xn = (x * r * ng_ref[...]).astype(BF16)
    wg = wg_ref[...].astype(BF16)
    g_col = lax.dot_general(xn, wg, nt, preferred_element_type=F32) + brow_ref[...]
    g_row = lax.dot_general(wg[0:8, :], xn, nt, preferred_element_type=F32) + bcol_ref[...]
    proj = lambda w_ref: lax.dot_general(xn, w_ref[...], nt, preferred_element_type=F32)
    ext_s[8:8 + L, 0:W_A] = proj(w_aq)
    ext_s[8:8 + L, W_A:2 * W_A] = proj(w_ak)
    v_s[...] = proj(w_av)
    piece = 256
    todo = [(w_ref, j, lo) for j, w_ref in enumerate((w_c0, w_c1, w_c2)) for lo in range(0, WBLK, piece)]

    assert piece == WG_C

    def emit_piece():
        w_ref, j, lo = todo.pop(0)
        col = j * WBLK + lo
        y = lax.dot_general(xn, w_ref[lo:lo + piece, :], nt, preferred_element_type=F32)
        if col < 2 * W_C:
            cos = jnp.concatenate([cos_ref[0], cos_ref[0]], axis=1)
            sin = jnp.concatenate([sin_ref[0], sin_ref[0]], axis=1)
            y = _rope(y, cos, sin)
            if col < W_C:
                y = y * (DH_C ** -0.5)
        pc_ref[0, :, col:col + piece] = y

    side_work = lambda: emit_piece() if todo else None
    get_v = lambda h: v_s[:, h * DH_A:(h + 1) * DH_A]
    _mlstm_core(g_col, g_row, get_v, ext_s, cw_ref, cb_ref, c_s, n_s, m_s, y_ref, L, L, side_work)
    while todo:
        emit_piece()
    _mlstm_state_out(c_out, n_out, m_out, tail_out, c_s, n_s, m_s, ext_s)


def _seq_a(x, cos_t, sin_t, norm_g, w_main, w_in_t, bias_row, bias_col, conv_w, conv_b, c0, n0, m0, tail0, l, *, L):
    rows = x.shape[0]
    nch = rows // L
    x3 = x.reshape(nch, L, D_MODEL)
    gate_blk = 4 * W_A // GATE_LANES
    once = pl.Buffered(1)
    chunk = lambda b, c: (c, 0, 0)
    layer3 = lambda b, c: (l, 0, 0)
    wblk = lambda blk: pl.BlockSpec((None, WBLK, D_MODEL), lambda b, c, blk=blk: (l, blk, 0), pipeline_mode=once)
    st5 = lambda b, c: (0, 0, 0, 0, 0)
    st4 = lambda b, c: (0, 0, 0, 0)
    first4 = lambda b, c: (0, 0, 0, 0)
    first3 = lambda b, c: (0, 0, 0)
    y, pc, c1, n1, m1, tail1 = pl.pallas_call(
        functools.partial(_seq_a_kernel, L=L),
        grid=(1, nch),
        in_specs=[
            pl.BlockSpec((1, L, D_MODEL), chunk),
            pl.BlockSpec((1, L, 2 * DH_C), chunk),
            pl.BlockSpec((1, L, 2 * DH_C), chunk),
            pl.BlockSpec((None, 1, D_MODEL), layer3, pipeline_mode=once),
        ] + [wblk(b) for b in (0, 1, 2, BLK_C0, BLK_C0 + 1, BLK_C0 + 2)] + [
            pl.BlockSpec((None, GATE_LANES, D_MODEL), lambda b, c: (l, gate_blk, 0), pipeline_mode=once),
            pl.BlockSpec((None, 1, GATE_LANES), layer3, pipeline_mode=once),
            pl.BlockSpec((None, 8, 1), layer3, pipeline_mode=once),
            pl.BlockSpec((None, CONV_W, 2 * W_A), layer3, pipeline_mode=once),
            pl.BlockSpec((None, 1, 2 * W_A), layer3, pipeline_mode=once),
            pl.BlockSpec((None, 1, H_A, DH_A, DH_A), st5, pipeline_mode=once),
            pl.BlockSpec((None, 1, H_A, 1, DH_A), st5, pipeline_mode=once),
            pl.BlockSpec((None, 1, H_A, 1, GATE_LANES), st5, pipeline_mode=once),
            pl.BlockSpec((None, 1, 8, 2 * W_A), st4, pipeline_mode=once),
        ],
        out_specs=[
            pl.BlockSpec((1, L, W_A), chunk),
            pl.BlockSpec((1, L, 3 * WBLK), chunk),
            pl.BlockSpec((1, 1, H_A, DH_A, DH_A), lambda b, c: (0, 0, 0, 0, 0)),
            pl.BlockSpec((1, H_A, 1, DH_A), first4),
            pl.BlockSpec((1, H_A, 1, GATE_LANES), first4),
            pl.BlockSpec((1, 8, 2 * W_A), first3),
        ],
        out_shape=[
            jax.ShapeDtypeStruct((nch, L, W_A), F32),
            jax.ShapeDtypeStruct((nch, L, 3 * WBLK), F32),
            jax.ShapeDtypeStruct((1, 1, H_A, DH_A, DH_A), F32),
            jax.ShapeDtypeStruct((1, H_A, 1, DH_A), F32),
            jax.ShapeDtypeStruct((1, H_A, 1, GATE_LANES), F32),
            jax.ShapeDtypeStruct((1, 8, 2 * W_A), F32),
        ],
        scratch_shapes=[
            pltpu.VMEM((H_A, DH_A, DH_A), F32),
            pltpu.VMEM((H_A, 1, DH_A), F32),
            pltpu.VMEM((H_A, 1, GATE_LANES), F32),
            pltpu.VMEM((L + 8, 2 * W_A), F32),
            pltpu.VMEM((L, W_A), F32),
        ],
        compiler_params=_cparams(2),
        name="seq_a",
    )(x3, cos_t.reshape(nch, L, 2 * DH_C), sin_t.reshape(nch, L, 2 * DH_C), norm_g, *([w_main] * 6), w_in_t,
      bias_row, bias_col, conv_w, conv_b, c0, n0, m0, tail0)
    return (y.reshape(rows, W_A), pc.reshape(rows, 3 * WBLK), c1[0], n1[:, :, 0, :], m1[:, :, 0, 0],
            tail1[:, 8 - (CONV_W - 1):, :])


def _mlstm(p, gcol, grow, bias_row, bias_col, conv_w, conv_b, c0, n0, m0, tail0, l, ls, c_prev, *, L, t_valid):
    batch = c0.shape[1]
    rows = p.shape[0]
    nch = rows // (batch * L)
    n_prev = 0 if c_prev is None else c_prev.shape[0]
    n_seq = MLSTM_SEQS if (nch == 1 and batch % MLSTM_SEQS == 0 and L % GATE_LANES != 0) else 1
    p3 = p.reshape(batch * nch, L, N_P1)
    gc3 = gcol.reshape(batch * nch, L, GATE_LANES)
    tm = grow.shape[2]
    if L % GATE_LANES == 0:
        per = tm // L
        gr3, gr_index = grow, (lambda b, c: ((b * nch + c) // per, 0, (b * nch + c) % per))
    else:
        gr3 = grow.transpose(1, 0, 2).reshape(8, batch * nch, L).transpose(1, 0, 2)
        gr_index = lambda b, c: (b * nch + c, 0, 0)
    kern = functools.partial(_mlstm_kernel, L=L, t_valid=t_valid, n_prev=n_prev, n_seq=n_seq)
    chunk = lambda b, c: (b * nch + c, 0, 0)
    layer3 = lambda b, c: (l, 0, 0)
    stack5 = lambda b, c: (0, b, 0, 0, 0)
    per_b4 = lambda b, c: (b, 0, 0, 0)
    per_b3 = lambda b, c: (b, 0, 0)
    st5 = lambda b, c: (ls, b, 0, 0, 0)
    st4 = lambda b, c: (ls, b, 0, 0)
    y, c1, n1, m1, tail1 = pl.pallas_call(
        kern,
        grid=(batch // n_seq, nch),
        in_specs=[
            pl.BlockSpec((n_seq, L, 3 * W_A), chunk),
            pl.BlockSpec((n_seq, L, GATE_LANES), chunk),
            pl.BlockSpec((n_seq, 8, L), gr_index),
            pl.BlockSpec((None, 1, GATE_LANES), layer3),
            pl.BlockSpec((None, 8, 1), layer3),
            pl.BlockSpec((None, CONV_W, 2 * W_A), layer3),
            pl.BlockSpec((None, 1, 2 * W_A), layer3),
            pl.BlockSpec((None, n_seq, H_A, DH_A, DH_A), st5),
            pl.BlockSpec((None, n_seq, H_A, 1, DH_A), st5),
            pl.BlockSpec((None, n_seq, H_A, 1, GATE_LANES), st5),
            pl.BlockSpec((None, n_seq, 8, 2 * W_A), st4),
        ] + ([pl.BlockSpec((n_prev, n_seq, H_A, DH_A, DH_A), stack5)] if n_prev else []),
        out_specs=[
            pl.BlockSpec((n_seq, L, W_A), chunk),
            pl.BlockSpec((n_prev + 1, n_seq, H_A, DH_A, DH_A), stack5),
            pl.BlockSpec((n_seq, H_A, 1, DH_A), per_b4),
            pl.BlockSpec((n_seq, H_A, 1, GATE_LANES), per_b4),
            pl.BlockSpec((n_seq, 8, 2 * W_A), per_b3),
        ],
        out_shape=[
            jax.ShapeDtypeStruct((batch * nch, L, W_A), F32),
            jax.ShapeDtypeStruct((n_prev + 1, batch, H_A, DH_A, DH_A), F32),
            jax.ShapeDtypeStruct((batch, H_A, 1, DH_A), F32),
            jax.ShapeDtypeStruct((batch, H_A, 1, GATE_LANES), F32),
            jax.ShapeDtypeStruct((batch, 8, 2 * W_A), F32),
        ],
        scratch_shapes=[
            pltpu.VMEM((n_seq, H_A, DH_A, DH_A), F32),
            pltpu.VMEM((n_seq, H_A, 1, DH_A), F32),
            pltpu.VMEM((n_seq, H_A, 1, GATE_LANES), F32),
            pltpu.VMEM((n_seq, L + 8, 2 * W_A), F32),
        ],
        compiler_params=_cparams(2),
        name="mlstm",
    )(p3, gc3, gr3, bias_row, bias_col, conv_w, conv_b, c0, n0, m0, tail0, *([c_prev] if n_prev else []))
    return y.reshape(rows, W_A), c1, n1[:, :, 0, :], m1[:, :, 0, 0], tail1[:, 8 - (CONV_W - 1):, :]


def _rope(x, cos, sin):
    width = x.shape[1]
    lane = lax.broadcasted_iota(jnp.int32, (1, width), 1) % DH_C
    half = ROT_DIM // 2
    partner = jnp.where(lane < half, pltpu.roll(x, width - half, 1), pltpu.roll(x, half, 1))
    return x * cos + partner * sin


def _dil_attn_kernel(q_ref, k_ref, v_ref, o_ref, md_ref, qs, kbuf, vbuf, stage, bias, *, d):
    i = pl.program_id(0)
    T = ATT_TILE
    HL = WG_C // 2
    nblk = T // SPAN_C
    nres = T // (SPAN_C * d)
    cur = (i % 2) * T
    prev = T - cur
    neg_inf = jnp.float32(-jnp.inf)

    def split(x):
        return x[:, 0:HL], x[:, HL:WG_C]

    def put(buf, lo, x):
        a, b = split(x)
        buf[0, pl.ds(lo, T), :] = a
        buf[1, pl.ds(lo, T), :] = b

    @pl.when(i == 0)
    def _():
        ii = lax.broadcasted_iota(jnp.int32, (SPAN_C, 2 * SPAN_C), 0)
        jj = lax.broadcasted_iota(jnp.int32, (SPAN_C, 2 * SPAN_C), 1)
        band = (jj >= ii) & (jj <= ii + SPAN_C)
        bias[1] = jnp.where(band, 0.0, neg_inf)
        bias[0] = jnp.where(band & (jj >= SPAN_C), 0.0, neg_inf)
        kbuf[:, T:2 * T, :] = jnp.zeros((2, T, HL), F32)
        vbuf[:, T:2 * T, :] = jnp.zeros((2, T, HL), F32)

    def put_residue_major(buf, lo, x):
        if d == 1:
            put(buf, lo, x)
            return
        put(stage, 0, x)
        for r in range(d):
            for half in range(2):
                buf[half, pl.ds(lo + r * (T // d), T // d), :] = stage[half, pl.ds(r, T // d, stride=d), :]

    put(qs, 0, q_ref[...])
    put_residue_major(kbuf, cur, k_ref[...])
    put_residue_major(vbuf, cur, v_ref[...])

    lane_head = lax.broadcasted_iota(jnp.int32, (SPAN_C, WG_C), 1) // DH_C
    hm = [(lane_head == h).astype(F32) for h in range(HG_C)]
    lane_head_lo = lax.broadcasted_iota(jnp.int32, (SPAN_C, WG_C), 1) % DH_C < DH_C // 2

    def rows(start, size):
        return pl.ds(start, size) if d == 1 else pl.ds(start, size, stride=d)

    def take(buf, start, size):
        return jnp.concatenate([buf[0, rows(start, size), :], buf[1, rows(start, size), :]], axis=1)

    def pick(per_head):
        out = jnp.broadcast_to(per_head[0], (SPAN_C, WG_C))
        for h in range(1, HG_C):
            out = jnp.where(lane_head >= h, per_head[h], out)
        return out

    def body(blk, carry):
        r = blk % d
        c = blk // d
        qstart = r + SPAN_C * d * c
        in_tile = cur + SPAN_C * (r * nres + c)
        before = jnp.where(c > 0, in_tile - SPAN_C, prev + SPAN_C * (r * nres + nres - 1))
        in_tile, before = pl.multiple_of(in_tile, SPAN_C), pl.multiple_of(before, SPAN_C)

        def keys(buf, start):
            return jnp.concatenate([buf[0, pl.ds(start, SPAN_C), :], buf[1, pl.ds(start, SPAN_C), :]], axis=1)

        qb = take(qs, qstart, SPAN_C)
        kb = jnp.concatenate([keys(kbuf, before), keys(kbuf, in_tile)], axis=0)
        vb = jnp.concatenate([keys(vbuf, before), keys(vbuf, in_tile)], axis=0)
        has_prev = jnp.logical_or(i > 0, c > 0).astype(jnp.int32)
        kb16, vb16 = kb.astype(BF16), vb.astype(BF16)
        pvs, mxs, dens = [], [], []
        for h0 in range(0, HG_C, 2):
            q2 = jnp.concatenate([qb * hm[h0], qb * hm[h0 + 1]], axis=0)
            s = _dot_nt(q2, kb16).reshape(2, SPAN_C, 2 * SPAN_C) + bias[has_prev]
            mx = jnp.max(s, axis=2, keepdims=True)
            p = jnp.exp(s - mx)
            den = jnp.sum(p, axis=2, keepdims=True)
            pv = _dot(p.reshape(2 * SPAN_C, 2 * SPAN_C), vb16).reshape(2, SPAN_C, WG_C)
            pvs += [pv[0], pv[1]]
            mxs += [mx[0], mx[1]]
            dens += [den[0], den[1]]
        outs = (pick(pvs), jnp.where(lane_head_lo, pick(mxs), pick(dens)))
        for ref, val in zip((o_ref, md_ref), outs):
            for half, part in enumerate(split(val)):
                ref[half, rows(qstart, SPAN_C), :] = part
        return carry

    lax.fori_loop(0, nblk, body, 0, unroll=16)


def _dil_attn(p, base, g, d):
    rows = p.shape[0]
    T = ATT_TILE
    kern = functools.partial(_dil_attn_kernel, d=d)
    col = lambda off: pl.BlockSpec((T, WG_C), lambda i, off=off: (i, (base + off - OFF_CQ) // WG_C + g))
    slab = pl.BlockSpec((2, T, WG_C // 2), lambda i: (0, i, 0))
    slab_shape = jax.ShapeDtypeStruct((2, rows, WG_C // 2), F32)
    return pl.pallas_call(
        kern,
        grid=(rows // T,),
        in_specs=[col(OFF_CQ), col(OFF_CK), col(OFF_CV)],
        out_specs=[slab, slab],
        out_shape=[slab_shape, slab_shape],
        scratch_shapes=[pltpu.VMEM((2, T, WG_C // 2), F32), pltpu.VMEM((2, 2 * T, WG_C // 2), F32),
                        pltpu.VMEM((2, 2 * T, WG_C // 2), F32), pltpu.VMEM((2, T, WG_C // 2), F32),
                        pltpu.VMEM((2, SPAN_C, 2 * SPAN_C), F32)],
        compiler_params=_cparams(1),
        name="dil_attn_d%d" % d,
    )(p, p, p)


def _sample_attn_kernel(pc0, pc1, pc2, cos_ref, sin_ref, c0_ref, c1_ref, c2_ref,
                        o0, o1, o2, m0, m1, m2, d0, d1, d2, kr_ref, *, n_tok, n_seq):
    reps = W_C // (2 * DH_C)
    cos = jnp.concatenate([cos_ref[...]] * reps, axis=1)
    sin = jnp.concatenate([sin_ref[...]] * reps, axis=1)
    nrow = HG_C * SAMPLE_ROWS
    rowh = lax.broadcasted_iota(jnp.int32, (nrow, WG_C), 0) // SAMPLE_ROWS
    laneh = lax.broadcasted_iota(jnp.int32, (nrow, WG_C), 1) // DH_C
    hm = (rowh == laneh).astype(F32)
    tok_col = lax.broadcasted_iota(jnp.int32, (nrow, 1), 0) % SAMPLE_ROWS
    out_row = lax.broadcasted_iota(jnp.int32, (SAMPLE_ROWS, WG_C), 0)
    neg_inf = jnp.float32(-jnp.inf)
    crefs = (c0_ref, c1_ref, c2_ref)
    orefs = (o0, o1, o2)
    mrefs = (m0, m1, m2)
    drefs = (d0, d1, d2)
    for e in range(n_seq):
        rows_e = slice(e * SAMPLE_ROWS, (e + 1) * SAMPLE_ROWS)
        pc = jnp.concatenate([pc0[e], pc1[e], pc2[e]], axis=1)
        qr = _rope(pc[:, 0:W_C], cos, sin) * (DH_C ** -0.5)
        kr = _rope(pc[:, W_C:2 * W_C], cos, sin)
        v = pc[:, 2 * W_C:3 * W_C]
        kr_ref[e] = kr
        for g, (_, d) in enumerate(DIL_PATTERNS):
            gs = slice(g * WG_C, (g + 1) * WG_C)
            n_buf = SPAN_C * d
            kn = kr[:, gs]
            vn = v[:, gs]
            qm = jnp.concatenate([qr[:, gs]] * HG_C, axis=0) * hm
            s = _dot(qm, crefs[g][e, 0])
            tok = lax.broadcasted_iota(jnp.int32, (nrow, n_buf), 0) % SAMPLE_ROWS
            pos = lax.broadcasted_iota(jnp.int32, (nrow, n_buf), 1)
            attends = (pos >= tok) if d == 1 else ((pos & (d - 1)) == tok)
            s = jnp.where(attends | (tok >= n_tok), s, neg_inf)
            new_ok = [(u <= tok_col) if d == 1 else (u == tok_col) for u in range(n_tok)]
            s_new = [jnp.where(ok, jnp.sum(qm * kn[u:u + 1, :], axis=1, keepdims=True), neg_inf)
                     for u, ok in enumerate(new_ok)]
            mx = jnp.max(s, axis=1, keepdims=True)
            for sn in s_new:
                mx = jnp.maximum(mx, sn)
            p = jnp.exp(s - mx)
            den = jnp.sum(p, axis=1, keepdims=True)
            acc = _dot_nt(p, crefs[g][e, 1])
            for u, sn in enumerate(s_new):
                pn = jnp.exp(sn - mx)
                den = den + pn
                acc = acc + pn * vn[u:u + 1, :]
            stacked = (acc * hm, mx * hm, den * hm)
            for ref, val, pad in zip((orefs[g], mrefs[g], drefs[g]), stacked, (0.0, 0.0, 1.0)):
                v8 = val[0:SAMPLE_ROWS]
                for h in range(1, HG_C):
                    v8 = v8 + val[h * SAMPLE_ROWS:(h + 1) * SAMPLE_ROWS]
                v8 = jnp.where(out_row < n_tok, v8, pad)
                for half in range(2):
                    ref[half, rows_e, :] = v8[:, half * (WG_C // 2):(half + 1) * (WG_C // 2)]


def _cache_views(caches):
    views = []
    for (win, d), c in zip(DIL_PATTERNS, caches):
        depth, batch, n_buf = c.shape[:3]
        assert n_buf == SPAN_C * d
        views.append(jnp.transpose(c, (0, 1, 3, 4, 5, 2)).reshape(depth, batch, 2, WG_C, n_buf))
    return views


def _sample_attn(p, cos_s, sin_s, views, l, n_tok):
    batch = p.shape[0] // SAMPLE_ROWS
    n_seq = SAMPLE_ATT_SEQS if batch % SAMPLE_ATT_SEQS == 0 else 1
    p3 = p.reshape(batch, SAMPLE_ROWS, N_P1)
    pcol = lambda k: pl.BlockSpec((n_seq, SAMPLE_ROWS, WBLK), lambda b, k=k: (b, 0, OFF_CQ // WBLK + k))
    tab = pl.BlockSpec((SAMPLE_ROWS, 2 * DH_C), lambda b: (0, 0))
    cache_spec = lambda d: pl.BlockSpec((None, n_seq, 2, WG_C, SPAN_C * d), lambda b: (l, b, 0, 0, 0))
    out = pl.BlockSpec((2, n_seq * SAMPLE_ROWS, WG_C // 2), lambda b: (0, b, 0))
    res = pl.pallas_call(
        functools.partial(_sample_attn_kernel, n_tok=n_tok, n_seq=n_seq),
        grid=(batch // n_seq,),
        in_specs=[pcol(0), pcol(1), pcol(2), tab, tab] + [cache_spec(d) for _, d in DIL_PATTERNS],
        out_specs=[out] * 9 + [pl.BlockSpec((n_seq, SAMPLE_ROWS, W_C), lambda b: (b, 0, 0))],
        out_shape=[jax.ShapeDtypeStruct((2, batch * SAMPLE_ROWS, WG_C // 2), F32)] * 9
        + [jax.ShapeDtypeStruct((batch, SAMPLE_ROWS, W_C), F32)],
        compiler_params=_cparams(1),
        name="sample_attn",
    )(p3, p3, p3, cos_s, sin_s, *views)
    return list(res[:9]), res[9].reshape(batch * SAMPLE_ROWS, W_C)


def _post_kernel(x_ref, h_ref, o0, o1, o2, md0, md1, md2, cz_ref, ng_ref, w_ao, w_az, w_bu, w_bv, w_bz, w_ga, w_gb,
                 w_gc, lg_ref, lb_ref, wsp_ref, bsp_ref, wb_ref, wo_ref, fg_ref, out_ref, *rest, nchunks, final,
                 emit_vn):
    if emit_vn:
        vn_out, vn_s, yb_s = rest
    else:
        vn_s, yb_s = rest
    nt = (((1,), (1,)), ((), ()))
    x = x_ref[...]
    r = lax.rsqrt(jnp.mean(x * x, axis=-1, keepdims=True) + EPS)
    xn = (x * r * ng_ref[...]).astype(BF16)
    proj = lambda w_ref: lax.dot_general(xn, w_ref[...], nt, preferred_element_type=F32)
    branch = lambda y, lo, hi: jnp.dot(y.astype(BF16), wb_ref[lo:hi, :], preferred_element_type=F32)

    ya = h_ref[...] * _sigmoid(proj(w_ao)) * _silu(proj(w_az))
    acc = _sigmoid(proj(w_ga)) * branch(ya, 0, W_A)

    v = proj(w_bv)
    mu = jnp.mean(v, axis=-1, keepdims=True)
    var = jnp.mean(jnp.square(v - mu), axis=-1, keepdims=True)
    vn = (v - mu) * lax.rsqrt(var + EPS) * lg_ref[...] + lb_ref[...]
    if emit_vn:
        vn_out[...] = vn
    vn_s[...] = vn.astype(BF16)
    gate_b = proj(w_bu) * _silu(proj(w_bz))
    tri = (lax.broadcasted_iota(jnp.int32, (CHUNK_B, CHUNK_B), 0)
           >= lax.broadcasted_iota(jnp.int32, (CHUNK_B, CHUNK_B), 1))
    for g in range(G_B):
        wg = jnp.where(tri, wsp_ref[g], 0.0).astype(BF16)
        cs = slice(g * DG_B, (g + 1) * DG_B)
        for c in range(nchunks):
            rs = slice(c * CHUNK_B, (c + 1) * CHUNK_B)
            yb_s[rs, cs] = jnp.dot(wg, vn_s[rs, cs], preferred_element_type=F32) + bsp_ref[g]
    acc = acc + _sigmoid(proj(w_gb)) * branch(gate_b * yb_s[...], W_A, W_A + W_B)

    unslab = lambda ref: jnp.concatenate([ref[0], ref[1]], axis=1)
    half_head = DH_C // 2
    lo = lax.broadcasted_iota(jnp.int32, (x.shape[0], WG_C // 2), 1) % DH_C < half_head

    def unpack(ref):
        m_parts, d_parts = [], []
        for half in range(2):
            v = ref[half]
            m_parts.append(jnp.where(lo, v, pltpu.roll(v, half_head, 1)))
            d_parts.append(jnp.where(lo, pltpu.roll(v, WG_C // 2 - half_head, 1), v))
        return jnp.concatenate(m_parts, axis=1), jnp.concatenate(d_parts, axis=1)

    ms, ds = zip(*[unpack(ref) for ref in (md0, md1, md2)])
    mx = jnp.maximum(jnp.maximum(ms[0], ms[1]), ms[2])
    es = [jnp.exp(m - mx) for m in ms]
    inv_tot = 1.0 / (es[0] * ds[0] + es[1] * ds[1] + es[2] * ds[2])
    pc = None
    for g, o_ref in enumerate((o0, o1, o2)):
        gs = slice(g * WG_C, (g + 1) * WG_C)
        yc = unslab(o_ref) * (es[g] * inv_tot) * _silu(cz_ref[:, gs])
        t = branch(yc, W_A + W_B + g * WG_C, W_A + W_B + (g + 1) * WG_C)
        pc = t if pc is None else pc + t
    acc = acc + _sigmoid(proj(w_gc)) * pc

    out = x + jnp.dot(acc.astype(BF16), wo_ref[...], preferred_element_type=F32)
    if final:
        rr = lax.rsqrt(jnp.mean(out * out, axis=-1, keepdims=True) + EPS)
        out = out * rr * fg_ref[...]
    out_ref[...] = out


def _post(x, p1, base, h, att, norm_g, w_main, ln_g, ln_b, w_sp, b_sp_col, w_branch, w_out, final_g, l, final,
          emit_vn):
    rows = x.shape[0]
    tm = min(rows, POST_TM)
    once = pl.Buffered(1)
    row = lambda w: pl.BlockSpec((tm, w), lambda i: (i, 0))
    slab = pl.BlockSpec((2, tm, WG_C // 2), lambda i: (0, i, 0))
    wblk = lambda b: pl.BlockSpec((None, WBLK, D_MODEL), lambda i, b=b: (l, b, 0), pipeline_mode=once)
    vec = lambda w: pl.BlockSpec((None, 1, w), lambda i: (l, 0, 0), pipeline_mode=once)
    in_specs = (
        [row(D_MODEL), row(W_A)] + [slab] * 6
        + [pl.BlockSpec((tm, W_C), lambda i: (i, (base + OFF_CZ - OFF_CQ) // W_C)), vec(D_MODEL)]
        + [wblk(b) for b in (BLK_AO, BLK_AZ, BLK_BU, BLK_BV, BLK_BZ, BLK_GA, BLK_GB, BLK_GC)]
        + [vec(W_B), vec(W_B),
           pl.BlockSpec((None, G_B, CHUNK_B, CHUNK_B), lambda i: (l, 0, 0, 0), pipeline_mode=once),
           pl.BlockSpec((None, G_B, CHUNK_B, 1), lambda i: (l, 0, 0, 0), pipeline_mode=once),
           pl.BlockSpec((None, W_A + W_B + W_C, D_MODEL), lambda i: (l, 0, 0), pipeline_mode=once),
           pl.BlockSpec((None, D_MODEL, D_MODEL), lambda i: (l, 0, 0), pipeline_mode=once),
           pl.BlockSpec((1, D_MODEL), lambda i: (0, 0), pipeline_mode=once)])
    out_specs = [row(D_MODEL)]
    out_shape = [jax.ShapeDtypeStruct((rows, D_MODEL), F32)]
    if emit_vn:
        out_specs.append(row(W_B))
        out_shape.append(jax.ShapeDtypeStruct((rows, W_B), F32))
    res = pl.pallas_call(
        functools.partial(_post_kernel, nchunks=tm // CHUNK_B, final=final, emit_vn=emit_vn),
        grid=(rows // tm,),
        in_specs=in_specs,
        out_specs=out_specs,
        out_shape=out_shape,
        scratch_shapes=[pltpu.VMEM((tm, W_B), BF16), pltpu.VMEM((tm, W_B), F32)],
        compiler_params=_cparams(1),
        name="post",
    )(x, h, *att, p1, norm_g, *([w_main] * 8), ln_g, ln_b, w_sp, b_sp_col, w_branch, w_out, final_g)
    return (res[0], res[1]) if emit_vn else (res[0], None)


def _rope_tables(pos):
    half = ROT_DIM // 2
    inv = ROPE_THETA ** (-jnp.arange(half, dtype=F32) / half)
    ang = inv[:, None] * pos.astype(F32)[None, :]
    lane = jnp.arange(2 * DH_C) % DH_C
    freq = jnp.arange(half)[:, None] == (lane % half)[None, :]
    rot = (lane < ROT_DIM)[None, :]
    sel_cos = (freq & rot).astype(F32)
    sel_sin = sel_cos * jnp.where(lane < half, -1.0, 1.0)[None, :]
    spread = lambda t, sel: lax.dot_general(t, sel, (((0,), (0,)), ((), ())), precision=lax.Precision.HIGHEST)
    return spread(jnp.cos(ang), sel_cos) + (~rot).astype(F32), spread(jnp.sin(ang), sel_sin)


def _stack_kv(k_rows, v_rows):
    b, t, _ = k_rows.shape
    return jnp.stack([k_rows.reshape(b, t, HG_C, DH_C), v_rows.reshape(b, t, HG_C, DH_C)], axis=2)


def kernel(x_prompt, x_sample, state_C, state_n, state_m, state_conv, cache_kv_w128, cache_kv_w512, cache_kv_w2048,
           norm_g, w_in, b_igate, b_fgate, conv_w, conv_b, ln_v_g, ln_v_b, w_spatial, b_spatial, w_branch, w_out,
           final_norm_g):
    depth = w_in.shape[0]
    bp, seq, _ = x_prompt.shape
    bs, n_tok, _ = x_sample.shape
    assert bp == 1 and seq % ATT_TILE == 0 and n_tok <= SAMPLE_ROWS // 2 and n_tok >= CONV_W - 1
    caches = (cache_kv_w128, cache_kv_w512, cache_kv_w2048)
    pad_tok = SAMPLE_ROWS - n_tok
    rows_s = bs * SAMPLE_ROWS

    w_in_t = jnp.swapaxes(w_in, 1, 2)
    w_main = _pack_w_in(w_in_t)
    gate_bias = jnp.concatenate([b_igate, b_fgate], axis=1)
    bias_row = jnp.pad(gate_bias, ((0, 0), (0, GATE_LANES - 2 * H_A)))[:, None, :]
    bias_col = gate_bias[:, :, None]
    wb16 = w_branch.astype(BF16)
    wo16 = w_out.astype(BF16)
    b_sp_col = b_spatial[..., None]
    norm_g3, conv_b3 = norm_g[:, None, :], conv_b[:, None, :]
    ln_g3, ln_b3 = ln_v_g[:, None, :], ln_v_b[:, None, :]
    seqs_per_chunk = CHUNK_B // SAMPLE_ROWS
    w8 = jnp.pad(w_spatial[:, :, :n_tok, :n_tok], ((0, 0), (0, 0), (0, pad_tok), (0, pad_tok)))
    w_sp_s = jnp.einsum('ab,lgij->lgaibj', jnp.eye(seqs_per_chunk, dtype=F32), w8).reshape(
        depth, G_B, CHUNK_B, CHUNK_B)
    b_sp_s = jnp.tile(jnp.pad(b_spatial[:, :, :n_tok], ((0, 0), (0, 0), (0, pad_tok))),
                      (1, 1, seqs_per_chunk))[..., None]

    cos_p, sin_p = _rope_tables(jnp.arange(seq))
    cos_s, sin_s = _rope_tables(PAST_LEN + jnp.arange(SAMPLE_ROWS))

    hp = x_prompt.reshape(seq, D_MODEL)
    hs = jnp.pad(x_sample, ((0, 0), (0, pad_tok), (0, 0))).reshape(rows_s, D_MODEL)
    fg = final_norm_g[None, :]
    zeros_c = jnp.zeros((1, 1, H_A, DH_A, DH_A), F32)
    zeros_n = jnp.zeros((1, 1, H_A, 1, DH_A), F32)
    zeros_m = jnp.zeros((1, 1, H_A, 1, GATE_LANES), F32)
    zeros_tail = jnp.zeros((1, 1, 8, 2 * W_A), F32)
    n0_s = state_n[:, :, :, None, :]
    m0_s = jnp.broadcast_to(state_m[:, :, :, None, None], (depth, bs, H_A, 1, GATE_LANES))
    tail0_s = jnp.pad(state_conv, ((0, 0), (0, 0), (8 - (CONV_W - 1), 0), (0, 0)))
    views = _cache_views(caches)

    c_stack_s = None
    p_out = {k: [] for k in ('C', 'n', 'm', 'conv', 'kv0', 'kv1', 'kv2')}
    s_out = {k: [] for k in ('n', 'm', 'conv', 'chunk_v', 'kv0', 'kv1', 'kv2')}
    for l in range(depth):
        final = l == depth - 1

        ya, pp, c1, n1, m1, conv1 = _seq_a(hp, cos_p, sin_p, norm_g3, w_main, w_in_t, bias_row, bias_col, conv_w, conv_b3,
                                           zeros_c, zeros_n, zeros_m, zeros_tail, l, L=MLSTM_L)
        att = [[], []]
        for g, (win, d) in enumerate(DIL_PATTERNS):
            for dst, val in zip(att, _dil_attn(pp, 0, g, d)):
                dst.append(val)
            keep = min(win, seq)
            k_g = pp[seq - keep:, OFF_CK - OFF_CQ + g * WG_C:OFF_CK - OFF_CQ + (g + 1) * WG_C]
            v_g = pp[seq - keep:, OFF_CV - OFF_CQ + g * WG_C:OFF_CV - OFF_CQ + (g + 1) * WG_C]
            p_out['kv%d' % g].append(_stack_kv(k_g[None], v_g[None]))
        hp, _ = _post(hp, pp, 0, ya, att[0] + att[1], norm_g3, w_main, ln_g3, ln_b3, w_spatial, b_sp_col, wb16, wo16, fg, l,
                      final, False)
        p_out['C'].append(c1)
        p_out['n'].append(n1)
        p_out['m'].append(m1)
        p_out['conv'].append(conv1)

        ps, gcol, grow = _inproj(hs, norm_g3, w_main, w_in_t, l)
        ya, c1, n1, m1, conv1 = _mlstm(ps, gcol, grow, bias_row, bias_col, conv_w, conv_b3,
                                       state_C, n0_s, m0_s, tail0_s, l, l, c_stack_s, L=SAMPLE_ROWS, t_valid=n_tok)
        c_stack_s = c1
        att_s, kr = _sample_attn(ps, cos_s, sin_s, views, l, n_tok)
        lo = lax.broadcasted_iota(jnp.int32, att_s[0].shape, 2) % DH_C < DH_C // 2
        att_s = list(att_s[:3]) + [jnp.where(lo, m, dd) for m, dd in zip(att_s[3:6], att_s[6:9])]
        hs, vn = _post(hs, ps, OFF_CQ, ya, att_s, norm_g3, w_main, ln_g3, ln_b3, w_sp_s, b_sp_s, wb16, wo16, fg, l,
                       final, True)
        s_out['n'].append(n1)
        s_out['m'].append(m1)
        s_out['conv'].append(conv1)
        s_out['chunk_v'].append(vn.reshape(bs, SAMPLE_ROWS, W_B)[:, :n_tok])
        kr3 = kr.reshape(bs, SAMPLE_ROWS, W_C)[:, :n_tok]
        v3 = ps[:, OFF_CV:OFF_CV + W_C].reshape(bs, SAMPLE_ROWS, W_C)[:, :n_tok]
        for g in range(len(DIL_PATTERNS)):
            gs = slice(g * WG_C, (g + 1) * WG_C)
            s_out['kv%d' % g].append(_stack_kv(kr3[:, :, gs], v3[:, :, gs]))

    stk = lambda d, k: jnp.stack(d[k], axis=0)
    y_prompt = hp.reshape(bp, seq, D_MODEL)
    y_sample = hs.reshape(bs, SAMPLE_ROWS, D_MODEL)[:, :n_tok]
    return (y_prompt, y_sample,
            stk(p_out, 'C'), stk(p_out, 'n'), stk(p_out, 'm'), stk(p_out, 'conv'),
            stk(p_out, 'kv0'), stk(p_out, 'kv1'), stk(p_out, 'kv2'),
            c_stack_s, stk(s_out, 'n'), stk(s_out, 'm'), stk(s_out, 'conv'), stk(s_out, 'chunk_v'),
            stk(s_out, 'kv0'), stk(s_out, 'kv1'), stk(s_out, 'kv2'))
```
